```python
import math
import jax
import jax.numpy as jnp
from jax import lax
import numpy as np

D_MODEL = 1024
BATCH = 2
SEQ = 8192
DEPTH = 2
DEC_BATCH = 128
DEC_SEQ = 4
PAST_LEN = 16384
PAGE_SIZE = 128

D_POOL = D_MODEL // 2
POOL_WINDOWS = (2, 4, 8, 16)
N_POOL_GROUPS = len(POOL_WINDOWS)
POOL_GROUP_DIM = D_POOL // N_POOL_GROUPS
POOL_HIST = max(POOL_WINDOWS) - 1
HEAD_DIM = 64
N_HEADS = (D_MODEL // 2) // HEAD_DIM
N_KV_HEADS = 2
GROUP = N_HEADS // N_KV_HEADS
Q_DIM = N_HEADS * HEAD_DIM
KV_DIM = N_KV_HEADS * HEAD_DIM
WINDOW = 128
ATTN_BLOCK = WINDOW
ATTN_SCALE = HEAD_DIM ** -0.5
REL_BUCKETS = 32
REL_MAX_DISTANCE = 128
D_IN = D_POOL + Q_DIM + 2 * KV_DIM + 2 * D_MODEL
IN_SPLITS = (D_POOL, D_POOL + Q_DIM, D_POOL + Q_DIM + KV_DIM,
             D_POOL + Q_DIM + 2 * KV_DIM, D_POOL + Q_DIM + 2 * KV_DIM + D_MODEL)
N_EXPERT_GROUPS = 4
EXPERTS_PER_GROUP = 8
N_EXPERTS = N_EXPERT_GROUPS * EXPERTS_PER_GROUP
TOP_K = 2
D_EXPERT = D_MODEL // 4
DEEPNORM_ALPHA = (2 * DEPTH) ** 0.25
DEEPNORM_BETA = (8 * DEPTH) ** -0.25
LN_EPS = 1e-5

kernel_name = 'hybrid_pool_swa_hmoe_decoder_step'


def layer_norm(x, g, b):
    xf = x.astype(jnp.float32)
    xc = xf - jnp.mean(xf, axis=-1, keepdims=True)
    var = jnp.mean(xc * xc, axis=-1, keepdims=True)
    return (xc * lax.rsqrt(var + LN_EPS) * g.astype(jnp.float32) + b.astype(jnp.float32)).astype(x.dtype)


def t5_bucket(dist):
    n = jnp.maximum(dist, 0)
    max_exact = REL_BUCKETS // 2
    nf = jnp.maximum(n, 1).astype(jnp.float32)
    large = max_exact + (jnp.log(nf / max_exact) / math.log(REL_MAX_DISTANCE / max_exact)
                         * (REL_BUCKETS - max_exact)).astype(jnp.int32)
    large = jnp.minimum(large, REL_BUCKETS - 1)
    return jnp.where(n < max_exact, n, large)


def pool_mixer(u, hist, start_pos, pool_w, pool_scale):
    B, L, _ = u.shape
    P = POOL_HIST
    ext = jnp.concatenate([hist, u], axis=1)
    pos = start_pos - P + jnp.arange(P + L)
    extf = jnp.where((pos >= 0)[None, :, None], ext.astype(jnp.float32), 0.0)
    csum = jnp.pad(jnp.cumsum(extf, axis=1), ((0, 0), (1, 0), (0, 0)))
    out_pos = start_pos + jnp.arange(L)
    means = []
    for gi, w in enumerate(POOL_WINDOWS):
        sl = slice(gi * POOL_GROUP_DIM, (gi + 1) * POOL_GROUP_DIM)
        s = csum[:, P + 1:P + L + 1, sl] - csum[:, P + 1 - w:P + L + 1 - w, sl]
        cnt = jnp.minimum(w, out_pos + 1).astype(jnp.float32)
        means.append(s / cnt[None, :, None])
    pooled = (jnp.concatenate(means, axis=-1) - u.astype(jnp.float32)).astype(u.dtype)
    mixed = jnp.einsum('blgc,gcd->blgd', pooled.reshape(B, L, N_POOL_GROUPS, POOL_GROUP_DIM), pool_w)
    a = mixed.reshape(B, L, D_POOL) * pool_scale
    return a, ext[:, -P:]


def swa_core(q, k, v, q_pos, k_pos, rel_bias, sinks):
    N, Lq = q.shape[:2]
    Lk = k.shape[1]
    qg = q.reshape(N, Lq, N_KV_HEADS, GROUP, HEAD_DIM)
    s = jnp.einsum('nqkgd,nskd->nkgqs', qg, k).astype(jnp.float32) * ATTN_SCALE
    dist = q_pos[:, :, None] - k_pos[:, None, :]
    bias = rel_bias[t5_bucket(dist)]
    bias = jnp.moveaxis(bias, -1, 1).reshape(-1, N_KV_HEADS, GROUP, Lq, Lk).astype(jnp.float32)
    mask = (dist >= 0) & (dist <= WINDOW) & (k_pos[:, None, :] >= 0)
    s = jnp.where(mask[:, None, None], s + bias, -jnp.inf)
    sk = sinks.astype(jnp.float32).reshape(1, N_KV_HEADS, GROUP, 1, 1)
    m = jnp.maximum(jnp.max(s, axis=-1, keepdims=True), sk)
    p = jnp.exp(s - m)
    p = p / (jnp.sum(p, axis=-1, keepdims=True) + jnp.exp(sk - m))
    o = jnp.einsum('nkgqs,nskd->nqkgd', p.astype(v.dtype), v)
    return o.reshape(N, Lq, Q_DIM)


def swa_prompt(q, k, v, rel_bias, sinks):
    B, L = q.shape[:2]
    nb = L // ATTN_BLOCK
    qb = q.reshape(B * nb, ATTN_BLOCK, N_HEADS, HEAD_DIM)

    def band(t):
        tb = t.reshape(B, nb, ATTN_BLOCK, N_KV_HEADS, HEAD_DIM)
        prev = jnp.pad(tb, ((0, 0), (1, 0), (0, 0), (0, 0), (0, 0)))[:, :-1]
        return jnp.concatenate([prev, tb], axis=2).reshape(B * nb, 2 * ATTN_BLOCK, N_KV_HEADS, HEAD_DIM)

    blk = jnp.arange(nb)[:, None] * ATTN_BLOCK
    q_pos = blk + jnp.arange(ATTN_BLOCK)[None]
    k_pos = blk - ATTN_BLOCK + jnp.arange(2 * ATTN_BLOCK)[None]
    q_pos = jnp.broadcast_to(q_pos[None], (B, nb, ATTN_BLOCK)).reshape(B * nb, ATTN_BLOCK)
    k_pos = jnp.broadcast_to(k_pos[None], (B, nb, 2 * ATTN_BLOCK)).reshape(B * nb, 2 * ATTN_BLOCK)
    o = swa_core(qb, band(k), band(v), q_pos, k_pos, rel_bias, sinks)
    return o.reshape(B, L, Q_DIM)


def swa_cached(q, k, v, k_hist, v_hist, start_pos, rel_bias, sinks):
    L = q.shape[1]
    k_ext = jnp.concatenate([k_hist, k], axis=1)
    v_ext = jnp.concatenate([v_hist, v], axis=1)
    q_pos = (start_pos + jnp.arange(L))[None]
    k_pos = (start_pos - WINDOW + jnp.arange(WINDOW + L))[None]
    o = swa_core(q, k_ext, v_ext, q_pos, k_pos, rel_bias, sinks)
    return o, k_ext[:, -WINDOW:], v_ext[:, -WINDOW:]


def token_mixer(x, w_in, pool_w, pool_scale, proj_a, proj_b, w_out, sinks, rel_bias,
                pool_hist, k_hist, v_hist, start_pos):
    B, L, _ = x.shape
    u = x @ w_in
    u_pool, q, k, v, g_a, g_b = jnp.split(u, IN_SPLITS, axis=-1)
    q = q.reshape(B, L, N_HEADS, HEAD_DIM)
    k = k.reshape(B, L, N_KV_HEADS, HEAD_DIM)
    v = v.reshape(B, L, N_KV_HEADS, HEAD_DIM)
    if pool_hist is None:
        pool_hist = jnp.zeros((B, POOL_HIST, D_POOL), u.dtype)
    a, pool_state = pool_mixer(u_pool, pool_hist, start_pos, pool_w, pool_scale)
    if k_hist is None:
        b = swa_prompt(q, k, v, rel_bias, sinks)
        k_state, v_state = k[:, -WINDOW:], v[:, -WINDOW:]
    else:
        b, k_state, v_state = swa_cached(q, k, v, k_hist, v_hist, start_pos, rel_bias, sinks)
    merged = jax.nn.sigmoid(g_a) * (a @ proj_a) + jax.nn.sigmoid(g_b) * (b @ proj_b)
    return merged @ w_out, k_state, v_state, pool_state


def hier_moe(x, wg_r, bg_r, we_r, be_r, w_gate, w_up, w_down):
    B, L, D = x.shape
    t = x.reshape(-1, D)
    glog = (t @ wg_r + bg_r).astype(jnp.float32)
    gprob = jax.nn.softmax(glog, axis=-1)
    _, gidx = lax.top_k(glog, 1)
    gw = jnp.take_along_axis(gprob, gidx, axis=-1)
    elog = (t @ we_r + be_r).astype(jnp.float32).reshape(-1, N_EXPERT_GROUPS, EXPERTS_PER_GROUP)
    elog_sel = jnp.take_along_axis(elog, gidx[:, :, None], axis=1)[:, 0]
    topv, topi = lax.top_k(elog_sel, TOP_K)
    ew = jax.nn.softmax(topv, axis=-1) * gw
    eidx = gidx * EXPERTS_PER_GROUP + topi
    comb = jnp.einsum('tk,tke->te', ew, jax.nn.one_hot(eidx, N_EXPERTS, dtype=jnp.float32))
    y = jnp.zeros(t.shape, jnp.float32)
    for e in range(N_EXPERTS):
        h = jax.nn.silu(t @ w_gate[e]) * (t @ w_up[e])
        y = y + comb[:, e:e + 1] * (h @ w_down[e]).astype(jnp.float32)
    return y.astype(x.dtype).reshape(B, L, D)


def trunk(x, start_pos, pool_hist, k_hist, v_hist, rel_bias, w_in, pool_w, pool_scale,
          proj_a, proj_b, w_out, attn_sinks, ln1_g, ln1_b, router_group_w, router_group_b,
          router_expert_w, router_expert_b, expert_w_gate, expert_w_up, expert_w_down,
          ln2_g, ln2_b):
    ks, vs, ps = [], [], []
    for l in range(DEPTH):
        if pool_hist is None:
            ph, kh, vh = None, None, None
        else:
            ph, kh, vh = pool_hist[l], k_hist[l], v_hist[l]
        mix, k_st, v_st, p_st = token_mixer(x, w_in[l], pool_w[l], pool_scale[l], proj_a[l],
                                            proj_b[l], w_out[l], attn_sinks[l], rel_bias,
                                            ph, kh, vh, start_pos)
        x = layer_norm(DEEPNORM_ALPHA * x + mix, ln1_g[l], ln1_b[l])
        ffn = hier_moe(x, router_group_w[l], router_group_b[l], router_expert_w[l],
                       router_expert_b[l], expert_w_gate[l], expert_w_up[l], expert_w_down[l])
        x = layer_norm(DEEPNORM_ALPHA * x + ffn, ln2_g[l], ln2_b[l])
        ks.append(k_st)
        vs.append(v_st)
        ps.append(p_st)
    return x, jnp.stack(ks), jnp.stack(vs), jnp.stack(ps)


def setup_inputs(seed: int = 0) -> dict:
    key = jax.random.key(seed)
    k = jax.random.split(key, 25)
    f32 = jnp.float32

    def nrm(kk, shape, scale):
        return jax.random.normal(kk, shape, f32) * scale

    return {
        'x_prompt': nrm(k[0], (BATCH, SEQ, D_MODEL), 1.0),
        'x_sample': nrm(k[1], (DEC_BATCH, DEC_SEQ, D_MODEL), 1.0),
        'cache_k': nrm(k[2], (DEPTH, DEC_BATCH, WINDOW, N_KV_HEADS, HEAD_DIM), 1.0),
        'cache_v': nrm(k[3], (DEPTH, DEC_BATCH, WINDOW, N_KV_HEADS, HEAD_DIM), 1.0),
        'state_pool': nrm(k[4], (DEPTH, DEC_BATCH, POOL_HIST, D_POOL), 1.0),
        'rel_bias': nrm(k[5], (REL_BUCKETS, N_HEADS), 0.5),
        'w_in': nrm(k[6], (DEPTH, D_MODEL, D_IN), D_MODEL ** -0.5),
        'pool_w': nrm(k[7], (DEPTH, N_POOL_GROUPS, POOL_GROUP_DIM, POOL_GROUP_DIM), POOL_GROUP_DIM ** -0.5),
        'pool_scale': 1.0 + nrm(k[8], (DEPTH, D_POOL), 0.1),
        'proj_a': nrm(k[9], (DEPTH, D_POOL, D_MODEL), DEEPNORM_BETA * D_POOL ** -0.5),
        'proj_b': nrm(k[10], (DEPTH, Q_DIM, D_MODEL), DEEPNORM_BETA * Q_DIM ** -0.5),
        'w_out': nrm(k[11], (DEPTH, D_MODEL, D_MODEL), DEEPNORM_BETA * D_MODEL ** -0.5),
        'attn_sinks': nrm(k[12], (DEPTH, N_HEADS), 1.0),
        'ln1_g': 1.0 + nrm(k[13], (DEPTH, D_MODEL), 0.02),
        'ln1_b': nrm(k[14], (DEPTH, D_MODEL), 0.02),
        'router_group_w': nrm(k[15], (DEPTH, D_MODEL, N_EXPERT_GROUPS), D_MODEL ** -0.5),
        'router_group_b': nrm(k[16], (DEPTH, N_EXPERT_GROUPS), 0.01),
        'router_expert_w': nrm(k[17], (DEPTH, D_MODEL, N_EXPERTS), D_MODEL ** -0.5),
        'router_expert_b': nrm(k[18], (DEPTH, N_EXPERTS), 0.01),
        'expert_w_gate': nrm(k[19], (DEPTH, N_EXPERTS, D_MODEL, D_EXPERT), D_MODEL ** -0.5),
        'expert_w_up': nrm(k[20], (DEPTH, N_EXPERTS, D_MODEL, D_EXPERT), D_MODEL ** -0.5),
        'expert_w_down': nrm(k[21], (DEPTH, N_EXPERTS, D_EXPERT, D_MODEL), DEEPNORM_BETA * D_EXPERT ** -0.5),
        'ln2_g': 1.0 + nrm(k[22], (DEPTH, D_MODEL), 0.02),
        'ln2_b': nrm(k[23], (DEPTH, D_MODEL), 0.02),
    }


def reference(x_prompt, x_sample, cache_k, cache_v, state_pool, rel_bias, w_in, pool_w,
              pool_scale, proj_a, proj_b, w_out, attn_sinks, ln1_g, ln1_b, router_group_w,
              router_group_b, router_expert_w, router_expert_b, expert_w_gate, expert_w_up,
              expert_w_down, ln2_g, ln2_b):
    y_prompt, prompt_k, prompt_v, prompt_pool = trunk(
        x_prompt, 0, None, None, None, rel_bias, w_in, pool_w, pool_scale, proj_a, proj_b,
        w_out, attn_sinks, ln1_g, ln1_b, router_group_w, router_group_b, router_expert_w,
        router_expert_b, expert_w_gate, expert_w_up, expert_w_down, ln2_g, ln2_b)
    y_sample, sample_k, sample_v, sample_pool = trunk(
        x_sample, PAST_LEN, state_pool, cache_k, cache_v, rel_bias, w_in, pool_w, pool_scale,
        proj_a, proj_b, w_out, attn_sinks, ln1_g, ln1_b, router_group_w, router_group_b,
        router_expert_w, router_expert_b, expert_w_gate, expert_w_up, expert_w_down, ln2_g, ln2_b)
    return (y_prompt, y_sample, prompt_k, prompt_v, prompt_pool, sample_k, sample_v, sample_pool)
```

```python
import functools
import math

import jax
import jax.numpy as jnp
from jax import lax
from jax.experimental import pallas as pl
from jax.experimental.pallas import tpu as pltpu

D_MODEL = 1024
DEPTH = 2
PAST_LEN = 16384
D_POOL = D_MODEL // 2
POOL_WINDOWS = (2, 4, 8, 16)
N_POOL_GROUPS = len(POOL_WINDOWS)
POOL_GROUP_DIM = D_POOL // N_POOL_GROUPS
POOL_HIST = max(POOL_WINDOWS) - 1
HEAD_DIM = 64
N_HEADS = (D_MODEL // 2) // HEAD_DIM
N_KV_HEADS = 2
GROUP = N_HEADS // N_KV_HEADS
Q_DIM = N_HEADS * HEAD_DIM
KV_DIM = N_KV_HEADS * HEAD_DIM
WINDOW = 128
ATTN_SCALE = HEAD_DIM ** -0.5
REL_BUCKETS = 32
REL_MAX_DISTANCE = 128
D_IN = D_POOL + Q_DIM + 2 * KV_DIM + 2 * D_MODEL
N_EXPERT_GROUPS = 4
EXPERTS_PER_GROUP = 8
N_EXPERTS = N_EXPERT_GROUPS * EXPERTS_PER_GROUP
D_EXPERT = D_MODEL // 4
DEEPNORM_ALPHA = (2 * DEPTH) ** 0.25
LN_EPS = 1e-5

LANES = 128
SUBLANES = 8
VMEM_LIMIT_BYTES = 56 * 1024 * 1024

_O_Q = D_POOL
_O_K = _O_Q + Q_DIM
_O_V = _O_K + KV_DIM
_O_GA = _O_V + KV_DIM
_O_GB = _O_GA + D_MODEL

ROW_CHUNKS = D_MODEL // LANES
KEY_PAD = 2 * WINDOW
EXPERT_TILE = 256
ROUTER_LANES = LANES
_GROUP_LANE0 = N_EXPERTS

BF16 = jnp.bfloat16
F32 = jnp.float32
NEG_INF = float("-inf")


def _cparams(*sem):
    return pltpu.CompilerParams(dimension_semantics=sem, vmem_limit_bytes=VMEM_LIMIT_BYTES)


def _resident(shape):
    nd = len(shape)
    return pl.BlockSpec(shape, lambda *_: (0,) * nd, pipeline_mode=pl.Buffered(1))


def _layer_norm(y, g, b):
    mu = jnp.mean(y, axis=-1, keepdims=True)
    yc = y - mu
    var = jnp.mean(yc * yc, axis=-1, keepdims=True)
    return yc * lax.rsqrt(var + LN_EPS) * g + b


def _softmax_pv(s, sink, v_bf16):
    m = jnp.maximum(jnp.max(s, axis=-1, keepdims=True), sink)
    p = jnp.exp(s - m)
    den = jnp.sum(p, axis=-1, keepdims=True) + jnp.exp(sink - m)
    o = jnp.dot(p.astype(BF16), v_bf16, preferred_element_type=F32)
    return o / den


def _half_lane_variants(t):
    lane = lax.broadcasted_iota(jnp.int32, t.shape, 1)
    lo = lane < HEAD_DIM
    tr = pltpu.roll(t, HEAD_DIM, 1)
    z = jnp.zeros_like(t)
    return ((jnp.where(lo, t, z), jnp.where(lo, z, tr)),
            (jnp.where(lo, tr, z), jnp.where(lo, z, t)))


def _mix_tail(x, a_in, b_in, ga, gb, pw, ps, pa, pb, wo, g, bt):
    a = jnp.dot(a_in.astype(BF16), pw, preferred_element_type=F32) * ps
    pa_o = jnp.dot(a.astype(BF16), pa, preferred_element_type=F32)
    pb_o = jnp.dot(b_in.astype(BF16), pb, preferred_element_type=F32)
    merged = jax.nn.sigmoid(ga) * pa_o + jax.nn.sigmoid(gb) * pb_o
    mix = jnp.dot(merged.astype(BF16), wo, preferred_element_type=F32)
    return _layer_norm(DEEPNORM_ALPHA * x + mix, g, bt)


def _prompt_mixer_kernel(x_ref, win_ref, pw_ref, ps_ref, pa_ref, pb_ref, wo_ref, bias_ref, sink_ref,
                         g_ref, bt_ref, x1_ref, kst_ref, vst_ref, pst_ref, kt_scr, v_scr, ext_scr, *, tm):
    i = pl.program_id(1)
    hist = 2 * SUBLANES

    @pl.when(i == 0)
    def _():
        kt_scr[:, :, :WINDOW] = jnp.zeros((4, LANES, WINDOW), BF16)
        v_scr[:, :WINDOW, :] = jnp.zeros((4, WINDOW, LANES), BF16)
        ext_scr[:hist, :] = jnp.zeros((hist, D_POOL), F32)

    @pl.when(i > 0)
    def _():
        kt_scr[:, :, :WINDOW] = kt_scr[:, :, tm:tm + WINDOW]
        v_scr[:, :WINDOW, :] = v_scr[:, tm:tm + WINDOW, :]
        ext_scr[:hist, :] = ext_scr[tm:tm + hist, :]

    x = x_ref[...]
    u = jnp.dot(x.astype(BF16), win_ref[...], preferred_element_type=F32)
    u_pool = u[:, :D_POOL]
    q = u[:, _O_Q:_O_K]
    k = u[:, _O_K:_O_V]
    v = u[:, _O_V:_O_GA]
    ga = u[:, _O_GA:_O_GB]
    gb = u[:, _O_GB:]

    kst_ref[...] = k[tm - WINDOW:, :]
    vst_ref[...] = v[tm - WINDOW:, :]
    pst_ref[...] = u_pool[tm - hist:, :]

    kt = k.T
    kz = jnp.zeros((HEAD_DIM, tm), F32)
    kt_scr[0, :, WINDOW:] = jnp.concatenate([kt[:HEAD_DIM], kz], axis=0).astype(BF16)
    kt_scr[1, :, WINDOW:] = jnp.concatenate([kz, kt[:HEAD_DIM]], axis=0).astype(BF16)
    kt_scr[2, :, WINDOW:] = jnp.concatenate([kt[HEAD_DIM:], kz], axis=0).astype(BF16)
    kt_scr[3, :, WINDOW:] = jnp.concatenate([kz, kt[HEAD_DIM:]], axis=0).astype(BF16)
    vv = _half_lane_variants(v)
    for kvh in range(N_KV_HEADS):
        for par in range(2):
            v_scr[2 * kvh + par, WINDOW:, :] = vv[kvh][par].astype(BF16)

    col = lax.broadcasted_iota(jnp.int32, (2 * WINDOW, 2 * WINDOW), 1)
    no_prev = col < jnp.where(i == 0, WINDOW, 0)
    qb = q.astype(BF16)
    b_rows = []
    for j in range(tm // WINDOW):
        r0 = j * WINDOW
        o_kvh = []
        for kvh in range(N_KV_HEADS):
            c0 = kvh * 2 * LANES
            q2 = jnp.concatenate([qb[r0:r0 + WINDOW, c0:c0 + LANES],
                                  qb[r0:r0 + WINDOW, c0 + LANES:c0 + 2 * LANES]], axis=0)
            acc = None
            for par in range(2):
                s = jnp.dot(q2, kt_scr[2 * kvh + par, :, r0:r0 + 2 * WINDOW], preferred_element_type=F32)
                s = s * ATTN_SCALE + bias_ref[kvh, par]
                if j == 0:
                    s = jnp.where(no_prev, NEG_INF, s)
                o = _softmax_pv(s, sink_ref[kvh, par], v_scr[2 * kvh + par, r0:r0 + 2 * WINDOW, :])
                acc = o if acc is None else acc + o
            o_kvh.append(acc)
        b_rows.append(jnp.concatenate([o_kvh[0][:WINDOW], o_kvh[0][WINDOW:],
                                       o_kvh[1][:WINDOW], o_kvh[1][WINDOW:]], axis=1))
    b = jnp.concatenate(b_rows, axis=0)

    ext_scr[hist:, :] = u_pool
    pos = i * tm + lax.broadcasted_iota(jnp.int32, (tm, 1), 0)
    means = []
    for gi, w in enumerate(POOL_WINDOWS):
        c0 = gi * POOL_GROUP_DIM
        acc = u_pool[:, c0:c0 + POOL_GROUP_DIM]
        for d in range(1, w):
            acc = acc + ext_scr[pl.ds(hist - d, tm), c0:c0 + POOL_GROUP_DIM]
        cnt = jnp.minimum(w, pos + 1).astype(F32)
        means.append(acc / cnt)
    pooled = jnp.concatenate(means, axis=1) - u_pool

    x1_ref[...] = _mix_tail(x, pooled, b, ga, gb, pw_ref[...], ps_ref[...], pa_ref[...], pb_ref[...],
                            wo_ref[...], g_ref[...], bt_ref[...])


def _prompt_mixer(x, lw, bias_p, sink_t, *, tm=512):
    bsz, seq, _ = x.shape
    hist = 2 * SUBLANES
    kern = functools.partial(_prompt_mixer_kernel, tm=tm)
    tile = lambda w: pl.BlockSpec((None, tm, w), lambda b, i: (b, i, 0))
    state = lambda r, w: pl.BlockSpec((None, r, w), lambda b, i: (b, 0, 0))
    return pl.pallas_call(
        kern,
        grid=(bsz, seq // tm),
        in_specs=[tile(D_MODEL), _resident((D_MODEL, D_IN)), _resident((D_POOL, D_POOL)),
                  _resident((1, D_POOL)), _resident((D_POOL, D_MODEL)), _resident((Q_DIM, D_MODEL)),
                  _resident((D_MODEL, D_MODEL)), _resident(bias_p.shape), _resident(sink_t.shape),
                  _resident((1, D_MODEL)), _resident((1, D_MODEL))],
        out_specs=[tile(D_MODEL), state(WINDOW, KV_DIM), state(WINDOW, KV_DIM), state(hist, D_POOL)],
        out_shape=[jax.ShapeDtypeStruct((bsz, seq, D_MODEL), F32),
                   jax.ShapeDtypeStruct((bsz, WINDOW, KV_DIM), F32),
                   jax.ShapeDtypeStruct((bsz, WINDOW, KV_DIM), F32),
                   jax.ShapeDtypeStruct((bsz, hist, D_POOL), F32)],
        scratch_shapes=[pltpu.VMEM((4, LANES, WINDOW + tm), BF16),
                        pltpu.VMEM((4, WINDOW + tm, LANES), BF16),
                        pltpu.VMEM((hist + tm, D_POOL), F32)],
        compiler_params=_cparams("arbitrary", "arbitrary"),
        name="prompt_mixer",
    )(x, lw["w_in"], lw["pool_w"], lw["pool_scale"], lw["proj_a"], lw["proj_b"], lw["w_out"],
      bias_p, sink_t, lw["ln1_g"], lw["ln1_b"])


def _in_proj_kernel(x_ref, win_ref, u_ref):
    u_ref[...] = jnp.dot(x_ref[...].astype(BF16), win_ref[...], preferred_element_type=F32)


def _in_proj(x, w_in):
    n = x.shape[0]
    return pl.pallas_call(
        _in_proj_kernel,
        out_shape=jax.ShapeDtypeStruct((n, D_IN), F32),
        compiler_params=pltpu.CompilerParams(vmem_limit_bytes=VMEM_LIMIT_BYTES),
        name="decode_in_proj",
    )(x, w_in)


def _decode_attn_kernel(q_ref, k_ref, v_ref, bias_ref, sink_ref, o_ref, *, g, rows):
    kk = _half_lane_variants(k_ref[...])
    vv = _half_lane_variants(v_ref[...])
    rblk = lax.broadcasted_iota(jnp.int32, (g * rows, g * KEY_PAD), 0) // rows
    cblk = lax.broadcasted_iota(jnp.int32, (g * rows, g * KEY_PAD), 1) // KEY_PAD
    diag = rblk == cblk
    for kvh in range(N_KV_HEADS):
        q2 = q_ref[kvh].astype(BF16)
        acc = None
        for par in range(2):
            full = lax.dot_general(q2, kk[kvh][par].astype(BF16), (((1,), (1,)), ((), ())),
                                   preferred_element_type=F32)
            s = jnp.concatenate([full[n * rows:(n + 1) * rows, n * KEY_PAD:(n + 1) * KEY_PAD]
                                 for n in range(g)], axis=0)
            s = s * ATTN_SCALE + bias_ref[kvh, par]
            sink = sink_ref[kvh, par]
            m = jnp.maximum(jnp.max(s, axis=-1, keepdims=True), sink)
            p = jnp.exp(s - m)
            den = jnp.sum(p, axis=-1, keepdims=True) + jnp.exp(sink - m)
            p_bd = jnp.where(diag, jnp.concatenate([p] * g, axis=1), 0.0).astype(BF16)
            o = jnp.dot(p_bd, vv[kvh][par].astype(BF16), preferred_element_type=F32) / den
            acc = o if acc is None else acc + o
        o_ref[kvh] = acc


def _decode_attn(q2, k_ext, v_ext, bias_d, sink_d, *, g=16):
    nseq = k_ext.shape[0] // KEY_PAD
    rows = q2.shape[1] // nseq
    kern = functools.partial(_decode_attn_kernel, g=g, rows=rows)
    return pl.pallas_call(
        kern,
        grid=(nseq // g,),
        in_specs=[pl.BlockSpec((N_KV_HEADS, g * rows, LANES), lambda i: (0, i, 0)),
                  pl.BlockSpec((g * KEY_PAD, LANES), lambda i: (i, 0)),
                  pl.BlockSpec((g * KEY_PAD, LANES), lambda i: (i, 0)),
                  _resident(bias_d.shape), _resident(sink_d.shape)],
        out_specs=pl.BlockSpec((N_KV_HEADS, g * rows, LANES), lambda i: (0, i, 0)),
        out_shape=jax.ShapeDtypeStruct(q2.shape, F32),
        compiler_params=_cparams("arbitrary"),
        name="decode_attn",
    )(q2, k_ext, v_ext, bias_d, sink_d)


def _decode_tail_kernel(x_ref, up_ref, hist_ref, b_ref, ga_ref, gb_ref, pw_ref, ps_ref, pa_ref, pb_ref,
                        wo_ref, g_ref, bt_ref, x1_ref, *, nseq, ntok, start_pos):
    def ext_row(r):
        return hist_ref[r] if r < POOL_HIST else up_ref[(r - POOL_HIST) * nseq:(r - POOL_HIST + 1) * nseq, :]

    pooled_rows = []
    for t in range(ntok):
        means = []
        for gi, w in enumerate(POOL_WINDOWS):
            c0 = gi * POOL_GROUP_DIM
            acc = None
            for d in range(w):
                r = POOL_HIST + t - d
                if start_pos - POOL_HIST + r < 0:
                    continue
                term = ext_row(r)[:, c0:c0 + POOL_GROUP_DIM]
                acc = term if acc is None else acc + term
            means.append(acc / float(min(w, start_pos + t + 1)))
        pooled_rows.append(jnp.concatenate(means, axis=1))
    up = up_ref[...]
    pooled = jnp.concatenate(pooled_rows, axis=0) - up
    x1_ref[...] = _mix_tail(x_ref[...], pooled, b_ref[...], ga_ref[...], gb_ref[...], pw_ref[...], ps_ref[...],
                            pa_ref[...], pb_ref[...], wo_ref[...], g_ref[...], bt_ref[...])


def _decode_tail(x, u_pool, hist_t, b, ga, gb, lw, *, nseq, ntok):
    kern = functools.partial(_decode_tail_kernel, nseq=nseq, ntok=ntok, start_pos=PAST_LEN)
    return pl.pallas_call(
        kern,
        out_shape=jax.ShapeDtypeStruct(x.shape, F32),
        compiler_params=pltpu.CompilerParams(vmem_limit_bytes=VMEM_LIMIT_BYTES),
        name="decode_tail",
    )(x, u_pool, hist_t, b, ga, gb, lw["pool_w"], lw["pool_scale"], lw["proj_a"], lw["proj_b"],
      lw["w_out"], lw["ln1_g"], lw["ln1_b"])


def _router_kernel(x_ref, whi_ref, wlo_ref, br_ref, meta_ref, cnt_ref, run_scr, *, tr):
    i = pl.program_id(0)

    @pl.when(i == 0)
    def _():
        run_scr[...] = jnp.zeros_like(run_scr)

    x = x_ref[...]
    xh = x.astype(BF16)
    xl = (x - xh.astype(F32)).astype(BF16)
    whi = whi_ref[...]
    logits = (jnp.dot(xh, whi, preferred_element_type=F32)
              + (jnp.dot(xh, wlo_ref[...], preferred_element_type=F32)
                 + jnp.dot(xl, whi, preferred_element_type=F32))) + br_ref[...]
    lane = lax.broadcasted_iota(jnp.int32, (tr, ROUTER_LANES), 1)
    big = jnp.int32(ROUTER_LANES)

    def masked_argmax(vals):
        vmax = jnp.max(vals, axis=-1, keepdims=True)
        idx = jnp.min(jnp.where(vals == vmax, lane, big), axis=-1, keepdims=True)
        return vmax, idx

    is_group = jnp.logical_and(lane >= _GROUP_LANE0, lane < _GROUP_LANE0 + N_EXPERT_GROUPS)
    glog = jnp.where(is_group, logits, NEG_INF)
    gmax, glane = masked_argmax(glog)
    gidx = glane - _GROUP_LANE0
    gw = 1.0 / jnp.sum(jnp.exp(glog - gmax), axis=-1, keepdims=True)
    in_group = jnp.logical_and(lane >= gidx * EXPERTS_PER_GROUP, lane < (gidx + 1) * EXPERTS_PER_GROUP)
    el = jnp.where(in_group, logits, NEG_INF)
    v1, e1 = masked_argmax(el)
    v2, e2 = masked_argmax(jnp.where(lane == e1, NEG_INF, el))
    r = jnp.exp(v2 - v1)
    w1 = gw / (1.0 + r)
    w2 = gw * r / (1.0 + r)

    is1 = lane == e1
    is2 = lane == e2
    oh = jnp.logical_or(is1, is2)
    rr = lax.broadcasted_iota(jnp.int32, (tr, tr), 0)
    cc = lax.broadcasted_iota(jnp.int32, (tr, tr), 1)
    lower = jnp.where(rr > cc, 1.0, 0.0).astype(BF16)
    before = jnp.dot(lower, jnp.where(oh, 1.0, 0.0).astype(BF16), preferred_element_type=F32) + run_scr[...]
    rank1 = jnp.sum(jnp.where(is1, before, 0.0), axis=-1, keepdims=True)
    rank2 = jnp.sum(jnp.where(is2, before, 0.0), axis=-1, keepdims=True)
    run_scr[...] += jnp.sum(jnp.where(oh, 1.0, 0.0), axis=0, keepdims=True)
    cnt_ref[...] = run_scr[...]

    cols = (e1.astype(F32), e2.astype(F32), w1, w2, rank1, rank2)
    meta = jnp.zeros((tr, ROUTER_LANES), F32)
    for c, val in enumerate(cols):
        meta = jnp.where(lane == c, val, meta)
    meta_ref[...] = meta


def _router(x, lw, *, tr=512):
    t = x.shape[0]
    kern = functools.partial(_router_kernel, tr=tr)
    return pl.pallas_call(
        kern,
        grid=(t // tr,),
        in_specs=[pl.BlockSpec((tr, D_MODEL), lambda i: (i, 0)),
                  _resident((D_MODEL, ROUTER_LANES)), _resident((D_MODEL, ROUTER_LANES)),
                  _resident((1, ROUTER_LANES))],
        out_specs=[pl.BlockSpec((tr, ROUTER_LANES), lambda i: (i, 0)),
                   pl.BlockSpec((1, ROUTER_LANES), lambda i: (0, 0))],
        out_shape=[jax.ShapeDtypeStruct((t, ROUTER_LANES), F32),
                   jax.ShapeDtypeStruct((1, ROUTER_LANES), F32)],
        scratch_shapes=[pltpu.VMEM((1, ROUTER_LANES), F32)],
        compiler_params=_cparams("arbitrary"),
        name="router",
    )(x, lw["router_hi"], lw["router_lo"], lw["router_b"])


def _rows_copy(src, dst, sem):
    return pltpu.make_async_copy(src, dst, sem)


def _dispatch_kernel(slot_ref, x_ref, xs_ref, stage, sems, *, td):
    i = pl.program_id(0)
    n = pl.num_programs(0)
    sb = lax.rem(i, 2)
    x = x_ref[...]
    for j in range(ROW_CHUNKS):
        stage[sb, pl.ds(j, td, stride=ROW_CHUNKS), :] = x[:, j * LANES:(j + 1) * LANES]

    def issue(r, carry):
        src = stage.at[sb, pl.ds(pl.multiple_of(r * ROW_CHUNKS, ROW_CHUNKS), ROW_CHUNKS)]
        for c in range(2):
            s = slot_ref[0, 0, 2 * r + c]
            dst = xs_ref.at[pl.ds(pl.multiple_of(s * ROW_CHUNKS, ROW_CHUNKS), ROW_CHUNKS)]
            _rows_copy(src, dst, sems.at[sb]).start()
        return carry

    lax.fori_loop(0, td, issue, 0)

    def drain(buf):
        for _ in range(2):
            _rows_copy(stage.at[buf], xs_ref.at[pl.ds(0, td * ROW_CHUNKS)], sems.at[buf]).wait()

    @pl.when(i > 0)
    def _():
        drain(1 - sb)

    @pl.when(i == n - 1)
    def _():
        drain(sb)


def _dispatch(x, slots, n_slots, *, td=256):
    t = x.shape[0]
    slots3 = slots.reshape(t // td, 1, 2 * td)
    kern = functools.partial(_dispatch_kernel, td=td)
    return pl.pallas_call(
        kern,
        grid=(t // td,),
        in_specs=[pl.BlockSpec((1, 1, 2 * td), lambda i: (i, 0, 0), memory_space=pltpu.SMEM),
                  pl.BlockSpec((td, D_MODEL), lambda i: (i, 0))],
        out_specs=pl.BlockSpec(memory_space=pl.ANY),
        out_shape=jax.ShapeDtypeStruct((n_slots * ROW_CHUNKS, LANES), F32),
        scratch_shapes=[pltpu.VMEM((2, td * ROW_CHUNKS, LANES), F32), pltpu.SemaphoreType.DMA((2,))],
        compiler_params=_cparams("arbitrary"),
        name="moe_dispatch",
    )(slots3, x)


def _expert_kernel(te_ref, nv_ref, xs_ref, wg_ref, wu_ref, wd_ref, os_ref):
    i = pl.program_id(0)

    @pl.when(i < nv_ref[0])
    def _():
        xt = jnp.concatenate([xs_ref[pl.ds(j, EXPERT_TILE, stride=ROW_CHUNKS), :] for j in range(ROW_CHUNKS)],
                             axis=1).astype(BF16)
        gate = jnp.dot(xt, wg_ref[...], preferred_element_type=F32)
        up = jnp.dot(xt, wu_ref[...], preferred_element_type=F32)
        h = (jax.nn.silu(gate) * up).astype(BF16)
        y = jnp.dot(h, wd_ref[...], preferred_element_type=F32)
        for j in range(ROW_CHUNKS):
            os_ref[pl.ds(j, EXPERT_TILE, stride=ROW_CHUNKS), :] = y[:, j * LANES:(j + 1) * LANES]


def _experts(xs, tile_expert, n_valid, lw):
    n_tiles = xs.shape[0] // (EXPERT_TILE * ROW_CHUNKS)
    rows = EXPERT_TILE * ROW_CHUNKS
    act = pl.BlockSpec((rows, LANES), lambda i, te, nv: (jnp.minimum(i, nv[0] - 1), 0))
    grid_spec = pltpu.PrefetchScalarGridSpec(
        num_scalar_prefetch=2,
        grid=(n_tiles,),
        in_specs=[act,
                  pl.BlockSpec((None, D_MODEL, D_EXPERT), lambda i, te, nv: (te[i], 0, 0)),
                  pl.BlockSpec((None, D_MODEL, D_EXPERT), lambda i, te, nv: (te[i], 0, 0)),
                  pl.BlockSpec((None, D_EXPERT, D_MODEL), lambda i, te, nv: (te[i], 0, 0))],
        out_specs=act,
    )
    return pl.pallas_call(
        _expert_kernel,
        grid_spec=grid_spec,
        out_shape=jax.ShapeDtypeStruct(xs.shape, F32),
        compiler_params=_cparams("arbitrary"),
        name="moe_experts",
    )(tile_expert, n_valid, xs, lw["w_gate"], lw["w_up"], lw["w_down"])


def _combine_kernel(slot_ref, x_ref, meta_ref, g_ref, bt_ref, os_ref, x2_ref, gbuf, sem, *, tc):
    def issue(r, carry):
        for c in range(2):
            s = slot_ref[0, 0, 2 * r + c]
            src = os_ref.at[pl.ds(pl.multiple_of(s * ROW_CHUNKS, ROW_CHUNKS), ROW_CHUNKS)]
            dst = gbuf.at[c, pl.ds(pl.multiple_of(r * ROW_CHUNKS, ROW_CHUNKS), ROW_CHUNKS)]
            _rows_copy(src, dst, sem.at[0]).start()
        return carry

    lax.fori_loop(0, tc, issue, 0)
    for c in range(2):
        _rows_copy(os_ref.at[pl.ds(0, tc * ROW_CHUNKS)], gbuf.at[c], sem.at[0]).wait()

    meta = meta_ref[...]
    y = None
    for c in range(2):
        rows = jnp.concatenate([gbuf[c, pl.ds(j, tc, stride=ROW_CHUNKS), :] for j in range(ROW_CHUNKS)], axis=1)
        term = meta[:, 2 + c:3 + c] * rows
        y = term if y is None else y + term
    x2_ref[...] = _layer_norm(DEEPNORM_ALPHA * x_ref[...] + y, g_ref[...], bt_ref[...])


def _combine(x, meta, slots, os_, lw, *, tc=256):
    t = x.shape[0]
    slots3 = slots.reshape(t // tc, 1, 2 * tc)
    kern = functools.partial(_combine_kernel, tc=tc)
    return pl.pallas_call(
        kern,
        grid=(t // tc,),
        in_specs=[pl.BlockSpec((1, 1, 2 * tc), lambda i: (i, 0, 0), memory_space=pltpu.SMEM),
                  pl.BlockSpec((tc, D_MODEL), lambda i: (i, 0)),
                  pl.BlockSpec((tc, ROUTER_LANES), lambda i: (i, 0)),
                  _resident((1, D_MODEL)), _resident((1, D_MODEL)),
                  pl.BlockSpec(memory_space=pl.ANY)],
        out_specs=pl.BlockSpec((tc, D_MODEL), lambda i: (i, 0)),
        out_shape=jax.ShapeDtypeStruct((t, D_MODEL), F32),
        scratch_shapes=[pltpu.VMEM((2, tc * ROW_CHUNKS, LANES), F32), pltpu.SemaphoreType.DMA((1,))],
        compiler_params=_cparams("arbitrary"),
        name="moe_combine",
    )(slots3, x, meta, lw["ln2_g"], lw["ln2_b"], os_)


def _moe(x, lw):
    t = x.shape[0]
    n_tiles = (2 * t) // EXPERT_TILE + N_EXPERTS
    meta, cnt = _router(x, lw)
    experts = meta[:, 0:2].astype(jnp.int32)
    ranks = meta[:, 4:6].astype(jnp.int32)
    counts = cnt[0, :N_EXPERTS].astype(jnp.int32)
    seg_tiles = (counts + EXPERT_TILE - 1) // EXPERT_TILE
    tile_ends = jnp.cumsum(seg_tiles)
    seg_start = (tile_ends - seg_tiles) * EXPERT_TILE
    slots = jnp.take(seg_start, experts) + ranks
    n_valid = tile_ends[-1:]
    tile_ids = jnp.minimum(jnp.arange(n_tiles, dtype=jnp.int32), n_valid[0] - 1)
    tile_expert = jnp.sum((tile_ends[None, :] <= tile_ids[:, None]).astype(jnp.int32), axis=1)
    tile_expert = jnp.minimum(tile_expert, N_EXPERTS - 1)
    xs = _dispatch(x, slots, n_tiles * EXPERT_TILE)
    os_ = _experts(xs, tile_expert, n_valid.astype(jnp.int32), lw)
    return _combine(x, meta, slots, os_, lw)


def _t5_bucket(dist):
    n = jnp.maximum(dist, 0)
    max_exact = REL_BUCKETS // 2
    nf = jnp.maximum(n, 1).astype(F32)
    large = max_exact + (jnp.log(nf / max_exact) / math.log(REL_MAX_DISTANCE / max_exact)
                         * (REL_BUCKETS - max_exact)).astype(jnp.int32)
    large = jnp.minimum(large, REL_BUCKETS - 1)
    return jnp.where(n < max_exact, n, large)


def _pair_parity_rows(per_head):
    out = []
    for kvh in range(N_KV_HEADS):
        out.append(jnp.stack([jnp.concatenate([per_head[GROUP * kvh + par], per_head[GROUP * kvh + 2 + par]], axis=0)
                              for par in range(2)]))
    return jnp.stack(out)


def _bias_tables(rel_bias, ntok, nseq_group):
    def table(dist, valid):
        bias = rel_bias[_t5_bucket(dist)]
        bias = jnp.where(valid[..., None], bias.astype(F32), NEG_INF)
        return jnp.moveaxis(bias, -1, 0)

    r = jnp.arange(WINDOW)[:, None]
    c = jnp.arange(2 * WINDOW)[None, :]
    dist = r + WINDOW - c
    bias_p = _pair_parity_rows(table(dist, (dist >= 0) & (dist <= WINDOW)))

    tq = jnp.arange(ntok)[:, None]
    cc = jnp.arange(KEY_PAD)[None, :]
    kpos = jnp.where(cc < WINDOW, cc - WINDOW, cc - WINDOW)
    dist_d = tq - kpos
    valid_d = (dist_d >= 0) & (dist_d <= WINDOW) & (cc < WINDOW + ntok) & (PAST_LEN + kpos >= 0)
    bias_d = _pair_parity_rows(table(dist_d, valid_d))
    bias_d = jnp.tile(bias_d, (1, 1, nseq_group, 1))
    return bias_p, bias_d


def _sink_tables(sinks, ntok, nseq_group):
    s = sinks.astype(F32)[:, None, None]
    sink_p = _pair_parity_rows(jnp.broadcast_to(s, (N_HEADS, WINDOW, 1)))
    sink_d = jnp.tile(_pair_parity_rows(jnp.broadcast_to(s, (N_HEADS, ntok, 1))), (1, 1, nseq_group, 1))
    return sink_p, sink_d


def _block_diag(pool_w):
    g, c, _ = pool_w.shape
    out = jnp.zeros((g * c, g * c), pool_w.dtype)
    for gi in range(g):
        out = out.at[gi * c:(gi + 1) * c, gi * c:(gi + 1) * c].set(pool_w[gi])
    return out


def _layer_weights(l, w_in, pool_w, pool_scale, proj_a, proj_b, w_out, ln1_g, ln1_b, router_group_w,
                   router_group_b, router_expert_w, router_expert_b, expert_w_gate, expert_w_up,
                   expert_w_down, ln2_g, ln2_b):
    rw = jnp.zeros((D_MODEL, ROUTER_LANES), F32)
    rw = rw.at[:, :N_EXPERTS].set(router_expert_w[l]).at[:, _GROUP_LANE0:_GROUP_LANE0 + N_EXPERT_GROUPS].set(
        router_group_w[l])
    rb = jnp.zeros((1, ROUTER_LANES), F32)
    rb = rb.at[0, :N_EXPERTS].set(router_expert_b[l]).at[0, _GROUP_LANE0:_GROUP_LANE0 + N_EXPERT_GROUPS].set(
        router_group_b[l])
    r_hi = rw.astype(BF16)
    return dict(
        w_in=w_in[l].astype(BF16), pool_w=_block_diag(pool_w[l]).astype(BF16),
        pool_scale=pool_scale[l].reshape(1, D_POOL), proj_a=proj_a[l].astype(BF16),
        proj_b=proj_b[l].astype(BF16), w_out=w_out[l].astype(BF16),
        ln1_g=ln1_g[l].reshape(1, D_MODEL), ln1_b=ln1_b[l].reshape(1, D_MODEL),
        router_hi=r_hi, router_lo=(rw - r_hi.astype(F32)).astype(BF16), router_b=rb,
        w_gate=expert_w_gate[l].astype(BF16), w_up=expert_w_up[l].astype(BF16),
        w_down=expert_w_down[l].astype(BF16),
        ln2_g=ln2_g[l].reshape(1, D_MODEL), ln2_b=ln2_b[l].reshape(1, D_MODEL))


def _decode_mixer(xs_tn, lw, cache_k, cache_v, pool_hist, bias_d, sink_d, *, nseq, ntok, g):
    u = _in_proj(xs_tn, lw["w_in"])
    u_pool = u[:, :D_POOL]
    k_new = u[:, _O_K:_O_V].reshape(ntok, nseq, KV_DIM)
    v_new = u[:, _O_V:_O_GA].reshape(ntok, nseq, KV_DIM)
    q = u[:, _O_Q:_O_K].reshape(ntok, nseq, N_KV_HEADS, 2, LANES)
    q2 = jnp.transpose(q, (2, 1, 3, 0, 4)).reshape(N_KV_HEADS, nseq * 2 * ntok, LANES)
    pad = jnp.zeros((nseq, KEY_PAD - WINDOW - ntok, KV_DIM), F32)
    k_ext = jnp.concatenate([cache_k, jnp.swapaxes(k_new, 0, 1), pad], axis=1).reshape(nseq * KEY_PAD, KV_DIM)
    v_ext = jnp.concatenate([cache_v, jnp.swapaxes(v_new, 0, 1), pad], axis=1).reshape(nseq * KEY_PAD, KV_DIM)
    o = _decode_attn(q2, k_ext, v_ext, bias_d, sink_d, g=g)
    b = jnp.transpose(o.reshape(N_KV_HEADS, nseq, 2, ntok, LANES), (3, 1, 0, 2, 4)).reshape(ntok * nseq, Q_DIM)
    hist_t = jnp.swapaxes(pool_hist, 0, 1)
    x1 = _decode_tail(xs_tn, u_pool, hist_t, b, u[:, _O_GA:_O_GB], u[:, _O_GB:], lw, nseq=nseq, ntok=ntok)
    return x1, k_new, v_new, u_pool.reshape(ntok, nseq, D_POOL)


def kernel(x_prompt, x_sample, cache_k, cache_v, state_pool, rel_bias, w_in, pool_w, pool_scale, proj_a, proj_b, w_out, attn_sinks, ln1_g, ln1_b, router_group_w, router_group_b, router_expert_w, router_expert_b, expert_w_gate, expert_w_up, expert_w_down, ln2_g, ln2_b):
    bsz, seq, _ = x_prompt.shape
    nseq, ntok, _ = x_sample.shape
    g = 16
    bias_p, bias_d = _bias_tables(rel_bias, ntok, g)
    n_prompt = bsz * seq

    xp = x_prompt
    xs_tn = jnp.swapaxes(x_sample, 0, 1).reshape(ntok * nseq, D_MODEL)
    ck = cache_k.reshape(DEPTH, nseq, WINDOW, KV_DIM)
    cv = cache_v.reshape(DEPTH, nseq, WINDOW, KV_DIM)
    outs = {name: [] for name in ("pk", "pv", "pp", "sk", "sv", "sp")}
    for l in range(DEPTH):
        lw = _layer_weights(l, w_in, pool_w, pool_scale, proj_a, proj_b, w_out, ln1_g, ln1_b, router_group_w,
                            router_group_b, router_expert_w, router_expert_b, expert_w_gate, expert_w_up,
                            expert_w_down, ln2_g, ln2_b)
        sink_p, sink_d = _sink_tables(attn_sinks[l], ntok, g)
        x1p, kst, vst, pst = _prompt_mixer(xp, lw, bias_p, sink_p)
        x1s, k_new, v_new, p_new = _decode_mixer(xs_tn, lw, ck[l], cv[l], state_pool[l], bias_d, sink_d,
                                                 nseq=nseq, ntok=ntok, g=g)
        x2 = _moe(jnp.concatenate([x1p.reshape(n_prompt, D_MODEL), x1s], axis=0), lw)
        xp = x2[:n_prompt].reshape(bsz, seq, D_MODEL)
        xs_tn = x2[n_prompt:]
        outs["pk"].append(kst.reshape(bsz, WINDOW, N_KV_HEADS, HEAD_DIM))
        outs["pv"].append(vst.reshape(bsz, WINDOW, N_KV_HEADS, HEAD_DIM))
        outs["pp"].append(pst[:, -POOL_HIST:])
        outs["sk"].append(jnp.concatenate([ck[l][:, ntok:], jnp.swapaxes(k_new, 0, 1)], axis=1).reshape(
            nseq, WINDOW, N_KV_HEADS, HEAD_DIM))
        outs["sv"].append(jnp.concatenate([cv[l][:, ntok:], jnp.swapaxes(v_new, 0, 1)], axis=1).reshape(
            nseq, WINDOW, N_KV_HEADS, HEAD_DIM))
        outs["sp"].append(jnp.concatenate([state_pool[l][:, ntok:], jnp.swapaxes(p_new, 0, 1)], axis=1))
    y_sample = jnp.swapaxes(xs_tn.reshape(ntok, nseq, D_MODEL), 0, 1)
    return (xp, y_sample, jnp.stack(outs["pk"]), jnp.stack(outs["pv"]), jnp.stack(outs["pp"]),
            jnp.stack(outs["sk"]), jnp.stack(outs["sv"]), jnp.stack(outs["sp"]))
```

```python
import functools
import math

import jax
import jax.numpy as jnp
from jax import lax
from jax.experimental import pallas as pl
from jax.experimental.pallas import tpu as pltpu

D_MODEL = 1024
DEPTH = 2
PAST_LEN = 16384
D_POOL = D_MODEL // 2
POOL_WINDOWS = (2, 4, 8, 16)
N_POOL_GROUPS = len(POOL_WINDOWS)
POOL_GROUP_DIM = D_POOL // N_POOL_GROUPS
POOL_HIST = max(POOL_WINDOWS) - 1
HEAD_DIM = 64
N_HEADS = (D_MODEL // 2) // HEAD_DIM
N_KV_HEADS = 2
GROUP = N_HEADS // N_KV_HEADS
Q_DIM = N_HEADS * HEAD_DIM
KV_DIM = N_KV_HEADS * HEAD_DIM
WINDOW = 128
ATTN_SCALE = HEAD_DIM ** -0.5
REL_BUCKETS = 32
REL_MAX_DISTANCE = 128
D_IN = D_POOL + Q_DIM + 2 * KV_DIM + 2 * D_MODEL
N_EXPERT_GROUPS = 4
EXPERTS_PER_GROUP = 8
N_EXPERTS = N_EXPERT_GROUPS * EXPERTS_PER_GROUP
D_EXPERT = D_MODEL // 4
DEEPNORM_ALPHA = (2 * DEPTH) ** 0.25
LN_EPS = 1e-5

LANES = 128
SUBLANES = 8
VMEM_LIMIT_BYTES = 56 * 1024 * 1024

_O_Q = D_POOL
_O_K = _O_Q + Q_DIM
_O_V = _O_K + KV_DIM
_O_GA = _O_V + KV_DIM
_O_GB = _O_GA + D_MODEL

ROW_CHUNKS = D_MODEL // LANES
KEY_PAD = 2 * WINDOW
EXPERT_TILE = 256
ROUTER_LANES = LANES
_GROUP_LANE0 = N_EXPERTS

BF16 = jnp.bfloat16
F32 = jnp.float32
NEG_INF = float("-inf")


def _cparams(*sem):
    return pltpu.CompilerParams(dimension_semantics=sem, vmem_limit_bytes=VMEM_LIMIT_BYTES)


def _resident(shape):
    nd = len(shape)
    return pl.BlockSpec(shape, lambda *_: (0,) * nd, pipeline_mode=pl.Buffered(1))


def _layer_norm(y, g, b):
    mu = jnp.mean(y, axis=-1, keepdims=True)
    yc = y - mu
    var = jnp.mean(yc * yc, axis=-1, keepdims=True)
    return yc * lax.rsqrt(var + LN_EPS) * g + b


def _softmax_pv(s, sink, v_bf16):
    m = jnp.maximum(jnp.max(s, axis=-1, keepdims=True), sink)
    p = jnp.exp(s - m)
    den = jnp.sum(p, axis=-1, keepdims=True) + jnp.exp(sink - m)
    o = jnp.dot(p.astype(BF16), v_bf16, preferred_element_type=F32)
    return o / den


def _half_lane_variants(t):
    lane = lax.broadcasted_iota(jnp.int32, t.shape, 1)
    lo = lane < HEAD_DIM
    tr = pltpu.roll(t, HEAD_DIM, 1)
    z = jnp.zeros_like(t)
    return ((jnp.where(lo, t, z), jnp.where(lo, z, tr)),
            (jnp.where(lo, tr, z), jnp.where(lo, z, t)))


def _mix_tail(x, a_in, b_in, ga, gb, pw, ps, pa, pb, wo, g, bt):
    a = jnp.dot(a_in.astype(BF16), pw, preferred_element_type=F32) * ps
    pa_o = jnp.dot(a.astype(BF16), pa, preferred_element_type=F32)
    pb_o = jnp.dot(b_in.astype(BF16), pb, preferred_element_type=F32)
    merged = jax.nn.sigmoid(ga) * pa_o + jax.nn.sigmoid(gb) * pb_o
    mix = jnp.dot(merged.astype(BF16), wo, preferred_element_type=F32)
    return _layer_norm(DEEPNORM_ALPHA * x + mix, g, bt)


def _prompt_mixer_kernel(x_ref, win_ref, pw_ref, ps_ref, pa_ref, pb_ref, wo_ref, bias_ref, sink_ref,
                         g_ref, bt_ref, x1_ref, kst_ref, vst_ref, pst_ref, kt_scr, v_scr, ext_scr, *, tm):
    i = pl.program_id(1)
    hist = 2 * SUBLANES

    @pl.when(i == 0)
    def _():
        kt_scr[:, :, :WINDOW] = jnp.zeros((4, LANES, WINDOW), BF16)
        v_scr[:, :WINDOW, :] = jnp.zeros((4, WINDOW, LANES), BF16)
        ext_scr[:hist, :] = jnp.zeros((hist, D_POOL), F32)

    @pl.when(i > 0)
    def _():
        kt_scr[:, :, :WINDOW] = kt_scr[:, :, tm:tm + WINDOW]
        v_scr[:, :WINDOW, :] = v_scr[:, tm:tm + WINDOW, :]
        ext_scr[:hist, :] = ext_scr[tm:tm + hist, :]

    x = x_ref[...]
    u = jnp.dot(x.astype(BF16), win_ref[...], preferred_element_type=F32)
    u_pool = u[:, :D_POOL]
    q = u[:, _O_Q:_O_K]
    k = u[:, _O_K:_O_V]
    v = u[:, _O_V:_O_GA]
    ga = u[:, _O_GA:_O_GB]
    gb = u[:, _O_GB:]

    kst_ref[...] = k[tm - WINDOW:, :]
    vst_ref[...] = v[tm - WINDOW:, :]
    pst_ref[...] = u_pool[tm - hist:, :]

    kt = k.T
    kz = jnp.zeros((HEAD_DIM, tm), F32)
    kt_scr[0, :, WINDOW:] = jnp.concatenate([kt[:HEAD_DIM], kz], axis=0).astype(BF16)
    kt_scr[1, :, WINDOW:] = jnp.concatenate([kz, kt[:HEAD_DIM]], axis=0).astype(BF16)
    kt_scr[2, :, WINDOW:] = jnp.concatenate([kt[HEAD_DIM:], kz], axis=0).astype(BF16)
    kt_scr[3, :, WINDOW:] = jnp.concatenate([kz, kt[HEAD_DIM:]], axis=0).astype(BF16)
    vv = _half_lane_variants(v)
    for kvh in range(N_KV_HEADS):
        for par in range(2):
            v_scr[2 * kvh + par, WINDOW:, :] = vv[kvh][par].astype(BF16)

    col = lax.broadcasted_iota(jnp.int32, (2 * WINDOW, 2 * WINDOW), 1)
    no_prev = col < jnp.where(i == 0, WINDOW, 0)
    qb = q.astype(BF16)
    b_rows = []
    for j in range(tm // WINDOW):
        r0 = j * WINDOW
        o_kvh = []
        for kvh in range(N_KV_HEADS):
            c0 = kvh * 2 * LANES
            q2 = jnp.concatenate([qb[r0:r0 + WINDOW, c0:c0 + LANES],
                                  qb[r0:r0 + WINDOW, c0 + LANES:c0 + 2 * LANES]], axis=0)
            acc = None
            for par in range(2):
                s = jnp.dot(q2, kt_scr[2 * kvh + par, :, r0:r0 + 2 * WINDOW], preferred_element_type=F32)
                s = s * ATTN_SCALE + bias_ref[kvh, par]
                if j == 0:
                    s = jnp.where(no_prev, NEG_INF, s)
                o = _softmax_pv(s, sink_ref[kvh, par], v_scr[2 * kvh + par, r0:r0 + 2 * WINDOW, :])
                acc = o if acc is None else acc + o
            o_kvh.append(acc)
        b_rows.append(jnp.concatenate([o_kvh[0][:WINDOW], o_kvh[0][WINDOW:],
                                       o_kvh[1][:WINDOW], o_kvh[1][WINDOW:]], axis=1))
    b = jnp.concatenate(b_rows, axis=0)

    ext_scr[hist:, :] = u_pool
    pos = i * tm + lax.broadcasted_iota(jnp.int32, (tm, 1), 0)
    means = []
    for gi, w in enumerate(POOL_WINDOWS):
        c0 = gi * POOL_GROUP_DIM
        acc = u_pool[:, c0:c0 + POOL_GROUP_DIM]
        for d in range(1, w):
            acc = acc + ext_scr[pl.ds(hist - d, tm), c0:c0 + POOL_GROUP_DIM]
        cnt = jnp.minimum(w, pos + 1).astype(F32)
        means.append(acc / cnt)
    pooled = jnp.concatenate(means, axis=1) - u_pool

    x1_ref[...] = _mix_tail(x, pooled, b, ga, gb, pw_ref[...], ps_ref[...], pa_ref[...], pb_ref[...],
                            wo_ref[...], g_ref[...], bt_ref[...])


def _prompt_mixer(x, lw, bias_p, sink_t, *, bsz, n_total, tm=512):
    seq = x.shape[0] // bsz
    nt = seq // tm
    hist = 2 * SUBLANES
    kern = functools.partial(_prompt_mixer_kernel, tm=tm)
    tile = pl.BlockSpec((tm, D_MODEL), lambda b, i: (b * nt + i, 0))
    state = lambda r, w: pl.BlockSpec((None, r, w), lambda b, i: (b, 0, 0))
    return pl.pallas_call(
        kern,
        grid=(bsz, nt),
        in_specs=[tile, _resident((D_MODEL, D_IN)), _resident((D_POOL, D_POOL)),
                  _resident((1, D_POOL)), _resident((D_POOL, D_MODEL)), _resident((Q_DIM, D_MODEL)),
                  _resident((D_MODEL, D_MODEL)), _resident(bias_p.shape), _resident(sink_t.shape),
                  _resident((1, D_MODEL)), _resident((1, D_MODEL))],
        out_specs=[tile, state(WINDOW, KV_DIM), state(WINDOW, KV_DIM), state(hist, D_POOL)],
        out_shape=[jax.ShapeDtypeStruct((n_total, D_MODEL), F32),
                   jax.ShapeDtypeStruct((bsz, WINDOW, KV_DIM), F32),
                   jax.ShapeDtypeStruct((bsz, WINDOW, KV_DIM), F32),
                   jax.ShapeDtypeStruct((bsz, hist, D_POOL), F32)],
        scratch_shapes=[pltpu.VMEM((4, LANES, WINDOW + tm), BF16),
                        pltpu.VMEM((4, WINDOW + tm, LANES), BF16),
                        pltpu.VMEM((hist + tm, D_POOL), F32)],
        compiler_params=_cparams("arbitrary", "arbitrary"),
        name="prompt_mixer",
    )(x, lw["w_in"], lw["pool_w"], lw["pool_scale"], lw["proj_a"], lw["proj_b"], lw["w_out"],
      bias_p, sink_t, lw["ln1_g"], lw["ln1_b"])


def _in_proj_kernel(x_ref, win_ref, u_ref):
    u_ref[...] = jnp.dot(x_ref[...].astype(BF16), win_ref[...], preferred_element_type=F32)


def _in_proj(x, w_in):
    n = x.shape[0]
    return pl.pallas_call(
        _in_proj_kernel,
        out_shape=jax.ShapeDtypeStruct((n, D_IN), F32),
        compiler_params=pltpu.CompilerParams(vmem_limit_bytes=VMEM_LIMIT_BYTES),
        name="decode_in_proj",
    )(x, w_in)


def _decode_attn_kernel(q_ref, k_ref, v_ref, bias_ref, sink_ref, o_ref, *, g, rows):
    kk = _half_lane_variants(k_ref[...])
    vv = _half_lane_variants(v_ref[...])
    rblk = lax.broadcasted_iota(jnp.int32, (g * rows, g * KEY_PAD), 0) // rows
    cblk = lax.broadcasted_iota(jnp.int32, (g * rows, g * KEY_PAD), 1) // KEY_PAD
    diag = rblk == cblk
    for kvh in range(N_KV_HEADS):
        q2 = q_ref[kvh].astype(BF16)
        acc = None
        for par in range(2):
            full = lax.dot_general(q2, kk[kvh][par].astype(BF16), (((1,), (1,)), ((), ())),
                                   preferred_element_type=F32)
            s = jnp.concatenate([full[n * rows:(n + 1) * rows, n * KEY_PAD:(n + 1) * KEY_PAD]
                                 for n in range(g)], axis=0)
            s = s * ATTN_SCALE + bias_ref[kvh, par]
            sink = sink_ref[kvh, par]
            m = jnp.maximum(jnp.max(s, axis=-1, keepdims=True), sink)
            p = jnp.exp(s - m)
            den = jnp.sum(p, axis=-1, keepdims=True) + jnp.exp(sink - m)
            p_bd = jnp.where(diag, jnp.concatenate([p] * g, axis=1), 0.0).astype(BF16)
            o = jnp.dot(p_bd, vv[kvh][par].astype(BF16), preferred_element_type=F32) / den
            acc = o if acc is None else acc + o
        o_ref[kvh] = acc


def _decode_attn(q2, k_ext, v_ext, bias_d, sink_d, *, g=16):
    nseq = k_ext.shape[0] // KEY_PAD
    rows = q2.shape[1] // nseq
    kern = functools.partial(_decode_attn_kernel, g=g, rows=rows)
    return pl.pallas_call(
        kern,
        grid=(nseq // g,),
        in_specs=[pl.BlockSpec((N_KV_HEADS, g * rows, LANES), lambda i: (0, i, 0)),
                  pl.BlockSpec((g * KEY_PAD, LANES), lambda i: (i, 0)),
                  pl.BlockSpec((g * KEY_PAD, LANES), lambda i: (i, 0)),
                  _resident(bias_d.shape), _resident(sink_d.shape)],
        out_specs=pl.BlockSpec((N_KV_HEADS, g * rows, LANES), lambda i: (0, i, 0)),
        out_shape=jax.ShapeDtypeStruct(q2.shape, F32),
        compiler_params=_cparams("arbitrary"),
        name="decode_attn",
    )(q2, k_ext, v_ext, bias_d, sink_d)


def _decode_tail_kernel(x_ref, u_ref, hist_ref, b_ref, pw_ref, ps_ref, pa_ref, pb_ref,
                        wo_ref, g_ref, bt_ref, buf_ref, x1_ref, *, nseq, ntok, start_pos):
    del buf_ref

    def ext_row(r, c0):
        if r < POOL_HIST:
            return hist_ref[r, :, c0:c0 + POOL_GROUP_DIM]
        return u_ref[(r - POOL_HIST) * nseq:(r - POOL_HIST + 1) * nseq, c0:c0 + POOL_GROUP_DIM]

    pooled_rows = []
    for t in range(ntok):
        means = []
        for gi, w in enumerate(POOL_WINDOWS):
            acc = None
            for d in range(w):
                r = POOL_HIST + t - d
                if start_pos - POOL_HIST + r < 0:
                    continue
                term = ext_row(r, gi * POOL_GROUP_DIM)
                acc = term if acc is None else acc + term
            means.append(acc / float(min(w, start_pos + t + 1)))
        pooled_rows.append(jnp.concatenate(means, axis=1))
    pooled = jnp.concatenate(pooled_rows, axis=0) - u_ref[:, :D_POOL]
    x1_ref[...] = _mix_tail(x_ref[...], pooled, b_ref[...], u_ref[:, _O_GA:_O_GB], u_ref[:, _O_GB:], pw_ref[...],
                            ps_ref[...], pa_ref[...], pb_ref[...], wo_ref[...], g_ref[...], bt_ref[...])


def _decode_tail(x, u, hist_t, b, lw, buf, *, nseq, ntok, row_block):
    n = ntok * nseq
    kern = functools.partial(_decode_tail_kernel, nseq=nseq, ntok=ntok, start_pos=PAST_LEN)
    whole = lambda a: pl.BlockSpec(a.shape, lambda i: (0,) * a.ndim)
    ins = (x, u, hist_t, b, lw["pool_w"], lw["pool_scale"], lw["proj_a"], lw["proj_b"], lw["w_out"],
           lw["ln1_g"], lw["ln1_b"])
    return pl.pallas_call(
        kern,
        grid=(1,),
        in_specs=[whole(a) for a in ins] + [pl.BlockSpec(memory_space=pl.ANY)],
        out_specs=pl.BlockSpec((n, D_MODEL), lambda i: (row_block, 0)),
        out_shape=jax.ShapeDtypeStruct(buf.shape, F32),
        input_output_aliases={len(ins): 0},
        compiler_params=_cparams("arbitrary"),
        name="decode_tail",
    )(*ins, buf)


def _router_kernel(x_ref, whi_ref, wlo_ref, br_ref, meta_ref, cnt_ref, run_scr, *, tr):
    i = pl.program_id(0)

    @pl.when(i == 0)
    def _():
        run_scr[...] = jnp.zeros_like(run_scr)

    x = x_ref[...]
    xh = x.astype(BF16)
    xl = (x - xh.astype(F32)).astype(BF16)
    whi = whi_ref[...]
    logits = (jnp.dot(xh, whi, preferred_element_type=F32)
              + (jnp.dot(xh, wlo_ref[...], preferred_element_type=F32)
                 + jnp.dot(xl, whi, preferred_element_type=F32))) + br_ref[...]
    lane = lax.broadcasted_iota(jnp.int32, (tr, ROUTER_LANES), 1)
    big = jnp.int32(ROUTER_LANES)

    def masked_argmax(vals):
        vmax = jnp.max(vals, axis=-1, keepdims=True)
        idx = jnp.min(jnp.where(vals == vmax, lane, big), axis=-1, keepdims=True)
        return vmax, idx

    is_group = jnp.logical_and(lane >= _GROUP_LANE0, lane < _GROUP_LANE0 + N_EXPERT_GROUPS)
    glog = jnp.where(is_group, logits, NEG_INF)
    gmax, glane = masked_argmax(glog)
    gidx = glane - _GROUP_LANE0
    gw = 1.0 / jnp.sum(jnp.exp(glog - gmax), axis=-1, keepdims=True)
    in_group = jnp.logical_and(lane >= gidx * EXPERTS_PER_GROUP, lane < (gidx + 1) * EXPERTS_PER_GROUP)
    el = jnp.where(in_group, logits, NEG_INF)
    v1, e1 = masked_argmax(el)
    v2, e2 = masked_argmax(jnp.where(lane == e1, NEG_INF, el))
    r = jnp.exp(v2 - v1)
    w1 = gw / (1.0 + r)
    w2 = gw * r / (1.0 + r)

    is1 = lane == e1
    is2 = lane == e2
    oh = jnp.logical_or(is1, is2)
    rr = lax.broadcasted_iota(jnp.int32, (tr, tr), 0)
    cc = lax.broadcasted_iota(jnp.int32, (tr, tr), 1)
    lower = jnp.where(rr > cc, 1.0, 0.0).astype(BF16)
    before = jnp.dot(lower, jnp.where(oh, 1.0, 0.0).astype(BF16), preferred_element_type=F32) + run_scr[...]
    rank1 = jnp.sum(jnp.where(is1, before, 0.0), axis=-1, keepdims=True)
    rank2 = jnp.sum(jnp.where(is2, before, 0.0), axis=-1, keepdims=True)
    run_scr[...] += jnp.sum(jnp.where(oh, 1.0, 0.0), axis=0, keepdims=True)
    cnt_ref[...] = run_scr[...]

    cols = (e1.astype(F32), e2.astype(F32), w1, w2, rank1, rank2)
    meta = jnp.zeros((tr, ROUTER_LANES), F32)
    for c, val in enumerate(cols):
        meta = jnp.where(lane == c, val, meta)
    meta_ref[...] = meta


def _router(x, lw, *, tr=512):
    t = x.shape[0]
    kern = functools.partial(_router_kernel, tr=tr)
    return pl.pallas_call(
        kern,
        grid=(t // tr,),
        in_specs=[pl.BlockSpec((tr, D_MODEL), lambda i: (i, 0)),
                  _resident((D_MODEL, ROUTER_LANES)), _resident((D_MODEL, ROUTER_LANES)),
                  _resident((1, ROUTER_LANES))],
        out_specs=[pl.BlockSpec((tr, ROUTER_LANES), lambda i: (i, 0)),
                   pl.BlockSpec((1, ROUTER_LANES), lambda i: (0, 0))],
        out_shape=[jax.ShapeDtypeStruct((t, ROUTER_LANES), F32),
                   jax.ShapeDtypeStruct((1, ROUTER_LANES), F32)],
        scratch_shapes=[pltpu.VMEM((1, ROUTER_LANES), F32)],
        compiler_params=_cparams("arbitrary"),
        name="router",
    )(x, lw["router_hi"], lw["router_lo"], lw["router_b"])


def _rows_copy(src, dst, sem):
    return pltpu.make_async_copy(src, dst, sem)


def _dispatch_kernel(slot_ref, x_ref, xs_ref, stage, sems, *, td):
    i = pl.program_id(0)
    n = pl.num_programs(0)
    sb = lax.rem(i, 2)
    x = x_ref[...]
    for j in range(ROW_CHUNKS):
        stage[sb, pl.ds(j, td, stride=ROW_CHUNKS), :] = x[:, j * LANES:(j + 1) * LANES]

    def issue(r, carry):
        src = stage.at[sb, pl.ds(pl.multiple_of(r * ROW_CHUNKS, ROW_CHUNKS), ROW_CHUNKS)]
        for c in range(2):
            s = slot_ref[0, 0, 2 * r + c]
            dst = xs_ref.at[pl.ds(pl.multiple_of(s * ROW_CHUNKS, ROW_CHUNKS), ROW_CHUNKS)]
            _rows_copy(src, dst, sems.at[sb]).start(priority=c)
        return carry

    lax.fori_loop(0, td, issue, 0)

    def drain(buf):
        for _ in range(2):
            _rows_copy(stage.at[buf], xs_ref.at[pl.ds(0, td * ROW_CHUNKS)], sems.at[buf]).wait()

    @pl.when(i > 0)
    def _():
        drain(1 - sb)

    @pl.when(i == n - 1)
    def _():
        drain(sb)


def _dispatch(x, slots, n_slots, *, td=256):
    t = x.shape[0]
    slots3 = slots.reshape(t // td, 1, 2 * td)
    kern = functools.partial(_dispatch_kernel, td=td)
    return pl.pallas_call(
        kern,
        grid=(t // td,),
        in_specs=[pl.BlockSpec((1, 1, 2 * td), lambda i: (i, 0, 0), memory_space=pltpu.SMEM),
                  pl.BlockSpec((td, D_MODEL), lambda i: (i, 0))],
        out_specs=pl.BlockSpec(memory_space=pl.ANY),
        out_shape=jax.ShapeDtypeStruct((n_slots * ROW_CHUNKS, LANES), F32),
        scratch_shapes=[pltpu.VMEM((2, td * ROW_CHUNKS, LANES), F32), pltpu.SemaphoreType.DMA((2,))],
        compiler_params=_cparams("arbitrary"),
        name="moe_dispatch",
    )(slots3, x)


def _expert_kernel(te_ref, nv_ref, xs_ref, wg_ref, wu_ref, wd_ref, os_ref, wg_bf, wu_bf, wd_bf):
    i = pl.program_id(0)
    valid = i < nv_ref[0]
    new_expert = jnp.logical_or(i == 0, te_ref[i] != te_ref[jnp.maximum(i - 1, 0)])

    @pl.when(jnp.logical_and(valid, new_expert))
    def _():
        wg_bf[...] = wg_ref[...].astype(BF16)
        wu_bf[...] = wu_ref[...].astype(BF16)
        wd_bf[...] = wd_ref[...].astype(BF16)

    @pl.when(valid)
    def _():
        xt = jnp.concatenate([xs_ref[pl.ds(j, EXPERT_TILE, stride=ROW_CHUNKS), :] for j in range(ROW_CHUNKS)],
                             axis=1).astype(BF16)
        gate = jnp.dot(xt, wg_bf[...], preferred_element_type=F32)
        up = jnp.dot(xt, wu_bf[...], preferred_element_type=F32)
        h = (jax.nn.silu(gate) * up).astype(BF16)
        y = jnp.dot(h, wd_bf[...], preferred_element_type=F32)
        for j in range(ROW_CHUNKS):
            os_ref[pl.ds(j, EXPERT_TILE, stride=ROW_CHUNKS), :] = y[:, j * LANES:(j + 1) * LANES]


def _experts(xs, tile_expert, n_valid, lw):
    n_tiles = xs.shape[0] // (EXPERT_TILE * ROW_CHUNKS)
    rows = EXPERT_TILE * ROW_CHUNKS
    act = pl.BlockSpec((rows, LANES), lambda i, te, nv: (jnp.minimum(i, nv[0] - 1), 0))
    grid_spec = pltpu.PrefetchScalarGridSpec(
        num_scalar_prefetch=2,
        grid=(n_tiles,),
        in_specs=[act,
                  pl.BlockSpec((None, D_MODEL, D_EXPERT), lambda i, te, nv: (te[i], 0, 0)),
                  pl.BlockSpec((None, D_MODEL, D_EXPERT), lambda i, te, nv: (te[i], 0, 0)),
                  pl.BlockSpec((None, D_EXPERT, D_MODEL), lambda i, te, nv: (te[i], 0, 0))],
        out_specs=act,
        scratch_shapes=[pltpu.VMEM((D_MODEL, D_EXPERT), BF16), pltpu.VMEM((D_MODEL, D_EXPERT), BF16),
                        pltpu.VMEM((D_EXPERT, D_MODEL), BF16)],
    )
    return pl.pallas_call(
        _expert_kernel,
        grid_spec=grid_spec,
        out_shape=jax.ShapeDtypeStruct(xs.shape, F32),
        compiler_params=_cparams("arbitrary"),
        name="moe_experts",
    )(tile_expert, n_valid, xs, lw["w_gate"], lw["w_up"], lw["w_down"])


def _gather_rows_issue(slot_ref, os_ref, gbuf, sems, buf, tc):
    def issue(r, carry):
        for c in range(2):
            s = slot_ref[0, 0, 2 * r + c]
            src = os_ref.at[pl.ds(pl.multiple_of(s * ROW_CHUNKS, ROW_CHUNKS), ROW_CHUNKS)]
            dst = gbuf.at[buf, c, pl.ds(pl.multiple_of(r * ROW_CHUNKS, ROW_CHUNKS), ROW_CHUNKS)]
            _rows_copy(src, dst, sems.at[buf]).start(priority=c)
        return carry

    lax.fori_loop(0, tc, issue, 0)


def _combine_kernel(slot_ref, slot_next_ref, x_ref, meta_ref, g_ref, bt_ref, os_ref, xp_ref, xd_ref,
                    gbuf, sems, *, tc, n_first):
    i = pl.program_id(0)
    n = pl.num_programs(0)
    sb = lax.rem(i, 2)

    @pl.when(i == 0)
    def _():
        _gather_rows_issue(slot_ref, os_ref, gbuf, sems, 0, tc)

    @pl.when(i + 1 < n)
    def _():
        _gather_rows_issue(slot_next_ref, os_ref, gbuf, sems, 1 - sb, tc)

    for c in range(2):
        _rows_copy(os_ref.at[pl.ds(0, tc * ROW_CHUNKS)], gbuf.at[sb, c], sems.at[sb]).wait()

    meta = meta_ref[...]
    y = None
    for c in range(2):
        rows = jnp.concatenate([gbuf[sb, c, pl.ds(j, tc, stride=ROW_CHUNKS), :] for j in range(ROW_CHUNKS)],
                               axis=1)
        term = meta[:, 2 + c:3 + c] * rows
        y = term if y is None else y + term
    x2 = _layer_norm(DEEPNORM_ALPHA * x_ref[...] + y, g_ref[...], bt_ref[...])

    @pl.when(i < n_first)
    def _():
        xp_ref[...] = x2

    @pl.when(i >= n_first)
    def _():
        xd_ref[...] = x2


def _combine(x, meta, slots, os_, lw, *, n_prompt, tc=256):
    t = x.shape[0]
    nblk = t // tc
    n_first = n_prompt // tc
    slots3 = slots.reshape(nblk, 1, 2 * tc)
    kern = functools.partial(_combine_kernel, tc=tc, n_first=n_first)
    smem_tile = lambda f: pl.BlockSpec((1, 1, 2 * tc), f, memory_space=pltpu.SMEM)
    return pl.pallas_call(
        kern,
        grid=(nblk,),
        in_specs=[smem_tile(lambda i: (i, 0, 0)),
                  smem_tile(lambda i: (jnp.minimum(i + 1, nblk - 1), 0, 0)),
                  pl.BlockSpec((tc, D_MODEL), lambda i: (i, 0)),
                  pl.BlockSpec((tc, ROUTER_LANES), lambda i: (i, 0)),
                  _resident((1, D_MODEL)), _resident((1, D_MODEL)),
                  pl.BlockSpec(memory_space=pl.ANY)],
        out_specs=[pl.BlockSpec((tc, D_MODEL), lambda i: (jnp.minimum(i, n_first - 1), 0)),
                   pl.BlockSpec((tc, D_MODEL), lambda i: (jnp.maximum(i - n_first, 0), 0))],
        out_shape=[jax.ShapeDtypeStruct((n_prompt, D_MODEL), F32),
                   jax.ShapeDtypeStruct((t - n_prompt, D_MODEL), F32)],
        scratch_shapes=[pltpu.VMEM((2, 2, tc * ROW_CHUNKS, LANES), F32), pltpu.SemaphoreType.DMA((2,))],
        compiler_params=_cparams("arbitrary"),
        name="moe_combine",
    )(slots3, slots3, x, meta, lw["ln2_g"], lw["ln2_b"], os_)


def _moe(x, lw, *, n_prompt):
    t = x.shape[0]
    n_tiles = (2 * t) // EXPERT_TILE + N_EXPERTS
    meta, cnt = _router(x, lw)
    experts = meta[:, 0:2].astype(jnp.int32)
    ranks = meta[:, 4:6].astype(jnp.int32)
    counts = cnt[0, :N_EXPERTS].astype(jnp.int32)
    seg_tiles = (counts + EXPERT_TILE - 1) // EXPERT_TILE
    tile_ends = jnp.cumsum(seg_tiles)
    seg_start = (tile_ends - seg_tiles) * EXPERT_TILE
    pick = experts[:, :, None] == jnp.arange(N_EXPERTS, dtype=jnp.int32)
    slots = jnp.sum(jnp.where(pick, seg_start, 0), axis=-1) + ranks
    n_valid = tile_ends[-1:]
    tile_ids = jnp.minimum(jnp.arange(n_tiles, dtype=jnp.int32), n_valid[0] - 1)
    tile_expert = jnp.sum((tile_ends[None, :] <= tile_ids[:, None]).astype(jnp.int32), axis=1)
    tile_expert = jnp.minimum(tile_expert, N_EXPERTS - 1)
    xs = _dispatch(x, slots, n_tiles * EXPERT_TILE)
    os_ = _experts(xs, tile_expert, n_valid.astype(jnp.int32), lw)
    return _combine(x, meta, slots, os_, lw, n_prompt=n_prompt)


def _t5_bucket(dist):
    n = jnp.maximum(dist, 0)
    max_exact = REL_BUCKETS // 2
    nf = jnp.maximum(n, 1).astype(F32)
    large = max_exact + (jnp.log(nf / max_exact) / math.log(REL_MAX_DISTANCE / max_exact)
                         * (REL_BUCKETS - max_exact)).astype(jnp.int32)
    large = jnp.minimum(large, REL_BUCKETS - 1)
    return jnp.where(n < max_exact, n, large)


def _pair_parity_rows(per_head):
    out = []
    for kvh in range(N_KV_HEADS):
        out.append(jnp.stack([jnp.concatenate([per_head[GROUP * kvh + par], per_head[GROUP * kvh + 2 + par]], axis=0)
                              for par in range(2)]))
    return jnp.stack(out)


def _bias_tables(rel_bias, ntok, nseq_group):
    def table(dist, valid):
        bias = rel_bias[_t5_bucket(dist)]
        bias = jnp.where(valid[..., None], bias.astype(F32), NEG_INF)
        return jnp.moveaxis(bias, -1, 0)

    r = jnp.arange(WINDOW)[:, None]
    c = jnp.arange(2 * WINDOW)[None, :]
    dist = r + WINDOW - c
    bias_p = _pair_parity_rows(table(dist, (dist >= 0) & (dist <= WINDOW)))

    tq = jnp.arange(ntok)[:, None]
    cc = jnp.arange(KEY_PAD)[None, :]
    kpos = jnp.where(cc < WINDOW, cc - WINDOW, cc - WINDOW)
    dist_d = tq - kpos
    valid_d = (dist_d >= 0) & (dist_d <= WINDOW) & (cc < WINDOW + ntok) & (PAST_LEN + kpos >= 0)
    bias_d = _pair_parity_rows(table(dist_d, valid_d))
    bias_d = jnp.tile(bias_d, (1, 1, nseq_group, 1))
    return bias_p, bias_d


def _sink_tables(sinks, ntok, nseq_group):
    s = sinks.astype(F32)[:, None, None]
    sink_p = _pair_parity_rows(jnp.broadcast_to(s, (N_HEADS, WINDOW, 1)))
    sink_d = jnp.tile(_pair_parity_rows(jnp.broadcast_to(s, (N_HEADS, ntok, 1))), (1, 1, nseq_group, 1))
    return sink_p, sink_d


def _block_diag(pool_w):
    g, c, _ = pool_w.shape
    out = jnp.zeros((g * c, g * c), pool_w.dtype)
    for gi in range(g):
        out = out.at[gi * c:(gi + 1) * c, gi * c:(gi + 1) * c].set(pool_w[gi])
    return out


def _layer_weights(l, w_in, pool_w, pool_scale, proj_a, proj_b, w_out, ln1_g, ln1_b, router_group_w,
                   router_group_b, router_expert_w, router_expert_b, expert_w_gate, expert_w_up,
                   expert_w_down, ln2_g, ln2_b):
    rw = jnp.zeros((D_MODEL, ROUTER_LANES), F32)
    rw = rw.at[:, :N_EXPERTS].set(router_expert_w[l]).at[:, _GROUP_LANE0:_GROUP_LANE0 + N_EXPERT_GROUPS].set(
        router_group_w[l])
    rb = jnp.zeros((1, ROUTER_LANES), F32)
    rb = rb.at[0, :N_EXPERTS].set(router_expert_b[l]).at[0, _GROUP_LANE0:_GROUP_LANE0 + N_EXPERT_GROUPS].set(
        router_group_b[l])
    r_hi = rw.astype(BF16)
    return dict(
        w_in=w_in[l].astype(BF16), pool_w=_block_diag(pool_w[l]).astype(BF16),
        pool_scale=pool_scale[l].reshape(1, D_POOL), proj_a=proj_a[l].astype(BF16),
        proj_b=proj_b[l].astype(BF16), w_out=w_out[l].astype(BF16),
        ln1_g=ln1_g[l].reshape(1, D_MODEL), ln1_b=ln1_b[l].reshape(1, D_MODEL),
        router_hi=r_hi, router_lo=(rw - r_hi.astype(F32)).astype(BF16), router_b=rb,
        w_gate=expert_w_gate[l], w_up=expert_w_up[l], w_down=expert_w_down[l],
        ln2_g=ln2_g[l].reshape(1, D_MODEL), ln2_b=ln2_b[l].reshape(1, D_MODEL))


def _decode_mixer(xs_tn, lw, cache_k, cache_v, pool_hist, bias_d, sink_d, buf, *, nseq, ntok, g, row_block):
    u = _in_proj(xs_tn, lw["w_in"])
    u_pool = u[:, :D_POOL]
    k_new = u[:, _O_K:_O_V].reshape(ntok, nseq, KV_DIM)
    v_new = u[:, _O_V:_O_GA].reshape(ntok, nseq, KV_DIM)
    q = u[:, _O_Q:_O_K].reshape(ntok, nseq, N_KV_HEADS, 2, LANES)
    q2 = jnp.transpose(q, (2, 1, 3, 0, 4)).reshape(N_KV_HEADS, nseq * 2 * ntok, LANES)
    pad = jnp.zeros((nseq, KEY_PAD - WINDOW - ntok, KV_DIM), F32)
    k_ext = jnp.concatenate([cache_k, jnp.swapaxes(k_new, 0, 1), pad], axis=1).reshape(nseq * KEY_PAD, KV_DIM)
    v_ext = jnp.concatenate([cache_v, jnp.swapaxes(v_new, 0, 1), pad], axis=1).reshape(nseq * KEY_PAD, KV_DIM)
    o = _decode_attn(q2, k_ext, v_ext, bias_d, sink_d, g=g)
    b = jnp.transpose(o.reshape(N_KV_HEADS, nseq, 2, ntok, LANES), (3, 1, 0, 2, 4)).reshape(ntok * nseq, Q_DIM)
    hist_t = jnp.swapaxes(pool_hist, 0, 1)
    buf = _decode_tail(xs_tn, u, hist_t, b, lw, buf, nseq=nseq, ntok=ntok, row_block=row_block)
    return buf, k_new, v_new, u_pool.reshape(ntok, nseq, D_POOL)


def kernel(x_prompt, x_sample, cache_k, cache_v, state_pool, rel_bias, w_in, pool_w, pool_scale, proj_a, proj_b, w_out, attn_sinks, ln1_g, ln1_b, router_group_w, router_group_b, router_expert_w, router_expert_b, expert_w_gate, expert_w_up, expert_w_down, ln2_g, ln2_b):
    bsz, seq, _ = x_prompt.shape
    nseq, ntok, _ = x_sample.shape
    g = 16
    bias_p, bias_d = _bias_tables(rel_bias, ntok, g)
    n_prompt = bsz * seq
    n_decode = ntok * nseq
    assert n_prompt % n_decode == 0

    xp = x_prompt.reshape(n_prompt, D_MODEL)
    xs_tn = jnp.swapaxes(x_sample, 0, 1).reshape(n_decode, D_MODEL)
    ck = cache_k.reshape(DEPTH, nseq, WINDOW, KV_DIM)
    cv = cache_v.reshape(DEPTH, nseq, WINDOW, KV_DIM)
    outs = {name: [] for name in ("pk", "pv", "pp", "sk", "sv", "sp")}
    for l in range(DEPTH):
        lw = _layer_weights(l, w_in, pool_w, pool_scale, proj_a, proj_b, w_out, ln1_g, ln1_b, router_group_w,
                            router_group_b, router_expert_w, router_expert_b, expert_w_gate, expert_w_up,
                            expert_w_down, ln2_g, ln2_b)
        sink_p, sink_d = _sink_tables(attn_sinks[l], ntok, g)
        x1, kst, vst, pst = _prompt_mixer(xp, lw, bias_p, sink_p, bsz=bsz, n_total=n_prompt + n_decode)
        x1, k_new, v_new, p_new = _decode_mixer(xs_tn, lw, ck[l], cv[l], state_pool[l], bias_d, sink_d, x1,
                                                nseq=nseq, ntok=ntok, g=g, row_block=n_prompt // n_decode)
        xp, xs_tn = _moe(x1, lw, n_prompt=n_prompt)
        outs["pk"].append(kst.reshape(bsz, WINDOW, N_KV_HEADS, HEAD_DIM))
        outs["pv"].append(vst.reshape(bsz, WINDOW, N_KV_HEADS, HEAD_DIM))
        outs["pp"].append(pst[:, -POOL_HIST:])
        outs["sk"].append(jnp.concatenate([ck[l][:, ntok:], jnp.swapaxes(k_new, 0, 1)], axis=1).reshape(
            nseq, WINDOW, N_KV_HEADS, HEAD_DIM))
        outs["sv"].append(jnp.concatenate([cv[l][:, ntok:], jnp.swapaxes(v_new, 0, 1)], axis=1).reshape(
            nseq, WINDOW, N_KV_HEADS, HEAD_DIM))
        outs["sp"].append(jnp.concatenate([state_pool[l][:, ntok:], jnp.swapaxes(p_new, 0, 1)], axis=1))
    y_sample = jnp.swapaxes(xs_tn.reshape(ntok, nseq, D_MODEL), 0, 1)
    return (xp.reshape(bsz, seq, D_MODEL), y_sample, jnp.stack(outs["pk"]), jnp.stack(outs["pv"]), jnp.stack(outs["pp"]),
            jnp.stack(outs["sk"]), jnp.stack(outs["sv"]), jnp.stack(outs["sp"]))
```

```python
import functools
import math

import jax
import jax.numpy as jnp
from jax import lax
from jax.experimental import pallas as pl
from jax.experimental.pallas import tpu as pltpu

D_MODEL = 1024
DEPTH = 2
PAST_LEN = 16384
D_POOL = D_MODEL // 2
POOL_WINDOWS = (2, 4, 8, 16)
N_POOL_GROUPS = len(POOL_WINDOWS)
POOL_GROUP_DIM = D_POOL // N_POOL_GROUPS
POOL_HIST = max(POOL_WINDOWS) - 1
HEAD_DIM = 64
N_HEADS = (D_MODEL // 2) // HEAD_DIM
N_KV_HEADS = 2
GROUP = N_HEADS // N_KV_HEADS
Q_DIM = N_HEADS * HEAD_DIM
KV_DIM = N_KV_HEADS * HEAD_DIM
WINDOW = 128
ATTN_SCALE = HEAD_DIM ** -0.5
REL_BUCKETS = 32
REL_MAX_DISTANCE = 128
D_IN = D_POOL + Q_DIM + 2 * KV_DIM + 2 * D_MODEL
N_EXPERT_GROUPS = 4
EXPERTS_PER_GROUP = 8
N_EXPERTS = N_EXPERT_GROUPS * EXPERTS_PER_GROUP
D_EXPERT = D_MODEL // 4
DEEPNORM_ALPHA = (2 * DEPTH) ** 0.25
LN_EPS = 1e-5

LANES = 128
SUBLANES = 8
VMEM_LIMIT_BYTES = 56 * 1024 * 1024

_O_Q = D_POOL
_O_K = _O_Q + Q_DIM
_O_V = _O_K + KV_DIM
_O_GA = _O_V + KV_DIM
_O_GB = _O_GA + D_MODEL

ROW_CHUNKS = D_MODEL // LANES
KEY_PAD = 2 * WINDOW
EXPERT_TILE = 256
ROUTER_LANES = LANES
_GROUP_LANE0 = N_EXPERTS

BF16 = jnp.bfloat16
F32 = jnp.float32
NEG_INF = float("-inf")


def _cparams(*sem):
    return pltpu.CompilerParams(dimension_semantics=sem, vmem_limit_bytes=VMEM_LIMIT_BYTES)


def _resident(shape):
    nd = len(shape)
    return pl.BlockSpec(shape, lambda *_: (0,) * nd, pipeline_mode=pl.Buffered(1))


def _layer_resident(stacked, layer):
    nd = stacked.ndim - 1
    return pl.BlockSpec((None,) + stacked.shape[1:], lambda *_: (layer,) + (0,) * nd,
                        pipeline_mode=pl.Buffered(1))


def _layer_norm(y, g, b):
    mu = jnp.mean(y, axis=-1, keepdims=True)
    yc = y - mu
    var = jnp.mean(yc * yc, axis=-1, keepdims=True)
    return yc * lax.rsqrt(var + LN_EPS) * g + b


def _softmax_pv(s, sink, v_bf16):
    m = jnp.maximum(jnp.max(s, axis=-1, keepdims=True), sink)
    p = jnp.exp(s - m)
    den = jnp.sum(p, axis=-1, keepdims=True) + jnp.exp(sink - m)
    o = jnp.dot(p.astype(BF16), v_bf16, preferred_element_type=F32)
    return o / den


def _half_lane_variants(t):
    lane = lax.broadcasted_iota(jnp.int32, t.shape, 1)
    lo = lane < HEAD_DIM
    tr = pltpu.roll(t, HEAD_DIM, 1)
    z = jnp.zeros_like(t)
    return ((jnp.where(lo, t, z), jnp.where(lo, z, tr)),
            (jnp.where(lo, tr, z), jnp.where(lo, z, t)))


def _mix_tail(x, a_in, b_in, ga, gb, pw, ps, pa, pb, wo, g, bt):
    a = jnp.dot(a_in.astype(BF16), pw, preferred_element_type=F32) * ps
    pa_o = jnp.dot(a.astype(BF16), pa, preferred_element_type=F32)
    pb_o = jnp.dot(b_in.astype(BF16), pb, preferred_element_type=F32)
    merged = jax.nn.sigmoid(ga) * pa_o + jax.nn.sigmoid(gb) * pb_o
    mix = jnp.dot(merged.astype(BF16), wo, preferred_element_type=F32)
    return _layer_norm(DEEPNORM_ALPHA * x + mix, g, bt)


def _prompt_mixer_kernel(x_ref, win_ref, pw_ref, ps_ref, pa_ref, pb_ref, wo_ref, bias_ref, sink_ref,
                         g_ref, bt_ref, x1_ref, kst_ref, vst_ref, pst_ref, kt_scr, v_scr, ext_scr, *, tm):
    i = pl.program_id(1)
    hist = 2 * SUBLANES

    @pl.when(i == 0)
    def _():
        kt_scr[:, :, :WINDOW] = jnp.zeros((4, LANES, WINDOW), BF16)
        v_scr[:, :WINDOW, :] = jnp.zeros((4, WINDOW, LANES), BF16)
        ext_scr[:hist, :] = jnp.zeros((hist, D_POOL), F32)

    @pl.when(i > 0)
    def _():
        kt_scr[:, :, :WINDOW] = kt_scr[:, :, tm:tm + WINDOW]
        v_scr[:, :WINDOW, :] = v_scr[:, tm:tm + WINDOW, :]
        ext_scr[:hist, :] = ext_scr[tm:tm + hist, :]

    x = x_ref[...]
    u = jnp.dot(x.astype(BF16), win_ref[...], preferred_element_type=F32)
    u_pool = u[:, :D_POOL]
    q = u[:, _O_Q:_O_K]
    k = u[:, _O_K:_O_V]
    v = u[:, _O_V:_O_GA]
    ga = u[:, _O_GA:_O_GB]
    gb = u[:, _O_GB:]

    kst_ref[...] = k[tm - WINDOW:, :]
    vst_ref[...] = v[tm - WINDOW:, :]
    pst_ref[...] = u_pool[tm - hist:, :]

    kt = k.T
    kz = jnp.zeros((HEAD_DIM, tm), F32)
    kt_scr[0, :, WINDOW:] = jnp.concatenate([kt[:HEAD_DIM], kz], axis=0).astype(BF16)
    kt_scr[1, :, WINDOW:] = jnp.concatenate([kz, kt[:HEAD_DIM]], axis=0).astype(BF16)
    kt_scr[2, :, WINDOW:] = jnp.concatenate([kt[HEAD_DIM:], kz], axis=0).astype(BF16)
    kt_scr[3, :, WINDOW:] = jnp.concatenate([kz, kt[HEAD_DIM:]], axis=0).astype(BF16)
    vv = _half_lane_variants(v)
    for kvh in range(N_KV_HEADS):
        for par in range(2):
            v_scr[2 * kvh + par, WINDOW:, :] = vv[kvh][par].astype(BF16)

    col = lax.broadcasted_iota(jnp.int32, (2 * WINDOW, 2 * WINDOW), 1)
    no_prev = col < jnp.where(i == 0, WINDOW, 0)
    qb = q.astype(BF16)
    b_rows = []
    for j in range(tm // WINDOW):
        r0 = j * WINDOW
        o_kvh = []
        for kvh in range(N_KV_HEADS):
            c0 = kvh * 2 * LANES
            q2 = jnp.concatenate([qb[r0:r0 + WINDOW, c0:c0 + LANES],
                                  qb[r0:r0 + WINDOW, c0 + LANES:c0 + 2 * LANES]], axis=0)
            acc = None
            for par in range(2):
                s = jnp.dot(q2, kt_scr[2 * kvh + par, :, r0:r0 + 2 * WINDOW], preferred_element_type=F32)
                s = s * ATTN_SCALE + bias_ref[kvh, par]
                if j == 0:
                    s = jnp.where(no_prev, NEG_INF, s)
                o = _softmax_pv(s, sink_ref[kvh, par], v_scr[2 * kvh + par, r0:r0 + 2 * WINDOW, :])
                acc = o if acc is None else acc + o
            o_kvh.append(acc)
        b_rows.append(jnp.concatenate([o_kvh[0][:WINDOW], o_kvh[0][WINDOW:],
                                       o_kvh[1][:WINDOW], o_kvh[1][WINDOW:]], axis=1))
    b = jnp.concatenate(b_rows, axis=0)

    ext_scr[hist:, :] = u_pool
    pos = i * tm + lax.broadcasted_iota(jnp.int32, (tm, 1), 0)
    means = []
    for gi, w in enumerate(POOL_WINDOWS):
        c0 = gi * POOL_GROUP_DIM
        acc = u_pool[:, c0:c0 + POOL_GROUP_DIM]
        for d in range(1, w):
            acc = acc + ext_scr[pl.ds(hist - d, tm), c0:c0 + POOL_GROUP_DIM]
        cnt = jnp.minimum(w, pos + 1).astype(F32)
        means.append(acc / cnt)
    pooled = jnp.concatenate(means, axis=1) - u_pool

    x1_ref[...] = _mix_tail(x, pooled, b, ga, gb, pw_ref[...], ps_ref[...], pa_ref[...], pb_ref[...],
                            wo_ref[...], g_ref[...], bt_ref[...])


def _prompt_mixer(x, lw, bias_p, sink_t, *, bsz, n_total, tm=512):
    seq = x.shape[0] // bsz
    nt = seq // tm
    hist = 2 * SUBLANES
    kern = functools.partial(_prompt_mixer_kernel, tm=tm)
    tile = pl.BlockSpec((tm, D_MODEL), lambda b, i: (b * nt + i, 0))
    state = lambda r, w: pl.BlockSpec((None, r, w), lambda b, i: (b, 0, 0))
    params = (lw["w_in"], lw["pool_w"], lw["pool_scale"], lw["proj_a"], lw["proj_b"], lw["w_out"])
    tail = (sink_t, lw["ln1_g"], lw["ln1_b"])
    return pl.pallas_call(
        kern,
        grid=(bsz, nt),
        in_specs=([tile] + [_layer_resident(a, lw["layer"]) for a in params] + [_resident(bias_p.shape)]
                  + [_layer_resident(a, lw["layer"]) for a in tail]),
        out_specs=[tile, state(WINDOW, KV_DIM), state(WINDOW, KV_DIM), state(hist, D_POOL)],
        out_shape=[jax.ShapeDtypeStruct((n_total, D_MODEL), F32),
                   jax.ShapeDtypeStruct((bsz, WINDOW, KV_DIM), F32),
                   jax.ShapeDtypeStruct((bsz, WINDOW, KV_DIM), F32),
                   jax.ShapeDtypeStruct((bsz, hist, D_POOL), F32)],
        scratch_shapes=[pltpu.VMEM((4, LANES, WINDOW + tm), BF16),
                        pltpu.VMEM((4, WINDOW + tm, LANES), BF16),
                        pltpu.VMEM((hist + tm, D_POOL), F32)],
        compiler_params=_cparams("arbitrary", "arbitrary"),
        name="prompt_mixer",
    )(x, lw["w_in"], lw["pool_w"], lw["pool_scale"], lw["proj_a"], lw["proj_b"], lw["w_out"],
      bias_p, sink_t, lw["ln1_g"], lw["ln1_b"])


def _in_proj_kernel(x_ref, win_ref, u_ref):
    u_ref[...] = jnp.dot(x_ref[...].astype(BF16), win_ref[...], preferred_element_type=F32)


def _in_proj(x, lw):
    n = x.shape[0]
    return pl.pallas_call(
        _in_proj_kernel,
        grid=(1,),
        in_specs=[pl.BlockSpec(x.shape, lambda i: (0, 0)), _layer_resident(lw["w_in"], lw["layer"])],
        out_specs=pl.BlockSpec((n, D_IN), lambda i: (0, 0)),
        out_shape=jax.ShapeDtypeStruct((n, D_IN), F32),
        compiler_params=_cparams("arbitrary"),
        name="decode_in_proj",
    )(x, lw["w_in"])


def _decode_attn_kernel(q_ref, kn_ref, vn_ref, kc_ref, vc_ref, bias_ref, biasn_ref, sink_ref, b_ref, *, g, ntok):
    slabs = 2 * ntok
    rows = slabs * g
    kk = _half_lane_variants(kc_ref[...])
    vv = _half_lane_variants(vc_ref[...])
    kn = [_half_lane_variants(kn_ref[t]) for t in range(ntok)]
    vn = [_half_lane_variants(vn_ref[t]) for t in range(ntok)]
    per_row = lambda a: jnp.concatenate([a] * slabs, axis=0)
    row_seq = lax.rem(lax.broadcasted_iota(jnp.int32, (rows, WINDOW), 0), g)
    diag = (lax.rem(lax.broadcasted_iota(jnp.int32, (rows, g * WINDOW), 0), g)
            == lax.broadcasted_iota(jnp.int32, (rows, g * WINDOW), 1) // WINDOW)
    for kvh in range(N_KV_HEADS):
        c0 = kvh * 2 * LANES
        q2f = jnp.concatenate([q_ref[t, :, c0 + p * LANES:c0 + (p + 1) * LANES]
                               for p in range(2) for t in range(ntok)], axis=0)
        q2 = q2f.astype(BF16)
        acc = None
        for par in range(2):
            full = lax.dot_general(q2, kk[kvh][par].astype(BF16), (((1,), (1,)), ((), ())),
                                   preferred_element_type=F32)
            s = jnp.zeros((rows, WINDOW), F32)
            for n in range(g):
                s = jnp.where(row_seq == n, full[:, n * WINDOW:(n + 1) * WINDOW], s)
            s = s * ATTN_SCALE + bias_ref[kvh, par]
            s_new = [jnp.sum(q2f * per_row(kn[t][kvh][par]), axis=-1, keepdims=True) * ATTN_SCALE
                     + biasn_ref[kvh, par, t] for t in range(ntok)]
            sink = sink_ref[kvh, par]
            m = jnp.maximum(jnp.max(s, axis=-1, keepdims=True), sink)
            for t in range(ntok):
                m = jnp.maximum(m, s_new[t])
            p = jnp.exp(s - m)
            den = jnp.sum(p, axis=-1, keepdims=True) + jnp.exp(sink - m)
            p_bd = jnp.where(diag, jnp.concatenate([p] * g, axis=1), 0.0).astype(BF16)
            o = jnp.dot(p_bd, vv[kvh][par].astype(BF16), preferred_element_type=F32)
            for t in range(ntok):
                p_new = jnp.exp(s_new[t] - m)
                den = den + p_new
                o = o + p_new * per_row(vn[t][kvh][par])
            o = o / den
            acc = o if acc is None else acc + o
        for p in range(2):
            for t in range(ntok):
                r0 = (p * ntok + t) * g
                b_ref[t, :, c0 + p * LANES:c0 + (p + 1) * LANES] = acc[r0:r0 + g]


def _decode_attn(u3, cache_k, cache_v, bias_d, biasn_d, sink_d, *, g, layer):
    ntok, nseq, _ = u3.shape
    cache = pl.BlockSpec((None, g * WINDOW, KV_DIM), lambda i: (layer, i, 0))
    kern = functools.partial(_decode_attn_kernel, g=g, ntok=ntok)
    new_cols = lambda w, off: pl.BlockSpec((ntok, g, w), lambda i: (0, i, off // w))
    return pl.pallas_call(
        kern,
        grid=(nseq // g,),
        in_specs=[new_cols(Q_DIM, _O_Q), new_cols(KV_DIM, _O_K), new_cols(KV_DIM, _O_V),
                  cache, cache,
                  _resident(bias_d.shape), _resident(biasn_d.shape), _layer_resident(sink_d, layer)],
        out_specs=pl.BlockSpec((ntok, g, Q_DIM), lambda i: (0, i, 0)),
        out_shape=jax.ShapeDtypeStruct((ntok, nseq, Q_DIM), F32),
        compiler_params=_cparams("arbitrary"),
        name="decode_attn",
    )(u3, u3, u3, cache_k, cache_v, bias_d, biasn_d, sink_d)


def _decode_tail_kernel(x_ref, u_ref, hist_ref, b_ref, pw_ref, ps_ref, pa_ref, pb_ref,
                        wo_ref, g_ref, bt_ref, buf_ref, x1_ref, *, nseq, ntok, start_pos):
    del buf_ref

    def ext_row(r, c0):
        if r < POOL_HIST:
            return hist_ref[r, :, c0:c0 + POOL_GROUP_DIM]
        return u_ref[(r - POOL_HIST) * nseq:(r - POOL_HIST + 1) * nseq, c0:c0 + POOL_GROUP_DIM]

    pooled_rows = []
    for t in range(ntok):
        means = []
        for gi, w in enumerate(POOL_WINDOWS):
            acc = None
            for d in range(w):
                r = POOL_HIST + t - d
                if start_pos - POOL_HIST + r < 0:
                    continue
                term = ext_row(r, gi * POOL_GROUP_DIM)
                acc = term if acc is None else acc + term
            means.append(acc / float(min(w, start_pos + t + 1)))
        pooled_rows.append(jnp.concatenate(means, axis=1))
    pooled = jnp.concatenate(pooled_rows, axis=0) - u_ref[:, :D_POOL]
    x1_ref[...] = _mix_tail(x_ref[...], pooled, b_ref[...], u_ref[:, _O_GA:_O_GB], u_ref[:, _O_GB:], pw_ref[...],
                            ps_ref[...], pa_ref[...], pb_ref[...], wo_ref[...], g_ref[...], bt_ref[...])


def _decode_tail(x, u, hist_t, b, lw, buf, *, nseq, ntok, row_block):
    n = ntok * nseq
    kern = functools.partial(_decode_tail_kernel, nseq=nseq, ntok=ntok, start_pos=PAST_LEN)
    whole = lambda a: pl.BlockSpec(a.shape, lambda i: (0,) * a.ndim)
    layered = lambda a: _layer_resident(a, lw["layer"])
    params = (lw["pool_w"], lw["pool_scale"], lw["proj_a"], lw["proj_b"], lw["w_out"], lw["ln1_g"], lw["ln1_b"])
    ins = (x, u, hist_t, b) + params
    return pl.pallas_call(
        kern,
        grid=(1,),
        in_specs=([whole(x), whole(u), layered(hist_t), whole(b)] + [layered(a) for a in params]
                  + [pl.BlockSpec(memory_space=pl.ANY)]),
        out_specs=pl.BlockSpec((n, D_MODEL), lambda i: (row_block, 0)),
        out_shape=jax.ShapeDtypeStruct(buf.shape, F32),
        input_output_aliases={len(ins): 0},
        compiler_params=_cparams("arbitrary"),
        name="decode_tail",
    )(*ins, buf)


def _router_kernel(x_ref, whi_ref, wlo_ref, br_ref, meta_ref, cnt_ref, run_scr, *, tr):
    i = pl.program_id(0)

    @pl.when(i == 0)
    def _():
        run_scr[...] = jnp.zeros_like(run_scr)

    x = x_ref[...]
    xh = x.astype(BF16)
    xl = (x - xh.astype(F32)).astype(BF16)
    whi = whi_ref[...]
    logits = (jnp.dot(xh, whi, preferred_element_type=F32)
              + (jnp.dot(xh, wlo_ref[...], preferred_element_type=F32)
                 + jnp.dot(xl, whi, preferred_element_type=F32))) + br_ref[...]
    lane = lax.broadcasted_iota(jnp.int32, (tr, ROUTER_LANES), 1)
    big = jnp.int32(ROUTER_LANES)

    def masked_argmax(vals):
        vmax = jnp.max(vals, axis=-1, keepdims=True)
        idx = jnp.min(jnp.where(vals == vmax, lane, big), axis=-1, keepdims=True)
        return vmax, idx

    is_group = jnp.logical_and(lane >= _GROUP_LANE0, lane < _GROUP_LANE0 + N_EXPERT_GROUPS)
    glog = jnp.where(is_group, logits, NEG_INF)
    gmax, glane = masked_argmax(glog)
    gidx = glane - _GROUP_LANE0
    gw = 1.0 / jnp.sum(jnp.exp(glog - gmax), axis=-1, keepdims=True)
    in_group = jnp.logical_and(lane >= gidx * EXPERTS_PER_GROUP, lane < (gidx + 1) * EXPERTS_PER_GROUP)
    el = jnp.where(in_group, logits, NEG_INF)
    v1, e1 = masked_argmax(el)
    v2, e2 = masked_argmax(jnp.where(lane == e1, NEG_INF, el))
    r = jnp.exp(v2 - v1)
    w1 = gw / (1.0 + r)
    w2 = gw * r / (1.0 + r)

    is1 = lane == e1
    is2 = lane == e2
    oh = jnp.logical_or(is1, is2)
    rr = lax.broadcasted_iota(jnp.int32, (tr, tr), 0)
    cc = lax.broadcasted_iota(jnp.int32, (tr, tr), 1)
    lower = jnp.where(rr > cc, 1.0, 0.0).astype(BF16)
    before = jnp.dot(lower, jnp.where(oh, 1.0, 0.0).astype(BF16), preferred_element_type=F32) + run_scr[...]
    rank1 = jnp.sum(jnp.where(is1, before, 0.0), axis=-1, keepdims=True)
    rank2 = jnp.sum(jnp.where(is2, before, 0.0), axis=-1, keepdims=True)
    run_scr[...] += jnp.sum(jnp.where(oh, 1.0, 0.0), axis=0, keepdims=True)
    cnt_ref[...] = run_scr[...]

    cols = (e1.astype(F32), e2.astype(F32), w1, w2, rank1, rank2)
    meta = jnp.zeros((tr, ROUTER_LANES), F32)
    for c, val in enumerate(cols):
        meta = jnp.where(lane == c, val, meta)
    meta_ref[...] = meta


def _router(x, lw, *, tr=512):
    t = x.shape[0]
    kern = functools.partial(_router_kernel, tr=tr)
    return pl.pallas_call(
        kern,
        grid=(t // tr,),
        in_specs=[pl.BlockSpec((tr, D_MODEL), lambda i: (i, 0)),
                  _layer_resident(lw["router_hi"], lw["layer"]), _layer_resident(lw["router_lo"], lw["layer"]),
                  _layer_resident(lw["router_b"], lw["layer"])],
        out_specs=[pl.BlockSpec((tr, ROUTER_LANES), lambda i: (i, 0)),
                   pl.BlockSpec((1, ROUTER_LANES), lambda i: (0, 0))],
        out_shape=[jax.ShapeDtypeStruct((t, ROUTER_LANES), F32),
                   jax.ShapeDtypeStruct((1, ROUTER_LANES), F32)],
        scratch_shapes=[pltpu.VMEM((1, ROUTER_LANES), F32)],
        compiler_params=_cparams("arbitrary"),
        name="router",
    )(x, lw["router_hi"], lw["router_lo"], lw["router_b"])


def _rows_copy(src, dst, sem):
    return pltpu.make_async_copy(src, dst, sem)


def _dispatch_kernel(slot_ref, x_ref, xs_ref, stage, sems, *, td):
    i = pl.program_id(0)
    n = pl.num_programs(0)
    sb = lax.rem(i, 2)
    x = x_ref[...]
    for j in range(ROW_CHUNKS):
        stage[sb, pl.ds(j, td, stride=ROW_CHUNKS), :] = x[:, j * LANES:(j + 1) * LANES]

    def issue(r, carry):
        src = stage.at[sb, pl.ds(pl.multiple_of(r * ROW_CHUNKS, ROW_CHUNKS), ROW_CHUNKS)]
        for c in range(2):
            s = slot_ref[0, 0, 2 * r + c]
            dst = xs_ref.at[pl.ds(pl.multiple_of(s * ROW_CHUNKS, ROW_CHUNKS), ROW_CHUNKS)]
            _rows_copy(src, dst, sems.at[sb]).start(priority=c)
        return carry

    lax.fori_loop(0, td, issue, 0)

    def drain(buf):
        for _ in range(2):
            _rows_copy(stage.at[buf], xs_ref.at[pl.ds(0, td * ROW_CHUNKS)], sems.at[buf]).wait()

    @pl.when(i > 0)
    def _():
        drain(1 - sb)

    @pl.when(i == n - 1)
    def _():
        drain(sb)


def _dispatch(x, slots, n_slots, *, td=256):
    t = x.shape[0]
    slots3 = slots.reshape(t // td, 1, 2 * td)
    kern = functools.partial(_dispatch_kernel, td=td)
    return pl.pallas_call(
        kern,
        grid=(t // td,),
        in_specs=[pl.BlockSpec((1, 1, 2 * td), lambda i: (i, 0, 0), memory_space=pltpu.SMEM),
                  pl.BlockSpec((td, D_MODEL), lambda i: (i, 0))],
        out_specs=pl.BlockSpec(memory_space=pl.ANY),
        out_shape=jax.ShapeDtypeStruct((n_slots * ROW_CHUNKS, LANES), F32),
        scratch_shapes=[pltpu.VMEM((2, td * ROW_CHUNKS, LANES), F32), pltpu.SemaphoreType.DMA((2,))],
        compiler_params=_cparams("arbitrary"),
        name="moe_dispatch",
    )(slots3, x)


def _expert_kernel(te_ref, nv_ref, xs_ref, wg_ref, wu_ref, wd_ref, os_ref, wg_bf, wu_bf, wd_bf):
    i = pl.program_id(0)
    valid = i < nv_ref[0]
    new_expert = jnp.logical_or(i == 0, te_ref[i] != te_ref[jnp.maximum(i - 1, 0)])

    @pl.when(jnp.logical_and(valid, new_expert))
    def _():
        wg_bf[...] = wg_ref[...].astype(BF16)
        wu_bf[...] = wu_ref[...].astype(BF16)
        wd_bf[...] = wd_ref[...].astype(BF16)

    @pl.when(valid)
    def _():
        xt = jnp.concatenate([xs_ref[pl.ds(j, EXPERT_TILE, stride=ROW_CHUNKS), :] for j in range(ROW_CHUNKS)],
                             axis=1).astype(BF16)
        gate = jnp.dot(xt, wg_bf[...], preferred_element_type=F32)
        up = jnp.dot(xt, wu_bf[...], preferred_element_type=F32)
        h = (jax.nn.silu(gate) * up).astype(BF16)
        y = jnp.dot(h, wd_bf[...], preferred_element_type=F32)
        for j in range(ROW_CHUNKS):
            os_ref[pl.ds(j, EXPERT_TILE, stride=ROW_CHUNKS), :] = y[:, j * LANES:(j + 1) * LANES]


def _experts(xs, tile_expert, n_valid, lw):
    layer = lw["layer"]
    n_tiles = xs.shape[0] // (EXPERT_TILE * ROW_CHUNKS)
    rows = EXPERT_TILE * ROW_CHUNKS
    act = pl.BlockSpec((rows, LANES), lambda i, te, nv: (jnp.minimum(i, nv[0] - 1), 0))
    grid_spec = pltpu.PrefetchScalarGridSpec(
        num_scalar_prefetch=2,
        grid=(n_tiles,),
        in_specs=[act,
                  pl.BlockSpec((None, None, D_MODEL, D_EXPERT), lambda i, te, nv: (layer, te[i], 0, 0)),
                  pl.BlockSpec((None, None, D_MODEL, D_EXPERT), lambda i, te, nv: (layer, te[i], 0, 0)),
                  pl.BlockSpec((None, None, D_EXPERT, D_MODEL), lambda i, te, nv: (layer, te[i], 0, 0))],
        out_specs=act,
        scratch_shapes=[pltpu.VMEM((D_MODEL, D_EXPERT), BF16), pltpu.VMEM((D_MODEL, D_EXPERT), BF16),
                        pltpu.VMEM((D_EXPERT, D_MODEL), BF16)],
    )
    return pl.pallas_call(
        _expert_kernel,
        grid_spec=grid_spec,
        out_shape=jax.ShapeDtypeStruct(xs.shape, F32),
        compiler_params=_cparams("arbitrary"),
        name="moe_experts",
    )(tile_expert, n_valid, xs, lw["w_gate"], lw["w_up"], lw["w_down"])


def _gather_rows_issue(slot_ref, os_ref, gbuf, sems, buf, tc):
    def issue(r, carry):
        for c in range(2):
            s = slot_ref[0, 0, 2 * r + c]
            src = os_ref.at[pl.ds(pl.multiple_of(s * ROW_CHUNKS, ROW_CHUNKS), ROW_CHUNKS)]
            dst = gbuf.at[buf, c, pl.ds(pl.multiple_of(r * ROW_CHUNKS, ROW_CHUNKS), ROW_CHUNKS)]
            _rows_copy(src, dst, sems.at[buf]).start(priority=c)
        return carry

    lax.fori_loop(0, tc, issue, 0)


def _combine_kernel(slot_ref, slot_next_ref, x_ref, meta_ref, g_ref, bt_ref, os_ref, xp_ref, xd_ref,
                    gbuf, sems, *, tc, n_first):
    i = pl.program_id(0)
    n = pl.num_programs(0)
    sb = lax.rem(i, 2)

    @pl.when(i == 0)
    def _():
        _gather_rows_issue(slot_ref, os_ref, gbuf, sems, 0, tc)

    @pl.when(i + 1 < n)
    def _():
        _gather_rows_issue(slot_next_ref, os_ref, gbuf, sems, 1 - sb, tc)

    for c in range(2):
        _rows_copy(os_ref.at[pl.ds(0, tc * ROW_CHUNKS)], gbuf.at[sb, c], sems.at[sb]).wait()

    meta = meta_ref[...]
    y = None
    for c in range(2):
        rows = jnp.concatenate([gbuf[sb, c, pl.ds(j, tc, stride=ROW_CHUNKS), :] for j in range(ROW_CHUNKS)],
                               axis=1)
        term = meta[:, 2 + c:3 + c] * rows
        y = term if y is None else y + term
    x2 = _layer_norm(DEEPNORM_ALPHA * x_ref[...] + y, g_ref[...], bt_ref[...])

    @pl.when(i < n_first)
    def _():
        xp_ref[...] = x2

    @pl.when(i >= n_first)
    def _():
        xd_ref[...] = x2


def _combine(x, meta, slots, os_, lw, *, n_prompt, tc=256):
    t = x.shape[0]
    nblk = t // tc
    n_first = n_prompt // tc
    slots3 = slots.reshape(nblk, 1, 2 * tc)
    kern = functools.partial(_combine_kernel, tc=tc, n_first=n_first)
    smem_tile = lambda f: pl.BlockSpec((1, 1, 2 * tc), f, memory_space=pltpu.SMEM)
    return pl.pallas_call(
        kern,
        grid=(nblk,),
        in_specs=[smem_tile(lambda i: (i, 0, 0)),
                  smem_tile(lambda i: (jnp.minimum(i + 1, nblk - 1), 0, 0)),
                  pl.BlockSpec((tc, D_MODEL), lambda i: (i, 0)),
                  pl.BlockSpec((tc, ROUTER_LANES), lambda i: (i, 0)),
                  _layer_resident(lw["ln2_g"], lw["layer"]), _layer_resident(lw["ln2_b"], lw["layer"]),
                  pl.BlockSpec(memory_space=pl.ANY)],
        out_specs=[pl.BlockSpec((tc, D_MODEL), lambda i: (jnp.minimum(i, n_first - 1), 0)),
                   pl.BlockSpec((tc, D_MODEL), lambda i: (jnp.maximum(i - n_first, 0), 0))],
        out_shape=[jax.ShapeDtypeStruct((n_prompt, D_MODEL), F32),
                   jax.ShapeDtypeStruct((t - n_prompt, D_MODEL), F32)],
        scratch_shapes=[pltpu.VMEM((2, 2, tc * ROW_CHUNKS, LANES), F32), pltpu.SemaphoreType.DMA((2,))],
        compiler_params=_cparams("arbitrary"),
        name="moe_combine",
    )(slots3, slots3, x, meta, lw["ln2_g"], lw["ln2_b"], os_)


def _moe(x, lw, *, n_prompt):
    t = x.shape[0]
    n_tiles = (2 * t) // EXPERT_TILE + N_EXPERTS
    meta, cnt = _router(x, lw)
    experts = meta[:, 0:2].astype(jnp.int32)
    ranks = meta[:, 4:6].astype(jnp.int32)
    counts = cnt[0, :N_EXPERTS].astype(jnp.int32)
    seg_tiles = (counts + EXPERT_TILE - 1) // EXPERT_TILE
    tile_ends = jnp.cumsum(seg_tiles)
    seg_start = (tile_ends - seg_tiles) * EXPERT_TILE
    pick = experts[:, :, None] == jnp.arange(N_EXPERTS, dtype=jnp.int32)
    slots = jnp.sum(jnp.where(pick, seg_start, 0), axis=-1) + ranks
    n_valid = tile_ends[-1:]
    tile_ids = jnp.minimum(jnp.arange(n_tiles, dtype=jnp.int32), n_valid[0] - 1)
    tile_expert = jnp.sum((tile_ends[None, :] <= tile_ids[:, None]).astype(jnp.int32), axis=1)
    tile_expert = jnp.minimum(tile_expert, N_EXPERTS - 1)
    xs = _dispatch(x, slots, n_tiles * EXPERT_TILE)
    os_ = _experts(xs, tile_expert, n_valid.astype(jnp.int32), lw)
    return _combine(x, meta, slots, os_, lw, n_prompt=n_prompt)


def _t5_bucket(dist):
    n = jnp.maximum(dist, 0)
    max_exact = REL_BUCKETS // 2
    nf = jnp.maximum(n, 1).astype(F32)
    large = max_exact + (jnp.log(nf / max_exact) / math.log(REL_MAX_DISTANCE / max_exact)
                         * (REL_BUCKETS - max_exact)).astype(jnp.int32)
    large = jnp.minimum(large, REL_BUCKETS - 1)
    return jnp.where(n < max_exact, n, large)


def _head_rows(per_head, rows_of):
    out = []
    for kvh in range(N_KV_HEADS):
        out.append(jnp.stack([jnp.concatenate([jnp.repeat(per_head[..., h, :, :], rep, axis=-2)
                                               for h, rep in rows_of(kvh, par)], axis=-2)
                              for par in range(2)], axis=-3))
    return jnp.stack(out, axis=-4)


def _prompt_rows(kvh, par):
    return [(GROUP * kvh + 2 * p + par, 1) for p in range(2)]


def _head_bias(rel_bias, dist, valid):
    bias = jnp.where(valid[..., None], rel_bias[_t5_bucket(dist)].astype(F32), NEG_INF)
    return jnp.moveaxis(bias, -1, 0)


def _prompt_bias(rel_bias):
    period = 4 * WINDOW
    off = jnp.arange(period)
    off = jnp.where(off < 2 * WINDOW, off, off - period)
    dist = WINDOW - off
    per_off = _head_bias(rel_bias, dist, (dist >= 0) & (dist <= WINDOW))
    tiled = jnp.tile(per_off, (1, WINDOW))[:, :WINDOW * (period - 1)]
    return tiled.reshape(N_HEADS, WINDOW, period - 1)[:, :, :2 * WINDOW]


def _attention_tables(rel_bias, sinks, ntok, g):
    bias_p = _head_rows(_prompt_bias(rel_bias), _prompt_rows)

    decode_rows = lambda kvh, par: [(GROUP * kvh + 2 * p + par, g) for p in range(2)]
    tq = jnp.arange(ntok)[:, None]
    kpos_old = jnp.arange(WINDOW)[None, :] - WINDOW
    kpos_new = jnp.arange(ntok)[None, :]
    tables = []
    for kpos in (kpos_old, kpos_new):
        d = tq - kpos
        per_head = _head_bias(rel_bias, d, (d >= 0) & (d <= WINDOW) & (PAST_LEN + kpos >= 0))
        tables.append(_head_rows(per_head, decode_rows))
    bias_d = tables[0]
    biasn_d = jnp.moveaxis(tables[1], -1, 2)[..., None]

    s = sinks.astype(F32)[:, :, None, None]
    sink_p = _head_rows(jnp.broadcast_to(s, s.shape[:2] + (WINDOW, 1)), _prompt_rows)
    sink_d = _head_rows(jnp.broadcast_to(s, s.shape[:2] + (ntok, 1)), decode_rows)
    return bias_p, bias_d, biasn_d, sink_p, sink_d


def _stacked_weights(w_in, pool_w, pool_scale, proj_a, proj_b, w_out, ln1_g, ln1_b, router_group_w,
                     router_group_b, router_expert_w, router_expert_b, expert_w_gate, expert_w_up,
                     expert_w_down, ln2_g, ln2_b):
    depth = w_in.shape[0]
    same_group = jnp.eye(N_POOL_GROUPS, dtype=bool)[None, :, None, :, None]
    pool_bd = jnp.where(same_group, pool_w[:, :, :, None, :], 0.0).reshape(depth, D_POOL, D_POOL)
    pad = ROUTER_LANES - N_EXPERTS - N_EXPERT_GROUPS
    rw = jnp.concatenate([router_expert_w, router_group_w, jnp.zeros((depth, D_MODEL, pad), F32)], axis=-1)
    rb = jnp.concatenate([router_expert_b, router_group_b, jnp.zeros((depth, pad), F32)], axis=-1)
    r_hi = rw.astype(BF16)
    row = lambda a: a[:, None, :]
    return dict(
        w_in=w_in.astype(BF16), pool_w=pool_bd.astype(BF16), pool_scale=row(pool_scale),
        proj_a=proj_a.astype(BF16), proj_b=proj_b.astype(BF16), w_out=w_out.astype(BF16),
        ln1_g=row(ln1_g), ln1_b=row(ln1_b),
        router_hi=r_hi, router_lo=(rw - r_hi.astype(F32)).astype(BF16), router_b=row(rb),
        w_gate=expert_w_gate, w_up=expert_w_up, w_down=expert_w_down,
        ln2_g=row(ln2_g), ln2_b=row(ln2_b))


def _decode_mixer(xs_tn, lw, cache_k, cache_v, hist_t, tables, buf, *, nseq, ntok, g, row_block):
    bias_d, biasn_d, sink_d = tables
    u = _in_proj(xs_tn, lw)
    u3 = u.reshape(ntok, nseq, D_IN)
    b = _decode_attn(u3, cache_k, cache_v, bias_d, biasn_d, sink_d, g=g, layer=lw["layer"])
    buf = _decode_tail(xs_tn, u, hist_t, b.reshape(ntok * nseq, Q_DIM), lw, buf, nseq=nseq, ntok=ntok,
                       row_block=row_block)
    return buf, u3


def kernel(x_prompt, x_sample, cache_k, cache_v, state_pool, rel_bias, w_in, pool_w, pool_scale, proj_a, proj_b, w_out, attn_sinks, ln1_g, ln1_b, router_group_w, router_group_b, router_expert_w, router_expert_b, expert_w_gate, expert_w_up, expert_w_down, ln2_g, ln2_b):
    bsz, seq, _ = x_prompt.shape
    nseq, ntok, _ = x_sample.shape
    g = 16
    depth = w_in.shape[0]
    n_prompt = bsz * seq
    n_decode = ntok * nseq
    assert n_prompt % n_decode == 0
    bias_p, bias_d, biasn_d, sink_p, sink_d = _attention_tables(rel_bias, attn_sinks, ntok, g)
    weights = _stacked_weights(w_in, pool_w, pool_scale, proj_a, proj_b, w_out, ln1_g, ln1_b, router_group_w,
                               router_group_b, router_expert_w, router_expert_b, expert_w_gate, expert_w_up,
                               expert_w_down, ln2_g, ln2_b)

    xp = x_prompt.reshape(n_prompt, D_MODEL)
    xs_tn = jnp.swapaxes(x_sample, 0, 1).reshape(n_decode, D_MODEL)
    ck = cache_k.reshape(depth, nseq * WINDOW, KV_DIM)
    cv = cache_v.reshape(depth, nseq * WINDOW, KV_DIM)
    hist_t = jnp.swapaxes(state_pool, 1, 2)
    prompt_state, decode_u = [], []
    for l in range(depth):
        lw = dict(weights, layer=l)
        x1, kst, vst, pst = _prompt_mixer(xp, lw, bias_p, sink_p, bsz=bsz, n_total=n_prompt + n_decode)
        x1, u3 = _decode_mixer(xs_tn, lw, ck, cv, hist_t, (bias_d, biasn_d, sink_d), x1,
                               nseq=nseq, ntok=ntok, g=g, row_block=n_prompt // n_decode)
        xp, xs_tn = _moe(x1, lw, n_prompt=n_prompt)
        prompt_state.append((kst, vst, pst))
        decode_u.append(u3)

    pk, pv, pp = (jnp.stack(parts) for parts in zip(*prompt_state))
    heads = lambda a: a.reshape(a.shape[:3] + (N_KV_HEADS, HEAD_DIM))

    def shifted(old, col, width):
        new = jnp.stack([jnp.swapaxes(u3[:, :, col:col + width], 0, 1) for u3 in decode_u])
        return jnp.concatenate([old[:, :, ntok:], new], axis=2)

    sk = shifted(ck.reshape(depth, nseq, WINDOW, KV_DIM), _O_K, KV_DIM)
    sv = shifted(cv.reshape(depth, nseq, WINDOW, KV_DIM), _O_V, KV_DIM)
    sp = shifted(state_pool, 0, D_POOL)
    y_sample = jnp.swapaxes(xs_tn.reshape(ntok, nseq, D_MODEL), 0, 1)
    return (xp.reshape(bsz, seq, D_MODEL), y_sample, heads(pk), heads(pv), pp[:, :, -POOL_HIST:],
            heads(sk), heads(sv), sp)
```

```python
import functools
import math

import jax
import jax.numpy as jnp
from jax import lax
from jax.experimental import pallas as pl
from jax.experimental.pallas import tpu as pltpu

D_MODEL = 1024
DEPTH = 2
PAST_LEN = 16384
D_POOL = D_MODEL // 2
POOL_WINDOWS = (2, 4, 8, 16)
N_POOL_GROUPS = len(POOL_WINDOWS)
POOL_GROUP_DIM = D_POOL // N_POOL_GROUPS
POOL_HIST = max(POOL_WINDOWS) - 1
HEAD_DIM = 64
N_HEADS = (D_MODEL // 2) // HEAD_DIM
N_KV_HEADS = 2
GROUP = N_HEADS // N_KV_HEADS
Q_DIM = N_HEADS * HEAD_DIM
KV_DIM = N_KV_HEADS * HEAD_DIM
WINDOW = 128
ATTN_SCALE = HEAD_DIM ** -0.5
REL_BUCKETS = 32
REL_MAX_DISTANCE = 128
D_IN = D_POOL + Q_DIM + 2 * KV_DIM + 2 * D_MODEL
N_EXPERT_GROUPS = 4
EXPERTS_PER_GROUP = 8
N_EXPERTS = N_EXPERT_GROUPS * EXPERTS_PER_GROUP
D_EXPERT = D_MODEL // 4
DEEPNORM_ALPHA = (2 * DEPTH) ** 0.25
LN_EPS = 1e-5

LANES = 128
SUBLANES = 8
VMEM_LIMIT_BYTES = 56 * 1024 * 1024

_O_Q = D_POOL
_O_K = _O_Q + Q_DIM
_O_V = _O_K + KV_DIM
_O_GA = _O_V + KV_DIM
_O_GB = _O_GA + D_MODEL

ROW_CHUNKS = D_MODEL // LANES
KEY_PAD = 2 * WINDOW
EXPERT_TILE = 128
PAIRS_PER_GROUP = EXPERTS_PER_GROUP * (EXPERTS_PER_GROUP - 1) // 2
N_CLASSES = N_EXPERT_GROUPS * PAIRS_PER_GROUP
_CLASS_EXPERTS = tuple((g * EXPERTS_PER_GROUP + a, g * EXPERTS_PER_GROUP + b)
                       for g in range(N_EXPERT_GROUPS)
                       for a in range(EXPERTS_PER_GROUP) for b in range(a + 1, EXPERTS_PER_GROUP))
ROUTER_LANES = LANES
_GROUP_LANE0 = N_EXPERTS

BF16 = jnp.bfloat16
F32 = jnp.float32
NEG_INF = float("-inf")


def _cparams(*sem):
    return pltpu.CompilerParams(dimension_semantics=sem, vmem_limit_bytes=VMEM_LIMIT_BYTES)


def _resident(shape):
    nd = len(shape)
    return pl.BlockSpec(shape, lambda *_: (0,) * nd, pipeline_mode=pl.Buffered(1))


def _layer_resident(stacked, layer):
    nd = stacked.ndim - 1
    return pl.BlockSpec((None,) + stacked.shape[1:], lambda *_: (layer,) + (0,) * nd,
                        pipeline_mode=pl.Buffered(1))


def _layer_norm(y, g, b):
    mu = jnp.mean(y, axis=-1, keepdims=True)
    yc = y - mu
    var = jnp.mean(yc * yc, axis=-1, keepdims=True)
    return yc * lax.rsqrt(var + LN_EPS) * g + b


def _softmax_pv(s, sink, v_bf16):
    m = jnp.maximum(jnp.max(s, axis=-1, keepdims=True), sink)
    p = jnp.exp(s - m)
    den = jnp.sum(p, axis=-1, keepdims=True) + jnp.exp(sink - m)
    o = jnp.dot(p.astype(BF16), v_bf16, preferred_element_type=F32)
    return o / den


def _half_lane_variants(t):
    lane = lax.broadcasted_iota(jnp.int32, t.shape, 1)
    lo = lane < HEAD_DIM
    tr = pltpu.roll(t, HEAD_DIM, 1)
    z = jnp.zeros_like(t)
    return ((jnp.where(lo, t, z), jnp.where(lo, z, tr)),
            (jnp.where(lo, tr, z), jnp.where(lo, z, t)))


def _mix_tail(x, a_in, b_in, ga, gb, pw, ps, pa, pb, wo, g, bt):
    a = jnp.dot(a_in.astype(BF16), pw, preferred_element_type=F32) * ps
    pa_o = jnp.dot(a.astype(BF16), pa, preferred_element_type=F32)
    pb_o = jnp.dot(b_in.astype(BF16), pb, preferred_element_type=F32)
    merged = jax.nn.sigmoid(ga) * pa_o + jax.nn.sigmoid(gb) * pb_o
    mix = jnp.dot(merged.astype(BF16), wo, preferred_element_type=F32)
    return _layer_norm(DEEPNORM_ALPHA * x + mix, g, bt)


def _prompt_mixer_kernel(x_ref, win_ref, pw_ref, ps_ref, pa_ref, pb_ref, wo_ref, bias_ref, sink_ref,
                         g_ref, bt_ref, x1_ref, kst_ref, vst_ref, pst_ref, kt_scr, v_scr, ext_scr, *, tm):
    i = pl.program_id(1)
    hist = 2 * SUBLANES

    @pl.when(i == 0)
    def _():
        kt_scr[:, :, :WINDOW] = jnp.zeros((4, LANES, WINDOW), BF16)
        v_scr[:, :WINDOW, :] = jnp.zeros((4, WINDOW, LANES), BF16)
        ext_scr[:hist, :] = jnp.zeros((hist, D_POOL), F32)

    @pl.when(i > 0)
    def _():
        kt_scr[:, :, :WINDOW] = kt_scr[:, :, tm:tm + WINDOW]
        v_scr[:, :WINDOW, :] = v_scr[:, tm:tm + WINDOW, :]
        ext_scr[:hist, :] = ext_scr[tm:tm + hist, :]

    x = x_ref[...]
    u = jnp.dot(x.astype(BF16), win_ref[...], preferred_element_type=F32)
    u_pool = u[:, :D_POOL]
    q = u[:, _O_Q:_O_K]
    k = u[:, _O_K:_O_V]
    v = u[:, _O_V:_O_GA]
    ga = u[:, _O_GA:_O_GB]
    gb = u[:, _O_GB:]

    kst_ref[...] = k[tm - WINDOW:, :]
    vst_ref[...] = v[tm - WINDOW:, :]
    pst_ref[...] = u_pool[tm - hist:, :]

    kt = k.T
    kz = jnp.zeros((HEAD_DIM, tm), F32)
    kt_scr[0, :, WINDOW:] = jnp.concatenate([kt[:HEAD_DIM], kz], axis=0).astype(BF16)
    kt_scr[1, :, WINDOW:] = jnp.concatenate([kz, kt[:HEAD_DIM]], axis=0).astype(BF16)
    kt_scr[2, :, WINDOW:] = jnp.concatenate([kt[HEAD_DIM:], kz], axis=0).astype(BF16)
    kt_scr[3, :, WINDOW:] = jnp.concatenate([kz, kt[HEAD_DIM:]], axis=0).astype(BF16)
    vv = _half_lane_variants(v)
    for kvh in range(N_KV_HEADS):
        for par in range(2):
            v_scr[2 * kvh + par, WINDOW:, :] = vv[kvh][par].astype(BF16)

    col = lax.broadcasted_iota(jnp.int32, (2 * WINDOW, 2 * WINDOW), 1)
    no_prev = col < jnp.where(i == 0, WINDOW, 0)
    qb = q.astype(BF16)
    b_rows = []
    for j in range(tm // WINDOW):
        r0 = j * WINDOW
        o_kvh = []
        for kvh in range(N_KV_HEADS):
            c0 = kvh * 2 * LANES
            q2 = jnp.concatenate([qb[r0:r0 + WINDOW, c0:c0 + LANES],
                                  qb[r0:r0 + WINDOW, c0 + LANES:c0 + 2 * LANES]], axis=0)
            acc = None
            for par in range(2):
                s = jnp.dot(q2, kt_scr[2 * kvh + par, :, r0:r0 + 2 * WINDOW], preferred_element_type=F32)
                s = s * ATTN_SCALE + bias_ref[kvh, par]
                if j == 0:
                    s = jnp.where(no_prev, NEG_INF, s)
                o = _softmax_pv(s, sink_ref[kvh, par], v_scr[2 * kvh + par, r0:r0 + 2 * WINDOW, :])
                acc = o if acc is None else acc + o
            o_kvh.append(acc)
        b_rows.append(jnp.concatenate([o_kvh[0][:WINDOW], o_kvh[0][WINDOW:],
                                       o_kvh[1][:WINDOW], o_kvh[1][WINDOW:]], axis=1))
    b = jnp.concatenate(b_rows, axis=0)

    ext_scr[hist:, :] = u_pool
    pos = i * tm + lax.broadcasted_iota(jnp.int32, (tm, 1), 0)
    means = []
    for gi, w in enumerate(POOL_WINDOWS):
        c0 = gi * POOL_GROUP_DIM
        acc = u_pool[:, c0:c0 + POOL_GROUP_DIM]
        for d in range(1, w):
            acc = acc + ext_scr[pl.ds(hist - d, tm), c0:c0 + POOL_GROUP_DIM]
        cnt = jnp.minimum(w, pos + 1).astype(F32)
        means.append(acc / cnt)
    pooled = jnp.concatenate(means, axis=1) - u_pool

    x1_ref[...] = _mix_tail(x, pooled, b, ga, gb, pw_ref[...], ps_ref[...], pa_ref[...], pb_ref[...],
                            wo_ref[...], g_ref[...], bt_ref[...])


def _prompt_mixer(x, lw, bias_p, sink_t, *, bsz, n_total, tm=512):
    seq = x.shape[0] // bsz
    nt = seq // tm
    hist = 2 * SUBLANES
    kern = functools.partial(_prompt_mixer_kernel, tm=tm)
    tile = pl.BlockSpec((tm, D_MODEL), lambda b, i: (b * nt + i, 0))
    state = lambda r, w: pl.BlockSpec((None, r, w), lambda b, i: (b, 0, 0))
    params = (lw["w_in"], lw["pool_w"], lw["pool_scale"], lw["proj_a"], lw["proj_b"], lw["w_out"])
    tail = (sink_t, lw["ln1_g"], lw["ln1_b"])
    return pl.pallas_call(
        kern,
        grid=(bsz, nt),
        in_specs=([tile] + [_layer_resident(a, lw["layer"]) for a in params] + [_resident(bias_p.shape)]
                  + [_layer_resident(a, lw["layer"]) for a in tail]),
        out_specs=[tile, state(WINDOW, KV_DIM), state(WINDOW, KV_DIM), state(hist, D_POOL)],
        out_shape=[jax.ShapeDtypeStruct((n_total, D_MODEL), F32),
                   jax.ShapeDtypeStruct((bsz, WINDOW, KV_DIM), F32),
                   jax.ShapeDtypeStruct((bsz, WINDOW, KV_DIM), F32),
                   jax.ShapeDtypeStruct((bsz, hist, D_POOL), F32)],
        scratch_shapes=[pltpu.VMEM((4, LANES, WINDOW + tm), BF16),
                        pltpu.VMEM((4, WINDOW + tm, LANES), BF16),
                        pltpu.VMEM((hist + tm, D_POOL), F32)],
        compiler_params=_cparams("arbitrary", "arbitrary"),
        name="prompt_mixer",
    )(x, lw["w_in"], lw["pool_w"], lw["pool_scale"], lw["proj_a"], lw["proj_b"], lw["w_out"],
      bias_p, sink_t, lw["ln1_g"], lw["ln1_b"])


def _in_proj_kernel(x_ref, win_ref, u_ref):
    u_ref[...] = jnp.dot(x_ref[...].astype(BF16), win_ref[...], preferred_element_type=F32)


def _in_proj(x, lw):
    n = x.shape[0]
    return pl.pallas_call(
        _in_proj_kernel,
        grid=(1,),
        in_specs=[pl.BlockSpec(x.shape, lambda i: (0, 0)), _layer_resident(lw["w_in"], lw["layer"])],
        out_specs=pl.BlockSpec((n, D_IN), lambda i: (0, 0)),
        out_shape=jax.ShapeDtypeStruct((n, D_IN), F32),
        compiler_params=_cparams("arbitrary"),
        name="decode_in_proj",
    )(x, lw["w_in"])


def _decode_attn_kernel(q_ref, kn_ref, vn_ref, kc_ref, vc_ref, bias_ref, biasn_ref, sink_ref, b_ref, *, g, ntok):
    slabs = 2 * ntok
    rows = slabs * g
    def cache_variants(ref):
        out = []
        for kvh in range(N_KV_HEADS):
            t = ref[:, :, kvh, :].reshape(g * WINDOW, HEAD_DIM)
            z = jnp.zeros_like(t)
            out.append((jnp.concatenate([t, z], axis=1), jnp.concatenate([z, t], axis=1)))
        return out

    kk = cache_variants(kc_ref)
    vv = cache_variants(vc_ref)
    kn = [_half_lane_variants(kn_ref[t]) for t in range(ntok)]
    vn = [_half_lane_variants(vn_ref[t]) for t in range(ntok)]
    per_row = lambda a: jnp.concatenate([a] * slabs, axis=0)
    row_seq = lax.rem(lax.broadcasted_iota(jnp.int32, (rows, WINDOW), 0), g)
    diag = (lax.rem(lax.broadcasted_iota(jnp.int32, (rows, g * WINDOW), 0), g)
            == lax.broadcasted_iota(jnp.int32, (rows, g * WINDOW), 1) // WINDOW)
    for kvh in range(N_KV_HEADS):
        c0 = kvh * 2 * LANES
        q2f = jnp.concatenate([q_ref[t, :, c0 + p * LANES:c0 + (p + 1) * LANES]
                               for p in range(2) for t in range(ntok)], axis=0)
        q2 = q2f.astype(BF16)
        acc = None
        for par in range(2):
            full = lax.dot_general(q2, kk[kvh][par].astype(BF16), (((1,), (1,)), ((), ())),
                                   preferred_element_type=F32)
            s = jnp.zeros((rows, WINDOW), F32)
            for n in range(g):
                s = jnp.where(row_seq == n, full[:, n * WINDOW:(n + 1) * WINDOW], s)
            s = s * ATTN_SCALE + bias_ref[kvh, par]
            s_new = [jnp.sum(q2f * per_row(kn[t][kvh][par]), axis=-1, keepdims=True) * ATTN_SCALE
                     + biasn_ref[kvh, par, t] for t in range(ntok)]
            sink = sink_ref[kvh, par]
            m = jnp.maximum(jnp.max(s, axis=-1, keepdims=True), sink)
            for t in range(ntok):
                m = jnp.maximum(m, s_new[t])
            p = jnp.exp(s - m)
            den = jnp.sum(p, axis=-1, keepdims=True) + jnp.exp(sink - m)
            p_bd = jnp.where(diag, jnp.concatenate([p] * g, axis=1), 0.0).astype(BF16)
            o = jnp.dot(p_bd, vv[kvh][par].astype(BF16), preferred_element_type=F32)
            for t in range(ntok):
                p_new = jnp.exp(s_new[t] - m)
                den = den + p_new
                o = o + p_new * per_row(vn[t][kvh][par])
            o = o / den
            acc = o if acc is None else acc + o
        for p in range(2):
            for t in range(ntok):
                r0 = (p * ntok + t) * g
                b_ref[t, :, c0 + p * LANES:c0 + (p + 1) * LANES] = acc[r0:r0 + g]


def _decode_attn(u3, cache_k, cache_v, bias_d, biasn_d, sink_d, *, g, layer):
    ntok, nseq, _ = u3.shape
    cache = pl.BlockSpec((None, g, WINDOW, N_KV_HEADS, HEAD_DIM), lambda i: (layer, i, 0, 0, 0))
    kern = functools.partial(_decode_attn_kernel, g=g, ntok=ntok)
    new_cols = lambda w, off: pl.BlockSpec((ntok, g, w), lambda i: (0, i, off // w))
    return pl.pallas_call(
        kern,
        grid=(nseq // g,),
        in_specs=[new_cols(Q_DIM, _O_Q), new_cols(KV_DIM, _O_K), new_cols(KV_DIM, _O_V),
                  cache, cache,
                  _resident(bias_d.shape), _resident(biasn_d.shape), _layer_resident(sink_d, layer)],
        out_specs=pl.BlockSpec((ntok, g, Q_DIM), lambda i: (0, i, 0)),
        out_shape=jax.ShapeDtypeStruct((ntok, nseq, Q_DIM), F32),
        compiler_params=_cparams("arbitrary"),
        name="decode_attn",
    )(u3, u3, u3, cache_k, cache_v, bias_d, biasn_d, sink_d)


def _decode_tail_kernel(x_ref, u_ref, hist_ref, b_ref, pw_ref, ps_ref, pa_ref, pb_ref,
                        wo_ref, g_ref, bt_ref, buf_ref, x1_ref, *, nseq, ntok, start_pos):
    del buf_ref

    def ext_row(r, c0):
        if r < POOL_HIST:
            return hist_ref[r, :, c0:c0 + POOL_GROUP_DIM]
        return u_ref[(r - POOL_HIST) * nseq:(r - POOL_HIST + 1) * nseq, c0:c0 + POOL_GROUP_DIM]

    pooled_rows = []
    for t in range(ntok):
        means = []
        for gi, w in enumerate(POOL_WINDOWS):
            acc = None
            for d in range(w):
                r = POOL_HIST + t - d
                if start_pos - POOL_HIST + r < 0:
                    continue
                term = ext_row(r, gi * POOL_GROUP_DIM)
                acc = term if acc is None else acc + term
            means.append(acc / float(min(w, start_pos + t + 1)))
        pooled_rows.append(jnp.concatenate(means, axis=1))
    pooled = jnp.concatenate(pooled_rows, axis=0) - u_ref[:, :D_POOL]
    x1_ref[...] = _mix_tail(x_ref[...], pooled, b_ref[...], u_ref[:, _O_GA:_O_GB], u_ref[:, _O_GB:], pw_ref[...],
                            ps_ref[...], pa_ref[...], pb_ref[...], wo_ref[...], g_ref[...], bt_ref[...])


def _decode_tail(x, u, hist_t, b, lw, buf, *, nseq, ntok, row_block):
    n = ntok * nseq
    kern = functools.partial(_decode_tail_kernel, nseq=nseq, ntok=ntok, start_pos=PAST_LEN)
    whole = lambda a: pl.BlockSpec(a.shape, lambda i: (0,) * a.ndim)
    layered = lambda a: _layer_resident(a, lw["layer"])
    params = (lw["pool_w"], lw["pool_scale"], lw["proj_a"], lw["proj_b"], lw["w_out"], lw["ln1_g"], lw["ln1_b"])
    ins = (x, u, hist_t, b) + params
    return pl.pallas_call(
        kern,
        grid=(1,),
        in_specs=([whole(x), whole(u), layered(hist_t), whole(b)] + [layered(a) for a in params]
                  + [pl.BlockSpec(memory_space=pl.ANY)]),
        out_specs=pl.BlockSpec((n, D_MODEL), lambda i: (row_block, 0)),
        out_shape=jax.ShapeDtypeStruct(buf.shape, F32),
        input_output_aliases={len(ins): 0},
        compiler_params=_cparams("arbitrary"),
        name="decode_tail",
    )(*ins, buf)


def _router_logits(x, whi, wlo, br):
    xh = x.astype(BF16)
    xl = (x - xh.astype(F32)).astype(BF16)
    return (jnp.dot(xh, whi, preferred_element_type=F32)
            + (jnp.dot(xh, wlo, preferred_element_type=F32)
               + jnp.dot(xl, whi, preferred_element_type=F32))) + br


def _group_logits(logits, lane):
    is_group = jnp.logical_and(lane >= _GROUP_LANE0, lane < _GROUP_LANE0 + N_EXPERT_GROUPS)
    return jnp.where(is_group, logits, NEG_INF)


def _router_kernel(x_ref, whi_ref, wlo_ref, br_ref, meta_ref, cnt_ref, run_scr, *, tr):
    i = pl.program_id(0)

    @pl.when(i == 0)
    def _():
        run_scr[...] = jnp.zeros_like(run_scr)

    logits = _router_logits(x_ref[...], whi_ref[...], wlo_ref[...], br_ref[...])
    lane = lax.broadcasted_iota(jnp.int32, (tr, ROUTER_LANES), 1)
    big = jnp.int32(ROUTER_LANES)

    def masked_argmax(vals):
        vmax = jnp.max(vals, axis=-1, keepdims=True)
        idx = jnp.min(jnp.where(vals == vmax, lane, big), axis=-1, keepdims=True)
        return vmax, idx

    _, glane = masked_argmax(_group_logits(logits, lane))
    gidx = glane - _GROUP_LANE0
    in_group = jnp.logical_and(lane >= gidx * EXPERTS_PER_GROUP, lane < (gidx + 1) * EXPERTS_PER_GROUP)
    el = jnp.where(in_group, logits, NEG_INF)
    _, e1 = masked_argmax(el)
    _, e2 = masked_argmax(jnp.where(lane == e1, NEG_INF, el))
    lo = jnp.minimum(e1, e2) - gidx * EXPERTS_PER_GROUP
    hi = jnp.maximum(e1, e2) - gidx * EXPERTS_PER_GROUP
    pair = lax.shift_right_logical(lo * (2 * EXPERTS_PER_GROUP - 1 - lo), 1) + (hi - lo - 1)
    cls = gidx * PAIRS_PER_GROUP + pair

    oh = jnp.where(lane == cls, 1.0, 0.0)
    rr = lax.broadcasted_iota(jnp.int32, (tr, tr), 0)
    cc = lax.broadcasted_iota(jnp.int32, (tr, tr), 1)
    lower = jnp.where(rr > cc, 1.0, 0.0).astype(BF16)
    before = jnp.dot(lower, oh.astype(BF16), preferred_element_type=F32) + run_scr[...]
    rank = jnp.sum(oh * before, axis=-1, keepdims=True)
    run_scr[...] += jnp.sum(oh, axis=0, keepdims=True)
    cnt_ref[...] = run_scr[...]

    meta_ref[...] = jnp.where(lane == 0, cls.astype(F32), jnp.where(lane == 1, rank, 0.0))


def _router(x, lw, *, tr=512):
    t = x.shape[0]
    kern = functools.partial(_router_kernel, tr=tr)
    return pl.pallas_call(
        kern,
        grid=(t // tr,),
        in_specs=[pl.BlockSpec((tr, D_MODEL), lambda i: (i, 0)),
                  _layer_resident(lw["router_hi"], lw["layer"]), _layer_resident(lw["router_lo"], lw["layer"]),
                  _layer_resident(lw["router_b"], lw["layer"])],
        out_specs=[pl.BlockSpec((tr, ROUTER_LANES), lambda i: (i, 0)),
                   pl.BlockSpec((1, ROUTER_LANES), lambda i: (0, 0))],
        out_shape=[jax.ShapeDtypeStruct((t, ROUTER_LANES), F32),
                   jax.ShapeDtypeStruct((1, ROUTER_LANES), F32)],
        scratch_shapes=[pltpu.VMEM((1, ROUTER_LANES), F32)],
        compiler_params=_cparams("arbitrary"),
        name="router",
    )(x, lw["router_hi"], lw["router_lo"], lw["router_b"])


def _rows_copy(src, dst, sem):
    return pltpu.make_async_copy(src, dst, sem)


def _dispatch_kernel(slot_ref, x_ref, xs_ref, stage, sems, *, td):
    i = pl.program_id(0)
    n = pl.num_programs(0)
    sb = lax.rem(i, 2)
    x = x_ref[...]
    for j in range(ROW_CHUNKS):
        stage[sb, pl.ds(j, td, stride=ROW_CHUNKS), :] = x[:, j * LANES:(j + 1) * LANES]

    def issue(r, carry):
        src = stage.at[sb, pl.ds(pl.multiple_of(r * ROW_CHUNKS, ROW_CHUNKS), ROW_CHUNKS)]
        s = slot_ref[0, 0, r]
        dst = xs_ref.at[pl.ds(pl.multiple_of(s * ROW_CHUNKS, ROW_CHUNKS), ROW_CHUNKS)]
        _rows_copy(src, dst, sems.at[sb]).start()
        return carry

    lax.fori_loop(0, td, issue, 0)

    def drain(buf):
        _rows_copy(stage.at[buf], xs_ref.at[pl.ds(0, td * ROW_CHUNKS)], sems.at[buf]).wait()

    @pl.when(i > 0)
    def _():
        drain(1 - sb)

    @pl.when(i == n - 1)
    def _():
        drain(sb)


def _dispatch(x, slots, n_slots, *, td=256):
    t = x.shape[0]
    slots3 = slots.reshape(t // td, 1, td)
    kern = functools.partial(_dispatch_kernel, td=td)
    return pl.pallas_call(
        kern,
        grid=(t // td,),
        in_specs=[pl.BlockSpec((1, 1, td), lambda i: (i, 0, 0), memory_space=pltpu.SMEM),
                  pl.BlockSpec((td, D_MODEL), lambda i: (i, 0))],
        out_specs=pl.BlockSpec(memory_space=pl.ANY),
        out_shape=jax.ShapeDtypeStruct((n_slots * ROW_CHUNKS, LANES), F32),
        scratch_shapes=[pltpu.VMEM((2, td * ROW_CHUNKS, LANES), F32), pltpu.SemaphoreType.DMA((2,))],
        compiler_params=_cparams("arbitrary"),
        name="moe_dispatch",
    )(slots3, x)


def _expert_kernel(ea_ref, eb_ref, nv_ref, xs_ref, rhi_ref, rlo_ref, rb_ref, wg_ref, wu_ref, wd_ref, os_ref,
                   wg_bf, wu_bf, wd_bf):
    i = pl.program_id(0)
    valid = i < nv_ref[0]
    ea = ea_ref[i]
    eb = eb_ref[i]
    group = ea // EXPERTS_PER_GROUP
    prev_group = ea_ref[jnp.maximum(i - 1, 0)] // EXPERTS_PER_GROUP
    new_group = jnp.logical_or(i == 0, group != prev_group)

    @pl.when(jnp.logical_and(valid, new_group))
    def _():
        for e in range(EXPERTS_PER_GROUP):
            wg_bf[e] = wg_ref[e].astype(BF16)
            wu_bf[e] = wu_ref[e].astype(BF16)
            wd_bf[e] = wd_ref[e].astype(BF16)

    @pl.when(valid)
    def _():
        x = jnp.concatenate([xs_ref[pl.ds(j, EXPERT_TILE, stride=ROW_CHUNKS), :] for j in range(ROW_CHUNKS)],
                            axis=1)
        logits = _router_logits(x, rhi_ref[...], rlo_ref[...], rb_ref[...])
        lane = lax.broadcasted_iota(jnp.int32, logits.shape, 1)
        pick = lambda l: jnp.sum(jnp.where(lane == l, logits, 0.0), axis=-1, keepdims=True)
        glog = _group_logits(logits, lane)
        gmax = jnp.max(glog, axis=-1, keepdims=True)
        gw = jnp.exp(pick(_GROUP_LANE0 + group) - gmax) / jnp.sum(jnp.exp(glog - gmax), axis=-1, keepdims=True)
        la = pick(ea)
        lb = pick(eb)
        m = jnp.maximum(la, lb)
        pa = jnp.exp(la - m)
        pb = jnp.exp(lb - m)
        scale = gw / (pa + pb)

        xt = x.astype(BF16)
        y = None
        for e, w in ((ea, pa * scale), (eb, pb * scale)):
            el = e - group * EXPERTS_PER_GROUP
            gate = jnp.dot(xt, wg_bf[el], preferred_element_type=F32)
            up = jnp.dot(xt, wu_bf[el], preferred_element_type=F32)
            h = (jax.nn.silu(gate) * up).astype(BF16)
            term = w * jnp.dot(h, wd_bf[el], preferred_element_type=F32)
            y = term if y is None else y + term
        for j in range(ROW_CHUNKS):
            os_ref[pl.ds(j, EXPERT_TILE, stride=ROW_CHUNKS), :] = y[:, j * LANES:(j + 1) * LANES]


def _experts(xs, tile_ea, tile_eb, n_valid, lw):
    layer = lw["layer"]
    n_tiles = xs.shape[0] // (EXPERT_TILE * ROW_CHUNKS)
    rows = EXPERT_TILE * ROW_CHUNKS
    act = pl.BlockSpec((rows, LANES), lambda i, ea, eb, nv: (jnp.minimum(i, nv[0] - 1), 0))
    group_of = lambda ea, i: ea[i] // EXPERTS_PER_GROUP
    group_w = lambda r, c: pl.BlockSpec((None, EXPERTS_PER_GROUP, r, c),
                                        lambda i, ea, eb, nv: (layer, group_of(ea, i), 0, 0),
                                        pipeline_mode=pl.Buffered(1))
    router = lambda a: pl.BlockSpec((None,) + a.shape[1:], lambda i, ea, eb, nv: (layer, 0, 0),
                                    pipeline_mode=pl.Buffered(1))
    grid_spec = pltpu.PrefetchScalarGridSpec(
        num_scalar_prefetch=3,
        grid=(n_tiles,),
        in_specs=[act, router(lw["router_hi"]), router(lw["router_lo"]), router(lw["router_b"]),
                  group_w(D_MODEL, D_EXPERT), group_w(D_MODEL, D_EXPERT), group_w(D_EXPERT, D_MODEL)],
        out_specs=act,
        scratch_shapes=[pltpu.VMEM((EXPERTS_PER_GROUP, D_MODEL, D_EXPERT), BF16),
                        pltpu.VMEM((EXPERTS_PER_GROUP, D_MODEL, D_EXPERT), BF16),
                        pltpu.VMEM((EXPERTS_PER_GROUP, D_EXPERT, D_MODEL), BF16)],
    )
    return pl.pallas_call(
        _expert_kernel,
        grid_spec=grid_spec,
        out_shape=jax.ShapeDtypeStruct(xs.shape, F32),
        compiler_params=_cparams("arbitrary"),
        name="moe_experts",
    )(tile_ea, tile_eb, n_valid, xs, lw["router_hi"], lw["router_lo"], lw["router_b"],
      lw["w_gate"], lw["w_up"], lw["w_down"])


def _gather_rows_issue(slot_ref, os_ref, gbuf, sems, buf, tc):
    def issue(r, carry):
        s = slot_ref[0, 0, r]
        src = os_ref.at[pl.ds(pl.multiple_of(s * ROW_CHUNKS, ROW_CHUNKS), ROW_CHUNKS)]
        dst = gbuf.at[buf, pl.ds(pl.multiple_of(r * ROW_CHUNKS, ROW_CHUNKS), ROW_CHUNKS)]
        _rows_copy(src, dst, sems.at[buf]).start()
        return carry

    lax.fori_loop(0, tc, issue, 0)


def _combine_kernel(slot_ref, slot_next_ref, x_ref, g_ref, bt_ref, os_ref, xp_ref, xd_ref,
                    gbuf, sems, *, tc, n_first):
    i = pl.program_id(0)
    n = pl.num_programs(0)
    sb = lax.rem(i, 2)

    @pl.when(i == 0)
    def _():
        _gather_rows_issue(slot_ref, os_ref, gbuf, sems, 0, tc)

    @pl.when(i + 1 < n)
    def _():
        _gather_rows_issue(slot_next_ref, os_ref, gbuf, sems, 1 - sb, tc)

    _rows_copy(os_ref.at[pl.ds(0, tc * ROW_CHUNKS)], gbuf.at[sb], sems.at[sb]).wait()
    y = jnp.concatenate([gbuf[sb, pl.ds(j, tc, stride=ROW_CHUNKS), :] for j in range(ROW_CHUNKS)], axis=1)
    x2 = _layer_norm(DEEPNORM_ALPHA * x_ref[...] + y, g_ref[...], bt_ref[...])

    @pl.when(i < n_first)
    def _():
        xp_ref[...] = x2

    @pl.when(i >= n_first)
    def _():
        xd_ref[...] = x2


def _combine(x, slots, os_, lw, *, n_prompt, tc=256):
    t = x.shape[0]
    nblk = t // tc
    n_first = n_prompt // tc
    slots3 = slots.reshape(nblk, 1, tc)
    kern = functools.partial(_combine_kernel, tc=tc, n_first=n_first)
    smem_tile = lambda f: pl.BlockSpec((1, 1, tc), f, memory_space=pltpu.SMEM)
    return pl.pallas_call(
        kern,
        grid=(nblk,),
        in_specs=[smem_tile(lambda i: (i, 0, 0)),
                  smem_tile(lambda i: (jnp.minimum(i + 1, nblk - 1), 0, 0)),
                  pl.BlockSpec((tc, D_MODEL), lambda i: (i, 0)),
                  _layer_resident(lw["ln2_g"], lw["layer"]), _layer_resident(lw["ln2_b"], lw["layer"]),
                  pl.BlockSpec(memory_space=pl.ANY)],
        out_specs=[pl.BlockSpec((tc, D_MODEL), lambda i: (jnp.minimum(i, n_first - 1), 0)),
                   pl.BlockSpec((tc, D_MODEL), lambda i: (jnp.maximum(i - n_first, 0), 0))],
        out_shape=[jax.ShapeDtypeStruct((n_prompt, D_MODEL), F32),
                   jax.ShapeDtypeStruct((t - n_prompt, D_MODEL), F32)],
        scratch_shapes=[pltpu.VMEM((2, tc * ROW_CHUNKS, LANES), F32), pltpu.SemaphoreType.DMA((2,))],
        compiler_params=_cparams("arbitrary"),
        name="moe_combine",
    )(slots3, slots3, x, lw["ln2_g"], lw["ln2_b"], os_)


def _moe(x, lw, *, n_prompt):
    t = x.shape[0]
    n_tiles = t // EXPERT_TILE + N_CLASSES
    meta, cnt = _router(x, lw)
    cls = meta[:, 0].astype(jnp.int32)
    rank = meta[:, 1].astype(jnp.int32)
    counts = cnt[0, :N_CLASSES].astype(jnp.int32)
    seg_tiles = (counts + EXPERT_TILE - 1) // EXPERT_TILE
    tile_ends = jnp.cumsum(seg_tiles)
    seg_start = (tile_ends - seg_tiles) * EXPERT_TILE
    class_ids = jnp.arange(N_CLASSES, dtype=jnp.int32)
    slots = jnp.sum(jnp.where(cls[:, None] == class_ids, seg_start, 0), axis=-1) + rank
    n_valid = tile_ends[-1:]
    tile_ids = jnp.minimum(jnp.arange(n_tiles, dtype=jnp.int32), n_valid[0] - 1)
    tile_class = jnp.sum((tile_ends[None, :] <= tile_ids[:, None]).astype(jnp.int32), axis=1)
    of_class = tile_class[:, None] == class_ids
    class_experts = jnp.asarray(_CLASS_EXPERTS, dtype=jnp.int32)
    tile_ea = jnp.sum(jnp.where(of_class, class_experts[:, 0], 0), axis=-1)
    tile_eb = jnp.sum(jnp.where(of_class, class_experts[:, 1], 0), axis=-1)
    xs = _dispatch(x, slots, n_tiles * EXPERT_TILE)
    os_ = _experts(xs, tile_ea, tile_eb, n_valid.astype(jnp.int32), lw)
    return _combine(x, slots, os_, lw, n_prompt=n_prompt)


def _t5_bucket(dist):
    n = jnp.maximum(dist, 0)
    max_exact = REL_BUCKETS // 2
    nf = jnp.maximum(n, 1).astype(F32)
    large = max_exact + (jnp.log(nf / max_exact) / math.log(REL_MAX_DISTANCE / max_exact)
                         * (REL_BUCKETS - max_exact)).astype(jnp.int32)
    large = jnp.minimum(large, REL_BUCKETS - 1)
    return jnp.where(n < max_exact, n, large)


def _head_rows(per_head, rows_of):
    out = []
    for kvh in range(N_KV_HEADS):
        out.append(jnp.stack([jnp.concatenate([jnp.repeat(per_head[..., h, :, :], rep, axis=-2)
                                               for h, rep in rows_of(kvh, par)], axis=-2)
                              for par in range(2)], axis=-3))
    return jnp.stack(out, axis=-4)


def _prompt_rows(kvh, par):
    return [(GROUP * kvh + 2 * p + par, 1) for p in range(2)]


def _head_bias(rel_bias, dist, valid):
    bias = jnp.where(valid[..., None], rel_bias[_t5_bucket(dist)].astype(F32), NEG_INF)
    return jnp.moveaxis(bias, -1, 0)


def _prompt_bias(rel_bias):
    period = 4 * WINDOW
    off = jnp.arange(period)
    off = jnp.where(off < 2 * WINDOW, off, off - period)
    dist = WINDOW - off
    per_off = _head_bias(rel_bias, dist, (dist >= 0) & (dist <= WINDOW))
    tiled = jnp.tile(per_off, (1, WINDOW))[:, :WINDOW * (period - 1)]
    return tiled.reshape(N_HEADS, WINDOW, period - 1)[:, :, :2 * WINDOW]


def _attention_tables(rel_bias, sinks, ntok, g):
    bias_p = _head_rows(_prompt_bias(rel_bias), _prompt_rows)

    decode_rows = lambda kvh, par: [(GROUP * kvh + 2 * p + par, g) for p in range(2)]
    tq = jnp.arange(ntok)[:, None]
    kpos_old = jnp.arange(WINDOW)[None, :] - WINDOW
    kpos_new = jnp.arange(ntok)[None, :]
    tables = []
    for kpos in (kpos_old, kpos_new):
        d = tq - kpos
        per_head = _head_bias(rel_bias, d, (d >= 0) & (d <= WINDOW) & (PAST_LEN + kpos >= 0))
        tables.append(_head_rows(per_head, decode_rows))
    bias_d = tables[0]
    biasn_d = jnp.moveaxis(tables[1], -1, 2)[..., None]

    s = sinks.astype(F32)[:, :, None, None]
    sink_p = _head_rows(jnp.broadcast_to(s, s.shape[:2] + (WINDOW, 1)), _prompt_rows)
    sink_d = _head_rows(jnp.broadcast_to(s, s.shape[:2] + (ntok, 1)), decode_rows)
    return bias_p, bias_d, biasn_d, sink_p, sink_d


def _stacked_weights(w_in, pool_w, pool_scale, proj_a, proj_b, w_out, ln1_g, ln1_b, router_group_w,
                     router_group_b, router_expert_w, router_expert_b, expert_w_gate, expert_w_up,
                     expert_w_down, ln2_g, ln2_b):
    depth = w_in.shape[0]
    same_group = jnp.eye(N_POOL_GROUPS, dtype=bool)[None, :, None, :, None]
    pool_bd = jnp.where(same_group, pool_w[:, :, :, None, :], 0.0).reshape(depth, D_POOL, D_POOL)
    pad = ROUTER_LANES - N_EXPERTS - N_EXPERT_GROUPS
    rw = jnp.concatenate([router_expert_w, router_group_w, jnp.zeros((depth, D_MODEL, pad), F32)], axis=-1)
    rb = jnp.concatenate([router_expert_b, router_group_b, jnp.zeros((depth, pad), F32)], axis=-1)
    r_hi = rw.astype(BF16)
    row = lambda a: a[:, None, :]
    return dict(
        w_in=w_in.astype(BF16), pool_w=pool_bd.astype(BF16), pool_scale=row(pool_scale),
        proj_a=proj_a.astype(BF16), proj_b=proj_b.astype(BF16), w_out=w_out.astype(BF16),
        ln1_g=row(ln1_g), ln1_b=row(ln1_b),
        router_hi=r_hi, router_lo=(rw - r_hi.astype(F32)).astype(BF16), router_b=row(rb),
        w_gate=expert_w_gate, w_up=expert_w_up, w_down=expert_w_down,
        ln2_g=row(ln2_g), ln2_b=row(ln2_b))


def _decode_mixer(xs_tn, lw, cache_k, cache_v, hist_t, tables, buf, *, nseq, ntok, g, row_block):
    bias_d, biasn_d, sink_d = tables
    u = _in_proj(xs_tn, lw)
    u3 = u.reshape(ntok, nseq, D_IN)
    b = _decode_attn(u3, cache_k, cache_v, bias_d, biasn_d, sink_d, g=g, layer=lw["layer"])
    buf = _decode_tail(xs_tn, u, hist_t, b.reshape(ntok * nseq, Q_DIM), lw, buf, nseq=nseq, ntok=ntok,
                       row_block=row_block)
    return buf, u3


def kernel(x_prompt, x_sample, cache_k, cache_v, state_pool, rel_bias, w_in, pool_w, pool_scale, proj_a, proj_b, w_out, attn_sinks, ln1_g, ln1_b, router_group_w, router_group_b, router_expert_w, router_expert_b, expert_w_gate, expert_w_up, expert_w_down, ln2_g, ln2_b):
    bsz, seq, _ = x_prompt.shape
    nseq, ntok, _ = x_sample.shape
    g = 16
    depth = w_in.shape[0]
    n_prompt = bsz * seq
    n_decode = ntok * nseq
    assert n_prompt % n_decode == 0
    bias_p, bias_d, biasn_d, sink_p, sink_d = _attention_tables(rel_bias, attn_sinks, ntok, g)
    weights = _stacked_weights(w_in, pool_w, pool_scale, proj_a, proj_b, w_out, ln1_g, ln1_b, router_group_w,
                               router_group_b, router_expert_w, router_expert_b, expert_w_gate, expert_w_up,
                               expert_w_down, ln2_g, ln2_b)

    xp = x_prompt.reshape(n_prompt, D_MODEL)
    xs_tn = jnp.swapaxes(x_sample, 0, 1).reshape(n_decode, D_MODEL)
    hist_t = jnp.swapaxes(state_pool, 1, 2)
    prompt_state, decode_u = [], []
    for l in range(depth):
        lw = dict(weights, layer=l)
        x1, kst, vst, pst = _prompt_mixer(xp, lw, bias_p, sink_p, bsz=bsz, n_total=n_prompt + n_decode)
        x1, u3 = _decode_mixer(xs_tn, lw, cache_k, cache_v, hist_t, (bias_d, biasn_d, sink_d), x1,
                               nseq=nseq, ntok=ntok, g=g, row_block=n_prompt // n_decode)
        xp, xs_tn = _moe(x1, lw, n_prompt=n_prompt)
        prompt_state.append((kst, vst, pst))
        decode_u.append(u3)

    pk, pv, pp = (jnp.stack(parts) for parts in zip(*prompt_state))
    heads = lambda a: a.reshape(a.shape[:3] + (N_KV_HEADS, HEAD_DIM))

    def shifted(old, col, width):
        new = jnp.stack([jnp.swapaxes(u3[:, :, col:col + width], 0, 1) for u3 in decode_u])
        return jnp.concatenate([old[:, :, ntok:], new.reshape(new.shape[:3] + old.shape[3:])], axis=2)

    y_sample = jnp.swapaxes(xs_tn.reshape(ntok, nseq, D_MODEL), 0, 1)
    return (xp.reshape(bsz, seq, D_MODEL), y_sample, heads(pk), heads(pv), pp[:, :, -POOL_HIST:],
            shifted(cache_k, _O_K, KV_DIM), shifted(cache_v, _O_V, KV_DIM), shifted(state_pool, 0, D_POOL))
```

```python
import functools
import math

import jax
import jax.numpy as jnp
from jax import lax
from jax.experimental import pallas as pl
from jax.experimental.pallas import tpu as pltpu

D_MODEL = 1024
DEPTH = 2
PAST_LEN = 16384
D_POOL = D_MODEL // 2
POOL_WINDOWS = (2, 4, 8, 16)
N_POOL_GROUPS = len(POOL_WINDOWS)
POOL_GROUP_DIM = D_POOL // N_POOL_GROUPS
POOL_HIST = max(POOL_WINDOWS) - 1
HEAD_DIM = 64
N_HEADS = (D_MODEL // 2) // HEAD_DIM
N_KV_HEADS = 2
GROUP = N_HEADS // N_KV_HEADS
Q_DIM = N_HEADS * HEAD_DIM
KV_DIM = N_KV_HEADS * HEAD_DIM
WINDOW = 128
ATTN_SCALE = HEAD_DIM ** -0.5
REL_BUCKETS = 32
REL_MAX_DISTANCE = 128
D_IN = D_POOL + Q_DIM + 2 * KV_DIM + 2 * D_MODEL
N_EXPERT_GROUPS = 4
EXPERTS_PER_GROUP = 8
N_EXPERTS = N_EXPERT_GROUPS * EXPERTS_PER_GROUP
D_EXPERT = D_MODEL // 4
DEEPNORM_ALPHA = (2 * DEPTH) ** 0.25
LN_EPS = 1e-5

LANES = 128
SUBLANES = 8
VMEM_LIMIT_BYTES = 56 * 1024 * 1024

_O_Q = D_POOL
_O_K = _O_Q + Q_DIM
_O_V = _O_K + KV_DIM
_O_GA = _O_V + KV_DIM
_O_GB = _O_GA + D_MODEL

ROW_CHUNKS = D_MODEL // LANES
KEY_PAD = 2 * WINDOW
EXPERT_TILE = 256
DMA_QUEUES = 2
PAIRS_PER_GROUP = EXPERTS_PER_GROUP * (EXPERTS_PER_GROUP - 1) // 2
N_CLASSES = N_EXPERT_GROUPS * PAIRS_PER_GROUP
_CLASS_EXPERTS = tuple((g * EXPERTS_PER_GROUP + a, g * EXPERTS_PER_GROUP + b)
                       for g in range(N_EXPERT_GROUPS)
                       for a in range(EXPERTS_PER_GROUP) for b in range(a + 1, EXPERTS_PER_GROUP))
ROUTER_LANES = LANES
_GROUP_LANE0 = N_EXPERTS
assert EXPERTS_PER_GROUP == SUBLANES and N_EXPERT_GROUPS <= SUBLANES

BF16 = jnp.bfloat16
F32 = jnp.float32
NEG_INF = float("-inf")


def _cparams(*sem):
    return pltpu.CompilerParams(dimension_semantics=sem, vmem_limit_bytes=VMEM_LIMIT_BYTES)


def _resident(shape):
    nd = len(shape)
    return pl.BlockSpec(shape, lambda *_: (0,) * nd, pipeline_mode=pl.Buffered(1))


def _layer_resident(stacked, layer):
    nd = stacked.ndim - 1
    return pl.BlockSpec((None,) + stacked.shape[1:], lambda *_: (layer,) + (0,) * nd,
                        pipeline_mode=pl.Buffered(1))


def _layer_norm(y, g, b):
    mu = jnp.mean(y, axis=-1, keepdims=True)
    yc = y - mu
    var = jnp.mean(yc * yc, axis=-1, keepdims=True)
    return yc * lax.rsqrt(var + LN_EPS) * g + b


def _softmax_pv(s, sink, v_bf16):
    m = jnp.maximum(jnp.max(s, axis=-1, keepdims=True), sink)
    p = jnp.exp(s - m)
    den = jnp.sum(p, axis=-1, keepdims=True) + jnp.exp(sink - m)
    o = jnp.dot(p.astype(BF16), v_bf16, preferred_element_type=F32)
    return o / den


def _half_lane_variants(t):
    lane = lax.broadcasted_iota(jnp.int32, t.shape, 1)
    lo = lane < HEAD_DIM
    tr = pltpu.roll(t, HEAD_DIM, 1)
    z = jnp.zeros_like(t)
    return ((jnp.where(lo, t, z), jnp.where(lo, z, tr)),
            (jnp.where(lo, tr, z), jnp.where(lo, z, t)))


def _mix_tail(x, a_in, b_in, ga, gb, pw, ps, pa, pb, wo, g, bt):
    a = jnp.dot(a_in.astype(BF16), pw, preferred_element_type=F32) * ps
    pa_o = jnp.dot(a.astype(BF16), pa, preferred_element_type=F32)
    pb_o = jnp.dot(b_in.astype(BF16), pb, preferred_element_type=F32)
    merged = jax.nn.sigmoid(ga) * pa_o + jax.nn.sigmoid(gb) * pb_o
    mix = jnp.dot(merged.astype(BF16), wo, preferred_element_type=F32)
    return _layer_norm(DEEPNORM_ALPHA * x + mix, g, bt)


def _prompt_mixer_kernel(x_ref, win_ref, pw_ref, ps_ref, pa_ref, pb_ref, wo_ref, bias_ref, sink_ref,
                         g_ref, bt_ref, x1_ref, kst_ref, vst_ref, pst_ref, kt_scr, v_scr, ext_scr, *, tm):
    i = pl.program_id(1)
    hist = 2 * SUBLANES

    @pl.when(i == 0)
    def _():
        kt_scr[:, :, :WINDOW] = jnp.zeros((4, LANES, WINDOW), BF16)
        v_scr[:, :WINDOW, :] = jnp.zeros((4, WINDOW, LANES), BF16)
        ext_scr[:hist, :] = jnp.zeros((hist, D_POOL), F32)

    @pl.when(i > 0)
    def _():
        kt_scr[:, :, :WINDOW] = kt_scr[:, :, tm:tm + WINDOW]
        v_scr[:, :WINDOW, :] = v_scr[:, tm:tm + WINDOW, :]
        ext_scr[:hist, :] = ext_scr[tm:tm + hist, :]

    x = x_ref[...]
    u = jnp.dot(x.astype(BF16), win_ref[...], preferred_element_type=F32)
    u_pool = u[:, :D_POOL]
    q = u[:, _O_Q:_O_K]
    k = u[:, _O_K:_O_V]
    v = u[:, _O_V:_O_GA]
    ga = u[:, _O_GA:_O_GB]
    gb = u[:, _O_GB:]

    kst_ref[...] = k[tm - WINDOW:, :]
    vst_ref[...] = v[tm - WINDOW:, :]
    pst_ref[...] = u_pool[tm - hist:, :]

    kt = k.T
    kz = jnp.zeros((HEAD_DIM, tm), F32)
    kt_scr[0, :, WINDOW:] = jnp.concatenate([kt[:HEAD_DIM], kz], axis=0).astype(BF16)
    kt_scr[1, :, WINDOW:] = jnp.concatenate([kz, kt[:HEAD_DIM]], axis=0).astype(BF16)
    kt_scr[2, :, WINDOW:] = jnp.concatenate([kt[HEAD_DIM:], kz], axis=0).astype(BF16)
    kt_scr[3, :, WINDOW:] = jnp.concatenate([kz, kt[HEAD_DIM:]], axis=0).astype(BF16)
    vv = _half_lane_variants(v)
    for kvh in range(N_KV_HEADS):
        for par in range(2):
            v_scr[2 * kvh + par, WINDOW:, :] = vv[kvh][par].astype(BF16)

    col = lax.broadcasted_iota(jnp.int32, (2 * WINDOW, 2 * WINDOW), 1)
    no_prev = col < jnp.where(i == 0, WINDOW, 0)
    qb = q.astype(BF16)
    b_rows = []
    for j in range(tm // WINDOW):
        r0 = j * WINDOW
        o_kvh = []
        for kvh in range(N_KV_HEADS):
            c0 = kvh * 2 * LANES
            q2 = jnp.concatenate([qb[r0:r0 + WINDOW, c0:c0 + LANES],
                                  qb[r0:r0 + WINDOW, c0 + LANES:c0 + 2 * LANES]], axis=0)
            acc = None
            for par in range(2):
                s = jnp.dot(q2, kt_scr[2 * kvh + par, :, r0:r0 + 2 * WINDOW], preferred_element_type=F32)
                s = s * ATTN_SCALE + bias_ref[kvh, par]
                if j == 0:
                    s = jnp.where(no_prev, NEG_INF, s)
                o = _softmax_pv(s, sink_ref[kvh, par], v_scr[2 * kvh + par, r0:r0 + 2 * WINDOW, :])
                acc = o if acc is None else acc + o
            o_kvh.append(acc)
        b_rows.append(jnp.concatenate([o_kvh[0][:WINDOW], o_kvh[0][WINDOW:],
                                       o_kvh[1][:WINDOW], o_kvh[1][WINDOW:]], axis=1))
    b = jnp.concatenate(b_rows, axis=0)

    ext_scr[hist:, :] = u_pool
    pos = i * tm + lax.broadcasted_iota(jnp.int32, (tm, 1), 0)
    means = []
    for gi, w in enumerate(POOL_WINDOWS):
        c0 = gi * POOL_GROUP_DIM
        acc = u_pool[:, c0:c0 + POOL_GROUP_DIM]
        for d in range(1, w):
            acc = acc + ext_scr[pl.ds(hist - d, tm), c0:c0 + POOL_GROUP_DIM]
        cnt = jnp.minimum(w, pos + 1).astype(F32)
        means.append(acc / cnt)
    pooled = jnp.concatenate(means, axis=1) - u_pool

    x1_ref[...] = _mix_tail(x, pooled, b, ga, gb, pw_ref[...], ps_ref[...], pa_ref[...], pb_ref[...],
                            wo_ref[...], g_ref[...], bt_ref[...])


def _prompt_mixer(x, lw, bias_p, sink_t, *, bsz, n_total, tm=512):
    seq = x.shape[0] // bsz
    nt = seq // tm
    hist = 2 * SUBLANES
    kern = functools.partial(_prompt_mixer_kernel, tm=tm)
    tile = pl.BlockSpec((tm, D_MODEL), lambda b, i: (b * nt + i, 0))
    state = lambda r, w: pl.BlockSpec((None, r, w), lambda b, i: (b, 0, 0))
    params = (lw["w_in"], lw["pool_w"], lw["pool_scale"], lw["proj_a"], lw["proj_b"], lw["w_out"])
    tail = (sink_t, lw["ln1_g"], lw["ln1_b"])
    return pl.pallas_call(
        kern,
        grid=(bsz, nt),
        in_specs=([tile] + [_layer_resident(a, lw["layer"]) for a in params] + [_resident(bias_p.shape)]
                  + [_layer_resident(a, lw["layer"]) for a in tail]),
        out_specs=[tile, state(WINDOW, KV_DIM), state(WINDOW, KV_DIM), state(hist, D_POOL)],
        out_shape=[jax.ShapeDtypeStruct((n_total, D_MODEL), F32),
                   jax.ShapeDtypeStruct((bsz, WINDOW, KV_DIM), F32),
                   jax.ShapeDtypeStruct((bsz, WINDOW, KV_DIM), F32),
                   jax.ShapeDtypeStruct((bsz, hist, D_POOL), F32)],
        scratch_shapes=[pltpu.VMEM((4, LANES, WINDOW + tm), BF16),
                        pltpu.VMEM((4, WINDOW + tm, LANES), BF16),
                        pltpu.VMEM((hist + tm, D_POOL), F32)],
        compiler_params=_cparams("arbitrary", "arbitrary"),
        name="prompt_mixer",
    )(x, lw["w_in"], lw["pool_w"], lw["pool_scale"], lw["proj_a"], lw["proj_b"], lw["w_out"],
      bias_p, sink_t, lw["ln1_g"], lw["ln1_b"])


def _in_proj_kernel(x_ref, win_ref, u_ref):
    u_ref[...] = jnp.dot(x_ref[...].astype(BF16), win_ref[...], preferred_element_type=F32)


def _in_proj(x, lw):
    n = x.shape[0]
    return pl.pallas_call(
        _in_proj_kernel,
        grid=(1,),
        in_specs=[pl.BlockSpec(x.shape, lambda i: (0, 0)), _layer_resident(lw["w_in"], lw["layer"])],
        out_specs=pl.BlockSpec((n, D_IN), lambda i: (0, 0)),
        out_shape=jax.ShapeDtypeStruct((n, D_IN), F32),
        compiler_params=_cparams("arbitrary"),
        name="decode_in_proj",
    )(x, lw["w_in"])


def _decode_attn_kernel(q_ref, kn_ref, vn_ref, kc_ref, vc_ref, bias_ref, biasn_ref, sink_ref, b_ref, *, g, ntok):
    slabs = 2 * ntok
    rows = slabs * g
    def cache_variants(ref):
        out = []
        for kvh in range(N_KV_HEADS):
            t = ref[:, :, kvh, :].reshape(g * WINDOW, HEAD_DIM)
            z = jnp.zeros_like(t)
            out.append((jnp.concatenate([t, z], axis=1), jnp.concatenate([z, t], axis=1)))
        return out

    kk = cache_variants(kc_ref)
    vv = cache_variants(vc_ref)
    kn = [_half_lane_variants(kn_ref[t]) for t in range(ntok)]
    vn = [_half_lane_variants(vn_ref[t]) for t in range(ntok)]
    per_row = lambda a: jnp.concatenate([a] * slabs, axis=0)
    row_seq = lax.rem(lax.broadcasted_iota(jnp.int32, (rows, WINDOW), 0), g)
    diag = (lax.rem(lax.broadcasted_iota(jnp.int32, (rows, g * WINDOW), 0), g)
            == lax.broadcasted_iota(jnp.int32, (rows, g * WINDOW), 1) // WINDOW)
    for kvh in range(N_KV_HEADS):
        c0 = kvh * 2 * LANES
        q2f = jnp.concatenate([q_ref[t, :, c0 + p * LANES:c0 + (p + 1) * LANES]
                               for p in range(2) for t in range(ntok)], axis=0)
        q2 = q2f.astype(BF16)
        acc = None
        for par in range(2):
            full = lax.dot_general(q2, kk[kvh][par].astype(BF16), (((1,), (1,)), ((), ())),
                                   preferred_element_type=F32)
            s = jnp.zeros((rows, WINDOW), F32)
            for n in range(g):
                s = jnp.where(row_seq == n, full[:, n * WINDOW:(n + 1) * WINDOW], s)
            s = s * ATTN_SCALE + bias_ref[kvh, par]
            s_new = [jnp.sum(q2f * per_row(kn[t][kvh][par]), axis=-1, keepdims=True) * ATTN_SCALE
                     + biasn_ref[kvh, par, t] for t in range(ntok)]
            sink = sink_ref[kvh, par]
            m = jnp.maximum(jnp.max(s, axis=-1, keepdims=True), sink)
            for t in range(ntok):
                m = jnp.maximum(m, s_new[t])
            p = jnp.exp(s - m)
            den = jnp.sum(p, axis=-1, keepdims=True) + jnp.exp(sink - m)
            p_bd = jnp.where(diag, jnp.concatenate([p] * g, axis=1), 0.0).astype(BF16)
            o = jnp.dot(p_bd, vv[kvh][par].astype(BF16), preferred_element_type=F32)
            for t in range(ntok):
                p_new = jnp.exp(s_new[t] - m)
                den = den + p_new
                o = o + p_new * per_row(vn[t][kvh][par])
            o = o / den
            acc = o if acc is None else acc + o
        for p in range(2):
            for t in range(ntok):
                r0 = (p * ntok + t) * g
                b_ref[t, :, c0 + p * LANES:c0 + (p + 1) * LANES] = acc[r0:r0 + g]


def _decode_attn(u3, cache_k, cache_v, bias_d, biasn_d, sink_d, *, g, layer):
    ntok, nseq, _ = u3.shape
    cache = pl.BlockSpec((None, g, WINDOW, N_KV_HEADS, HEAD_DIM), lambda i: (layer, i, 0, 0, 0))
    kern = functools.partial(_decode_attn_kernel, g=g, ntok=ntok)
    new_cols = lambda w, off: pl.BlockSpec((ntok, g, w), lambda i: (0, i, off // w))
    return pl.pallas_call(
        kern,
        grid=(nseq // g,),
        in_specs=[new_cols(Q_DIM, _O_Q), new_cols(KV_DIM, _O_K), new_cols(KV_DIM, _O_V),
                  cache, cache,
                  _resident(bias_d.shape), _resident(biasn_d.shape), _layer_resident(sink_d, layer)],
        out_specs=pl.BlockSpec((ntok, g, Q_DIM), lambda i: (0, i, 0)),
        out_shape=jax.ShapeDtypeStruct((ntok, nseq, Q_DIM), F32),
        compiler_params=_cparams("arbitrary"),
        name="decode_attn",
    )(u3, u3, u3, cache_k, cache_v, bias_d, biasn_d, sink_d)


def _decode_tail_kernel(x_ref, u_ref, hist_ref, b_ref, pw_ref, ps_ref, pa_ref, pb_ref,
                        wo_ref, g_ref, bt_ref, buf_ref, x1_ref, *, nseq, ntok, start_pos):
    del buf_ref

    def ext_row(r, c0):
        if r < POOL_HIST:
            return hist_ref[r, :, c0:c0 + POOL_GROUP_DIM]
        return u_ref[(r - POOL_HIST) * nseq:(r - POOL_HIST + 1) * nseq, c0:c0 + POOL_GROUP_DIM]

    pooled_rows = []
    for t in range(ntok):
        means = []
        for gi, w in enumerate(POOL_WINDOWS):
            acc = None
            for d in range(w):
                r = POOL_HIST + t - d
                if start_pos - POOL_HIST + r < 0:
                    continue
                term = ext_row(r, gi * POOL_GROUP_DIM)
                acc = term if acc is None else acc + term
            means.append(acc / float(min(w, start_pos + t + 1)))
        pooled_rows.append(jnp.concatenate(means, axis=1))
    pooled = jnp.concatenate(pooled_rows, axis=0) - u_ref[:, :D_POOL]
    x1_ref[...] = _mix_tail(x_ref[...], pooled, b_ref[...], u_ref[:, _O_GA:_O_GB], u_ref[:, _O_GB:], pw_ref[...],
                            ps_ref[...], pa_ref[...], pb_ref[...], wo_ref[...], g_ref[...], bt_ref[...])


def _decode_tail(x, u, hist_t, b, lw, buf, *, nseq, ntok, row_block):
    n = ntok * nseq
    kern = functools.partial(_decode_tail_kernel, nseq=nseq, ntok=ntok, start_pos=PAST_LEN)
    whole = lambda a: pl.BlockSpec(a.shape, lambda i: (0,) * a.ndim)
    layered = lambda a: _layer_resident(a, lw["layer"])
    params = (lw["pool_w"], lw["pool_scale"], lw["proj_a"], lw["proj_b"], lw["w_out"], lw["ln1_g"], lw["ln1_b"])
    ins = (x, u, hist_t, b) + params
    return pl.pallas_call(
        kern,
        grid=(1,),
        in_specs=([whole(x), whole(u), layered(hist_t), whole(b)] + [layered(a) for a in params]
                  + [pl.BlockSpec(memory_space=pl.ANY)]),
        out_specs=pl.BlockSpec((n, D_MODEL), lambda i: (row_block, 0)),
        out_shape=jax.ShapeDtypeStruct(buf.shape, F32),
        input_output_aliases={len(ins): 0},
        compiler_params=_cparams("arbitrary"),
        name="decode_tail",
    )(*ins, buf)


def _router_logits(x, whi, wlo, br):
    xh = x.astype(BF16)
    xl = (x - xh.astype(F32)).astype(BF16)
    return (jnp.dot(xh, whi, preferred_element_type=F32)
            + (jnp.dot(xh, wlo, preferred_element_type=F32)
               + jnp.dot(xl, whi, preferred_element_type=F32))) + br


def _group_logits(logits, lane):
    is_group = jnp.logical_and(lane >= _GROUP_LANE0, lane < _GROUP_LANE0 + N_EXPERT_GROUPS)
    return jnp.where(is_group, logits, NEG_INF)


def _router_kernel(x_ref, whi_ref, wlo_ref, br_ref, meta_ref, cnt_ref, run_scr, *, tr):
    i = pl.program_id(0)

    @pl.when(i == 0)
    def _():
        run_scr[...] = jnp.zeros_like(run_scr)

    lt = _router_logits(x_ref[...], whi_ref[...], wlo_ref[...], br_ref[...]).T
    row = lax.broadcasted_iota(jnp.int32, (EXPERTS_PER_GROUP, tr), 0)
    none = jnp.int32(EXPERTS_PER_GROUP)

    def argmax_rows(vals):
        vmax = jnp.max(vals, axis=0, keepdims=True)
        return jnp.min(jnp.where(vals == vmax, row, none), axis=0, keepdims=True)

    glog = jnp.where(row < N_EXPERT_GROUPS, lt[_GROUP_LANE0:_GROUP_LANE0 + EXPERTS_PER_GROUP], NEG_INF)
    gidx = argmax_rows(glog)
    el = lt[:EXPERTS_PER_GROUP]
    for g in range(1, N_EXPERT_GROUPS):
        el = jnp.where(gidx == g, lt[g * EXPERTS_PER_GROUP:(g + 1) * EXPERTS_PER_GROUP], el)
    e1 = argmax_rows(el)
    e2 = argmax_rows(jnp.where(row == e1, NEG_INF, el))
    lo = jnp.minimum(e1, e2)
    hi = jnp.maximum(e1, e2)
    pair = lax.shift_right_logical(lo * (2 * EXPERTS_PER_GROUP - 1 - lo), 1) + (hi - lo - 1)
    cls = gidx * PAIRS_PER_GROUP + pair

    oh = jnp.where(lax.broadcasted_iota(jnp.int32, (ROUTER_LANES, tr), 0) == cls, 1.0, 0.0)
    rr = lax.broadcasted_iota(jnp.int32, (tr, tr), 0)
    cc = lax.broadcasted_iota(jnp.int32, (tr, tr), 1)
    earlier = jnp.where(rr < cc, 1.0, 0.0).astype(BF16)
    before = jnp.dot(oh.astype(BF16), earlier, preferred_element_type=F32) + run_scr[...]
    rank = jnp.sum(oh * before, axis=0, keepdims=True)
    run_scr[...] += jnp.sum(oh, axis=1, keepdims=True)
    cnt_ref[...] = run_scr[...]

    meta_ref[...] = jnp.where(row == 0, cls.astype(F32), jnp.where(row == 1, rank, 0.0))


def _router(x, lw, *, tr=512):
    t = x.shape[0]
    kern = functools.partial(_router_kernel, tr=tr)
    return pl.pallas_call(
        kern,
        grid=(t // tr,),
        in_specs=[pl.BlockSpec((tr, D_MODEL), lambda i: (i, 0)),
                  _layer_resident(lw["router_hi"], lw["layer"]), _layer_resident(lw["router_lo"], lw["layer"]),
                  _layer_resident(lw["router_b"], lw["layer"])],
        out_specs=[pl.BlockSpec((None, SUBLANES, tr), lambda i: (i, 0, 0)),
                   pl.BlockSpec((ROUTER_LANES, 1), lambda i: (0, 0))],
        out_shape=[jax.ShapeDtypeStruct((t // tr, SUBLANES, tr), F32),
                   jax.ShapeDtypeStruct((ROUTER_LANES, 1), F32)],
        scratch_shapes=[pltpu.VMEM((ROUTER_LANES, 1), F32)],
        compiler_params=_cparams("arbitrary"),
        name="router",
    )(x, lw["router_hi"], lw["router_lo"], lw["router_b"])


def _rows_copy(src, dst, sem):
    return pltpu.make_async_copy(src, dst, sem)


def _dispatch_kernel(slot_ref, x_ref, xs_ref, stage, sems, *, td):
    i = pl.program_id(0)
    n = pl.num_programs(0)
    sb = lax.rem(i, 2)
    x = x_ref[...]
    for j in range(ROW_CHUNKS):
        stage[sb, pl.ds(j, td, stride=ROW_CHUNKS), :] = x[:, j * LANES:(j + 1) * LANES]

    def issue(r2, carry):
        for q in range(DMA_QUEUES):
            r = r2 * DMA_QUEUES + q
            src = stage.at[sb, pl.ds(pl.multiple_of(r * ROW_CHUNKS, ROW_CHUNKS), ROW_CHUNKS)]
            s = slot_ref[0, 0, r]
            dst = xs_ref.at[pl.ds(pl.multiple_of(s * ROW_CHUNKS, ROW_CHUNKS), ROW_CHUNKS)]
            _rows_copy(src, dst, sems.at[sb]).start(priority=q)
        return carry

    lax.fori_loop(0, td // DMA_QUEUES, issue, 0)

    def drain(buf):
        _rows_copy(stage.at[buf], xs_ref.at[pl.ds(0, td * ROW_CHUNKS)], sems.at[buf]).wait()

    @pl.when(i > 0)
    def _():
        drain(1 - sb)

    @pl.when(i == n - 1)
    def _():
        drain(sb)


def _dispatch(x, slots, n_slots, *, td=256):
    t = x.shape[0]
    slots3 = slots.reshape(t // td, 1, td)
    kern = functools.partial(_dispatch_kernel, td=td)
    return pl.pallas_call(
        kern,
        grid=(t // td,),
        in_specs=[pl.BlockSpec((1, 1, td), lambda i: (i, 0, 0), memory_space=pltpu.SMEM),
                  pl.BlockSpec((td, D_MODEL), lambda i: (i, 0))],
        out_specs=pl.BlockSpec(memory_space=pl.ANY),
        out_shape=jax.ShapeDtypeStruct((n_slots * ROW_CHUNKS, LANES), F32),
        scratch_shapes=[pltpu.VMEM((2, td * ROW_CHUNKS, LANES), F32), pltpu.SemaphoreType.DMA((2,))],
        compiler_params=_cparams("arbitrary"),
        name="moe_dispatch",
    )(slots3, x)


def _expert_kernel(ea_ref, eb_ref, nv_ref, xs_ref, rhi_ref, rlo_ref, rb_ref, wg_ref, wu_ref, wd_ref, os_ref,
                   wg_bf, wu_bf, wd_bf):
    i = pl.program_id(0)
    valid = i < nv_ref[0]
    ea = ea_ref[i]
    eb = eb_ref[i]
    group = ea // EXPERTS_PER_GROUP
    prev_group = ea_ref[jnp.maximum(i - 1, 0)] // EXPERTS_PER_GROUP
    new_group = jnp.logical_or(i == 0, group != prev_group)

    @pl.when(jnp.logical_and(valid, new_group))
    def _():
        for e in range(EXPERTS_PER_GROUP):
            wg_bf[e] = wg_ref[e].astype(BF16)
            wu_bf[e] = wu_ref[e].astype(BF16)
            wd_bf[e] = wd_ref[e].astype(BF16)

    @pl.when(valid)
    def _():
        x = jnp.concatenate([xs_ref[pl.ds(j, EXPERT_TILE, stride=ROW_CHUNKS), :] for j in range(ROW_CHUNKS)],
                            axis=1)
        logits = _router_logits(x, rhi_ref[...], rlo_ref[...], rb_ref[...])
        lane = lax.broadcasted_iota(jnp.int32, logits.shape, 1)
        pick = lambda l: jnp.sum(jnp.where(lane == l, logits, 0.0), axis=-1, keepdims=True)
        glog = _group_logits(logits, lane)
        gmax = jnp.max(glog, axis=-1, keepdims=True)
        gw = jnp.exp(pick(_GROUP_LANE0 + group) - gmax) / jnp.sum(jnp.exp(glog - gmax), axis=-1, keepdims=True)
        la = pick(ea)
        lb = pick(eb)
        m = jnp.maximum(la, lb)
        pa = jnp.exp(la - m)
        pb = jnp.exp(lb - m)
        scale = gw / (pa + pb)

        xt = x.astype(BF16)
        y = None
        for e, w in ((ea, pa * scale), (eb, pb * scale)):
            el = e - group * EXPERTS_PER_GROUP
            gate = jnp.dot(xt, wg_bf[el], preferred_element_type=F32)
            up = jnp.dot(xt, wu_bf[el], preferred_element_type=F32)
            h = (jax.nn.silu(gate) * up).astype(BF16)
            term = w * jnp.dot(h, wd_bf[el], preferred_element_type=F32)
            y = term if y is None else y + term
        for j in range(ROW_CHUNKS):
            os_ref[pl.ds(j, EXPERT_TILE, stride=ROW_CHUNKS), :] = y[:, j * LANES:(j + 1) * LANES]


def _experts(xs, tile_ea, tile_eb, n_valid, lw):
    layer = lw["layer"]
    n_tiles = xs.shape[0] // (EXPERT_TILE * ROW_CHUNKS)
    rows = EXPERT_TILE * ROW_CHUNKS
    act = pl.BlockSpec((rows, LANES), lambda i, ea, eb, nv: (jnp.minimum(i, nv[0] - 1), 0))
    group_of = lambda ea, i: ea[i] // EXPERTS_PER_GROUP
    group_w = lambda r, c: pl.BlockSpec((None, EXPERTS_PER_GROUP, r, c),
                                        lambda i, ea, eb, nv: (layer, group_of(ea, i), 0, 0),
                                        pipeline_mode=pl.Buffered(1))
    router = lambda a: pl.BlockSpec((None,) + a.shape[1:], lambda i, ea, eb, nv: (layer, 0, 0),
                                    pipeline_mode=pl.Buffered(1))
    grid_spec = pltpu.PrefetchScalarGridSpec(
        num_scalar_prefetch=3,
        grid=(n_tiles,),
        in_specs=[act, router(lw["router_hi"]), router(lw["router_lo"]), router(lw["router_b"]),
                  group_w(D_MODEL, D_EXPERT), group_w(D_MODEL, D_EXPERT), group_w(D_EXPERT, D_MODEL)],
        out_specs=act,
        scratch_shapes=[pltpu.VMEM((EXPERTS_PER_GROUP, D_MODEL, D_EXPERT), BF16),
                        pltpu.VMEM((EXPERTS_PER_GROUP, D_MODEL, D_EXPERT), BF16),
                        pltpu.VMEM((EXPERTS_PER_GROUP, D_EXPERT, D_MODEL), BF16)],
    )
    return pl.pallas_call(
        _expert_kernel,
        grid_spec=grid_spec,
        out_shape=jax.ShapeDtypeStruct(xs.shape, F32),
        compiler_params=_cparams("arbitrary"),
        name="moe_experts",
    )(tile_ea, tile_eb, n_valid, xs, lw["router_hi"], lw["router_lo"], lw["router_b"],
      lw["w_gate"], lw["w_up"], lw["w_down"])


def _gather_rows_issue(slot_ref, os_ref, gbuf, sems, buf, tc):
    def issue(r2, carry):
        for q in range(DMA_QUEUES):
            r = r2 * DMA_QUEUES + q
            s = slot_ref[0, 0, r]
            src = os_ref.at[pl.ds(pl.multiple_of(s * ROW_CHUNKS, ROW_CHUNKS), ROW_CHUNKS)]
            dst = gbuf.at[buf, pl.ds(pl.multiple_of(r * ROW_CHUNKS, ROW_CHUNKS), ROW_CHUNKS)]
            _rows_copy(src, dst, sems.at[buf]).start(priority=q)
        return carry

    lax.fori_loop(0, tc // DMA_QUEUES, issue, 0)


def _combine_kernel(slot_ref, slot_next_ref, x_ref, g_ref, bt_ref, os_ref, xp_ref, xd_ref,
                    gbuf, sems, *, tc, n_first):
    i = pl.program_id(0)
    n = pl.num_programs(0)
    sb = lax.rem(i, 2)

    @pl.when(i == 0)
    def _():
        _gather_rows_issue(slot_ref, os_ref, gbuf, sems, 0, tc)

    @pl.when(i + 1 < n)
    def _():
        _gather_rows_issue(slot_next_ref, os_ref, gbuf, sems, 1 - sb, tc)

    _rows_copy(os_ref.at[pl.ds(0, tc * ROW_CHUNKS)], gbuf.at[sb], sems.at[sb]).wait()
    y = jnp.concatenate([gbuf[sb, pl.ds(j, tc, stride=ROW_CHUNKS), :] for j in range(ROW_CHUNKS)], axis=1)
    x2 = _layer_norm(DEEPNORM_ALPHA * x_ref[...] + y, g_ref[...], bt_ref[...])

    @pl.when(i < n_first)
    def _():
        xp_ref[...] = x2

    @pl.when(i >= n_first)
    def _():
        xd_ref[...] = x2


def _combine(x, slots, os_, lw, *, n_prompt, tc=256):
    t = x.shape[0]
    nblk = t // tc
    n_first = n_prompt // tc
    slots3 = slots.reshape(nblk, 1, tc)
    kern = functools.partial(_combine_kernel, tc=tc, n_first=n_first)
    smem_tile = lambda f: pl.BlockSpec((1, 1, tc), f, memory_space=pltpu.SMEM)
    return pl.pallas_call(
        kern,
        grid=(nblk,),
        in_specs=[smem_tile(lambda i: (i, 0, 0)),
                  smem_tile(lambda i: (jnp.minimum(i + 1, nblk - 1), 0, 0)),
                  pl.BlockSpec((tc, D_MODEL), lambda i: (i, 0)),
                  _layer_resident(lw["ln2_g"], lw["layer"]), _layer_resident(lw["ln2_b"], lw["layer"]),
                  pl.BlockSpec(memory_space=pl.ANY)],
        out_specs=[pl.BlockSpec((tc, D_MODEL), lambda i: (jnp.minimum(i, n_first - 1), 0)),
                   pl.BlockSpec((tc, D_MODEL), lambda i: (jnp.maximum(i - n_first, 0), 0))],
        out_shape=[jax.ShapeDtypeStruct((n_prompt, D_MODEL), F32),
                   jax.ShapeDtypeStruct((t - n_prompt, D_MODEL), F32)],
        scratch_shapes=[pltpu.VMEM((2, tc * ROW_CHUNKS, LANES), F32), pltpu.SemaphoreType.DMA((2,))],
        compiler_params=_cparams("arbitrary"),
        name="moe_combine",
    )(slots3, slots3, x, lw["ln2_g"], lw["ln2_b"], os_)


def _moe(x, lw, *, n_prompt):
    t = x.shape[0]
    n_tiles = t // EXPERT_TILE + N_CLASSES
    meta, cnt = _router(x, lw)
    cls = meta[:, 0, :].reshape(t).astype(jnp.int32)
    rank = meta[:, 1, :].reshape(t).astype(jnp.int32)
    counts = cnt[:N_CLASSES, 0].astype(jnp.int32)
    seg_tiles = (counts + EXPERT_TILE - 1) // EXPERT_TILE
    tile_ends = jnp.cumsum(seg_tiles)
    seg_start = (tile_ends - seg_tiles) * EXPERT_TILE
    class_ids = jnp.arange(N_CLASSES, dtype=jnp.int32)
    slots = jnp.sum(jnp.where(cls[:, None] == class_ids, seg_start, 0), axis=-1) + rank
    n_valid = tile_ends[-1:]
    tile_ids = jnp.minimum(jnp.arange(n_tiles, dtype=jnp.int32), n_valid[0] - 1)
    tile_class = jnp.sum((tile_ends[None, :] <= tile_ids[:, None]).astype(jnp.int32), axis=1)
    of_class = tile_class[:, None] == class_ids
    class_experts = jnp.asarray(_CLASS_EXPERTS, dtype=jnp.int32)
    tile_ea = jnp.sum(jnp.where(of_class, class_experts[:, 0], 0), axis=-1)
    tile_eb = jnp.sum(jnp.where(of_class, class_experts[:, 1], 0), axis=-1)
    xs = _dispatch(x, slots, n_tiles * EXPERT_TILE)
    os_ = _experts(xs, tile_ea, tile_eb, n_valid.astype(jnp.int32), lw)
    return _combine(x, slots, os_, lw, n_prompt=n_prompt)


def _t5_bucket(dist):
    n = jnp.maximum(dist, 0)
    max_exact = REL_BUCKETS // 2
    nf = jnp.maximum(n, 1).astype(F32)
    large = max_exact + (jnp.log(nf / max_exact) / math.log(REL_MAX_DISTANCE / max_exact)
                         * (REL_BUCKETS - max_exact)).astype(jnp.int32)
    large = jnp.minimum(large, REL_BUCKETS - 1)
    return jnp.where(n < max_exact, n, large)


def _head_rows(per_head, rows_of):
    out = []
    for kvh in range(N_KV_HEADS):
        out.append(jnp.stack([jnp.concatenate([jnp.repeat(per_head[..., h, :, :], rep, axis=-2)
                                               for h, rep in rows_of(kvh, par)], axis=-2)
                              for par in range(2)], axis=-3))
    return jnp.stack(out, axis=-4)


def _prompt_rows(kvh, par):
    return [(GROUP * kvh + 2 * p + par, 1) for p in range(2)]


def _head_bias(rel_bias, dist, valid):
    bias = jnp.where(valid[..., None], rel_bias[_t5_bucket(dist)].astype(F32), NEG_INF)
    return jnp.moveaxis(bias, -1, 0)


def _prompt_bias(rel_bias):
    period = 4 * WINDOW
    off = jnp.arange(period)
    off = jnp.where(off < 2 * WINDOW, off, off - period)
    dist = WINDOW - off
    per_off = _head_bias(rel_bias, dist, (dist >= 0) & (dist <= WINDOW))
    tiled = jnp.tile(per_off, (1, WINDOW))[:, :WINDOW * (period - 1)]
    return tiled.reshape(N_HEADS, WINDOW, period - 1)[:, :, :2 * WINDOW]


def _attention_tables(rel_bias, sinks, ntok, g):
    bias_p = _head_rows(_prompt_bias(rel_bias), _prompt_rows)

    decode_rows = lambda kvh, par: [(GROUP * kvh + 2 * p + par, g) for p in range(2)]
    tq = jnp.arange(ntok)[:, None]
    kpos_old = jnp.arange(WINDOW)[None, :] - WINDOW
    kpos_new = jnp.arange(ntok)[None, :]
    tables = []
    for kpos in (kpos_old, kpos_new):
        d = tq - kpos
        per_head = _head_bias(rel_bias, d, (d >= 0) & (d <= WINDOW) & (PAST_LEN + kpos >= 0))
        tables.append(_head_rows(per_head, decode_rows))
    bias_d = tables[0]
    biasn_d = jnp.moveaxis(tables[1], -1, 2)[..., None]

    s = sinks.astype(F32)[:, :, None, None]
    sink_p = _head_rows(jnp.broadcast_to(s, s.shape[:2] + (WINDOW, 1)), _prompt_rows)
    sink_d = _head_rows(jnp.broadcast_to(s, s.shape[:2] + (ntok, 1)), decode_rows)
    return bias_p, bias_d, biasn_d, sink_p, sink_d


def _stacked_weights(w_in, pool_w, pool_scale, proj_a, proj_b, w_out, ln1_g, ln1_b, router_group_w,
                     router_group_b, router_expert_w, router_expert_b, expert_w_gate, expert_w_up,
                     expert_w_down, ln2_g, ln2_b):
    depth = w_in.shape[0]
    same_group = jnp.eye(N_POOL_GROUPS, dtype=bool)[None, :, None, :, None]
    pool_bd = jnp.where(same_group, pool_w[:, :, :, None, :], 0.0).reshape(depth, D_POOL, D_POOL)
    pad = ROUTER_LANES - N_EXPERTS - N_EXPERT_GROUPS
    rw = jnp.concatenate([router_expert_w, router_group_w, jnp.zeros((depth, D_MODEL, pad), F32)], axis=-1)
    rb = jnp.concatenate([router_expert_b, router_group_b, jnp.zeros((depth, pad), F32)], axis=-1)
    r_hi = rw.astype(BF16)
    row = lambda a: a[:, None, :]
    return dict(
        w_in=w_in.astype(BF16), pool_w=pool_bd.astype(BF16), pool_scale=row(pool_scale),
        proj_a=proj_a.astype(BF16), proj_b=proj_b.astype(BF16), w_out=w_out.astype(BF16),
        ln1_g=row(ln1_g), ln1_b=row(ln1_b),
        router_hi=r_hi, router_lo=(rw - r_hi.astype(F32)).astype(BF16), router_b=row(rb),
        w_gate=expert_w_gate, w_up=expert_w_up, w_down=expert_w_down,
        ln2_g=row(ln2_g), ln2_b=row(ln2_b))


def _decode_mixer(xs_tn, lw, cache_k, cache_v, hist_t, tables, buf, *, nseq, ntok, g, row_block):
    bias_d, biasn_d, sink_d = tables
    u = _in_proj(xs_tn, lw)
    u3 = u.reshape(ntok, nseq, D_IN)
    b = _decode_attn(u3, cache_k, cache_v, bias_d, biasn_d, sink_d, g=g, layer=lw["layer"])
    buf = _decode_tail(xs_tn, u, hist_t, b.reshape(ntok * nseq, Q_DIM), lw, buf, nseq=nseq, ntok=ntok,
                       row_block=row_block)
    return buf, u3


def kernel(x_prompt, x_sample, cache_k, cache_v, state_pool, rel_bias, w_in, pool_w, pool_scale, proj_a, proj_b, w_out, attn_sinks, ln1_g, ln1_b, router_group_w, router_group_b, router_expert_w, router_expert_b, expert_w_gate, expert_w_up, expert_w_down, ln2_g, ln2_b):
    bsz, seq, _ = x_prompt.shape
    nseq, ntok, _ = x_sample.shape
    g = 16
    depth = w_in.shape[0]
    n_prompt = bsz * seq
    n_decode = ntok * nseq
    assert n_prompt % n_decode == 0
    bias_p, bias_d, biasn_d, sink_p, sink_d = _attention_tables(rel_bias, attn_sinks, ntok, g)
    weights = _stacked_weights(w_in, pool_w, pool_scale, proj_a, proj_b, w_out, ln1_g, ln1_b, router_group_w,
                               router_group_b, router_expert_w, router_expert_b, expert_w_gate, expert_w_up,
                               expert_w_down, ln2_g, ln2_b)

    xp = x_prompt.reshape(n_prompt, D_MODEL)
    xs_tn = jnp.swapaxes(x_sample, 0, 1).reshape(n_decode, D_MODEL)
    hist_t = jnp.swapaxes(state_pool, 1, 2)
    prompt_state, decode_u = [], []
    for l in range(depth):
        lw = dict(weights, layer=l)
        x1, kst, vst, pst = _prompt_mixer(xp, lw, bias_p, sink_p, bsz=bsz, n_total=n_prompt + n_decode)
        x1, u3 = _decode_mixer(xs_tn, lw, cache_k, cache_v, hist_t, (bias_d, biasn_d, sink_d), x1,
                               nseq=nseq, ntok=ntok, g=g, row_block=n_prompt // n_decode)
        xp, xs_tn = _moe(x1, lw, n_prompt=n_prompt)
        prompt_state.append((kst, vst, pst))
        decode_u.append(u3)

    pk, pv, pp = (jnp.stack(parts) for parts in zip(*prompt_state))
    heads = lambda a: a.reshape(a.shape[:3] + (N_KV_HEADS, HEAD_DIM))

    def shifted(old, col, width):
        new = jnp.stack([jnp.swapaxes(u3[:, :, col:col + width], 0, 1) for u3 in decode_u])
        return jnp.concatenate([old[:, :, ntok:], new.reshape(new.shape[:3] + old.shape[3:])], axis=2)

    y_sample = jnp.swapaxes(xs_tn.reshape(ntok, nseq, D_MODEL), 0, 1)
    return (xp.reshape(bsz, seq, D_MODEL), y_sample, heads(pk), heads(pv), pp[:, :, -POOL_HIST:],
            shifted(cache_k, _O_K, KV_DIM), shifted(cache_v, _O_V, KV_DIM), shifted(state_pool, 0, D_POOL))
```

```python
import functools
import math

import jax
import jax.numpy as jnp
from jax import lax
from jax.experimental import pallas as pl
from jax.experimental.pallas import tpu as pltpu

D_MODEL = 1024
DEPTH = 2
PAST_LEN = 16384
D_POOL = D_MODEL // 2
POOL_WINDOWS = (2, 4, 8, 16)
N_POOL_GROUPS = len(POOL_WINDOWS)
POOL_GROUP_DIM = D_POOL // N_POOL_GROUPS
POOL_HIST = max(POOL_WINDOWS) - 1
HEAD_DIM = 64
N_HEADS = (D_MODEL // 2) // HEAD_DIM
N_KV_HEADS = 2
GROUP = N_HEADS // N_KV_HEADS
Q_DIM = N_HEADS * HEAD_DIM
KV_DIM = N_KV_HEADS * HEAD_DIM
WINDOW = 128
ATTN_SCALE = HEAD_DIM ** -0.5
REL_BUCKETS = 32
REL_MAX_DISTANCE = 128
D_IN = D_POOL + Q_DIM + 2 * KV_DIM + 2 * D_MODEL
N_EXPERT_GROUPS = 4
EXPERTS_PER_GROUP = 8
N_EXPERTS = N_EXPERT_GROUPS * EXPERTS_PER_GROUP
D_EXPERT = D_MODEL // 4
DEEPNORM_ALPHA = (2 * DEPTH) ** 0.25
LN_EPS = 1e-5

LANES = 128
SUBLANES = 8
VMEM_LIMIT_BYTES = 56 * 1024 * 1024

_O_Q = D_POOL
_O_K = _O_Q + Q_DIM
_O_V = _O_K + KV_DIM
_O_GA = _O_V + KV_DIM
_O_GB = _O_GA + D_MODEL

ROW_CHUNKS = D_MODEL // LANES
KEY_PAD = 2 * WINDOW
EXPERT_TILE = 256
DMA_QUEUES = 2
PAIRS_PER_GROUP = EXPERTS_PER_GROUP * (EXPERTS_PER_GROUP - 1) // 2
N_CLASSES = N_EXPERT_GROUPS * PAIRS_PER_GROUP
_CLASS_EXPERTS = tuple((g * EXPERTS_PER_GROUP + a, g * EXPERTS_PER_GROUP + b)
                       for g in range(N_EXPERT_GROUPS)
                       for a in range(EXPERTS_PER_GROUP) for b in range(a + 1, EXPERTS_PER_GROUP))
ROUTER_LANES = LANES
_GROUP_LANE0 = N_EXPERTS
assert EXPERTS_PER_GROUP == SUBLANES and N_EXPERT_GROUPS <= SUBLANES

BF16 = jnp.bfloat16
F32 = jnp.float32
NEG_INF = float("-inf")


def _cparams(*sem):
    return pltpu.CompilerParams(dimension_semantics=sem, vmem_limit_bytes=VMEM_LIMIT_BYTES)


def _resident(shape):
    nd = len(shape)
    return pl.BlockSpec(shape, lambda *_: (0,) * nd, pipeline_mode=pl.Buffered(1))


def _layer_resident(stacked, layer):
    nd = stacked.ndim - 1
    return pl.BlockSpec((None,) + stacked.shape[1:], lambda *_: (layer,) + (0,) * nd,
                        pipeline_mode=pl.Buffered(1))


def _layer_norm(y, g, b):
    mu = jnp.mean(y, axis=-1, keepdims=True)
    yc = y - mu
    var = jnp.mean(yc * yc, axis=-1, keepdims=True)
    return yc * lax.rsqrt(var + LN_EPS) * g + b


def _softmax_pv(s, sink, v_bf16):
    m = jnp.maximum(jnp.max(s, axis=-1, keepdims=True), sink)
    p = jnp.exp(s - m)
    den = jnp.sum(p, axis=-1, keepdims=True) + jnp.exp(sink - m)
    o = jnp.dot(p.astype(BF16), v_bf16, preferred_element_type=F32)
    return o / den


def _half_lane_variants(t):
    lane = lax.broadcasted_iota(jnp.int32, t.shape, 1)
    lo = lane < HEAD_DIM
    tr = pltpu.roll(t, HEAD_DIM, 1)
    z = jnp.zeros_like(t)
    return ((jnp.where(lo, t, z), jnp.where(lo, z, tr)),
            (jnp.where(lo, tr, z), jnp.where(lo, z, t)))


def _mix_tail(x, a_in, b_in, ga, gb, pw, ps, pa, pb, wo, g, bt):
    a = jnp.dot(a_in.astype(BF16), pw, preferred_element_type=F32) * ps
    pa_o = jnp.dot(a.astype(BF16), pa, preferred_element_type=F32)
    pb_o = jnp.dot(b_in.astype(BF16), pb, preferred_element_type=F32)
    merged = jax.nn.sigmoid(ga) * pa_o + jax.nn.sigmoid(gb) * pb_o
    mix = jnp.dot(merged.astype(BF16), wo, preferred_element_type=F32)
    return _layer_norm(DEEPNORM_ALPHA * x + mix, g, bt)


def _prompt_mixer_kernel(x_ref, win_ref, pw_ref, ps_ref, pa_ref, pb_ref, wo_ref, bias_ref, sink_ref,
                         g_ref, bt_ref, x1_ref, kst_ref, vst_ref, pst_ref, kt_scr, v_scr, ext_scr, *, tm):
    i = pl.program_id(1)
    hist = 2 * SUBLANES

    @pl.when(i == 0)
    def _():
        kt_scr[:, :, :WINDOW] = jnp.zeros((4, LANES, WINDOW), BF16)
        v_scr[:, :WINDOW, :] = jnp.zeros((4, WINDOW, LANES), BF16)
        ext_scr[:hist, :] = jnp.zeros((hist, D_POOL), F32)

    @pl.when(i > 0)
    def _():
        kt_scr[:, :, :WINDOW] = kt_scr[:, :, tm:tm + WINDOW]
        v_scr[:, :WINDOW, :] = v_scr[:, tm:tm + WINDOW, :]
        ext_scr[:hist, :] = ext_scr[tm:tm + hist, :]

    x = x_ref[...]
    u = jnp.dot(x.astype(BF16), win_ref[...], preferred_element_type=F32)
    u_pool = u[:, :D_POOL]
    q = u[:, _O_Q:_O_K]
    k = u[:, _O_K:_O_V]
    v = u[:, _O_V:_O_GA]
    ga = u[:, _O_GA:_O_GB]
    gb = u[:, _O_GB:]

    kst_ref[...] = k[tm - WINDOW:, :]
    vst_ref[...] = v[tm - WINDOW:, :]
    pst_ref[...] = u_pool[tm - hist:, :]

    kt = k.T
    kz = jnp.zeros((HEAD_DIM, tm), F32)
    kt_scr[0, :, WINDOW:] = jnp.concatenate([kt[:HEAD_DIM], kz], axis=0).astype(BF16)
    kt_scr[1, :, WINDOW:] = jnp.concatenate([kz, kt[:HEAD_DIM]], axis=0).astype(BF16)
    kt_scr[2, :, WINDOW:] = jnp.concatenate([kt[HEAD_DIM:], kz], axis=0).astype(BF16)
    kt_scr[3, :, WINDOW:] = jnp.concatenate([kz, kt[HEAD_DIM:]], axis=0).astype(BF16)
    vv = _half_lane_variants(v)
    for kvh in range(N_KV_HEADS):
        for par in range(2):
            v_scr[2 * kvh + par, WINDOW:, :] = vv[kvh][par].astype(BF16)

    col = lax.broadcasted_iota(jnp.int32, (2 * WINDOW, 2 * WINDOW), 1)
    no_prev = col < jnp.where(i == 0, WINDOW, 0)
    qb = q.astype(BF16)
    b_rows = []
    for j in range(tm // WINDOW):
        r0 = j * WINDOW
        o_kvh = []
        for kvh in range(N_KV_HEADS):
            c0 = kvh * 2 * LANES
            q2 = jnp.concatenate([qb[r0:r0 + WINDOW, c0:c0 + LANES],
                                  qb[r0:r0 + WINDOW, c0 + LANES:c0 + 2 * LANES]], axis=0)
            acc = None
            for par in range(2):
                s = jnp.dot(q2, kt_scr[2 * kvh + par, :, r0:r0 + 2 * WINDOW], preferred_element_type=F32)
                s = s * ATTN_SCALE + bias_ref[kvh, par]
                if j == 0:
                    s = jnp.where(no_prev, NEG_INF, s)
                o = _softmax_pv(s, sink_ref[kvh, par], v_scr[2 * kvh + par, r0:r0 + 2 * WINDOW, :])
                acc = o if acc is None else acc + o
            o_kvh.append(acc)
        b_rows.append(jnp.concatenate([o_kvh[0][:WINDOW], o_kvh[0][WINDOW:],
                                       o_kvh[1][:WINDOW], o_kvh[1][WINDOW:]], axis=1))
    b = jnp.concatenate(b_rows, axis=0)

    ext_scr[hist:, :] = u_pool
    pos = i * tm + lax.broadcasted_iota(jnp.int32, (tm, 1), 0)
    means = []
    for gi, w in enumerate(POOL_WINDOWS):
        c0 = gi * POOL_GROUP_DIM
        acc = u_pool[:, c0:c0 + POOL_GROUP_DIM]
        for d in range(1, w):
            acc = acc + ext_scr[pl.ds(hist - d, tm), c0:c0 + POOL_GROUP_DIM]
        cnt = jnp.minimum(w, pos + 1).astype(F32)
        means.append(acc / cnt)
    pooled = jnp.concatenate(means, axis=1) - u_pool

    x1_ref[...] = _mix_tail(x, pooled, b, ga, gb, pw_ref[...], ps_ref[...], pa_ref[...], pb_ref[...],
                            wo_ref[...], g_ref[...], bt_ref[...])


def _prompt_mixer(x, lw, bias_p, sink_t, *, bsz, n_total, tm=512):
    seq = x.shape[0] // bsz
    nt = seq // tm
    hist = 2 * SUBLANES
    kern = functools.partial(_prompt_mixer_kernel, tm=tm)
    tile = pl.BlockSpec((tm, D_MODEL), lambda b, i: (b * nt + i, 0))
    state = lambda r, w: pl.BlockSpec((None, r, w), lambda b, i: (b, 0, 0))
    params = (lw["w_in"], lw["pool_w"], lw["pool_scale"], lw["proj_a"], lw["proj_b"], lw["w_out"])
    tail = (sink_t, lw["ln1_g"], lw["ln1_b"])
    return pl.pallas_call(
        kern,
        grid=(bsz, nt),
        in_specs=([tile] + [_layer_resident(a, lw["layer"]) for a in params] + [_resident(bias_p.shape)]
                  + [_layer_resident(a, lw["layer"]) for a in tail]),
        out_specs=[tile, state(WINDOW, KV_DIM), state(WINDOW, KV_DIM), state(hist, D_POOL)],
        out_shape=[jax.ShapeDtypeStruct((n_total, D_MODEL), F32),
                   jax.ShapeDtypeStruct((bsz, WINDOW, KV_DIM), F32),
                   jax.ShapeDtypeStruct((bsz, WINDOW, KV_DIM), F32),
                   jax.ShapeDtypeStruct((bsz, hist, D_POOL), F32)],
        scratch_shapes=[pltpu.VMEM((4, LANES, WINDOW + tm), BF16),
                        pltpu.VMEM((4, WINDOW + tm, LANES), BF16),
                        pltpu.VMEM((hist + tm, D_POOL), F32)],
        compiler_params=_cparams("arbitrary", "arbitrary"),
        name="prompt_mixer",
    )(x, lw["w_in"], lw["pool_w"], lw["pool_scale"], lw["proj_a"], lw["proj_b"], lw["w_out"],
      bias_p, sink_t, lw["ln1_g"], lw["ln1_b"])


def _in_proj_kernel(x_ref, win_ref, u_ref):
    u_ref[...] = jnp.dot(x_ref[...].astype(BF16), win_ref[...], preferred_element_type=F32)


def _in_proj(x, lw):
    n = x.shape[0]
    return pl.pallas_call(
        _in_proj_kernel,
        grid=(1,),
        in_specs=[pl.BlockSpec(x.shape, lambda i: (0, 0)), _layer_resident(lw["w_in"], lw["layer"])],
        out_specs=pl.BlockSpec((n, D_IN), lambda i: (0, 0)),
        out_shape=jax.ShapeDtypeStruct((n, D_IN), F32),
        compiler_params=_cparams("arbitrary"),
        name="decode_in_proj",
    )(x, lw["w_in"])


def _decode_attn_kernel(q_ref, kn_ref, vn_ref, kc_ref, vc_ref, bias_ref, biasn_ref, sink_ref, b_ref, *, g, ntok):
    slabs = 2 * ntok
    rows = slabs * g
    def cache_variants(ref):
        z = jnp.zeros((HEAD_DIM, WINDOW), F32)
        out = []
        for kvh in range(N_KV_HEADS):
            out.append((jnp.concatenate([jnp.concatenate([ref[n, kvh], z], axis=0) for n in range(g)], axis=1),
                        jnp.concatenate([jnp.concatenate([z, ref[n, kvh]], axis=0) for n in range(g)], axis=1)))
        return out

    kk = cache_variants(kc_ref)
    vv = cache_variants(vc_ref)
    kn = [_half_lane_variants(kn_ref[t]) for t in range(ntok)]
    vn = [_half_lane_variants(vn_ref[t]) for t in range(ntok)]
    per_row = lambda a: jnp.concatenate([a] * slabs, axis=0)
    row_seq = lax.rem(lax.broadcasted_iota(jnp.int32, (rows, WINDOW), 0), g)
    diag = (lax.rem(lax.broadcasted_iota(jnp.int32, (rows, g * WINDOW), 0), g)
            == lax.broadcasted_iota(jnp.int32, (rows, g * WINDOW), 1) // WINDOW)
    for kvh in range(N_KV_HEADS):
        c0 = kvh * 2 * LANES
        q2f = jnp.concatenate([q_ref[t, :, c0 + p * LANES:c0 + (p + 1) * LANES]
                               for p in range(2) for t in range(ntok)], axis=0)
        q2 = q2f.astype(BF16)
        acc = None
        for par in range(2):
            full = jnp.dot(q2, kk[kvh][par].astype(BF16), preferred_element_type=F32)
            s = jnp.zeros((rows, WINDOW), F32)
            for n in range(g):
                s = jnp.where(row_seq == n, full[:, n * WINDOW:(n + 1) * WINDOW], s)
            s = s * ATTN_SCALE + bias_ref[kvh, par]
            s_new = [jnp.sum(q2f * per_row(kn[t][kvh][par]), axis=-1, keepdims=True) * ATTN_SCALE
                     + biasn_ref[kvh, par, t] for t in range(ntok)]
            sink = sink_ref[kvh, par]
            m = jnp.maximum(jnp.max(s, axis=-1, keepdims=True), sink)
            for t in range(ntok):
                m = jnp.maximum(m, s_new[t])
            p = jnp.exp(s - m)
            den = jnp.sum(p, axis=-1, keepdims=True) + jnp.exp(sink - m)
            p_bd = jnp.where(diag, jnp.concatenate([p] * g, axis=1), 0.0).astype(BF16)
            o = lax.dot_general(p_bd, vv[kvh][par].astype(BF16), (((1,), (1,)), ((), ())),
                                preferred_element_type=F32)
            for t in range(ntok):
                p_new = jnp.exp(s_new[t] - m)
                den = den + p_new
                o = o + p_new * per_row(vn[t][kvh][par])
            o = o / den
            acc = o if acc is None else acc + o
        for p in range(2):
            for t in range(ntok):
                r0 = (p * ntok + t) * g
                b_ref[t, :, c0 + p * LANES:c0 + (p + 1) * LANES] = acc[r0:r0 + g]


def _decode_attn(u3, cache_k, cache_v, bias_d, biasn_d, sink_d, *, g, layer):
    ntok, nseq, _ = u3.shape
    cache = pl.BlockSpec((None, g, N_KV_HEADS, HEAD_DIM, WINDOW), lambda i: (layer, i, 0, 0, 0))
    kern = functools.partial(_decode_attn_kernel, g=g, ntok=ntok)
    new_cols = lambda w, off: pl.BlockSpec((ntok, g, w), lambda i: (0, i, off // w))
    return pl.pallas_call(
        kern,
        grid=(nseq // g,),
        in_specs=[new_cols(Q_DIM, _O_Q), new_cols(KV_DIM, _O_K), new_cols(KV_DIM, _O_V),
                  cache, cache,
                  _resident(bias_d.shape), _resident(biasn_d.shape), _layer_resident(sink_d, layer)],
        out_specs=pl.BlockSpec((ntok, g, Q_DIM), lambda i: (0, i, 0)),
        out_shape=jax.ShapeDtypeStruct((ntok, nseq, Q_DIM), F32),
        compiler_params=_cparams("arbitrary"),
        name="decode_attn",
    )(u3, u3, u3, cache_k, cache_v, bias_d, biasn_d, sink_d)


def _decode_tail_kernel(x_ref, u_ref, hist_ref, b_ref, pw_ref, ps_ref, pa_ref, pb_ref,
                        wo_ref, g_ref, bt_ref, buf_ref, x1_ref, *, nseq, ntok, start_pos):
    del buf_ref

    def ext_row(r, c0):
        if r < POOL_HIST:
            return hist_ref[r, :, c0:c0 + POOL_GROUP_DIM]
        return u_ref[(r - POOL_HIST) * nseq:(r - POOL_HIST + 1) * nseq, c0:c0 + POOL_GROUP_DIM]

    pooled_rows = []
    for t in range(ntok):
        means = []
        for gi, w in enumerate(POOL_WINDOWS):
            acc = None
            for d in range(w):
                r = POOL_HIST + t - d
                if start_pos - POOL_HIST + r < 0:
                    continue
                term = ext_row(r, gi * POOL_GROUP_DIM)
                acc = term if acc is None else acc + term
            means.append(acc / float(min(w, start_pos + t + 1)))
        pooled_rows.append(jnp.concatenate(means, axis=1))
    pooled = jnp.concatenate(pooled_rows, axis=0) - u_ref[:, :D_POOL]
    x1_ref[...] = _mix_tail(x_ref[...], pooled, b_ref[...], u_ref[:, _O_GA:_O_GB], u_ref[:, _O_GB:], pw_ref[...],
                            ps_ref[...], pa_ref[...], pb_ref[...], wo_ref[...], g_ref[...], bt_ref[...])


def _decode_tail(x, u, hist_t, b, lw, buf, *, nseq, ntok, row_block):
    n = ntok * nseq
    kern = functools.partial(_decode_tail_kernel, nseq=nseq, ntok=ntok, start_pos=PAST_LEN)
    whole = lambda a: pl.BlockSpec(a.shape, lambda i: (0,) * a.ndim)
    layered = lambda a: _layer_resident(a, lw["layer"])
    params = (lw["pool_w"], lw["pool_scale"], lw["proj_a"], lw["proj_b"], lw["w_out"], lw["ln1_g"], lw["ln1_b"])
    ins = (x, u, hist_t, b) + params
    return pl.pallas_call(
        kern,
        grid=(1,),
        in_specs=([whole(x), whole(u), layered(hist_t), whole(b)] + [layered(a) for a in params]
                  + [pl.BlockSpec(memory_space=pl.ANY)]),
        out_specs=pl.BlockSpec((n, D_MODEL), lambda i: (row_block, 0)),
        out_shape=jax.ShapeDtypeStruct(buf.shape, F32),
        input_output_aliases={len(ins): 0},
        compiler_params=_cparams("arbitrary"),
        name="decode_tail",
    )(*ins, buf)


def _router_logits(x, whi, wlo, br):
    xh = x.astype(BF16)
    xl = (x - xh.astype(F32)).astype(BF16)
    return (jnp.dot(xh, whi, preferred_element_type=F32)
            + (jnp.dot(xh, wlo, preferred_element_type=F32)
               + jnp.dot(xl, whi, preferred_element_type=F32))) + br


def _group_logits(logits, lane):
    is_group = jnp.logical_and(lane >= _GROUP_LANE0, lane < _GROUP_LANE0 + N_EXPERT_GROUPS)
    return jnp.where(is_group, logits, NEG_INF)


def _router_kernel(x_ref, whi_ref, wlo_ref, br_ref, meta_ref, cnt_ref, run_scr, *, tr):
    i = pl.program_id(0)

    @pl.when(i == 0)
    def _():
        run_scr[...] = jnp.zeros_like(run_scr)

    lt = _router_logits(x_ref[...], whi_ref[...], wlo_ref[...], br_ref[...]).T
    row = lax.broadcasted_iota(jnp.int32, (EXPERTS_PER_GROUP, tr), 0)
    none = jnp.int32(EXPERTS_PER_GROUP)

    def argmax_rows(vals):
        vmax = jnp.max(vals, axis=0, keepdims=True)
        return jnp.min(jnp.where(vals == vmax, row, none), axis=0, keepdims=True)

    glog = jnp.where(row < N_EXPERT_GROUPS, lt[_GROUP_LANE0:_GROUP_LANE0 + EXPERTS_PER_GROUP], NEG_INF)
    gidx = argmax_rows(glog)
    el = lt[:EXPERTS_PER_GROUP]
    for g in range(1, N_EXPERT_GROUPS):
        el = jnp.where(gidx == g, lt[g * EXPERTS_PER_GROUP:(g + 1) * EXPERTS_PER_GROUP], el)
    e1 = argmax_rows(el)
    e2 = argmax_rows(jnp.where(row == e1, NEG_INF, el))
    lo = jnp.minimum(e1, e2)
    hi = jnp.maximum(e1, e2)
    pair = lax.shift_right_logical(lo * (2 * EXPERTS_PER_GROUP - 1 - lo), 1) + (hi - lo - 1)
    cls = gidx * PAIRS_PER_GROUP + pair

    oh = jnp.where(lax.broadcasted_iota(jnp.int32, (ROUTER_LANES, tr), 0) == cls, 1.0, 0.0)
    rr = lax.broadcasted_iota(jnp.int32, (tr, tr), 0)
    cc = lax.broadcasted_iota(jnp.int32, (tr, tr), 1)
    earlier = jnp.where(rr < cc, 1.0, 0.0).astype(BF16)
    before = jnp.dot(oh.astype(BF16), earlier, preferred_element_type=F32) + run_scr[...]
    rank = jnp.sum(oh * before, axis=0, keepdims=True)
    run_scr[...] += jnp.sum(oh, axis=1, keepdims=True)
    cnt_ref[...] = run_scr[...]

    meta_ref[...] = jnp.where(row == 0, cls.astype(F32), jnp.where(row == 1, rank, 0.0))


def _router(x, lw, *, tr=512):
    t = x.shape[0]
    kern = functools.partial(_router_kernel, tr=tr)
    return pl.pallas_call(
        kern,
        grid=(t // tr,),
        in_specs=[pl.BlockSpec((tr, D_MODEL), lambda i: (i, 0)),
                  _layer_resident(lw["router_hi"], lw["layer"]), _layer_resident(lw["router_lo"], lw["layer"]),
                  _layer_resident(lw["router_b"], lw["layer"])],
        out_specs=[pl.BlockSpec((None, SUBLANES, tr), lambda i: (i, 0, 0)),
                   pl.BlockSpec((ROUTER_LANES, 1), lambda i: (0, 0))],
        out_shape=[jax.ShapeDtypeStruct((t // tr, SUBLANES, tr), F32),
                   jax.ShapeDtypeStruct((ROUTER_LANES, 1), F32)],
        scratch_shapes=[pltpu.VMEM((ROUTER_LANES, 1), F32)],
        compiler_params=_cparams("arbitrary"),
        name="router",
    )(x, lw["router_hi"], lw["router_lo"], lw["router_b"])


def _rows_copy(src, dst, sem):
    return pltpu.make_async_copy(src, dst, sem)


def _dispatch_kernel(slot_ref, x_ref, xs_ref, stage, sems, *, td):
    i = pl.program_id(0)
    n = pl.num_programs(0)
    sb = lax.rem(i, 2)
    x = x_ref[...]
    for j in range(ROW_CHUNKS):
        stage[sb, pl.ds(j, td, stride=ROW_CHUNKS), :] = x[:, j * LANES:(j + 1) * LANES]

    def issue(r2, carry):
        for q in range(DMA_QUEUES):
            r = r2 * DMA_QUEUES + q
            src = stage.at[sb, pl.ds(pl.multiple_of(r * ROW_CHUNKS, ROW_CHUNKS), ROW_CHUNKS)]
            s = slot_ref[0, 0, r]
            dst = xs_ref.at[pl.ds(pl.multiple_of(s * ROW_CHUNKS, ROW_CHUNKS), ROW_CHUNKS)]
            _rows_copy(src, dst, sems.at[sb]).start(priority=q)
        return carry

    lax.fori_loop(0, td // DMA_QUEUES, issue, 0)

    def drain(buf):
        _rows_copy(stage.at[buf], xs_ref.at[pl.ds(0, td * ROW_CHUNKS)], sems.at[buf]).wait()

    @pl.when(i > 0)
    def _():
        drain(1 - sb)

    @pl.when(i == n - 1)
    def _():
        drain(sb)


def _dispatch(x, slots, n_slots, *, td=256):
    t = x.shape[0]
    slots3 = slots.reshape(t // td, 1, td)
    kern = functools.partial(_dispatch_kernel, td=td)
    return pl.pallas_call(
        kern,
        grid=(t // td,),
        in_specs=[pl.BlockSpec((1, 1, td), lambda i: (i, 0, 0), memory_space=pltpu.SMEM),
                  pl.BlockSpec((td, D_MODEL), lambda i: (i, 0))],
        out_specs=pl.BlockSpec(memory_space=pl.ANY),
        out_shape=jax.ShapeDtypeStruct((n_slots * ROW_CHUNKS, LANES), F32),
        scratch_shapes=[pltpu.VMEM((2, td * ROW_CHUNKS, LANES), F32), pltpu.SemaphoreType.DMA((2,))],
        compiler_params=_cparams("arbitrary"),
        name="moe_dispatch",
    )(slots3, x)


def _expert_kernel(ea_ref, eb_ref, nv_ref, xs_ref, rhi_ref, rb_ref, wg_ref, wu_ref, wd_ref, os_ref,
                   wg_bf, wu_bf, wd_bf):
    i = pl.program_id(0)
    valid = i < nv_ref[0]
    ea = ea_ref[i]
    eb = eb_ref[i]
    group = ea // EXPERTS_PER_GROUP
    prev_group = ea_ref[jnp.maximum(i - 1, 0)] // EXPERTS_PER_GROUP
    new_group = jnp.logical_or(i == 0, group != prev_group)

    @pl.when(jnp.logical_and(valid, new_group))
    def _():
        for e in range(EXPERTS_PER_GROUP):
            wg_bf[e] = wg_ref[e].astype(BF16)
            wu_bf[e] = wu_ref[e].astype(BF16)
            wd_bf[e] = wd_ref[e].astype(BF16)

    @pl.when(valid)
    def _():
        x = jnp.concatenate([xs_ref[pl.ds(j, EXPERT_TILE, stride=ROW_CHUNKS), :] for j in range(ROW_CHUNKS)],
                            axis=1)
        xt = x.astype(BF16)
        logits = jnp.dot(xt, rhi_ref[...], preferred_element_type=F32) + rb_ref[...]
        lane = lax.broadcasted_iota(jnp.int32, logits.shape, 1)
        pick = lambda l: jnp.sum(jnp.where(lane == l, logits, 0.0), axis=-1, keepdims=True)
        glog = _group_logits(logits, lane)
        gmax = jnp.max(glog, axis=-1, keepdims=True)
        gw = jnp.exp(pick(_GROUP_LANE0 + group) - gmax) / jnp.sum(jnp.exp(glog - gmax), axis=-1, keepdims=True)
        la = pick(ea)
        lb = pick(eb)
        m = jnp.maximum(la, lb)
        pa = jnp.exp(la - m)
        pb = jnp.exp(lb - m)
        scale = gw / (pa + pb)

        y = None
        for e, w in ((ea, pa * scale), (eb, pb * scale)):
            el = e - group * EXPERTS_PER_GROUP
            gate = jnp.dot(xt, wg_bf[el], preferred_element_type=F32)
            up = jnp.dot(xt, wu_bf[el], preferred_element_type=F32)
            h = (jax.nn.silu(gate) * up).astype(BF16)
            term = w * jnp.dot(h, wd_bf[el], preferred_element_type=F32)
            y = term if y is None else y + term
        for j in range(ROW_CHUNKS):
            os_ref[pl.ds(j, EXPERT_TILE, stride=ROW_CHUNKS), :] = y[:, j * LANES:(j + 1) * LANES]


def _experts(xs, tile_ea, tile_eb, n_valid, lw):
    layer = lw["layer"]
    n_tiles = xs.shape[0] // (EXPERT_TILE * ROW_CHUNKS)
    rows = EXPERT_TILE * ROW_CHUNKS
    act = pl.BlockSpec((rows, LANES), lambda i, ea, eb, nv: (jnp.minimum(i, nv[0] - 1), 0))
    group_of = lambda ea, i: ea[i] // EXPERTS_PER_GROUP
    group_w = lambda r, c: pl.BlockSpec((None, EXPERTS_PER_GROUP, r, c),
                                        lambda i, ea, eb, nv: (layer, group_of(ea, i), 0, 0),
                                        pipeline_mode=pl.Buffered(1))
    router = lambda a: pl.BlockSpec((None,) + a.shape[1:], lambda i, ea, eb, nv: (layer, 0, 0),
                                    pipeline_mode=pl.Buffered(1))
    grid_spec = pltpu.PrefetchScalarGridSpec(
        num_scalar_prefetch=3,
        grid=(n_tiles,),
        in_specs=[act, router(lw["router_hi"]), router(lw["router_b"]),
                  group_w(D_MODEL, D_EXPERT), group_w(D_MODEL, D_EXPERT), group_w(D_EXPERT, D_MODEL)],
        out_specs=act,
        scratch_shapes=[pltpu.VMEM((EXPERTS_PER_GROUP, D_MODEL, D_EXPERT), BF16),
                        pltpu.VMEM((EXPERTS_PER_GROUP, D_MODEL, D_EXPERT), BF16),
                        pltpu.VMEM((EXPERTS_PER_GROUP, D_EXPERT, D_MODEL), BF16)],
    )
    return pl.pallas_call(
        _expert_kernel,
        grid_spec=grid_spec,
        out_shape=jax.ShapeDtypeStruct(xs.shape, F32),
        compiler_params=_cparams("arbitrary"),
        name="moe_experts",
    )(tile_ea, tile_eb, n_valid, xs, lw["router_hi"], lw["router_b"],
      lw["w_gate"], lw["w_up"], lw["w_down"])


def _gather_rows_issue(slot_ref, os_ref, gbuf, sems, buf, tc):
    def issue(r2, carry):
        for q in range(DMA_QUEUES):
            r = r2 * DMA_QUEUES + q
            s = slot_ref[0, 0, r]
            src = os_ref.at[pl.ds(pl.multiple_of(s * ROW_CHUNKS, ROW_CHUNKS), ROW_CHUNKS)]
            dst = gbuf.at[buf, pl.ds(pl.multiple_of(r * ROW_CHUNKS, ROW_CHUNKS), ROW_CHUNKS)]
            _rows_copy(src, dst, sems.at[buf]).start(priority=q)
        return carry

    lax.fori_loop(0, tc // DMA_QUEUES, issue, 0)


def _combine_kernel(slot_ref, slot_next_ref, x_ref, g_ref, bt_ref, os_ref, xp_ref, xd_ref,
                    gbuf, sems, *, tc, n_first):
    i = pl.program_id(0)
    n = pl.num_programs(0)
    sb = lax.rem(i, 2)

    @pl.when(i == 0)
    def _():
        _gather_rows_issue(slot_ref, os_ref, gbuf, sems, 0, tc)

    @pl.when(i + 1 < n)
    def _():
        _gather_rows_issue(slot_next_ref, os_ref, gbuf, sems, 1 - sb, tc)

    _rows_copy(os_ref.at[pl.ds(0, tc * ROW_CHUNKS)], gbuf.at[sb], sems.at[sb]).wait()
    y = jnp.concatenate([gbuf[sb, pl.ds(j, tc, stride=ROW_CHUNKS), :] for j in range(ROW_CHUNKS)], axis=1)
    x2 = _layer_norm(DEEPNORM_ALPHA * x_ref[...] + y, g_ref[...], bt_ref[...])

    @pl.when(i < n_first)
    def _():
        xp_ref[...] = x2

    @pl.when(i >= n_first)
    def _():
        xd_ref[...] = x2


def _combine(x, slots, os_, lw, *, n_prompt, tc=256):
    t = x.shape[0]
    nblk = t // tc
    n_first = n_prompt // tc
    slots3 = slots.reshape(nblk, 1, tc)
    kern = functools.partial(_combine_kernel, tc=tc, n_first=n_first)
    smem_tile = lambda f: pl.BlockSpec((1, 1, tc), f, memory_space=pltpu.SMEM)
    return pl.pallas_call(
        kern,
        grid=(nblk,),
        in_specs=[smem_tile(lambda i: (i, 0, 0)),
                  smem_tile(lambda i: (jnp.minimum(i + 1, nblk - 1), 0, 0)),
                  pl.BlockSpec((tc, D_MODEL), lambda i: (i, 0)),
                  _layer_resident(lw["ln2_g"], lw["layer"]), _layer_resident(lw["ln2_b"], lw["layer"]),
                  pl.BlockSpec(memory_space=pl.ANY)],
        out_specs=[pl.BlockSpec((tc, D_MODEL), lambda i: (jnp.minimum(i, n_first - 1), 0)),
                   pl.BlockSpec((tc, D_MODEL), lambda i: (jnp.maximum(i - n_first, 0), 0))],
        out_shape=[jax.ShapeDtypeStruct((n_prompt, D_MODEL), F32),
                   jax.ShapeDtypeStruct((t - n_prompt, D_MODEL), F32)],
        scratch_shapes=[pltpu.VMEM((2, tc * ROW_CHUNKS, LANES), F32), pltpu.SemaphoreType.DMA((2,))],
        compiler_params=_cparams("arbitrary"),
        name="moe_combine",
    )(slots3, slots3, x, lw["ln2_g"], lw["ln2_b"], os_)


def _moe(x, lw, *, n_prompt):
    t = x.shape[0]
    n_tiles = t // EXPERT_TILE + N_CLASSES
    meta, cnt = _router(x, lw)
    cls = meta[:, 0, :].reshape(t).astype(jnp.int32)
    rank = meta[:, 1, :].reshape(t).astype(jnp.int32)
    counts = cnt[:N_CLASSES, 0].astype(jnp.int32)
    seg_tiles = (counts + EXPERT_TILE - 1) // EXPERT_TILE
    tile_ends = jnp.cumsum(seg_tiles)
    seg_start = (tile_ends - seg_tiles) * EXPERT_TILE
    class_ids = jnp.arange(N_CLASSES, dtype=jnp.int32)
    slots = jnp.sum(jnp.where(cls[:, None] == class_ids, seg_start, 0), axis=-1) + rank
    n_valid = tile_ends[-1:]
    tile_ids = jnp.minimum(jnp.arange(n_tiles, dtype=jnp.int32), n_valid[0] - 1)
    tile_class = jnp.sum((tile_ends[None, :] <= tile_ids[:, None]).astype(jnp.int32), axis=1)
    of_class = tile_class[:, None] == class_ids
    class_experts = jnp.asarray(_CLASS_EXPERTS, dtype=jnp.int32)
    tile_ea = jnp.sum(jnp.where(of_class, class_experts[:, 0], 0), axis=-1)
    tile_eb = jnp.sum(jnp.where(of_class, class_experts[:, 1], 0), axis=-1)
    xs = _dispatch(x, slots, n_tiles * EXPERT_TILE)
    os_ = _experts(xs, tile_ea, tile_eb, n_valid.astype(jnp.int32), lw)
    return _combine(x, slots, os_, lw, n_prompt=n_prompt)


def _t5_bucket(dist):
    n = jnp.maximum(dist, 0)
    max_exact = REL_BUCKETS // 2
    nf = jnp.maximum(n, 1).astype(F32)
    large = max_exact + (jnp.log(nf / max_exact) / math.log(REL_MAX_DISTANCE / max_exact)
                         * (REL_BUCKETS - max_exact)).astype(jnp.int32)
    large = jnp.minimum(large, REL_BUCKETS - 1)
    return jnp.where(n < max_exact, n, large)


def _head_rows(per_head, rows_of):
    out = []
    for kvh in range(N_KV_HEADS):
        out.append(jnp.stack([jnp.concatenate([jnp.repeat(per_head[..., h, :, :], rep, axis=-2)
                                               for h, rep in rows_of(kvh, par)], axis=-2)
                              for par in range(2)], axis=-3))
    return jnp.stack(out, axis=-4)


def _prompt_rows(kvh, par):
    return [(GROUP * kvh + 2 * p + par, 1) for p in range(2)]


def _head_bias(rel_bias, dist, valid):
    bias = jnp.where(valid[..., None], rel_bias[_t5_bucket(dist)].astype(F32), NEG_INF)
    return jnp.moveaxis(bias, -1, 0)


def _prompt_bias(rel_bias):
    period = 4 * WINDOW
    off = jnp.arange(period)
    off = jnp.where(off < 2 * WINDOW, off, off - period)
    dist = WINDOW - off
    per_off = _head_bias(rel_bias, dist, (dist >= 0) & (dist <= WINDOW))
    tiled = jnp.tile(per_off, (1, WINDOW))[:, :WINDOW * (period - 1)]
    return tiled.reshape(N_HEADS, WINDOW, period - 1)[:, :, :2 * WINDOW]


def _attention_tables(rel_bias, sinks, ntok, g):
    bias_p = _head_rows(_prompt_bias(rel_bias), _prompt_rows)

    decode_rows = lambda kvh, par: [(GROUP * kvh + 2 * p + par, g) for p in range(2)]
    tq = jnp.arange(ntok)[:, None]
    kpos_old = jnp.arange(WINDOW)[None, :] - WINDOW
    kpos_new = jnp.arange(ntok)[None, :]
    tables = []
    for kpos in (kpos_old, kpos_new):
        d = tq - kpos
        per_head = _head_bias(rel_bias, d, (d >= 0) & (d <= WINDOW) & (PAST_LEN + kpos >= 0))
        tables.append(_head_rows(per_head, decode_rows))
    bias_d = tables[0]
    biasn_d = jnp.moveaxis(tables[1], -1, 2)[..., None]

    s = sinks.astype(F32)[:, :, None, None]
    sink_p = _head_rows(jnp.broadcast_to(s, s.shape[:2] + (WINDOW, 1)), _prompt_rows)
    sink_d = _head_rows(jnp.broadcast_to(s, s.shape[:2] + (ntok, 1)), decode_rows)
    return bias_p, bias_d, biasn_d, sink_p, sink_d


def _stacked_weights(w_in, pool_w, pool_scale, proj_a, proj_b, w_out, ln1_g, ln1_b, router_group_w,
                     router_group_b, router_expert_w, router_expert_b, expert_w_gate, expert_w_up,
                     expert_w_down, ln2_g, ln2_b):
    depth = w_in.shape[0]
    same_group = jnp.eye(N_POOL_GROUPS, dtype=bool)[None, :, None, :, None]
    pool_bd = jnp.where(same_group, pool_w[:, :, :, None, :], 0.0).reshape(depth, D_POOL, D_POOL)
    pad = ROUTER_LANES - N_EXPERTS - N_EXPERT_GROUPS
    rw = jnp.concatenate([router_expert_w, router_group_w, jnp.zeros((depth, D_MODEL, pad), F32)], axis=-1)
    rb = jnp.concatenate([router_expert_b, router_group_b, jnp.zeros((depth, pad), F32)], axis=-1)
    r_hi = rw.astype(BF16)
    row = lambda a: a[:, None, :]
    return dict(
        w_in=w_in.astype(BF16), pool_w=pool_bd.astype(BF16), pool_scale=row(pool_scale),
        proj_a=proj_a.astype(BF16), proj_b=proj_b.astype(BF16), w_out=w_out.astype(BF16),
        ln1_g=row(ln1_g), ln1_b=row(ln1_b),
        router_hi=r_hi, router_lo=(rw - r_hi.astype(F32)).astype(BF16), router_b=row(rb),
        w_gate=expert_w_gate, w_up=expert_w_up, w_down=expert_w_down,
        ln2_g=row(ln2_g), ln2_b=row(ln2_b))


def _decode_mixer(xs_tn, lw, cache_k, cache_v, hist_t, tables, buf, *, nseq, ntok, g, row_block):
    bias_d, biasn_d, sink_d = tables
    u = _in_proj(xs_tn, lw)
    u3 = u.reshape(ntok, nseq, D_IN)
    b = _decode_attn(u3, cache_k, cache_v, bias_d, biasn_d, sink_d, g=g, layer=lw["layer"])
    buf = _decode_tail(xs_tn, u, hist_t, b.reshape(ntok * nseq, Q_DIM), lw, buf, nseq=nseq, ntok=ntok,
                       row_block=row_block)
    return buf, u3


def kernel(x_prompt, x_sample, cache_k, cache_v, state_pool, rel_bias, w_in, pool_w, pool_scale, proj_a, proj_b, w_out, attn_sinks, ln1_g, ln1_b, router_group_w, router_group_b, router_expert_w, router_expert_b, expert_w_gate, expert_w_up, expert_w_down, ln2_g, ln2_b):
    bsz, seq, _ = x_prompt.shape
    nseq, ntok, _ = x_sample.shape
    g = 16
    depth = w_in.shape[0]
    n_prompt = bsz * seq
    n_decode = ntok * nseq
    assert n_prompt % n_decode == 0
    bias_p, bias_d, biasn_d, sink_p, sink_d = _attention_tables(rel_bias, attn_sinks, ntok, g)
    weights = _stacked_weights(w_in, pool_w, pool_scale, proj_a, proj_b, w_out, ln1_g, ln1_b, router_group_w,
                               router_group_b, router_expert_w, router_expert_b, expert_w_gate, expert_w_up,
                               expert_w_down, ln2_g, ln2_b)

    xp = x_prompt.reshape(n_prompt, D_MODEL)
    xs_tn = jnp.swapaxes(x_sample, 0, 1).reshape(n_decode, D_MODEL)
    hist_t = jnp.swapaxes(state_pool, 1, 2)
    ck_t = jnp.transpose(cache_k, (0, 1, 3, 4, 2))
    cv_t = jnp.transpose(cache_v, (0, 1, 3, 4, 2))
    prompt_state, decode_u = [], []
    for l in range(depth):
        lw = dict(weights, layer=l)
        x1, kst, vst, pst = _prompt_mixer(xp, lw, bias_p, sink_p, bsz=bsz, n_total=n_prompt + n_decode)
        x1, u3 = _decode_mixer(xs_tn, lw, ck_t, cv_t, hist_t, (bias_d, biasn_d, sink_d), x1,
                               nseq=nseq, ntok=ntok, g=g, row_block=n_prompt // n_decode)
        xp, xs_tn = _moe(x1, lw, n_prompt=n_prompt)
        prompt_state.append((kst, vst, pst))
        decode_u.append(u3)

    pk, pv, pp = (jnp.stack(parts) for parts in zip(*prompt_state))
    heads = lambda a: a.reshape(a.shape[:3] + (N_KV_HEADS, HEAD_DIM))

    def shifted(old, col, width):
        new = jnp.stack([jnp.swapaxes(u3[:, :, col:col + width], 0, 1) for u3 in decode_u])
        return jnp.concatenate([old[:, :, ntok:], new.reshape(new.shape[:3] + old.shape[3:])], axis=2)

    y_sample = jnp.swapaxes(xs_tn.reshape(ntok, nseq, D_MODEL), 0, 1)
    return (xp.reshape(bsz, seq, D_MODEL), y_sample, heads(pk), heads(pv), pp[:, :, -POOL_HIST:],
            shifted(cache_k, _O_K, KV_DIM), shifted(cache_v, _O_V, KV_DIM), shifted(state_pool, 0, D_POOL))
```

```python
import functools
import math

import jax
import jax.numpy as jnp
from jax import lax
from jax.experimental import pallas as pl
from jax.experimental.pallas import tpu as pltpu

D_MODEL = 1024
DEPTH = 2
PAST_LEN = 16384
D_POOL = D_MODEL // 2
POOL_WINDOWS = (2, 4, 8, 16)
N_POOL_GROUPS = len(POOL_WINDOWS)
POOL_GROUP_DIM = D_POOL // N_POOL_GROUPS
POOL_HIST = max(POOL_WINDOWS) - 1
HEAD_DIM = 64
N_HEADS = (D_MODEL // 2) // HEAD_DIM
N_KV_HEADS = 2
GROUP = N_HEADS // N_KV_HEADS
Q_DIM = N_HEADS * HEAD_DIM
KV_DIM = N_KV_HEADS * HEAD_DIM
WINDOW = 128
ATTN_SCALE = HEAD_DIM ** -0.5
REL_BUCKETS = 32
REL_MAX_DISTANCE = 128
D_IN = D_POOL + Q_DIM + 2 * KV_DIM + 2 * D_MODEL
N_EXPERT_GROUPS = 4
EXPERTS_PER_GROUP = 8
N_EXPERTS = N_EXPERT_GROUPS * EXPERTS_PER_GROUP
D_EXPERT = D_MODEL // 4
DEEPNORM_ALPHA = (2 * DEPTH) ** 0.25
LN_EPS = 1e-5

LANES = 128
SUBLANES = 8
VMEM_LIMIT_BYTES = 56 * 1024 * 1024

_O_Q = D_POOL
_O_K = _O_Q + Q_DIM
_O_V = _O_K + KV_DIM
_O_GA = _O_V + KV_DIM
_O_GB = _O_GA + D_MODEL

ROW_CHUNKS = D_MODEL // LANES
KEY_PAD = 2 * WINDOW
EXPERT_TILE = 256
EXPERT_ROW_STEP = 32
DMA_QUEUES = 2
PAIRS_PER_GROUP = EXPERTS_PER_GROUP * (EXPERTS_PER_GROUP - 1) // 2
N_CLASSES = N_EXPERT_GROUPS * PAIRS_PER_GROUP
_CLASS_EXPERTS = tuple((g * EXPERTS_PER_GROUP + a, g * EXPERTS_PER_GROUP + b)
                       for g in range(N_EXPERT_GROUPS)
                       for a in range(EXPERTS_PER_GROUP) for b in range(a + 1, EXPERTS_PER_GROUP))
ROUTER_LANES = LANES
_GROUP_LANE0 = N_EXPERTS
assert EXPERTS_PER_GROUP == SUBLANES and N_EXPERT_GROUPS <= SUBLANES

BF16 = jnp.bfloat16
F32 = jnp.float32
NEG_INF = float("-inf")


def _cparams(*sem):
    return pltpu.CompilerParams(dimension_semantics=sem, vmem_limit_bytes=VMEM_LIMIT_BYTES)


def _resident(shape):
    nd = len(shape)
    return pl.BlockSpec(shape, lambda *_: (0,) * nd, pipeline_mode=pl.Buffered(1))


def _layer_resident(stacked, layer):
    nd = stacked.ndim - 1
    return pl.BlockSpec((None,) + stacked.shape[1:], lambda *_: (layer,) + (0,) * nd,
                        pipeline_mode=pl.Buffered(1))


def _layer_norm(y, g, b):
    mu = jnp.mean(y, axis=-1, keepdims=True)
    yc = y - mu
    var = jnp.mean(yc * yc, axis=-1, keepdims=True)
    return yc * lax.rsqrt(var + LN_EPS) * g + b


def _softmax_pv(s, sink, v_bf16):
    m = jnp.maximum(jnp.max(s, axis=-1, keepdims=True), sink)
    p = jnp.exp(s - m)
    den = jnp.sum(p, axis=-1, keepdims=True) + jnp.exp(sink - m)
    o = jnp.dot(p.astype(BF16), v_bf16, preferred_element_type=F32)
    return o / den


def _half_lane_variants(t):
    lane = lax.broadcasted_iota(jnp.int32, t.shape, 1)
    lo = lane < HEAD_DIM
    tr = pltpu.roll(t, HEAD_DIM, 1)
    z = jnp.zeros_like(t)
    return ((jnp.where(lo, t, z), jnp.where(lo, z, tr)),
            (jnp.where(lo, tr, z), jnp.where(lo, z, t)))


def _mix_tail(x, a_in, b_in, ga, gb, pw, ps, pa, pb, wo, g, bt):
    a = jnp.dot(a_in.astype(BF16), pw, preferred_element_type=F32) * ps
    pa_o = jnp.dot(a.astype(BF16), pa, preferred_element_type=F32)
    pb_o = jnp.dot(b_in.astype(BF16), pb, preferred_element_type=F32)
    merged = jax.nn.sigmoid(ga) * pa_o + jax.nn.sigmoid(gb) * pb_o
    mix = jnp.dot(merged.astype(BF16), wo, preferred_element_type=F32)
    return _layer_norm(DEEPNORM_ALPHA * x + mix, g, bt)


def _prompt_mixer_kernel(x_ref, win_ref, pw_ref, ps_ref, pa_ref, pb_ref, wo_ref, bias_ref, sink_ref,
                         g_ref, bt_ref, x1_ref, kst_ref, vst_ref, pst_ref, kt_scr, v_scr, ext_scr, *, tm):
    i = pl.program_id(1)
    hist = 2 * SUBLANES

    @pl.when(i == 0)
    def _():
        kt_scr[:, :, :WINDOW] = jnp.zeros((4, LANES, WINDOW), BF16)
        v_scr[:, :WINDOW, :] = jnp.zeros((4, WINDOW, LANES), BF16)
        ext_scr[:hist, :] = jnp.zeros((hist, D_POOL), F32)

    @pl.when(i > 0)
    def _():
        kt_scr[:, :, :WINDOW] = kt_scr[:, :, tm:tm + WINDOW]
        v_scr[:, :WINDOW, :] = v_scr[:, tm:tm + WINDOW, :]
        ext_scr[:hist, :] = ext_scr[tm:tm + hist, :]

    x = x_ref[...]
    u = jnp.dot(x.astype(BF16), win_ref[...], preferred_element_type=F32)
    u_pool = u[:, :D_POOL]
    q = u[:, _O_Q:_O_K]
    k = u[:, _O_K:_O_V]
    v = u[:, _O_V:_O_GA]
    ga = u[:, _O_GA:_O_GB]
    gb = u[:, _O_GB:]

    kst_ref[...] = k[tm - WINDOW:, :]
    vst_ref[...] = v[tm - WINDOW:, :]
    pst_ref[...] = u_pool[tm - hist:, :]

    kt = k.T
    kz = jnp.zeros((HEAD_DIM, tm), F32)
    kt_scr[0, :, WINDOW:] = jnp.concatenate([kt[:HEAD_DIM], kz], axis=0).astype(BF16)
    kt_scr[1, :, WINDOW:] = jnp.concatenate([kz, kt[:HEAD_DIM]], axis=0).astype(BF16)
    kt_scr[2, :, WINDOW:] = jnp.concatenate([kt[HEAD_DIM:], kz], axis=0).astype(BF16)
    kt_scr[3, :, WINDOW:] = jnp.concatenate([kz, kt[HEAD_DIM:]], axis=0).astype(BF16)
    vv = _half_lane_variants(v)
    for kvh in range(N_KV_HEADS):
        for par in range(2):
            v_scr[2 * kvh + par, WINDOW:, :] = vv[kvh][par].astype(BF16)

    col = lax.broadcasted_iota(jnp.int32, (2 * WINDOW, 2 * WINDOW), 1)
    no_prev = col < jnp.where(i == 0, WINDOW, 0)
    qb = q.astype(BF16)
    b_rows = []
    for j in range(tm // WINDOW):
        r0 = j * WINDOW
        o_kvh = []
        for kvh in range(N_KV_HEADS):
            c0 = kvh * 2 * LANES
            q2 = jnp.concatenate([qb[r0:r0 + WINDOW, c0:c0 + LANES],
                                  qb[r0:r0 + WINDOW, c0 + LANES:c0 + 2 * LANES]], axis=0)
            acc = None
            for par in range(2):
                s = jnp.dot(q2, kt_scr[2 * kvh + par, :, r0:r0 + 2 * WINDOW], preferred_element_type=F32)
                s = s * ATTN_SCALE + bias_ref[kvh, par]
                if j == 0:
                    s = jnp.where(no_prev, NEG_INF, s)
                o = _softmax_pv(s, sink_ref[kvh, par], v_scr[2 * kvh + par, r0:r0 + 2 * WINDOW, :])
                acc = o if acc is None else acc + o
            o_kvh.append(acc)
        b_rows.append(jnp.concatenate([o_kvh[0][:WINDOW], o_kvh[0][WINDOW:],
                                       o_kvh[1][:WINDOW], o_kvh[1][WINDOW:]], axis=1))
    b = jnp.concatenate(b_rows, axis=0)

    ext_scr[hist:, :] = u_pool
    pos = i * tm + lax.broadcasted_iota(jnp.int32, (tm, 1), 0)
    means = []
    for gi, w in enumerate(POOL_WINDOWS):
        c0 = gi * POOL_GROUP_DIM
        acc = u_pool[:, c0:c0 + POOL_GROUP_DIM]
        for d in range(1, w):
            acc = acc + ext_scr[pl.ds(hist - d, tm), c0:c0 + POOL_GROUP_DIM]
        cnt = jnp.minimum(w, pos + 1).astype(F32)
        means.append(acc / cnt)
    pooled = jnp.concatenate(means, axis=1) - u_pool

    x1_ref[...] = _mix_tail(x, pooled, b, ga, gb, pw_ref[...], ps_ref[...], pa_ref[...], pb_ref[...],
                            wo_ref[...], g_ref[...], bt_ref[...])


def _prompt_mixer(x, lw, bias_p, sink_t, *, bsz, n_total, tm=512):
    seq = x.shape[0] // bsz
    nt = seq // tm
    hist = 2 * SUBLANES
    kern = functools.partial(_prompt_mixer_kernel, tm=tm)
    tile = pl.BlockSpec((tm, D_MODEL), lambda b, i: (b * nt + i, 0))
    state = lambda r, w: pl.BlockSpec((None, r, w), lambda b, i: (b, 0, 0))
    params = (lw["w_in"], lw["pool_w"], lw["pool_scale"], lw["proj_a"], lw["proj_b"], lw["w_out"])
    tail = (sink_t, lw["ln1_g"], lw["ln1_b"])
    return pl.pallas_call(
        kern,
        grid=(bsz, nt),
        in_specs=([tile] + [_layer_resident(a, lw["layer"]) for a in params] + [_resident(bias_p.shape)]
                  + [_layer_resident(a, lw["layer"]) for a in tail]),
        out_specs=[tile, state(WINDOW, KV_DIM), state(WINDOW, KV_DIM), state(hist, D_POOL)],
        out_shape=[jax.ShapeDtypeStruct((n_total, D_MODEL), F32),
                   jax.ShapeDtypeStruct((bsz, WINDOW, KV_DIM), F32),
                   jax.ShapeDtypeStruct((bsz, WINDOW, KV_DIM), F32),
                   jax.ShapeDtypeStruct((bsz, hist, D_POOL), F32)],
        scratch_shapes=[pltpu.VMEM((4, LANES, WINDOW + tm), BF16),
                        pltpu.VMEM((4, WINDOW + tm, LANES), BF16),
                        pltpu.VMEM((hist + tm, D_POOL), F32)],
        compiler_params=_cparams("arbitrary", "arbitrary"),
        name="prompt_mixer",
    )(x, lw["w_in"], lw["pool_w"], lw["pool_scale"], lw["proj_a"], lw["proj_b"], lw["w_out"],
      bias_p, sink_t, lw["ln1_g"], lw["ln1_b"])


def _in_proj_kernel(x_ref, win_ref, u_ref):
    u_ref[...] = jnp.dot(x_ref[...].astype(BF16), win_ref[...], preferred_element_type=F32)


def _in_proj(x, lw):
    n = x.shape[0]
    return pl.pallas_call(
        _in_proj_kernel,
        grid=(1,),
        in_specs=[pl.BlockSpec(x.shape, lambda i: (0, 0)), _layer_resident(lw["w_in"], lw["layer"])],
        out_specs=pl.BlockSpec((n, D_IN), lambda i: (0, 0)),
        out_shape=jax.ShapeDtypeStruct((n, D_IN), F32),
        compiler_params=_cparams("arbitrary"),
        name="decode_in_proj",
    )(x, lw["w_in"])


def _decode_attn_kernel(q_ref, kn_ref, vn_ref, kc_ref, vc_ref, bias_ref, biasn_ref, sink_ref, b_ref, *, g, ntok):
    slabs = 2 * ntok
    rows = slabs * g
    def cache_variants(ref):
        z = jnp.zeros((HEAD_DIM, WINDOW), F32)
        out = []
        for kvh in range(N_KV_HEADS):
            out.append((jnp.concatenate([jnp.concatenate([ref[n, kvh], z], axis=0) for n in range(g)], axis=1),
                        jnp.concatenate([jnp.concatenate([z, ref[n, kvh]], axis=0) for n in range(g)], axis=1)))
        return out

    kk = cache_variants(kc_ref)
    vv = cache_variants(vc_ref)
    kn = [_half_lane_variants(kn_ref[t]) for t in range(ntok)]
    vn = [_half_lane_variants(vn_ref[t]) for t in range(ntok)]
    per_row = lambda a: jnp.concatenate([a] * slabs, axis=0)
    row_seq = lax.rem(lax.broadcasted_iota(jnp.int32, (rows, WINDOW), 0), g)
    diag = (lax.rem(lax.broadcasted_iota(jnp.int32, (rows, g * WINDOW), 0), g)
            == lax.broadcasted_iota(jnp.int32, (rows, g * WINDOW), 1) // WINDOW)
    for kvh in range(N_KV_HEADS):
        c0 = kvh * 2 * LANES
        q2f = jnp.concatenate([q_ref[t, :, c0 + p * LANES:c0 + (p + 1) * LANES]
                               for p in range(2) for t in range(ntok)], axis=0)
        q2 = q2f.astype(BF16)
        acc = None
        for par in range(2):
            full = jnp.dot(q2, kk[kvh][par].astype(BF16), preferred_element_type=F32)
            s = jnp.zeros((rows, WINDOW), F32)
            for n in range(g):
                s = jnp.where(row_seq == n, full[:, n * WINDOW:(n + 1) * WINDOW], s)
            s = s * ATTN_SCALE + bias_ref[kvh, par]
            s_new = [jnp.sum(q2f * per_row(kn[t][kvh][par]), axis=-1, keepdims=True) * ATTN_SCALE
                     + biasn_ref[kvh, par, t] for t in range(ntok)]
            sink = sink_ref[kvh, par]
            m = jnp.maximum(jnp.max(s, axis=-1, keepdims=True), sink)
            for t in range(ntok):
                m = jnp.maximum(m, s_new[t])
            p = jnp.exp(s - m)
            den = jnp.sum(p, axis=-1, keepdims=True) + jnp.exp(sink - m)
            p_bd = jnp.where(diag, jnp.concatenate([p] * g, axis=1), 0.0).astype(BF16)
            o = lax.dot_general(p_bd, vv[kvh][par].astype(BF16), (((1,), (1,)), ((), ())),
                                preferred_element_type=F32)
            for t in range(ntok):
                p_new = jnp.exp(s_new[t] - m)
                den = den + p_new
                o = o + p_new * per_row(vn[t][kvh][par])
            o = o / den
            acc = o if acc is None else acc + o
        for p in range(2):
            for t in range(ntok):
                r0 = (p * ntok + t) * g
                b_ref[t, :, c0 + p * LANES:c0 + (p + 1) * LANES] = acc[r0:r0 + g]


def _decode_attn(u3, cache_k, cache_v, bias_d, biasn_d, sink_d, *, g, layer):
    ntok, nseq, _ = u3.shape
    cache = pl.BlockSpec((None, g, N_KV_HEADS, HEAD_DIM, WINDOW), lambda i: (layer, i, 0, 0, 0))
    kern = functools.partial(_decode_attn_kernel, g=g, ntok=ntok)
    new_cols = lambda w, off: pl.BlockSpec((ntok, g, w), lambda i: (0, i, off // w))
    return pl.pallas_call(
        kern,
        grid=(nseq // g,),
        in_specs=[new_cols(Q_DIM, _O_Q), new_cols(KV_DIM, _O_K), new_cols(KV_DIM, _O_V),
                  cache, cache,
                  _resident(bias_d.shape), _resident(biasn_d.shape), _layer_resident(sink_d, layer)],
        out_specs=pl.BlockSpec((ntok, g, Q_DIM), lambda i: (0, i, 0)),
        out_shape=jax.ShapeDtypeStruct((ntok, nseq, Q_DIM), F32),
        compiler_params=_cparams("arbitrary"),
        name="decode_attn",
    )(u3, u3, u3, cache_k, cache_v, bias_d, biasn_d, sink_d)


def _decode_tail_kernel(x_ref, u_ref, hist_ref, b_ref, pw_ref, ps_ref, pa_ref, pb_ref,
                        wo_ref, g_ref, bt_ref, buf_ref, x1_ref, *, nseq, ntok, start_pos):
    del buf_ref

    def ext_row(r, c0):
        if r < POOL_HIST:
            return hist_ref[r, :, c0:c0 + POOL_GROUP_DIM]
        return u_ref[(r - POOL_HIST) * nseq:(r - POOL_HIST + 1) * nseq, c0:c0 + POOL_GROUP_DIM]

    pooled_rows = []
    for t in range(ntok):
        means = []
        for gi, w in enumerate(POOL_WINDOWS):
            acc = None
            for d in range(w):
                r = POOL_HIST + t - d
                if start_pos - POOL_HIST + r < 0:
                    continue
                term = ext_row(r, gi * POOL_GROUP_DIM)
                acc = term if acc is None else acc + term
            means.append(acc / float(min(w, start_pos + t + 1)))
        pooled_rows.append(jnp.concatenate(means, axis=1))
    pooled = jnp.concatenate(pooled_rows, axis=0) - u_ref[:, :D_POOL]
    x1_ref[...] = _mix_tail(x_ref[...], pooled, b_ref[...], u_ref[:, _O_GA:_O_GB], u_ref[:, _O_GB:], pw_ref[...],
                            ps_ref[...], pa_ref[...], pb_ref[...], wo_ref[...], g_ref[...], bt_ref[...])


def _decode_tail(x, u, hist_t, b, lw, buf, *, nseq, ntok, row_block):
    n = ntok * nseq
    kern = functools.partial(_decode_tail_kernel, nseq=nseq, ntok=ntok, start_pos=PAST_LEN)
    whole = lambda a: pl.BlockSpec(a.shape, lambda i: (0,) * a.ndim)
    layered = lambda a: _layer_resident(a, lw["layer"])
    params = (lw["pool_w"], lw["pool_scale"], lw["proj_a"], lw["proj_b"], lw["w_out"], lw["ln1_g"], lw["ln1_b"])
    ins = (x, u, hist_t, b) + params
    return pl.pallas_call(
        kern,
        grid=(1,),
        in_specs=([whole(x), whole(u), layered(hist_t), whole(b)] + [layered(a) for a in params]
                  + [pl.BlockSpec(memory_space=pl.ANY)]),
        out_specs=pl.BlockSpec((n, D_MODEL), lambda i: (row_block, 0)),
        out_shape=jax.ShapeDtypeStruct(buf.shape, F32),
        input_output_aliases={len(ins): 0},
        compiler_params=_cparams("arbitrary"),
        name="decode_tail",
    )(*ins, buf)


def _router_logits(x, whi, wlo, br):
    xh = x.astype(BF16)
    xl = (x - xh.astype(F32)).astype(BF16)
    return (jnp.dot(xh, whi, preferred_element_type=F32)
            + (jnp.dot(xh, wlo, preferred_element_type=F32)
               + jnp.dot(xl, whi, preferred_element_type=F32))) + br


def _group_logits(logits, lane):
    is_group = jnp.logical_and(lane >= _GROUP_LANE0, lane < _GROUP_LANE0 + N_EXPERT_GROUPS)
    return jnp.where(is_group, logits, NEG_INF)


def _router_kernel(x_ref, whi_ref, wlo_ref, br_ref, meta_ref, cnt_ref, run_scr, *, tr):
    i = pl.program_id(0)

    @pl.when(i == 0)
    def _():
        run_scr[...] = jnp.zeros_like(run_scr)

    lt = _router_logits(x_ref[...], whi_ref[...], wlo_ref[...], br_ref[...]).T
    row = lax.broadcasted_iota(jnp.int32, (EXPERTS_PER_GROUP, tr), 0)
    none = jnp.int32(EXPERTS_PER_GROUP)

    def argmax_rows(vals):
        vmax = jnp.max(vals, axis=0, keepdims=True)
        return jnp.min(jnp.where(vals == vmax, row, none), axis=0, keepdims=True)

    glog = jnp.where(row < N_EXPERT_GROUPS, lt[_GROUP_LANE0:_GROUP_LANE0 + EXPERTS_PER_GROUP], NEG_INF)
    gidx = argmax_rows(glog)
    el = lt[:EXPERTS_PER_GROUP]
    for g in range(1, N_EXPERT_GROUPS):
        el = jnp.where(gidx == g, lt[g * EXPERTS_PER_GROUP:(g + 1) * EXPERTS_PER_GROUP], el)
    e1 = argmax_rows(el)
    e2 = argmax_rows(jnp.where(row == e1, NEG_INF, el))
    lo = jnp.minimum(e1, e2)
    hi = jnp.maximum(e1, e2)
    pair = lax.shift_right_logical(lo * (2 * EXPERTS_PER_GROUP - 1 - lo), 1) + (hi - lo - 1)
    cls = gidx * PAIRS_PER_GROUP + pair

    oh = jnp.where(lax.broadcasted_iota(jnp.int32, (ROUTER_LANES, tr), 0) == cls, 1.0, 0.0)
    rr = lax.broadcasted_iota(jnp.int32, (tr, tr), 0)
    cc = lax.broadcasted_iota(jnp.int32, (tr, tr), 1)
    earlier = jnp.where(rr < cc, 1.0, 0.0).astype(BF16)
    before = jnp.dot(oh.astype(BF16), earlier, preferred_element_type=F32) + run_scr[...]
    rank = jnp.sum(oh * before, axis=0, keepdims=True)
    run_scr[...] += jnp.sum(oh, axis=1, keepdims=True)
    cnt_ref[...] = run_scr[...]

    meta_ref[...] = jnp.where(row == 0, cls.astype(F32), jnp.where(row == 1, rank, 0.0))


def _router(x, lw, *, tr=512):
    t = x.shape[0]
    kern = functools.partial(_router_kernel, tr=tr)
    return pl.pallas_call(
        kern,
        grid=(t // tr,),
        in_specs=[pl.BlockSpec((tr, D_MODEL), lambda i: (i, 0)),
                  _layer_resident(lw["router_hi"], lw["layer"]), _layer_resident(lw["router_lo"], lw["layer"]),
                  _layer_resident(lw["router_b"], lw["layer"])],
        out_specs=[pl.BlockSpec((None, SUBLANES, tr), lambda i: (i, 0, 0)),
                   pl.BlockSpec((ROUTER_LANES, 1), lambda i: (0, 0))],
        out_shape=[jax.ShapeDtypeStruct((t // tr, SUBLANES, tr), F32),
                   jax.ShapeDtypeStruct((ROUTER_LANES, 1), F32)],
        scratch_shapes=[pltpu.VMEM((ROUTER_LANES, 1), F32)],
        compiler_params=_cparams("arbitrary"),
        name="router",
    )(x, lw["router_hi"], lw["router_lo"], lw["router_b"])


def _rows_copy(src, dst, sem):
    return pltpu.make_async_copy(src, dst, sem)


def _dispatch_kernel(slot_ref, x_ref, xs_ref, stage, sems, *, td):
    i = pl.program_id(0)
    n = pl.num_programs(0)
    sb = lax.rem(i, 2)
    x = x_ref[...]
    for j in range(ROW_CHUNKS):
        stage[sb, pl.ds(j, td, stride=ROW_CHUNKS), :] = x[:, j * LANES:(j + 1) * LANES]

    def issue(r2, carry):
        for q in range(DMA_QUEUES):
            r = r2 * DMA_QUEUES + q
            src = stage.at[sb, pl.ds(pl.multiple_of(r * ROW_CHUNKS, ROW_CHUNKS), ROW_CHUNKS)]
            s = slot_ref[0, 0, r]
            dst = xs_ref.at[pl.ds(pl.multiple_of(s * ROW_CHUNKS, ROW_CHUNKS), ROW_CHUNKS)]
            _rows_copy(src, dst, sems.at[sb]).start(priority=q)
        return carry

    lax.fori_loop(0, td // DMA_QUEUES, issue, 0)

    def drain(buf):
        _rows_copy(stage.at[buf], xs_ref.at[pl.ds(0, td * ROW_CHUNKS)], sems.at[buf]).wait()

    @pl.when(i > 0)
    def _():
        drain(1 - sb)

    @pl.when(i == n - 1)
    def _():
        drain(sb)


def _dispatch(x, slots, n_slots, *, td=512):
    t = x.shape[0]
    slots3 = slots.reshape(t // td, 1, td)
    kern = functools.partial(_dispatch_kernel, td=td)
    return pl.pallas_call(
        kern,
        grid=(t // td,),
        in_specs=[pl.BlockSpec((1, 1, td), lambda i: (i, 0, 0), memory_space=pltpu.SMEM),
                  pl.BlockSpec((td, D_MODEL), lambda i: (i, 0))],
        out_specs=pl.BlockSpec(memory_space=pl.ANY),
        out_shape=jax.ShapeDtypeStruct((n_slots * ROW_CHUNKS, LANES), F32),
        scratch_shapes=[pltpu.VMEM((2, td * ROW_CHUNKS, LANES), F32), pltpu.SemaphoreType.DMA((2,))],
        compiler_params=_cparams("arbitrary"),
        name="moe_dispatch",
    )(slots3, x)


def _expert_kernel(ea_ref, eb_ref, rows_ref, nv_ref, xs_ref, rhi_ref, rb_ref, wg_ref, wu_ref, wd_ref, os_ref,
                   wg_bf, wu_bf, wd_bf):
    i = pl.program_id(0)
    valid = i < nv_ref[0]
    ea = ea_ref[i]
    eb = eb_ref[i]
    row_steps = (rows_ref[i] + EXPERT_ROW_STEP - 1) // EXPERT_ROW_STEP
    group = ea // EXPERTS_PER_GROUP
    prev_group = ea_ref[jnp.maximum(i - 1, 0)] // EXPERTS_PER_GROUP
    new_group = jnp.logical_or(i == 0, group != prev_group)

    @pl.when(jnp.logical_and(valid, new_group))
    def _():
        for e in range(EXPERTS_PER_GROUP):
            wg_bf[e] = wg_ref[e].astype(BF16)
            wu_bf[e] = wu_ref[e].astype(BF16)
            wd_bf[e] = wd_ref[e].astype(BF16)

    def body(m):
        x = jnp.concatenate([xs_ref[pl.ds(j, m, stride=ROW_CHUNKS), :] for j in range(ROW_CHUNKS)], axis=1)
        xt = x.astype(BF16)
        logits = jnp.dot(xt, rhi_ref[...], preferred_element_type=F32) + rb_ref[...]
        lane = lax.broadcasted_iota(jnp.int32, logits.shape, 1)
        pick = lambda l: jnp.sum(jnp.where(lane == l, logits, 0.0), axis=-1, keepdims=True)
        glog = _group_logits(logits, lane)
        gmax = jnp.max(glog, axis=-1, keepdims=True)
        gw = jnp.exp(pick(_GROUP_LANE0 + group) - gmax) / jnp.sum(jnp.exp(glog - gmax), axis=-1, keepdims=True)
        la = pick(ea)
        lb = pick(eb)
        lmax = jnp.maximum(la, lb)
        pa = jnp.exp(la - lmax)
        pb = jnp.exp(lb - lmax)
        scale = gw / (pa + pb)

        y = None
        for e, w in ((ea, pa * scale), (eb, pb * scale)):
            el = e - group * EXPERTS_PER_GROUP
            gate = jnp.dot(xt, wg_bf[el], preferred_element_type=F32)
            up = jnp.dot(xt, wu_bf[el], preferred_element_type=F32)
            h = (jax.nn.silu(gate) * up).astype(BF16)
            term = w * jnp.dot(h, wd_bf[el], preferred_element_type=F32)
            y = term if y is None else y + term
        for j in range(ROW_CHUNKS):
            os_ref[pl.ds(j, m, stride=ROW_CHUNKS), :] = y[:, j * LANES:(j + 1) * LANES]

    for k in range(1, EXPERT_TILE // EXPERT_ROW_STEP + 1):
        pl.when(jnp.logical_and(valid, row_steps == k))(functools.partial(body, k * EXPERT_ROW_STEP))


def _experts(xs, tile_ea, tile_eb, tile_rows, n_valid, lw):
    layer = lw["layer"]
    n_tiles = xs.shape[0] // (EXPERT_TILE * ROW_CHUNKS)
    rows = EXPERT_TILE * ROW_CHUNKS
    act = pl.BlockSpec((rows, LANES), lambda i, ea, eb, nr, nv: (jnp.minimum(i, nv[0] - 1), 0))
    group_of = lambda ea, i: ea[i] // EXPERTS_PER_GROUP
    group_w = lambda r, c: pl.BlockSpec((None, EXPERTS_PER_GROUP, r, c),
                                        lambda i, ea, eb, nr, nv: (layer, group_of(ea, i), 0, 0),
                                        pipeline_mode=pl.Buffered(1))
    router = lambda a: pl.BlockSpec((None,) + a.shape[1:], lambda i, ea, eb, nr, nv: (layer, 0, 0),
                                    pipeline_mode=pl.Buffered(1))
    grid_spec = pltpu.PrefetchScalarGridSpec(
        num_scalar_prefetch=4,
        grid=(n_tiles,),
        in_specs=[act, router(lw["router_hi"]), router(lw["router_b"]),
                  group_w(D_MODEL, D_EXPERT), group_w(D_MODEL, D_EXPERT), group_w(D_EXPERT, D_MODEL)],
        out_specs=act,
        scratch_shapes=[pltpu.VMEM((EXPERTS_PER_GROUP, D_MODEL, D_EXPERT), BF16),
                        pltpu.VMEM((EXPERTS_PER_GROUP, D_MODEL, D_EXPERT), BF16),
                        pltpu.VMEM((EXPERTS_PER_GROUP, D_EXPERT, D_MODEL), BF16)],
    )
    return pl.pallas_call(
        _expert_kernel,
        grid_spec=grid_spec,
        out_shape=jax.ShapeDtypeStruct(xs.shape, F32),
        compiler_params=_cparams("arbitrary"),
        name="moe_experts",
    )(tile_ea, tile_eb, tile_rows, n_valid, xs, lw["router_hi"], lw["router_b"],
      lw["w_gate"], lw["w_up"], lw["w_down"])


def _gather_rows_issue(slot_ref, os_ref, gbuf, sems, buf, tc):
    def issue(r2, carry):
        for q in range(DMA_QUEUES):
            r = r2 * DMA_QUEUES + q
            s = slot_ref[0, 0, r]
            src = os_ref.at[pl.ds(pl.multiple_of(s * ROW_CHUNKS, ROW_CHUNKS), ROW_CHUNKS)]
            dst = gbuf.at[buf, pl.ds(pl.multiple_of(r * ROW_CHUNKS, ROW_CHUNKS), ROW_CHUNKS)]
            _rows_copy(src, dst, sems.at[buf]).start(priority=q)
        return carry

    lax.fori_loop(0, tc // DMA_QUEUES, issue, 0)


def _combine_kernel(slot_ref, slot_next_ref, x_ref, g_ref, bt_ref, os_ref, xp_ref, xd_ref,
                    gbuf, sems, *, tc, n_first):
    i = pl.program_id(0)
    n = pl.num_programs(0)
    sb = lax.rem(i, 2)

    @pl.when(i == 0)
    def _():
        _gather_rows_issue(slot_ref, os_ref, gbuf, sems, 0, tc)

    @pl.when(i + 1 < n)
    def _():
        _gather_rows_issue(slot_next_ref, os_ref, gbuf, sems, 1 - sb, tc)

    _rows_copy(os_ref.at[pl.ds(0, tc * ROW_CHUNKS)], gbuf.at[sb], sems.at[sb]).wait()
    y = jnp.concatenate([gbuf[sb, pl.ds(j, tc, stride=ROW_CHUNKS), :] for j in range(ROW_CHUNKS)], axis=1)
    x2 = _layer_norm(DEEPNORM_ALPHA * x_ref[...] + y, g_ref[...], bt_ref[...])

    @pl.when(i < n_first)
    def _():
        xp_ref[...] = x2

    @pl.when(i >= n_first)
    def _():
        xd_ref[...] = x2


def _combine(x, slots, os_, lw, *, n_prompt, tc=512):
    t = x.shape[0]
    nblk = t // tc
    n_first = n_prompt // tc
    slots3 = slots.reshape(nblk, 1, tc)
    kern = functools.partial(_combine_kernel, tc=tc, n_first=n_first)
    smem_tile = lambda f: pl.BlockSpec((1, 1, tc), f, memory_space=pltpu.SMEM)
    return pl.pallas_call(
        kern,
        grid=(nblk,),
        in_specs=[smem_tile(lambda i: (i, 0, 0)),
                  smem_tile(lambda i: (jnp.minimum(i + 1, nblk - 1), 0, 0)),
                  pl.BlockSpec((tc, D_MODEL), lambda i: (i, 0)),
                  _layer_resident(lw["ln2_g"], lw["layer"]), _layer_resident(lw["ln2_b"], lw["layer"]),
                  pl.BlockSpec(memory_space=pl.ANY)],
        out_specs=[pl.BlockSpec((tc, D_MODEL), lambda i: (jnp.minimum(i, n_first - 1), 0)),
                   pl.BlockSpec((tc, D_MODEL), lambda i: (jnp.maximum(i - n_first, 0), 0))],
        out_shape=[jax.ShapeDtypeStruct((n_prompt, D_MODEL), F32),
                   jax.ShapeDtypeStruct((t - n_prompt, D_MODEL), F32)],
        scratch_shapes=[pltpu.VMEM((2, tc * ROW_CHUNKS, LANES), F32), pltpu.SemaphoreType.DMA((2,))],
        compiler_params=_cparams("arbitrary"),
        name="moe_combine",
    )(slots3, slots3, x, lw["ln2_g"], lw["ln2_b"], os_)


def _moe(x, lw, *, n_prompt):
    t = x.shape[0]
    n_tiles = t // EXPERT_TILE + N_CLASSES
    meta, cnt = _router(x, lw)
    cls = meta[:, 0, :].reshape(t).astype(jnp.int32)
    rank = meta[:, 1, :].reshape(t).astype(jnp.int32)
    counts = cnt[:N_CLASSES, 0].astype(jnp.int32)
    seg_tiles = (counts + EXPERT_TILE - 1) // EXPERT_TILE
    tile_ends = jnp.cumsum(seg_tiles)
    seg_start = (tile_ends - seg_tiles) * EXPERT_TILE
    class_ids = jnp.arange(N_CLASSES, dtype=jnp.int32)
    slots = jnp.sum(jnp.where(cls[:, None] == class_ids, seg_start, 0), axis=-1) + rank
    n_valid = tile_ends[-1:]
    tile_ids = jnp.minimum(jnp.arange(n_tiles, dtype=jnp.int32), n_valid[0] - 1)
    tile_class = jnp.sum((tile_ends[None, :] <= tile_ids[:, None]).astype(jnp.int32), axis=1)
    of_class = tile_class[:, None] == class_ids
    class_experts = jnp.asarray(_CLASS_EXPERTS, dtype=jnp.int32)
    tile_ea = jnp.sum(jnp.where(of_class, class_experts[:, 0], 0), axis=-1)
    tile_eb = jnp.sum(jnp.where(of_class, class_experts[:, 1], 0), axis=-1)
    rows_left = jnp.sum(jnp.where(of_class, seg_start + counts, 0), axis=-1) - tile_ids * EXPERT_TILE
    tile_rows = jnp.clip(rows_left, 0, EXPERT_TILE)
    xs = _dispatch(x, slots, n_tiles * EXPERT_TILE)
    os_ = _experts(xs, tile_ea, tile_eb, tile_rows, n_valid.astype(jnp.int32), lw)
    return _combine(x, slots, os_, lw, n_prompt=n_prompt)


def _t5_bucket(dist):
    n = jnp.maximum(dist, 0)
    max_exact = REL_BUCKETS // 2
    nf = jnp.maximum(n, 1).astype(F32)
    large = max_exact + (jnp.log(nf / max_exact) / math.log(REL_MAX_DISTANCE / max_exact)
                         * (REL_BUCKETS - max_exact)).astype(jnp.int32)
    large = jnp.minimum(large, REL_BUCKETS - 1)
    return jnp.where(n < max_exact, n, large)


def _head_rows(per_head, rows_of):
    out = []
    for kvh in range(N_KV_HEADS):
        out.append(jnp.stack([jnp.concatenate([jnp.repeat(per_head[..., h, :, :], rep, axis=-2)
                                               for h, rep in rows_of(kvh, par)], axis=-2)
                              for par in range(2)], axis=-3))
    return jnp.stack(out, axis=-4)


def _prompt_rows(kvh, par):
    return [(GROUP * kvh + 2 * p + par, 1) for p in range(2)]


def _head_bias(rel_bias, dist, valid):
    bias = jnp.where(valid[..., None], rel_bias[_t5_bucket(dist)].astype(F32), NEG_INF)
    return jnp.moveaxis(bias, -1, 0)


def _prompt_bias(rel_bias):
    period = 4 * WINDOW
    off = jnp.arange(period)
    off = jnp.where(off < 2 * WINDOW, off, off - period)
    dist = WINDOW - off
    per_off = _head_bias(rel_bias, dist, (dist >= 0) & (dist <= WINDOW))
    tiled = jnp.tile(per_off, (1, WINDOW))[:, :WINDOW * (period - 1)]
    return tiled.reshape(N_HEADS, WINDOW, period - 1)[:, :, :2 * WINDOW]


def _attention_tables(rel_bias, sinks, ntok, g):
    bias_p = _head_rows(_prompt_bias(rel_bias), _prompt_rows)

    decode_rows = lambda kvh, par: [(GROUP * kvh + 2 * p + par, g) for p in range(2)]
    tq = jnp.arange(ntok)[:, None]
    kpos_old = jnp.arange(WINDOW)[None, :] - WINDOW
    kpos_new = jnp.arange(ntok)[None, :]
    tables = []
    for kpos in (kpos_old, kpos_new):
        d = tq - kpos
        per_head = _head_bias(rel_bias, d, (d >= 0) & (d <= WINDOW) & (PAST_LEN + kpos >= 0))
        tables.append(_head_rows(per_head, decode_rows))
    bias_d = tables[0]
    biasn_d = jnp.moveaxis(tables[1], -1, 2)[..., None]

    s = sinks.astype(F32)[:, :, None, None]
    sink_p = _head_rows(jnp.broadcast_to(s, s.shape[:2] + (WINDOW, 1)), _prompt_rows)
    sink_d = _head_rows(jnp.broadcast_to(s, s.shape[:2] + (ntok, 1)), decode_rows)
    return bias_p, bias_d, biasn_d, sink_p, sink_d


def _stacked_weights(w_in, pool_w, pool_scale, proj_a, proj_b, w_out, ln1_g, ln1_b, router_group_w,
                     router_group_b, router_expert_w, router_expert_b, expert_w_gate, expert_w_up,
                     expert_w_down, ln2_g, ln2_b):
    depth = w_in.shape[0]
    same_group = jnp.eye(N_POOL_GROUPS, dtype=bool)[None, :, None, :, None]
    pool_bd = jnp.where(same_group, pool_w[:, :, :, None, :], 0.0).reshape(depth, D_POOL, D_POOL)
    pad = ROUTER_LANES - N_EXPERTS - N_EXPERT_GROUPS
    rw = jnp.concatenate([router_expert_w, router_group_w, jnp.zeros((depth, D_MODEL, pad), F32)], axis=-1)
    rb = jnp.concatenate([router_expert_b, router_group_b, jnp.zeros((depth, pad), F32)], axis=-1)
    r_hi = rw.astype(BF16)
    row = lambda a: a[:, None, :]
    return dict(
        w_in=w_in.astype(BF16), pool_w=pool_bd.astype(BF16), pool_scale=row(pool_scale),
        proj_a=proj_a.astype(BF16), proj_b=proj_b.astype(BF16), w_out=w_out.astype(BF16),
        ln1_g=row(ln1_g), ln1_b=row(ln1_b),
        router_hi=r_hi, router_lo=(rw - r_hi.astype(F32)).astype(BF16), router_b=row(rb),
        w_gate=expert_w_gate, w_up=expert_w_up, w_down=expert_w_down,
        ln2_g=row(ln2_g), ln2_b=row(ln2_b))


def _decode_mixer(xs_tn, lw, cache_k, cache_v, hist_t, tables, buf, *, nseq, ntok, g, row_block):
    bias_d, biasn_d, sink_d = tables
    u = _in_proj(xs_tn, lw)
    u3 = u.reshape(ntok, nseq, D_IN)
    b = _decode_attn(u3, cache_k, cache_v, bias_d, biasn_d, sink_d, g=g, layer=lw["layer"])
    buf = _decode_tail(xs_tn, u, hist_t, b.reshape(ntok * nseq, Q_DIM), lw, buf, nseq=nseq, ntok=ntok,
                       row_block=row_block)
    return buf, u3


def kernel(x_prompt, x_sample, cache_k, cache_v, state_pool, rel_bias, w_in, pool_w, pool_scale, proj_a, proj_b, w_out, attn_sinks, ln1_g, ln1_b, router_group_w, router_group_b, router_expert_w, router_expert_b, expert_w_gate, expert_w_up, expert_w_down, ln2_g, ln2_b):
    bsz, seq, _ = x_prompt.shape
    nseq, ntok, _ = x_sample.shape
    g = 16
    depth = w_in.shape[0]
    n_prompt = bsz * seq
    n_decode = ntok * nseq
    assert n_prompt % n_decode == 0
    bias_p, bias_d, biasn_d, sink_p, sink_d = _attention_tables(rel_bias, attn_sinks, ntok, g)
    weights = _stacked_weights(w_in, pool_w, pool_scale, proj_a, proj_b, w_out, ln1_g, ln1_b, router_group_w,
                               router_group_b, router_expert_w, router_expert_b, expert_w_gate, expert_w_up,
                               expert_w_down, ln2_g, ln2_b)

    xp = x_prompt.reshape(n_prompt, D_MODEL)
    xs_tn = jnp.swapaxes(x_sample, 0, 1).reshape(n_decode, D_MODEL)
    hist_t = jnp.swapaxes(state_pool, 1, 2)
    ck_t = jnp.transpose(cache_k, (0, 1, 3, 4, 2))
    cv_t = jnp.transpose(cache_v, (0, 1, 3, 4, 2))
    prompt_state, decode_u = [], []
    for l in range(depth):
        lw = dict(weights, layer=l)
        x1, kst, vst, pst = _prompt_mixer(xp, lw, bias_p, sink_p, bsz=bsz, n_total=n_prompt + n_decode)
        x1, u3 = _decode_mixer(xs_tn, lw, ck_t, cv_t, hist_t, (bias_d, biasn_d, sink_d), x1,
                               nseq=nseq, ntok=ntok, g=g, row_block=n_prompt // n_decode)
        xp, xs_tn = _moe(x1, lw, n_prompt=n_prompt)
        prompt_state.append((kst, vst, pst))
        decode_u.append(u3)

    pk, pv, pp = (jnp.stack(parts) for parts in zip(*prompt_state))
    heads = lambda a: a.reshape(a.shape[:3] + (N_KV_HEADS, HEAD_DIM))

    def shifted(old, col, width):
        new = jnp.stack([jnp.swapaxes(u3[:, :, col:col + width], 0, 1) for u3 in decode_u])
        return jnp.concatenate([old[:, :, ntok:], new.reshape(new.shape[:3] + old.shape[3:])], axis=2)

    y_sample = jnp.swapaxes(xs_tn.reshape(ntok, nseq, D_MODEL), 0, 1)
    return (xp.reshape(bsz, seq, D_MODEL), y_sample, heads(pk), heads(pv), pp[:, :, -POOL_HIST:],
            shifted(cache_k, _O_K, KV_DIM), shifted(cache_v, _O_V, KV_DIM), shifted(state_pool, 0, D_POOL))
```

```python
import functools
import math

import jax
import jax.numpy as jnp
from jax import lax
from jax.experimental import pallas as pl
from jax.experimental.pallas import tpu as pltpu

D_MODEL = 1024
DEPTH = 2
PAST_LEN = 16384
D_POOL = D_MODEL // 2
POOL_WINDOWS = (2, 4, 8, 16)
N_POOL_GROUPS = len(POOL_WINDOWS)
POOL_GROUP_DIM = D_POOL // N_POOL_GROUPS
POOL_HIST = max(POOL_WINDOWS) - 1
HEAD_DIM = 64
N_HEADS = (D_MODEL // 2) // HEAD_DIM
N_KV_HEADS = 2
GROUP = N_HEADS // N_KV_HEADS
Q_DIM = N_HEADS * HEAD_DIM
KV_DIM = N_KV_HEADS * HEAD_DIM
WINDOW = 128
ATTN_SCALE = HEAD_DIM ** -0.5
REL_BUCKETS = 32
REL_MAX_DISTANCE = 128
D_IN = D_POOL + Q_DIM + 2 * KV_DIM + 2 * D_MODEL
N_EXPERT_GROUPS = 4
EXPERTS_PER_GROUP = 8
N_EXPERTS = N_EXPERT_GROUPS * EXPERTS_PER_GROUP
D_EXPERT = D_MODEL // 4
DEEPNORM_ALPHA = (2 * DEPTH) ** 0.25
LN_EPS = 1e-5

LANES = 128
SUBLANES = 8
VMEM_LIMIT_BYTES = 56 * 1024 * 1024

_O_Q = D_POOL
_O_K = _O_Q + Q_DIM
_O_V = _O_K + KV_DIM
_O_GA = _O_V + KV_DIM
_O_GB = _O_GA + D_MODEL

ROW_CHUNKS = D_MODEL // LANES
KEY_PAD = 2 * WINDOW
EXPERT_TILE = 256
EXPERT_ROW_STEP = 32
TILES_PER_STEP = 2
DMA_QUEUES = 2
ISSUE_UNROLL = 8
PAIRS_PER_GROUP = EXPERTS_PER_GROUP * (EXPERTS_PER_GROUP - 1) // 2
N_CLASSES = N_EXPERT_GROUPS * PAIRS_PER_GROUP
_CLASS_EXPERTS = tuple((g * EXPERTS_PER_GROUP + a, g * EXPERTS_PER_GROUP + b)
                       for g in range(N_EXPERT_GROUPS)
                       for a in range(EXPERTS_PER_GROUP) for b in range(a + 1, EXPERTS_PER_GROUP))
ROUTER_LANES = LANES
_GROUP_LANE0 = N_EXPERTS
assert EXPERTS_PER_GROUP == SUBLANES and N_EXPERT_GROUPS <= SUBLANES

BF16 = jnp.bfloat16
F32 = jnp.float32
NEG_INF = float("-inf")


def _cparams(*sem):
    return pltpu.CompilerParams(dimension_semantics=sem, vmem_limit_bytes=VMEM_LIMIT_BYTES)


def _resident(shape):
    nd = len(shape)
    return pl.BlockSpec(shape, lambda *_: (0,) * nd, pipeline_mode=pl.Buffered(1))


def _layer_resident(stacked, layer):
    nd = stacked.ndim - 1
    return pl.BlockSpec((None,) + stacked.shape[1:], lambda *_: (layer,) + (0,) * nd,
                        pipeline_mode=pl.Buffered(1))


def _layer_norm(y, g, b):
    mu = jnp.mean(y, axis=-1, keepdims=True)
    yc = y - mu
    var = jnp.mean(yc * yc, axis=-1, keepdims=True)
    return yc * lax.rsqrt(var + LN_EPS) * g + b


def _softmax_pv(s, sink, v_bf16):
    m = jnp.maximum(jnp.max(s, axis=-1, keepdims=True), sink)
    p = jnp.exp(s - m)
    den = jnp.sum(p, axis=-1, keepdims=True) + jnp.exp(sink - m)
    o = jnp.dot(p.astype(BF16), v_bf16, preferred_element_type=F32)
    return o / den


def _half_lane_variants(t):
    lane = lax.broadcasted_iota(jnp.int32, t.shape, 1)
    lo = lane < HEAD_DIM
    tr = pltpu.roll(t, HEAD_DIM, 1)
    z = jnp.zeros_like(t)
    return ((jnp.where(lo, t, z), jnp.where(lo, z, tr)),
            (jnp.where(lo, tr, z), jnp.where(lo, z, t)))


def _mix_tail(x, a_in, b_in, ga, gb, pw, ps, pa, pb, wo, g, bt):
    a = jnp.dot(a_in.astype(BF16), pw, preferred_element_type=F32) * ps
    pa_o = jnp.dot(a.astype(BF16), pa, preferred_element_type=F32)
    pb_o = jnp.dot(b_in.astype(BF16), pb, preferred_element_type=F32)
    merged = jax.nn.sigmoid(ga) * pa_o + jax.nn.sigmoid(gb) * pb_o
    mix = jnp.dot(merged.astype(BF16), wo, preferred_element_type=F32)
    return _layer_norm(DEEPNORM_ALPHA * x + mix, g, bt)


def _prompt_mixer_kernel(x_ref, win_ref, pw_ref, ps_ref, pa_ref, pb_ref, wo_ref, bias_ref, sink_ref,
                         g_ref, bt_ref, x1_ref, kst_ref, vst_ref, pst_ref, kt_scr, v_scr, ext_scr, *, tm):
    i = pl.program_id(1)
    hist = 2 * SUBLANES

    @pl.when(i == 0)
    def _():
        kt_scr[:, :, :WINDOW] = jnp.zeros((4, LANES, WINDOW), BF16)
        v_scr[:, :WINDOW, :] = jnp.zeros((4, WINDOW, LANES), BF16)
        ext_scr[:hist, :] = jnp.zeros((hist, D_POOL), F32)

    @pl.when(i > 0)
    def _():
        kt_scr[:, :, :WINDOW] = kt_scr[:, :, tm:tm + WINDOW]
        v_scr[:, :WINDOW, :] = v_scr[:, tm:tm + WINDOW, :]
        ext_scr[:hist, :] = ext_scr[tm:tm + hist, :]

    x = x_ref[...]
    u = jnp.dot(x.astype(BF16), win_ref[...], preferred_element_type=F32)
    u_pool = u[:, :D_POOL]
    q = u[:, _O_Q:_O_K]
    k = u[:, _O_K:_O_V]
    v = u[:, _O_V:_O_GA]
    ga = u[:, _O_GA:_O_GB]
    gb = u[:, _O_GB:]

    kst_ref[...] = k[tm - WINDOW:, :]
    vst_ref[...] = v[tm - WINDOW:, :]
    pst_ref[...] = u_pool[tm - hist:, :]

    kt = k.T
    kz = jnp.zeros((HEAD_DIM, tm), F32)
    kt_scr[0, :, WINDOW:] = jnp.concatenate([kt[:HEAD_DIM], kz], axis=0).astype(BF16)
    kt_scr[1, :, WINDOW:] = jnp.concatenate([kz, kt[:HEAD_DIM]], axis=0).astype(BF16)
    kt_scr[2, :, WINDOW:] = jnp.concatenate([kt[HEAD_DIM:], kz], axis=0).astype(BF16)
    kt_scr[3, :, WINDOW:] = jnp.concatenate([kz, kt[HEAD_DIM:]], axis=0).astype(BF16)
    vv = _half_lane_variants(v)
    for kvh in range(N_KV_HEADS):
        for par in range(2):
            v_scr[2 * kvh + par, WINDOW:, :] = vv[kvh][par].astype(BF16)

    col = lax.broadcasted_iota(jnp.int32, (2 * WINDOW, 2 * WINDOW), 1)
    no_prev = col < jnp.where(i == 0, WINDOW, 0)
    qb = q.astype(BF16)
    b_rows = []
    for j in range(tm // WINDOW):
        r0 = j * WINDOW
        o_kvh = []
        for kvh in range(N_KV_HEADS):
            c0 = kvh * 2 * LANES
            q2 = jnp.concatenate([qb[r0:r0 + WINDOW, c0:c0 + LANES],
                                  qb[r0:r0 + WINDOW, c0 + LANES:c0 + 2 * LANES]], axis=0)
            acc = None
            for par in range(2):
                s = jnp.dot(q2, kt_scr[2 * kvh + par, :, r0:r0 + 2 * WINDOW], preferred_element_type=F32)
                s = s * ATTN_SCALE + bias_ref[kvh, par]
                if j == 0:
                    s = jnp.where(no_prev, NEG_INF, s)
                o = _softmax_pv(s, sink_ref[kvh, par], v_scr[2 * kvh + par, r0:r0 + 2 * WINDOW, :])
                acc = o if acc is None else acc + o
            o_kvh.append(acc)
        b_rows.append(jnp.concatenate([o_kvh[0][:WINDOW], o_kvh[0][WINDOW:],
                                       o_kvh[1][:WINDOW], o_kvh[1][WINDOW:]], axis=1))
    b = jnp.concatenate(b_rows, axis=0)

    ext_scr[hist:, :] = u_pool
    pos = i * tm + lax.broadcasted_iota(jnp.int32, (tm, 1), 0)
    means = []
    for gi, w in enumerate(POOL_WINDOWS):
        c0 = gi * POOL_GROUP_DIM
        acc = u_pool[:, c0:c0 + POOL_GROUP_DIM]
        for d in range(1, w):
            acc = acc + ext_scr[pl.ds(hist - d, tm), c0:c0 + POOL_GROUP_DIM]
        cnt = jnp.minimum(w, pos + 1).astype(F32)
        means.append(acc / cnt)
    pooled = jnp.concatenate(means, axis=1) - u_pool

    x1_ref[...] = _mix_tail(x, pooled, b, ga, gb, pw_ref[...], ps_ref[...], pa_ref[...], pb_ref[...],
                            wo_ref[...], g_ref[...], bt_ref[...])


def _prompt_mixer(x, lw, bias_p, sink_t, *, bsz, n_total, tm=512):
    seq = x.shape[0] // bsz
    nt = seq // tm
    hist = 2 * SUBLANES
    kern = functools.partial(_prompt_mixer_kernel, tm=tm)
    tile = pl.BlockSpec((tm, D_MODEL), lambda b, i: (b * nt + i, 0))
    state = lambda r, w: pl.BlockSpec((None, r, w), lambda b, i: (b, 0, 0))
    params = (lw["w_in"], lw["pool_w"], lw["pool_scale"], lw["proj_a"], lw["proj_b"], lw["w_out"])
    tail = (sink_t, lw["ln1_g"], lw["ln1_b"])
    return pl.pallas_call(
        kern,
        grid=(bsz, nt),
        in_specs=([tile] + [_layer_resident(a, lw["layer"]) for a in params] + [_resident(bias_p.shape)]
                  + [_layer_resident(a, lw["layer"]) for a in tail]),
        out_specs=[tile, state(WINDOW, KV_DIM), state(WINDOW, KV_DIM), state(hist, D_POOL)],
        out_shape=[jax.ShapeDtypeStruct((n_total, D_MODEL), F32),
                   jax.ShapeDtypeStruct((bsz, WINDOW, KV_DIM), F32),
                   jax.ShapeDtypeStruct((bsz, WINDOW, KV_DIM), F32),
                   jax.ShapeDtypeStruct((bsz, hist, D_POOL), F32)],
        scratch_shapes=[pltpu.VMEM((4, LANES, WINDOW + tm), BF16),
                        pltpu.VMEM((4, WINDOW + tm, LANES), BF16),
                        pltpu.VMEM((hist + tm, D_POOL), F32)],
        compiler_params=_cparams("arbitrary", "arbitrary"),
        name="prompt_mixer",
    )(x, lw["w_in"], lw["pool_w"], lw["pool_scale"], lw["proj_a"], lw["proj_b"], lw["w_out"],
      bias_p, sink_t, lw["ln1_g"], lw["ln1_b"])


def _in_proj_kernel(x_ref, win_ref, u_ref):
    u_ref[...] = jnp.dot(x_ref[...].astype(BF16), win_ref[...], preferred_element_type=F32)


def _in_proj(x, lw):
    n = x.shape[0]
    return pl.pallas_call(
        _in_proj_kernel,
        grid=(1,),
        in_specs=[pl.BlockSpec(x.shape, lambda i: (0, 0)), _layer_resident(lw["w_in"], lw["layer"])],
        out_specs=pl.BlockSpec((n, D_IN), lambda i: (0, 0)),
        out_shape=jax.ShapeDtypeStruct((n, D_IN), F32),
        compiler_params=_cparams("arbitrary"),
        name="decode_in_proj",
    )(x, lw["w_in"])


def _decode_attn_kernel(q_ref, kn_ref, vn_ref, kc_ref, vc_ref, bias_ref, biasn_ref, sink_ref, *rest, g, ntok):
    b_ref, sk_ref, sv_ref = rest[-3:]
    slabs = 2 * ntok
    rows = slabs * g

    lane = lax.broadcasted_iota(jnp.int32, (KV_DIM, WINDOW), 1)
    fill = jnp.zeros((LANES - g, KV_DIM), F32)
    for new_ref, old_ref, out_ref in ((kn_ref, kc_ref, sk_ref), (vn_ref, vc_ref, sv_ref)):
        new_t = [jnp.concatenate([new_ref[t], fill], axis=0).T for t in range(ntok)]
        for n in range(g):
            tile = pltpu.roll(old_ref[n].reshape(KV_DIM, WINDOW), WINDOW - ntok, 1)
            for t in range(ntok):
                tile = jnp.where(lane == WINDOW - ntok + t, new_t[t][:, n:n + 1], tile)
            out_ref[n] = tile.reshape(N_KV_HEADS, HEAD_DIM, WINDOW)

    def cache_variants(ref):
        z = jnp.zeros((HEAD_DIM, WINDOW), F32)
        out = []
        for kvh in range(N_KV_HEADS):
            out.append((jnp.concatenate([jnp.concatenate([ref[n, kvh], z], axis=0) for n in range(g)], axis=1),
                        jnp.concatenate([jnp.concatenate([z, ref[n, kvh]], axis=0) for n in range(g)], axis=1)))
        return out

    kk = cache_variants(kc_ref)
    vv = cache_variants(vc_ref)
    kn = [_half_lane_variants(kn_ref[t]) for t in range(ntok)]
    vn = [_half_lane_variants(vn_ref[t]) for t in range(ntok)]
    per_row = lambda a: jnp.concatenate([a] * slabs, axis=0)
    row_seq = lax.rem(lax.broadcasted_iota(jnp.int32, (rows, WINDOW), 0), g)
    diag = (lax.rem(lax.broadcasted_iota(jnp.int32, (rows, g * WINDOW), 0), g)
            == lax.broadcasted_iota(jnp.int32, (rows, g * WINDOW), 1) // WINDOW)
    for kvh in range(N_KV_HEADS):
        c0 = kvh * 2 * LANES
        q2f = jnp.concatenate([q_ref[t, :, c0 + p * LANES:c0 + (p + 1) * LANES]
                               for p in range(2) for t in range(ntok)], axis=0)
        q2 = q2f.astype(BF16)
        acc = None
        for par in range(2):
            full = jnp.dot(q2, kk[kvh][par].astype(BF16), preferred_element_type=F32)
            s = jnp.zeros((rows, WINDOW), F32)
            for n in range(g):
                s = jnp.where(row_seq == n, full[:, n * WINDOW:(n + 1) * WINDOW], s)
            s = s * ATTN_SCALE + bias_ref[kvh, par]
            s_new = [jnp.sum(q2f * per_row(kn[t][kvh][par]), axis=-1, keepdims=True) * ATTN_SCALE
                     + biasn_ref[kvh, par, t] for t in range(ntok)]
            sink = sink_ref[kvh, par]
            m = jnp.maximum(jnp.max(s, axis=-1, keepdims=True), sink)
            for t in range(ntok):
                m = jnp.maximum(m, s_new[t])
            p = jnp.exp(s - m)
            den = jnp.sum(p, axis=-1, keepdims=True) + jnp.exp(sink - m)
            p_bd = jnp.where(diag, jnp.concatenate([p] * g, axis=1), 0.0).astype(BF16)
            o = lax.dot_general(p_bd, vv[kvh][par].astype(BF16), (((1,), (1,)), ((), ())),
                                preferred_element_type=F32)
            for t in range(ntok):
                p_new = jnp.exp(s_new[t] - m)
                den = den + p_new
                o = o + p_new * per_row(vn[t][kvh][par])
            o = o / den
            acc = o if acc is None else acc + o
        for p in range(2):
            for t in range(ntok):
                r0 = (p * ntok + t) * g
                b_ref[t, :, c0 + p * LANES:c0 + (p + 1) * LANES] = acc[r0:r0 + g]


def _decode_attn(u3, cache_k, cache_v, bias_d, biasn_d, sink_d, new_caches, *, g, layer):
    ntok, nseq, _ = u3.shape
    cache = pl.BlockSpec((None, g, N_KV_HEADS, HEAD_DIM, WINDOW), lambda i: (layer, i, 0, 0, 0))
    kern = functools.partial(_decode_attn_kernel, g=g, ntok=ntok)
    new_cols = lambda w, off: pl.BlockSpec((ntok, g, w), lambda i: (0, i, off // w))
    ins = [u3, u3, u3, cache_k, cache_v, bias_d, biasn_d, sink_d]
    in_specs = [new_cols(Q_DIM, _O_Q), new_cols(KV_DIM, _O_K), new_cols(KV_DIM, _O_V), cache, cache,
                _resident(bias_d.shape), _resident(biasn_d.shape), _layer_resident(sink_d, layer)]
    aliases = {}
    if new_caches is not None:
        aliases = {len(ins): 1, len(ins) + 1: 2}
        ins += list(new_caches)
        in_specs += [pl.BlockSpec(memory_space=pl.ANY)] * 2
    return pl.pallas_call(
        kern,
        grid=(nseq // g,),
        in_specs=in_specs,
        out_specs=[pl.BlockSpec((ntok, g, Q_DIM), lambda i: (0, i, 0)), cache, cache],
        out_shape=[jax.ShapeDtypeStruct((ntok, nseq, Q_DIM), F32),
                   jax.ShapeDtypeStruct(cache_k.shape, F32), jax.ShapeDtypeStruct(cache_v.shape, F32)],
        input_output_aliases=aliases,
        compiler_params=_cparams("arbitrary"),
        name="decode_attn",
    )(*ins)


def _decode_tail_kernel(x_ref, u_ref, hist_ref, b_ref, pw_ref, ps_ref, pa_ref, pb_ref,
                        wo_ref, g_ref, bt_ref, buf_ref, x1_ref, *, nseq, ntok, start_pos):
    del buf_ref

    def ext_row(r, c0):
        if r < POOL_HIST:
            return hist_ref[r, :, c0:c0 + POOL_GROUP_DIM]
        return u_ref[(r - POOL_HIST) * nseq:(r - POOL_HIST + 1) * nseq, c0:c0 + POOL_GROUP_DIM]

    pooled_rows = []
    for t in range(ntok):
        means = []
        for gi, w in enumerate(POOL_WINDOWS):
            acc = None
            for d in range(w):
                r = POOL_HIST + t - d
                if start_pos - POOL_HIST + r < 0:
                    continue
                term = ext_row(r, gi * POOL_GROUP_DIM)
                acc = term if acc is None else acc + term
            means.append(acc / float(min(w, start_pos + t + 1)))
        pooled_rows.append(jnp.concatenate(means, axis=1))
    pooled = jnp.concatenate(pooled_rows, axis=0) - u_ref[:, :D_POOL]
    x1_ref[...] = _mix_tail(x_ref[...], pooled, b_ref[...], u_ref[:, _O_GA:_O_GB], u_ref[:, _O_GB:], pw_ref[...],
                            ps_ref[...], pa_ref[...], pb_ref[...], wo_ref[...], g_ref[...], bt_ref[...])


def _decode_tail(x, u, hist_t, b, lw, buf, *, nseq, ntok, row_block):
    n = ntok * nseq
    kern = functools.partial(_decode_tail_kernel, nseq=nseq, ntok=ntok, start_pos=PAST_LEN)
    whole = lambda a: pl.BlockSpec(a.shape, lambda i: (0,) * a.ndim)
    layered = lambda a: _layer_resident(a, lw["layer"])
    params = (lw["pool_w"], lw["pool_scale"], lw["proj_a"], lw["proj_b"], lw["w_out"], lw["ln1_g"], lw["ln1_b"])
    ins = (x, u, hist_t, b) + params
    return pl.pallas_call(
        kern,
        grid=(1,),
        in_specs=([whole(x), whole(u), layered(hist_t), whole(b)] + [layered(a) for a in params]
                  + [pl.BlockSpec(memory_space=pl.ANY)]),
        out_specs=pl.BlockSpec((n, D_MODEL), lambda i: (row_block, 0)),
        out_shape=jax.ShapeDtypeStruct(buf.shape, F32),
        input_output_aliases={len(ins): 0},
        compiler_params=_cparams("arbitrary"),
        name="decode_tail",
    )(*ins, buf)


def _router_logits(x, whi, wlo, br):
    xh = x.astype(BF16)
    xl = (x - xh.astype(F32)).astype(BF16)
    return (jnp.dot(xh, whi, preferred_element_type=F32)
            + (jnp.dot(xh, wlo, preferred_element_type=F32)
               + jnp.dot(xl, whi, preferred_element_type=F32))) + br


def _group_logits(logits, lane):
    is_group = jnp.logical_and(lane >= _GROUP_LANE0, lane < _GROUP_LANE0 + N_EXPERT_GROUPS)
    return jnp.where(is_group, logits, NEG_INF)


def _router_kernel(x_ref, whi_ref, wlo_ref, br_ref, meta_ref, cnt_ref, run_scr, *, tr):
    i = pl.program_id(0)

    @pl.when(i == 0)
    def _():
        run_scr[...] = jnp.zeros_like(run_scr)

    lt = _router_logits(x_ref[...], whi_ref[...], wlo_ref[...], br_ref[...]).T
    row = lax.broadcasted_iota(jnp.int32, (EXPERTS_PER_GROUP, tr), 0)
    none = jnp.int32(EXPERTS_PER_GROUP)

    def argmax_rows(vals):
        vmax = jnp.max(vals, axis=0, keepdims=True)
        return jnp.min(jnp.where(vals == vmax, row, none), axis=0, keepdims=True)

    glog = jnp.where(row < N_EXPERT_GROUPS, lt[_GROUP_LANE0:_GROUP_LANE0 + EXPERTS_PER_GROUP], NEG_INF)
    gidx = argmax_rows(glog)
    el = lt[:EXPERTS_PER_GROUP]
    for g in range(1, N_EXPERT_GROUPS):
        el = jnp.where(gidx == g, lt[g * EXPERTS_PER_GROUP:(g + 1) * EXPERTS_PER_GROUP], el)
    e1 = argmax_rows(el)
    e2 = argmax_rows(jnp.where(row == e1, NEG_INF, el))
    lo = jnp.minimum(e1, e2)
    hi = jnp.maximum(e1, e2)
    pair = lax.shift_right_logical(lo * (2 * EXPERTS_PER_GROUP - 1 - lo), 1) + (hi - lo - 1)
    cls = gidx * PAIRS_PER_GROUP + pair

    oh = jnp.where(lax.broadcasted_iota(jnp.int32, (ROUTER_LANES, tr), 0) == cls, 1.0, 0.0)
    rr = lax.broadcasted_iota(jnp.int32, (tr, tr), 0)
    cc = lax.broadcasted_iota(jnp.int32, (tr, tr), 1)
    earlier = jnp.where(rr < cc, 1.0, 0.0).astype(BF16)
    before = jnp.dot(oh.astype(BF16), earlier, preferred_element_type=F32) + run_scr[...]
    rank = jnp.sum(oh * before, axis=0, keepdims=True)
    run_scr[...] += jnp.sum(oh, axis=1, keepdims=True)
    cnt_ref[...] = run_scr[...]

    meta_ref[...] = jnp.where(row == 0, cls.astype(F32), jnp.where(row == 1, rank, 0.0))


def _router(x, lw, *, tr=512):
    t = x.shape[0]
    kern = functools.partial(_router_kernel, tr=tr)
    return pl.pallas_call(
        kern,
        grid=(t // tr,),
        in_specs=[pl.BlockSpec((tr, D_MODEL), lambda i: (i, 0)),
                  _layer_resident(lw["router_hi"], lw["layer"]), _layer_resident(lw["router_lo"], lw["layer"]),
                  _layer_resident(lw["router_b"], lw["layer"])],
        out_specs=[pl.BlockSpec((None, SUBLANES, tr), lambda i: (i, 0, 0)),
                   pl.BlockSpec((ROUTER_LANES, 1), lambda i: (0, 0))],
        out_shape=[jax.ShapeDtypeStruct((t // tr, SUBLANES, tr), F32),
                   jax.ShapeDtypeStruct((ROUTER_LANES, 1), F32)],
        scratch_shapes=[pltpu.VMEM((ROUTER_LANES, 1), F32)],
        compiler_params=_cparams("arbitrary"),
        name="router",
    )(x, lw["router_hi"], lw["router_lo"], lw["router_b"])


def _rows_copy(src, dst, sem):
    return pltpu.make_async_copy(src, dst, sem)


def _dispatch_kernel(slot_ref, x_ref, xs_ref, stage, sems, *, td):
    i = pl.program_id(0)
    n = pl.num_programs(0)
    sb = lax.rem(i, 2)
    x = x_ref[...]
    for j in range(ROW_CHUNKS):
        stage[sb, pl.ds(j, td, stride=ROW_CHUNKS), :] = x[:, j * LANES:(j + 1) * LANES]

    def issue(rb, carry):
        for q in range(ISSUE_UNROLL):
            r = rb * ISSUE_UNROLL + q
            src = stage.at[sb, pl.ds(pl.multiple_of(r * ROW_CHUNKS, ROW_CHUNKS), ROW_CHUNKS)]
            s = slot_ref[0, 0, r]
            dst = xs_ref.at[pl.ds(pl.multiple_of(s * ROW_CHUNKS, ROW_CHUNKS), ROW_CHUNKS)]
            _rows_copy(src, dst, sems.at[sb]).start(priority=q % DMA_QUEUES)
        return carry

    lax.fori_loop(0, td // ISSUE_UNROLL, issue, 0)

    def drain(buf):
        _rows_copy(stage.at[buf], xs_ref.at[pl.ds(0, td * ROW_CHUNKS)], sems.at[buf]).wait()

    @pl.when(i > 0)
    def _():
        drain(1 - sb)

    @pl.when(i == n - 1)
    def _():
        drain(sb)


def _dispatch(x, slots, n_slots, *, td=512):
    t = x.shape[0]
    slots3 = slots.reshape(t // td, 1, td)
    kern = functools.partial(_dispatch_kernel, td=td)
    return pl.pallas_call(
        kern,
        grid=(t // td,),
        in_specs=[pl.BlockSpec((1, 1, td), lambda i: (i, 0, 0), memory_space=pltpu.SMEM),
                  pl.BlockSpec((td, D_MODEL), lambda i: (i, 0))],
        out_specs=pl.BlockSpec(memory_space=pl.ANY),
        out_shape=jax.ShapeDtypeStruct((n_slots * ROW_CHUNKS, LANES), F32),
        scratch_shapes=[pltpu.VMEM((2, td * ROW_CHUNKS, LANES), F32), pltpu.SemaphoreType.DMA((2,))],
        compiler_params=_cparams("arbitrary"),
        name="moe_dispatch",
    )(slots3, x)


def _expert_kernel(ea_ref, eb_ref, rows_ref, nv_ref, xs_ref, rhi_ref, rb_ref, wg_ref, wu_ref, wd_ref, os_ref,
                   wg_bf, wu_bf, wd_bf):
    i = pl.program_id(0)
    first = i * TILES_PER_STEP
    valid = first < nv_ref[0]
    group = ea_ref[first] // EXPERTS_PER_GROUP
    prev_group = ea_ref[jnp.maximum(first - 1, 0)] // EXPERTS_PER_GROUP
    new_group = jnp.logical_or(i == 0, group != prev_group)

    @pl.when(jnp.logical_and(valid, new_group))
    def _():
        for e in range(EXPERTS_PER_GROUP):
            wg_bf[e] = wg_ref[e].astype(BF16)
            wu_bf[e] = wu_ref[e].astype(BF16)
            wd_bf[e] = wd_ref[e].astype(BF16)

    def body(row0, ea, eb, m):
        x = jnp.concatenate([xs_ref[pl.ds(row0 + j, m, stride=ROW_CHUNKS), :] for j in range(ROW_CHUNKS)],
                            axis=1)
        xt = x.astype(BF16)
        logits = jnp.dot(xt, rhi_ref[...], preferred_element_type=F32) + rb_ref[...]
        lane = lax.broadcasted_iota(jnp.int32, logits.shape, 1)
        pick = lambda l: jnp.sum(jnp.where(lane == l, logits, 0.0), axis=-1, keepdims=True)
        glog = _group_logits(logits, lane)
        gmax = jnp.max(glog, axis=-1, keepdims=True)
        gw = jnp.exp(pick(_GROUP_LANE0 + group) - gmax) / jnp.sum(jnp.exp(glog - gmax), axis=-1, keepdims=True)
        la = pick(ea)
        lb = pick(eb)
        lmax = jnp.maximum(la, lb)
        pa = jnp.exp(la - lmax)
        pb = jnp.exp(lb - lmax)
        scale = gw / (pa + pb)

        y = None
        for e, w in ((ea, pa * scale), (eb, pb * scale)):
            el = e - group * EXPERTS_PER_GROUP
            gate = jnp.dot(xt, wg_bf[el], preferred_element_type=F32)
            up = jnp.dot(xt, wu_bf[el], preferred_element_type=F32)
            h = (jax.nn.silu(gate) * up).astype(BF16)
            term = w * jnp.dot(h, wd_bf[el], preferred_element_type=F32)
            y = term if y is None else y + term
        for j in range(ROW_CHUNKS):
            os_ref[pl.ds(row0 + j, m, stride=ROW_CHUNKS), :] = y[:, j * LANES:(j + 1) * LANES]

    for s in range(TILES_PER_STEP):
        tile = first + s
        row_steps = (rows_ref[tile] + EXPERT_ROW_STEP - 1) // EXPERT_ROW_STEP
        for k in range(1, EXPERT_TILE // EXPERT_ROW_STEP + 1):
            pl.when(jnp.logical_and(valid, row_steps == k))(
                functools.partial(body, s * EXPERT_TILE * ROW_CHUNKS, ea_ref[tile], eb_ref[tile],
                                  k * EXPERT_ROW_STEP))


def _experts(xs, tile_ea, tile_eb, tile_rows, n_valid, lw):
    layer = lw["layer"]
    n_steps = xs.shape[0] // (TILES_PER_STEP * EXPERT_TILE * ROW_CHUNKS)
    rows = TILES_PER_STEP * EXPERT_TILE * ROW_CHUNKS
    act = pl.BlockSpec((rows, LANES),
                       lambda i, ea, eb, nr, nv: (jnp.minimum(i, nv[0] // TILES_PER_STEP - 1), 0))
    group_of = lambda ea, i: ea[i * TILES_PER_STEP] // EXPERTS_PER_GROUP
    group_w = lambda r, c: pl.BlockSpec((None, EXPERTS_PER_GROUP, r, c),
                                        lambda i, ea, eb, nr, nv: (layer, group_of(ea, i), 0, 0),
                                        pipeline_mode=pl.Buffered(1))
    router = lambda a: pl.BlockSpec((None,) + a.shape[1:], lambda i, ea, eb, nr, nv: (layer, 0, 0),
                                    pipeline_mode=pl.Buffered(1))
    grid_spec = pltpu.PrefetchScalarGridSpec(
        num_scalar_prefetch=4,
        grid=(n_steps,),
        in_specs=[act, router(lw["router_hi"]), router(lw["router_b"]),
                  group_w(D_MODEL, D_EXPERT), group_w(D_MODEL, D_EXPERT), group_w(D_EXPERT, D_MODEL)],
        out_specs=act,
        scratch_shapes=[pltpu.VMEM((EXPERTS_PER_GROUP, D_MODEL, D_EXPERT), BF16),
                        pltpu.VMEM((EXPERTS_PER_GROUP, D_MODEL, D_EXPERT), BF16),
                        pltpu.VMEM((EXPERTS_PER_GROUP, D_EXPERT, D_MODEL), BF16)],
    )
    return pl.pallas_call(
        _expert_kernel,
        grid_spec=grid_spec,
        out_shape=jax.ShapeDtypeStruct(xs.shape, F32),
        compiler_params=_cparams("arbitrary"),
        name="moe_experts",
    )(tile_ea, tile_eb, tile_rows, n_valid, xs, lw["router_hi"], lw["router_b"],
      lw["w_gate"], lw["w_up"], lw["w_down"])


def _gather_rows_issue(slot_ref, os_ref, gbuf, sems, buf, tc):
    def issue(rb, carry):
        for q in range(ISSUE_UNROLL):
            r = rb * ISSUE_UNROLL + q
            s = slot_ref[0, 0, r]
            src = os_ref.at[pl.ds(pl.multiple_of(s * ROW_CHUNKS, ROW_CHUNKS), ROW_CHUNKS)]
            dst = gbuf.at[buf, pl.ds(pl.multiple_of(r * ROW_CHUNKS, ROW_CHUNKS), ROW_CHUNKS)]
            _rows_copy(src, dst, sems.at[buf]).start(priority=q % DMA_QUEUES)
        return carry

    lax.fori_loop(0, tc // ISSUE_UNROLL, issue, 0)


def _combine_kernel(slot_ref, slot_next_ref, x_ref, g_ref, bt_ref, os_ref, xp_ref, xd_ref,
                    gbuf, sems, *, tc, n_first):
    i = pl.program_id(0)
    n = pl.num_programs(0)
    sb = lax.rem(i, 2)

    @pl.when(i == 0)
    def _():
        _gather_rows_issue(slot_ref, os_ref, gbuf, sems, 0, tc)

    @pl.when(i + 1 < n)
    def _():
        _gather_rows_issue(slot_next_ref, os_ref, gbuf, sems, 1 - sb, tc)

    _rows_copy(os_ref.at[pl.ds(0, tc * ROW_CHUNKS)], gbuf.at[sb], sems.at[sb]).wait()
    y = jnp.concatenate([gbuf[sb, pl.ds(j, tc, stride=ROW_CHUNKS), :] for j in range(ROW_CHUNKS)], axis=1)
    x2 = _layer_norm(DEEPNORM_ALPHA * x_ref[...] + y, g_ref[...], bt_ref[...])

    @pl.when(i < n_first)
    def _():
        xp_ref[...] = x2

    @pl.when(i >= n_first)
    def _():
        xd_ref[...] = x2


def _combine(x, slots, os_, lw, *, n_prompt, tc=512):
    t = x.shape[0]
    nblk = t // tc
    n_first = n_prompt // tc
    slots3 = slots.reshape(nblk, 1, tc)
    kern = functools.partial(_combine_kernel, tc=tc, n_first=n_first)
    smem_tile = lambda f: pl.BlockSpec((1, 1, tc), f, memory_space=pltpu.SMEM)
    return pl.pallas_call(
        kern,
        grid=(nblk,),
        in_specs=[smem_tile(lambda i: (i, 0, 0)),
                  smem_tile(lambda i: (jnp.minimum(i + 1, nblk - 1), 0, 0)),
                  pl.BlockSpec((tc, D_MODEL), lambda i: (i, 0)),
                  _layer_resident(lw["ln2_g"], lw["layer"]), _layer_resident(lw["ln2_b"], lw["layer"]),
                  pl.BlockSpec(memory_space=pl.ANY)],
        out_specs=[pl.BlockSpec((tc, D_MODEL), lambda i: (jnp.minimum(i, n_first - 1), 0)),
                   pl.BlockSpec((tc, D_MODEL), lambda i: (jnp.maximum(i - n_first, 0), 0))],
        out_shape=[jax.ShapeDtypeStruct((n_prompt, D_MODEL), F32),
                   jax.ShapeDtypeStruct((t - n_prompt, D_MODEL), F32)],
        scratch_shapes=[pltpu.VMEM((2, tc * ROW_CHUNKS, LANES), F32), pltpu.SemaphoreType.DMA((2,))],
        compiler_params=_cparams("arbitrary"),
        name="moe_combine",
    )(slots3, slots3, x, lw["ln2_g"], lw["ln2_b"], os_)


def _moe(x, lw, *, n_prompt):
    t = x.shape[0]
    n_tiles = t // EXPERT_TILE + N_CLASSES + N_EXPERT_GROUPS * (TILES_PER_STEP - 1)
    n_tiles = -(-n_tiles // TILES_PER_STEP) * TILES_PER_STEP
    meta, cnt = _router(x, lw)
    cls = meta[:, 0, :].reshape(t).astype(jnp.int32)
    rank = meta[:, 1, :].reshape(t).astype(jnp.int32)
    counts = cnt[:N_CLASSES, 0].astype(jnp.int32)
    seg_tiles = (counts + EXPERT_TILE - 1) // EXPERT_TILE
    group_tiles = jnp.sum(seg_tiles.reshape(N_EXPERT_GROUPS, PAIRS_PER_GROUP), axis=1)
    group_pad = (-group_tiles) % TILES_PER_STEP
    last_of_group = jnp.arange(N_CLASSES) % PAIRS_PER_GROUP == PAIRS_PER_GROUP - 1
    seg_tiles = seg_tiles + jnp.where(last_of_group, jnp.repeat(group_pad, PAIRS_PER_GROUP), 0)
    tile_ends = jnp.cumsum(seg_tiles)
    seg_start = (tile_ends - seg_tiles) * EXPERT_TILE
    class_ids = jnp.arange(N_CLASSES, dtype=jnp.int32)
    slots = jnp.sum(jnp.where(cls[:, None] == class_ids, seg_start, 0), axis=-1) + rank
    n_valid = tile_ends[-1:]
    tile_ids = jnp.minimum(jnp.arange(n_tiles, dtype=jnp.int32), n_valid[0] - 1)
    tile_class = jnp.sum((tile_ends[None, :] <= tile_ids[:, None]).astype(jnp.int32), axis=1)
    of_class = tile_class[:, None] == class_ids
    class_experts = jnp.asarray(_CLASS_EXPERTS, dtype=jnp.int32)
    tile_ea = jnp.sum(jnp.where(of_class, class_experts[:, 0], 0), axis=-1)
    tile_eb = jnp.sum(jnp.where(of_class, class_experts[:, 1], 0), axis=-1)
    rows_left = jnp.sum(jnp.where(of_class, seg_start + counts, 0), axis=-1) - tile_ids * EXPERT_TILE
    tile_rows = jnp.clip(rows_left, 0, EXPERT_TILE)
    xs = _dispatch(x, slots, n_tiles * EXPERT_TILE)
    os_ = _experts(xs, tile_ea, tile_eb, tile_rows, n_valid.astype(jnp.int32), lw)
    return _combine(x, slots, os_, lw, n_prompt=n_prompt)


def _t5_bucket(dist):
    n = jnp.maximum(dist, 0)
    max_exact = REL_BUCKETS // 2
    nf = jnp.maximum(n, 1).astype(F32)
    large = max_exact + (jnp.log(nf / max_exact) / math.log(REL_MAX_DISTANCE / max_exact)
                         * (REL_BUCKETS - max_exact)).astype(jnp.int32)
    large = jnp.minimum(large, REL_BUCKETS - 1)
    return jnp.where(n < max_exact, n, large)


def _head_rows(per_head, rows_of):
    out = []
    for kvh in range(N_KV_HEADS):
        out.append(jnp.stack([jnp.concatenate([jnp.repeat(per_head[..., h, :, :], rep, axis=-2)
                                               for h, rep in rows_of(kvh, par)], axis=-2)
                              for par in range(2)], axis=-3))
    return jnp.stack(out, axis=-4)


def _prompt_rows(kvh, par):
    return [(GROUP * kvh + 2 * p + par, 1) for p in range(2)]


def _head_bias(rel_bias, dist, valid):
    bias = jnp.where(valid[..., None], rel_bias[_t5_bucket(dist)].astype(F32), NEG_INF)
    return jnp.moveaxis(bias, -1, 0)


def _prompt_bias(rel_bias):
    period = 4 * WINDOW
    off = jnp.arange(period)
    off = jnp.where(off < 2 * WINDOW, off, off - period)
    dist = WINDOW - off
    per_off = _head_bias(rel_bias, dist, (dist >= 0) & (dist <= WINDOW))
    tiled = jnp.tile(per_off, (1, WINDOW))[:, :WINDOW * (period - 1)]
    return tiled.reshape(N_HEADS, WINDOW, period - 1)[:, :, :2 * WINDOW]


def _attention_tables(rel_bias, sinks, ntok, g):
    bias_p = _head_rows(_prompt_bias(rel_bias), _prompt_rows)

    decode_rows = lambda kvh, par: [(GROUP * kvh + 2 * p + par, g) for p in range(2)]
    tq = jnp.arange(ntok)[:, None]
    kpos_old = jnp.arange(WINDOW)[None, :] - WINDOW
    kpos_new = jnp.arange(ntok)[None, :]
    tables = []
    for kpos in (kpos_old, kpos_new):
        d = tq - kpos
        per_head = _head_bias(rel_bias, d, (d >= 0) & (d <= WINDOW) & (PAST_LEN + kpos >= 0))
        tables.append(_head_rows(per_head, decode_rows))
    bias_d = tables[0]
    biasn_d = jnp.moveaxis(tables[1], -1, 2)[..., None]

    s = sinks.astype(F32)[:, :, None, None]
    sink_p = _head_rows(jnp.broadcast_to(s, s.shape[:2] + (WINDOW, 1)), _prompt_rows)
    sink_d = _head_rows(jnp.broadcast_to(s, s.shape[:2] + (ntok, 1)), decode_rows)
    return bias_p, bias_d, biasn_d, sink_p, sink_d


def _stacked_weights(w_in, pool_w, pool_scale, proj_a, proj_b, w_out, ln1_g, ln1_b, router_group_w,
                     router_group_b, router_expert_w, router_expert_b, expert_w_gate, expert_w_up,
                     expert_w_down, ln2_g, ln2_b):
    depth = w_in.shape[0]
    same_group = jnp.eye(N_POOL_GROUPS, dtype=bool)[None, :, None, :, None]
    pool_bd = jnp.where(same_group, pool_w[:, :, :, None, :], 0.0).reshape(depth, D_POOL, D_POOL)
    pad = ROUTER_LANES - N_EXPERTS - N_EXPERT_GROUPS
    rw = jnp.concatenate([router_expert_w, router_group_w, jnp.zeros((depth, D_MODEL, pad), F32)], axis=-1)
    rb = jnp.concatenate([router_expert_b, router_group_b, jnp.zeros((depth, pad), F32)], axis=-1)
    r_hi = rw.astype(BF16)
    row = lambda a: a[:, None, :]
    return dict(
        w_in=w_in.astype(BF16), pool_w=pool_bd.astype(BF16), pool_scale=row(pool_scale),
        proj_a=proj_a.astype(BF16), proj_b=proj_b.astype(BF16), w_out=w_out.astype(BF16),
        ln1_g=row(ln1_g), ln1_b=row(ln1_b),
        router_hi=r_hi, router_lo=(rw - r_hi.astype(F32)).astype(BF16), router_b=row(rb),
        w_gate=expert_w_gate, w_up=expert_w_up, w_down=expert_w_down,
        ln2_g=row(ln2_g), ln2_b=row(ln2_b))


def _decode_mixer(xs_tn, lw, cache_k, cache_v, hist_t, tables, buf, new_caches, *, nseq, ntok, g, row_block):
    bias_d, biasn_d, sink_d = tables
    u = _in_proj(xs_tn, lw)
    u3 = u.reshape(ntok, nseq, D_IN)
    b, new_k, new_v = _decode_attn(u3, cache_k, cache_v, bias_d, biasn_d, sink_d, new_caches, g=g,
                                   layer=lw["layer"])
    buf = _decode_tail(xs_tn, u, hist_t, b.reshape(ntok * nseq, Q_DIM), lw, buf, nseq=nseq, ntok=ntok,
                       row_block=row_block)
    return buf, u3, (new_k, new_v)


def kernel(x_prompt, x_sample, cache_k, cache_v, state_pool, rel_bias, w_in, pool_w, pool_scale, proj_a, proj_b, w_out, attn_sinks, ln1_g, ln1_b, router_group_w, router_group_b, router_expert_w, router_expert_b, expert_w_gate, expert_w_up, expert_w_down, ln2_g, ln2_b):
    bsz, seq, _ = x_prompt.shape
    nseq, ntok, _ = x_sample.shape
    g = 16
    depth = w_in.shape[0]
    n_prompt = bsz * seq
    n_decode = ntok * nseq
    assert n_prompt % n_decode == 0
    bias_p, bias_d, biasn_d, sink_p, sink_d = _attention_tables(rel_bias, attn_sinks, ntok, g)
    weights = _stacked_weights(w_in, pool_w, pool_scale, proj_a, proj_b, w_out, ln1_g, ln1_b, router_group_w,
                               router_group_b, router_expert_w, router_expert_b, expert_w_gate, expert_w_up,
                               expert_w_down, ln2_g, ln2_b)

    xp = x_prompt.reshape(n_prompt, D_MODEL)
    xs_tn = jnp.swapaxes(x_sample, 0, 1).reshape(n_decode, D_MODEL)
    hist_t = jnp.swapaxes(state_pool, 1, 2)
    ck_t = jnp.transpose(cache_k, (0, 1, 3, 4, 2))
    cv_t = jnp.transpose(cache_v, (0, 1, 3, 4, 2))
    prompt_state, decode_u, new_caches = [], [], None
    for l in range(depth):
        lw = dict(weights, layer=l)
        x1, kst, vst, pst = _prompt_mixer(xp, lw, bias_p, sink_p, bsz=bsz, n_total=n_prompt + n_decode)
        x1, u3, new_caches = _decode_mixer(xs_tn, lw, ck_t, cv_t, hist_t, (bias_d, biasn_d, sink_d), x1,
                                           new_caches, nseq=nseq, ntok=ntok, g=g,
                                           row_block=n_prompt // n_decode)
        xp, xs_tn = _moe(x1, lw, n_prompt=n_prompt)
        prompt_state.append((kst, vst, pst))
        decode_u.append(u3)

    pk, pv, pp = (jnp.stack(parts) for parts in zip(*prompt_state))
    heads = lambda a: a.reshape(a.shape[:3] + (N_KV_HEADS, HEAD_DIM))
    new_pool = jnp.stack([jnp.swapaxes(u3[:, :, :D_POOL], 0, 1) for u3 in decode_u])
    sample_pool = jnp.concatenate([state_pool[:, :, ntok:], new_pool], axis=2)
    sample_k, sample_v = (jnp.transpose(c, (0, 1, 4, 2, 3)) for c in new_caches)
    y_sample = jnp.swapaxes(xs_tn.reshape(ntok, nseq, D_MODEL), 0, 1)
    return (xp.reshape(bsz, seq, D_MODEL), y_sample, heads(pk), heads(pv), pp[:, :, -POOL_HIST:],
            sample_k, sample_v, sample_pool)
```

```python
import functools
import math

import jax
import jax.numpy as jnp
from jax import lax
from jax.experimental import pallas as pl
from jax.experimental.pallas import tpu as pltpu

D_MODEL = 1024
DEPTH = 2
PAST_LEN = 16384
D_POOL = D_MODEL // 2
POOL_WINDOWS = (2, 4, 8, 16)
N_POOL_GROUPS = len(POOL_WINDOWS)
POOL_GROUP_DIM = D_POOL // N_POOL_GROUPS
POOL_HIST = max(POOL_WINDOWS) - 1
assert all(w & (w - 1) == 0 for w in POOL_WINDOWS)
HEAD_DIM = 64
N_HEADS = (D_MODEL // 2) // HEAD_DIM
N_KV_HEADS = 2
GROUP = N_HEADS // N_KV_HEADS
Q_DIM = N_HEADS * HEAD_DIM
KV_DIM = N_KV_HEADS * HEAD_DIM
WINDOW = 128
ATTN_SCALE = HEAD_DIM ** -0.5
assert math.log2(ATTN_SCALE).is_integer()
REL_BUCKETS = 32
REL_MAX_DISTANCE = 128
D_IN = D_POOL + Q_DIM + 2 * KV_DIM + 2 * D_MODEL
N_EXPERT_GROUPS = 4
EXPERTS_PER_GROUP = 8
N_EXPERTS = N_EXPERT_GROUPS * EXPERTS_PER_GROUP
D_EXPERT = D_MODEL // 4
DEEPNORM_ALPHA = (2 * DEPTH) ** 0.25
LN_EPS = 1e-5

LANES = 128
SUBLANES = 8
VMEM_LIMIT_BYTES = 56 * 1024 * 1024
EXPERT_VMEM_LIMIT_BYTES = 62 * 1024 * 1024

_O_Q = D_POOL
_O_K = _O_Q + Q_DIM
_O_V = _O_K + KV_DIM
_O_GA = _O_V + KV_DIM
_O_GB = _O_GA + D_MODEL

ROW_CHUNKS = D_MODEL // LANES
KEY_PAD = 2 * WINDOW
EXPERT_TILE = 256
OUT_PROJ_CHUNKS = 2
EXPERT_ROW_STEP = 32
TILES_PER_STEP = 2
DMA_QUEUES = 2
ISSUE_UNROLL = 8
PAIRS_PER_GROUP = EXPERTS_PER_GROUP * (EXPERTS_PER_GROUP - 1) // 2
N_CLASSES = N_EXPERT_GROUPS * PAIRS_PER_GROUP
_CLASS_EXPERTS = tuple((g * EXPERTS_PER_GROUP + a, g * EXPERTS_PER_GROUP + b)
                       for g in range(N_EXPERT_GROUPS)
                       for a in range(EXPERTS_PER_GROUP) for b in range(a + 1, EXPERTS_PER_GROUP))
ROUTER_LANES = LANES
_GROUP_LANE0 = N_EXPERTS
assert EXPERTS_PER_GROUP == SUBLANES and N_EXPERT_GROUPS <= SUBLANES

BF16 = jnp.bfloat16
F32 = jnp.float32
NEG_INF = float("-inf")


def _cparams(*sem, flags=None, vmem_limit_bytes=VMEM_LIMIT_BYTES):
    return pltpu.CompilerParams(dimension_semantics=sem, vmem_limit_bytes=vmem_limit_bytes, flags=flags)


def _resident(shape):
    nd = len(shape)
    return pl.BlockSpec(shape, lambda *_: (0,) * nd, pipeline_mode=pl.Buffered(1))


def _layer_resident(stacked, layer):
    nd = stacked.ndim - 1
    return pl.BlockSpec((None,) + stacked.shape[1:], lambda *_: (layer,) + (0,) * nd,
                        pipeline_mode=pl.Buffered(1))


def _layer_norm(y, g, b):
    mu = jnp.mean(y, axis=-1, keepdims=True)
    yc = y - mu
    var = jnp.mean(yc * yc, axis=-1, keepdims=True)
    return yc * lax.rsqrt(var + LN_EPS) * g + b


def _softmax_pv(s, sink, v_bf16):
    m = jnp.maximum(jnp.max(s, axis=-1, keepdims=True), sink)
    p = jnp.exp(s - m)
    den = jnp.sum(p, axis=-1, keepdims=True) + jnp.exp(sink - m)
    o = jnp.dot(p.astype(BF16), v_bf16, preferred_element_type=F32)
    return o / den


def _half_lane_variants(t):
    lane = lax.broadcasted_iota(jnp.int32, t.shape, 1)
    lo = lane < HEAD_DIM
    tr = pltpu.roll(t, HEAD_DIM, 1)
    z = jnp.zeros_like(t)
    return ((jnp.where(lo, t, z), jnp.where(lo, z, tr)),
            (jnp.where(lo, tr, z), jnp.where(lo, z, t)))


def _mix_tail(x, a_in, b_in, ga, gb, pw, ps, pa, pb, wo, g, bt):
    a = jnp.dot(a_in.astype(BF16), pw, preferred_element_type=F32) * ps
    pa_o = jnp.dot(a.astype(BF16), pa, preferred_element_type=F32)
    pb_o = jnp.dot(b_in.astype(BF16), pb, preferred_element_type=F32)
    merged = (jax.nn.sigmoid(ga) * pa_o + jax.nn.sigmoid(gb) * pb_o).astype(BF16)
    rows = x.shape[0]
    chunk = rows // OUT_PROJ_CHUNKS
    out = []
    for r0 in range(0, rows, chunk):
        mix = jnp.dot(merged[r0:r0 + chunk], wo, preferred_element_type=F32)
        out.append(_layer_norm(DEEPNORM_ALPHA * x[r0:r0 + chunk] + mix, g, bt))
    return jnp.concatenate(out, axis=0)


def _prompt_mixer_kernel(x_ref, win_ref, pw_ref, ps_ref, pa_ref, pb_ref, wo_ref, bias_ref, sink_ref,
                         g_ref, bt_ref, x1_ref, kst_ref, vst_ref, pst_ref, kt_scr, v_scr, ext_scr, *, tm):
    i = pl.program_id(1)
    hist = 2 * SUBLANES

    @pl.when(i == 0)
    def _():
        kt_scr[:, :, :WINDOW] = jnp.zeros((4, LANES, WINDOW), BF16)
        v_scr[:, :WINDOW, :] = jnp.zeros((4, WINDOW, LANES), BF16)
        ext_scr[:hist, :] = jnp.zeros((hist, D_POOL), F32)

    @pl.when(i > 0)
    def _():
        kt_scr[:, :, :WINDOW] = kt_scr[:, :, tm:tm + WINDOW]
        v_scr[:, :WINDOW, :] = v_scr[:, tm:tm + WINDOW, :]
        ext_scr[:hist, :] = ext_scr[tm:tm + hist, :]

    x = x_ref[...]
    u = jnp.dot(x.astype(BF16), win_ref[...], preferred_element_type=F32)
    u_pool = u[:, :D_POOL]
    q = u[:, _O_Q:_O_K]
    k = u[:, _O_K:_O_V]
    v = u[:, _O_V:_O_GA]
    ga = u[:, _O_GA:_O_GB]
    gb = u[:, _O_GB:]

    kst_ref[...] = k[tm - WINDOW:, :]
    vst_ref[...] = v[tm - WINDOW:, :]
    pst_ref[...] = u_pool[tm - hist:, :]

    kt = k.T
    kz = jnp.zeros((HEAD_DIM, tm), F32)
    kt_scr[0, :, WINDOW:] = jnp.concatenate([kt[:HEAD_DIM], kz], axis=0).astype(BF16)
    kt_scr[1, :, WINDOW:] = jnp.concatenate([kz, kt[:HEAD_DIM]], axis=0).astype(BF16)
    kt_scr[2, :, WINDOW:] = jnp.concatenate([kt[HEAD_DIM:], kz], axis=0).astype(BF16)
    kt_scr[3, :, WINDOW:] = jnp.concatenate([kz, kt[HEAD_DIM:]], axis=0).astype(BF16)
    vv = _half_lane_variants(v)
    for kvh in range(N_KV_HEADS):
        for par in range(2):
            v_scr[2 * kvh + par, WINDOW:, :] = vv[kvh][par].astype(BF16)

    col = lax.broadcasted_iota(jnp.int32, (2 * WINDOW, 2 * WINDOW), 1)
    no_prev = col < jnp.where(i == 0, WINDOW, 0)
    qb = (q * ATTN_SCALE).astype(BF16)
    b_rows = []
    for j in range(tm // WINDOW):
        r0 = j * WINDOW
        o_kvh = []
        for kvh in range(N_KV_HEADS):
            c0 = kvh * 2 * LANES
            q2 = jnp.concatenate([qb[r0:r0 + WINDOW, c0:c0 + LANES],
                                  qb[r0:r0 + WINDOW, c0 + LANES:c0 + 2 * LANES]], axis=0)
            acc = None
            for par in range(2):
                s = jnp.dot(q2, kt_scr[2 * kvh + par, :, r0:r0 + 2 * WINDOW], preferred_element_type=F32)
                s = s + bias_ref[kvh, par]
                if j == 0:
                    s = jnp.where(no_prev, NEG_INF, s)
                o = _softmax_pv(s, sink_ref[kvh, par], v_scr[2 * kvh + par, r0:r0 + 2 * WINDOW, :])
                acc = o if acc is None else acc + o
            o_kvh.append(acc)
        b_rows.append(jnp.concatenate([o_kvh[0][:WINDOW], o_kvh[0][WINDOW:],
                                       o_kvh[1][:WINDOW], o_kvh[1][WINDOW:]], axis=1))
    b = jnp.concatenate(b_rows, axis=0)

    ext_scr[hist:, :] = u_pool
    pos = i * tm + lax.broadcasted_iota(jnp.int32, (tm, 1), 0)
    means = []
    for gi, w in enumerate(POOL_WINDOWS):
        c0 = gi * POOL_GROUP_DIM
        acc = ext_scr[:, c0:c0 + POOL_GROUP_DIM]
        d = 1
        while d < w:
            acc = acc + pltpu.roll(acc, d, 0)
            d *= 2
        inv_cnt = 1.0 / jnp.minimum(w, pos + 1).astype(F32)
        means.append(acc[hist:] * inv_cnt)
    pooled = jnp.concatenate(means, axis=1) - u_pool

    x1_ref[...] = _mix_tail(x, pooled, b, ga, gb, pw_ref[...], ps_ref[...], pa_ref[...], pb_ref[...],
                            wo_ref[...], g_ref[...], bt_ref[...])


def _prompt_mixer(x, lw, bias_p, sink_t, *, bsz, n_total, tm=512):
    seq = x.shape[0] // bsz
    nt = seq // tm
    hist = 2 * SUBLANES
    kern = functools.partial(_prompt_mixer_kernel, tm=tm)
    tile = pl.BlockSpec((tm, D_MODEL), lambda b, i: (b * nt + i, 0))
    state = lambda r, w: pl.BlockSpec((None, r, w), lambda b, i: (b, 0, 0))
    params = (lw["w_in"], lw["pool_w"], lw["pool_scale"], lw["proj_a"], lw["proj_b"], lw["w_out"])
    tail = (sink_t, lw["ln1_g"], lw["ln1_b"])
    return pl.pallas_call(
        kern,
        grid=(bsz, nt),
        in_specs=([tile] + [_layer_resident(a, lw["layer"]) for a in params] + [_resident(bias_p.shape)]
                  + [_layer_resident(a, lw["layer"]) for a in tail]),
        out_specs=[tile, state(WINDOW, KV_DIM), state(WINDOW, KV_DIM), state(hist, D_POOL)],
        out_shape=[jax.ShapeDtypeStruct((n_total, D_MODEL), F32),
                   jax.ShapeDtypeStruct((bsz, WINDOW, KV_DIM), F32),
                   jax.ShapeDtypeStruct((bsz, WINDOW, KV_DIM), F32),
                   jax.ShapeDtypeStruct((bsz, hist, D_POOL), F32)],
        scratch_shapes=[pltpu.VMEM((4, LANES, WINDOW + tm), BF16),
                        pltpu.VMEM((4, WINDOW + tm, LANES), BF16),
                        pltpu.VMEM((hist + tm, D_POOL), F32)],
        compiler_params=_cparams("arbitrary", "arbitrary"),
        name="prompt_mixer",
    )(x, lw["w_in"], lw["pool_w"], lw["pool_scale"], lw["proj_a"], lw["proj_b"], lw["w_out"],
      bias_p, sink_t, lw["ln1_g"], lw["ln1_b"])


def _in_proj_kernel(x_ref, win_ref, u_ref):
    u_ref[...] = jnp.dot(x_ref[...].astype(BF16), win_ref[...], preferred_element_type=F32)


def _in_proj(x, lw):
    n = x.shape[0]
    return pl.pallas_call(
        _in_proj_kernel,
        grid=(1,),
        in_specs=[pl.BlockSpec(x.shape, lambda i: (0, 0)), _layer_resident(lw["w_in"], lw["layer"])],
        out_specs=pl.BlockSpec((n, D_IN), lambda i: (0, 0)),
        out_shape=jax.ShapeDtypeStruct((n, D_IN), F32),
        compiler_params=_cparams("arbitrary"),
        name="decode_in_proj",
    )(x, lw["w_in"])


def _decode_attn_kernel(q_ref, kn_ref, vn_ref, kc_ref, vc_ref, bias_ref, biasn_ref, sink_ref, *rest, g, ntok):
    b_ref, sk_ref, sv_ref = rest[-3:]
    slabs = 2 * ntok
    rows = slabs * g

    lane = lax.broadcasted_iota(jnp.int32, (KV_DIM, WINDOW), 1)
    fill = jnp.zeros((LANES - g, KV_DIM), F32)
    for new_ref, old_ref, out_ref in ((kn_ref, kc_ref, sk_ref), (vn_ref, vc_ref, sv_ref)):
        new_t = [jnp.concatenate([new_ref[t], fill], axis=0).T for t in range(ntok)]
        for n in range(g):
            tile = pltpu.roll(old_ref[n].reshape(KV_DIM, WINDOW), WINDOW - ntok, 1)
            for t in range(ntok):
                tile = jnp.where(lane == WINDOW - ntok + t, new_t[t][:, n:n + 1], tile)
            out_ref[n] = tile.reshape(N_KV_HEADS, HEAD_DIM, WINDOW)

    def cache_variants(ref):
        z = jnp.zeros((HEAD_DIM, WINDOW), F32)
        out = []
        for kvh in range(N_KV_HEADS):
            out.append((jnp.concatenate([jnp.concatenate([ref[n, kvh], z], axis=0) for n in range(g)], axis=1),
                        jnp.concatenate([jnp.concatenate([z, ref[n, kvh]], axis=0) for n in range(g)], axis=1)))
        return out

    kk = cache_variants(kc_ref)
    vv = cache_variants(vc_ref)
    kn = [_half_lane_variants(kn_ref[t]) for t in range(ntok)]
    vn = [_half_lane_variants(vn_ref[t]) for t in range(ntok)]
    per_row = lambda a: jnp.concatenate([a] * slabs, axis=0)
    row_seq = lax.rem(lax.broadcasted_iota(jnp.int32, (rows, WINDOW), 0), g)
    diag = (lax.rem(lax.broadcasted_iota(jnp.int32, (rows, g * WINDOW), 0), g)
            == lax.broadcasted_iota(jnp.int32, (rows, g * WINDOW), 1) // WINDOW)
    for kvh in range(N_KV_HEADS):
        c0 = kvh * 2 * LANES
        q2f = jnp.concatenate([q_ref[t, :, c0 + p * LANES:c0 + (p + 1) * LANES]
                               for p in range(2) for t in range(ntok)], axis=0)
        q2 = q2f.astype(BF16)
        acc = None
        for par in range(2):
            full = jnp.dot(q2, kk[kvh][par].astype(BF16), preferred_element_type=F32)
            s = jnp.zeros((rows, WINDOW), F32)
            for n in range(g):
                s = jnp.where(row_seq == n, full[:, n * WINDOW:(n + 1) * WINDOW], s)
            s = s * ATTN_SCALE + bias_ref[kvh, par]
            s_new = [jnp.sum(q2f * per_row(kn[t][kvh][par]), axis=-1, keepdims=True) * ATTN_SCALE
                     + biasn_ref[kvh, par, t] for t in range(ntok)]
            sink = sink_ref[kvh, par]
            m = jnp.maximum(jnp.max(s, axis=-1, keepdims=True), sink)
            for t in range(ntok):
                m = jnp.maximum(m, s_new[t])
            p = jnp.exp(s - m)
            den = jnp.sum(p, axis=-1, keepdims=True) + jnp.exp(sink - m)
            p_bd = jnp.where(diag, jnp.concatenate([p] * g, axis=1), 0.0).astype(BF16)
            o = lax.dot_general(p_bd, vv[kvh][par].astype(BF16), (((1,), (1,)), ((), ())),
                                preferred_element_type=F32)
            for t in range(ntok):
                p_new = jnp.exp(s_new[t] - m)
                den = den + p_new
                o = o + p_new * per_row(vn[t][kvh][par])
            o = o / den
            acc = o if acc is None else acc + o
        for p in range(2):
            for t in range(ntok):
                r0 = (p * ntok + t) * g
                b_ref[t, :, c0 + p * LANES:c0 + (p + 1) * LANES] = acc[r0:r0 + g]


def _decode_attn(u3, cache_k, cache_v, bias_d, biasn_d, sink_d, new_caches, *, g, layer):
    ntok, nseq, _ = u3.shape
    cache = pl.BlockSpec((None, g, N_KV_HEADS, HEAD_DIM, WINDOW), lambda i: (layer, i, 0, 0, 0))
    kern = functools.partial(_decode_attn_kernel, g=g, ntok=ntok)
    new_cols = lambda w, off: pl.BlockSpec((ntok, g, w), lambda i: (0, i, off // w))
    ins = [u3, u3, u3, cache_k, cache_v, bias_d, biasn_d, sink_d]
    in_specs = [new_cols(Q_DIM, _O_Q), new_cols(KV_DIM, _O_K), new_cols(KV_DIM, _O_V), cache, cache,
                _resident(bias_d.shape), _resident(biasn_d.shape), _layer_resident(sink_d, layer)]
    aliases = {}
    if new_caches is not None:
        aliases = {len(ins): 1, len(ins) + 1: 2}
        ins += list(new_caches)
        in_specs += [pl.BlockSpec(memory_space=pl.ANY)] * 2
    return pl.pallas_call(
        kern,
        grid=(nseq // g,),
        in_specs=in_specs,
        out_specs=[pl.BlockSpec((ntok, g, Q_DIM), lambda i: (0, i, 0)), cache, cache],
        out_shape=[jax.ShapeDtypeStruct((ntok, nseq, Q_DIM), F32),
                   jax.ShapeDtypeStruct(cache_k.shape, F32), jax.ShapeDtypeStruct(cache_v.shape, F32)],
        input_output_aliases=aliases,
        compiler_params=_cparams("arbitrary"),
        name="decode_attn",
    )(*ins)


def _decode_tail_kernel(x_ref, u_ref, hist_ref, b_ref, pw_ref, ps_ref, pa_ref, pb_ref,
                        wo_ref, g_ref, bt_ref, buf_ref, x1_ref, *, nseq, ntok, start_pos):
    del buf_ref

    def ext_row(r, c0):
        if r < POOL_HIST:
            return hist_ref[r, :, c0:c0 + POOL_GROUP_DIM]
        return u_ref[(r - POOL_HIST) * nseq:(r - POOL_HIST + 1) * nseq, c0:c0 + POOL_GROUP_DIM]

    pooled_rows = []
    for t in range(ntok):
        means = []
        for gi, w in enumerate(POOL_WINDOWS):
            acc = None
            for d in range(w):
                r = POOL_HIST + t - d
                if start_pos - POOL_HIST + r < 0:
                    continue
                term = ext_row(r, gi * POOL_GROUP_DIM)
                acc = term if acc is None else acc + term
            means.append(acc / float(min(w, start_pos + t + 1)))
        pooled_rows.append(jnp.concatenate(means, axis=1))
    pooled = jnp.concatenate(pooled_rows, axis=0) - u_ref[:, :D_POOL]
    x1_ref[...] = _mix_tail(x_ref[...], pooled, b_ref[...], u_ref[:, _O_GA:_O_GB], u_ref[:, _O_GB:], pw_ref[...],
                            ps_ref[...], pa_ref[...], pb_ref[...], wo_ref[...], g_ref[...], bt_ref[...])


def _decode_tail(x, u, hist_t, b, lw, buf, *, nseq, ntok, row_block):
    n = ntok * nseq
    kern = functools.partial(_decode_tail_kernel, nseq=nseq, ntok=ntok, start_pos=PAST_LEN)
    whole = lambda a: pl.BlockSpec(a.shape, lambda i: (0,) * a.ndim)
    layered = lambda a: _layer_resident(a, lw["layer"])
    params = (lw["pool_w"], lw["pool_scale"], lw["proj_a"], lw["proj_b"], lw["w_out"], lw["ln1_g"], lw["ln1_b"])
    ins = (x, u, hist_t, b) + params
    return pl.pallas_call(
        kern,
        grid=(1,),
        in_specs=([whole(x), whole(u), layered(hist_t), whole(b)] + [layered(a) for a in params]
                  + [pl.BlockSpec(memory_space=pl.ANY)]),
        out_specs=pl.BlockSpec((n, D_MODEL), lambda i: (row_block, 0)),
        out_shape=jax.ShapeDtypeStruct(buf.shape, F32),
        input_output_aliases={len(ins): 0},
        compiler_params=_cparams("arbitrary"),
        name="decode_tail",
    )(*ins, buf)


def _router_logits(x, whi, wlo, br):
    xh = x.astype(BF16)
    xl = (x - xh.astype(F32)).astype(BF16)
    return (jnp.dot(xh, whi, preferred_element_type=F32)
            + (jnp.dot(xh, wlo, preferred_element_type=F32)
               + jnp.dot(xl, whi, preferred_element_type=F32))) + br


def _group_logits(logits, lane):
    is_group = jnp.logical_and(lane >= _GROUP_LANE0, lane < _GROUP_LANE0 + N_EXPERT_GROUPS)
    return jnp.where(is_group, logits, NEG_INF)


def _router_kernel(x_ref, whi_ref, wlo_ref, br_ref, meta_ref, cnt_ref, run_scr, *, tr):
    i = pl.program_id(0)

    @pl.when(i == 0)
    def _():
        run_scr[...] = jnp.zeros_like(run_scr)

    lt = _router_logits(x_ref[...], whi_ref[...], wlo_ref[...], br_ref[...]).T
    row = lax.broadcasted_iota(jnp.int32, (EXPERTS_PER_GROUP, tr), 0)
    none = jnp.int32(EXPERTS_PER_GROUP)

    def argmax_rows(vals):
        vmax = jnp.max(vals, axis=0, keepdims=True)
        return jnp.min(jnp.where(vals == vmax, row, none), axis=0, keepdims=True)

    glog = jnp.where(row < N_EXPERT_GROUPS, lt[_GROUP_LANE0:_GROUP_LANE0 + EXPERTS_PER_GROUP], NEG_INF)
    gidx = argmax_rows(glog)
    el = lt[:EXPERTS_PER_GROUP]
    for g in range(1, N_EXPERT_GROUPS):
        el = jnp.where(gidx == g, lt[g * EXPERTS_PER_GROUP:(g + 1) * EXPERTS_PER_GROUP], el)
    e1 = argmax_rows(el)
    e2 = argmax_rows(jnp.where(row == e1, NEG_INF, el))
    lo = jnp.minimum(e1, e2)
    hi = jnp.maximum(e1, e2)
    pair = lax.shift_right_logical(lo * (2 * EXPERTS_PER_GROUP - 1 - lo), 1) + (hi - lo - 1)
    cls = gidx * PAIRS_PER_GROUP + pair

    oh = jnp.where(lax.broadcasted_iota(jnp.int32, (ROUTER_LANES, tr), 0) == cls, 1.0, 0.0)
    rr = lax.broadcasted_iota(jnp.int32, (tr, tr), 0)
    cc = lax.broadcasted_iota(jnp.int32, (tr, tr), 1)
    earlier = jnp.where(rr < cc, 1.0, 0.0).astype(BF16)
    before = jnp.dot(oh.astype(BF16), earlier, preferred_element_type=F32) + run_scr[...]
    rank = jnp.sum(oh * before, axis=0, keepdims=True)
    run_scr[...] += jnp.sum(oh, axis=1, keepdims=True)
    cnt_ref[...] = run_scr[...]

    meta_ref[...] = jnp.where(row == 0, cls.astype(F32), jnp.where(row == 1, rank, 0.0))


def _router(x, lw, *, tr=512):
    t = x.shape[0]
    kern = functools.partial(_router_kernel, tr=tr)
    return pl.pallas_call(
        kern,
        grid=(t // tr,),
        in_specs=[pl.BlockSpec((tr, D_MODEL), lambda i: (i, 0)),
                  _layer_resident(lw["router_hi"], lw["layer"]), _layer_resident(lw["router_lo"], lw["layer"]),
                  _layer_resident(lw["router_b"], lw["layer"])],
        out_specs=[pl.BlockSpec((None, SUBLANES, tr), lambda i: (i, 0, 0)),
                   pl.BlockSpec((ROUTER_LANES, 1), lambda i: (0, 0))],
        out_shape=[jax.ShapeDtypeStruct((t // tr, SUBLANES, tr), F32),
                   jax.ShapeDtypeStruct((ROUTER_LANES, 1), F32)],
        scratch_shapes=[pltpu.VMEM((ROUTER_LANES, 1), F32)],
        compiler_params=_cparams("arbitrary"),
        name="router",
    )(x, lw["router_hi"], lw["router_lo"], lw["router_b"])


def _rows_copy(src, dst, sem):
    return pltpu.make_async_copy(src, dst, sem)


def _dispatch_kernel(slot_ref, x_ref, xs_ref, stage, sems, *, td):
    i = pl.program_id(0)
    n = pl.num_programs(0)
    sb = lax.rem(i, 2)
    x = x_ref[...]
    for j in range(ROW_CHUNKS):
        stage[sb, pl.ds(j, td, stride=ROW_CHUNKS), :] = x[:, j * LANES:(j + 1) * LANES]

    def issue(rb, carry):
        for q in range(ISSUE_UNROLL):
            r = rb * ISSUE_UNROLL + q
            src = stage.at[sb, pl.ds(pl.multiple_of(r * ROW_CHUNKS, ROW_CHUNKS), ROW_CHUNKS)]
            s = slot_ref[0, 0, r]
            dst = xs_ref.at[pl.ds(pl.multiple_of(s * ROW_CHUNKS, ROW_CHUNKS), ROW_CHUNKS)]
            _rows_copy(src, dst, sems.at[sb]).start(priority=q % DMA_QUEUES)
        return carry

    lax.fori_loop(0, td // ISSUE_UNROLL, issue, 0)

    def drain(buf):
        _rows_copy(stage.at[buf], xs_ref.at[pl.ds(0, td * ROW_CHUNKS)], sems.at[buf]).wait()

    @pl.when(i > 0)
    def _():
        drain(1 - sb)

    @pl.when(i == n - 1)
    def _():
        drain(sb)


def _dispatch(x, slots, n_slots, *, td=512):
    t = x.shape[0]
    slots3 = slots.reshape(t // td, 1, td)
    kern = functools.partial(_dispatch_kernel, td=td)
    return pl.pallas_call(
        kern,
        grid=(t // td,),
        in_specs=[pl.BlockSpec((1, 1, td), lambda i: (i, 0, 0), memory_space=pltpu.SMEM),
                  pl.BlockSpec((td, D_MODEL), lambda i: (i, 0))],
        out_specs=pl.BlockSpec(memory_space=pl.ANY),
        out_shape=jax.ShapeDtypeStruct((n_slots * ROW_CHUNKS, LANES), F32),
        scratch_shapes=[pltpu.VMEM((2, td * ROW_CHUNKS, LANES), F32), pltpu.SemaphoreType.DMA((2,))],
        compiler_params=_cparams("arbitrary"),
        name="moe_dispatch",
    )(slots3, x)


def _expert_kernel(ea_ref, eb_ref, rows_ref, next_ref, nv_ref, xs_ref, rhi_ref, rb_ref, wg_hbm, wu_hbm, wd_hbm,
                   os_ref, wg_f32, wu_f32, wd_f32, wg_bf, wu_bf, wd_bf, sems, *, layer):
    i = pl.program_id(0)
    first = i * TILES_PER_STEP
    valid = first < nv_ref[0]
    group = ea_ref[first] // EXPERTS_PER_GROUP
    prev_group = ea_ref[jnp.maximum(first - 1, 0)] // EXPERTS_PER_GROUP
    new_group = jnp.logical_or(i == 0, group != prev_group)

    def weight_copies(grp):
        experts = pl.ds(grp * EXPERTS_PER_GROUP, EXPERTS_PER_GROUP)
        return [pltpu.make_async_copy(src.at[layer, experts], dst, sems.at[k])
                for k, (src, dst) in enumerate(((wg_hbm, wg_f32), (wu_hbm, wu_f32), (wd_hbm, wd_f32)))]

    @pl.when(jnp.logical_and(valid, i == 0))
    def _():
        for c in weight_copies(group):
            c.start()

    @pl.when(jnp.logical_and(valid, new_group))
    def _():
        for c in weight_copies(group):
            c.wait()
        for e in range(EXPERTS_PER_GROUP):
            wg_bf[e] = wg_f32[e].astype(BF16)
            wu_bf[e] = wu_f32[e].astype(BF16)
            wd_bf[e] = wd_f32[e].astype(BF16)
        nxt = next_ref[first]

        @pl.when(nxt >= 0)
        def _():
            for c in weight_copies(nxt):
                c.start()

    def body(row0, ea, eb, m):
        x = jnp.concatenate([xs_ref[pl.ds(row0 + j, m, stride=ROW_CHUNKS), :] for j in range(ROW_CHUNKS)],
                            axis=1)
        xt = x.astype(BF16)
        logits = jnp.dot(xt, rhi_ref[...], preferred_element_type=F32) + rb_ref[...]
        lane = lax.broadcasted_iota(jnp.int32, logits.shape, 1)
        pick = lambda l: jnp.sum(jnp.where(lane == l, logits, 0.0), axis=-1, keepdims=True)
        glog = _group_logits(logits, lane)
        gmax = jnp.max(glog, axis=-1, keepdims=True)
        gw = jnp.exp(pick(_GROUP_LANE0 + group) - gmax) / jnp.sum(jnp.exp(glog - gmax), axis=-1, keepdims=True)
        la = pick(ea)
        lb = pick(eb)
        lmax = jnp.maximum(la, lb)
        pa = jnp.exp(la - lmax)
        pb = jnp.exp(lb - lmax)
        scale = gw / (pa + pb)

        y = None
        for e, w in ((ea, pa * scale), (eb, pb * scale)):
            el = e - group * EXPERTS_PER_GROUP
            gate = jnp.dot(xt, wg_bf[el], preferred_element_type=F32)
            up = jnp.dot(xt, wu_bf[el], preferred_element_type=F32)
            h = (jax.nn.silu(gate) * up).astype(BF16)
            term = w * jnp.dot(h, wd_bf[el], preferred_element_type=F32)
            y = term if y is None else y + term
        for j in range(ROW_CHUNKS):
            os_ref[pl.ds(row0 + j, m, stride=ROW_CHUNKS), :] = y[:, j * LANES:(j + 1) * LANES]

    for s in range(TILES_PER_STEP):
        tile = first + s
        row_steps = (rows_ref[tile] + EXPERT_ROW_STEP - 1) // EXPERT_ROW_STEP
        for k in range(1, EXPERT_TILE // EXPERT_ROW_STEP + 1):
            pl.when(jnp.logical_and(valid, row_steps == k))(
                functools.partial(body, s * EXPERT_TILE * ROW_CHUNKS, ea_ref[tile], eb_ref[tile],
                                  k * EXPERT_ROW_STEP))


def _experts(xs, tile_ea, tile_eb, tile_rows, tile_next_group, n_valid, lw):
    layer = lw["layer"]
    n_steps = xs.shape[0] // (TILES_PER_STEP * EXPERT_TILE * ROW_CHUNKS)
    rows = TILES_PER_STEP * EXPERT_TILE * ROW_CHUNKS
    act = pl.BlockSpec((rows, LANES),
                       lambda i, ea, eb, nr, nx, nv: (jnp.minimum(i, nv[0] // TILES_PER_STEP - 1), 0))
    router = lambda a: pl.BlockSpec((None,) + a.shape[1:], lambda i, ea, eb, nr, nx, nv: (layer, 0, 0),
                                    pipeline_mode=pl.Buffered(1))
    in_hbm = pl.BlockSpec(memory_space=pl.ANY)
    gate_shape = (EXPERTS_PER_GROUP, D_MODEL, D_EXPERT)
    down_shape = (EXPERTS_PER_GROUP, D_EXPERT, D_MODEL)
    grid_spec = pltpu.PrefetchScalarGridSpec(
        num_scalar_prefetch=5,
        grid=(n_steps,),
        in_specs=[act, router(lw["router_hi"]), router(lw["router_b"]), in_hbm, in_hbm, in_hbm],
        out_specs=act,
        scratch_shapes=[pltpu.VMEM(gate_shape, F32), pltpu.VMEM(gate_shape, F32), pltpu.VMEM(down_shape, F32),
                        pltpu.VMEM(gate_shape, BF16), pltpu.VMEM(gate_shape, BF16), pltpu.VMEM(down_shape, BF16),
                        pltpu.SemaphoreType.DMA((3,))],
    )
    return pl.pallas_call(
        functools.partial(_expert_kernel, layer=layer),
        grid_spec=grid_spec,
        out_shape=jax.ShapeDtypeStruct(xs.shape, F32),
        compiler_params=_cparams("arbitrary", vmem_limit_bytes=EXPERT_VMEM_LIMIT_BYTES),
        name="moe_experts",
    )(tile_ea, tile_eb, tile_rows, tile_next_group, n_valid, xs, lw["router_hi"], lw["router_b"],
      lw["w_gate"], lw["w_up"], lw["w_down"])


def _gather_rows_issue(slot_ref, os_ref, gbuf, sems, buf, tc):
    def issue(rb, carry):
        for q in range(ISSUE_UNROLL):
            r = rb * ISSUE_UNROLL + q
            s = slot_ref[0, 0, r]
            src = os_ref.at[pl.ds(pl.multiple_of(s * ROW_CHUNKS, ROW_CHUNKS), ROW_CHUNKS)]
            dst = gbuf.at[buf, pl.ds(pl.multiple_of(r * ROW_CHUNKS, ROW_CHUNKS), ROW_CHUNKS)]
            _rows_copy(src, dst, sems.at[buf]).start(priority=q % DMA_QUEUES)
        return carry

    lax.fori_loop(0, tc // ISSUE_UNROLL, issue, 0)


def _combine_kernel(slot_ref, slot_next_ref, x_ref, g_ref, bt_ref, os_ref, xp_ref, xd_ref,
                    gbuf, sems, *, tc, n_first):
    i = pl.program_id(0)
    n = pl.num_programs(0)
    sb = lax.rem(i, 2)

    @pl.when(i == 0)
    def _():
        _gather_rows_issue(slot_ref, os_ref, gbuf, sems, 0, tc)

    @pl.when(i + 1 < n)
    def _():
        _gather_rows_issue(slot_next_ref, os_ref, gbuf, sems, 1 - sb, tc)

    _rows_copy(os_ref.at[pl.ds(0, tc * ROW_CHUNKS)], gbuf.at[sb], sems.at[sb]).wait()
    y = jnp.concatenate([gbuf[sb, pl.ds(j, tc, stride=ROW_CHUNKS), :] for j in range(ROW_CHUNKS)], axis=1)
    x2 = _layer_norm(DEEPNORM_ALPHA * x_ref[...] + y, g_ref[...], bt_ref[...])

    @pl.when(i < n_first)
    def _():
        xp_ref[...] = x2

    @pl.when(i >= n_first)
    def _():
        xd_ref[...] = x2


def _combine(x, slots, os_, lw, *, n_prompt, tc=512):
    t = x.shape[0]
    nblk = t // tc
    n_first = n_prompt // tc
    slots3 = slots.reshape(nblk, 1, tc)
    kern = functools.partial(_combine_kernel, tc=tc, n_first=n_first)
    smem_tile = lambda f: pl.BlockSpec((1, 1, tc), f, memory_space=pltpu.SMEM)
    return pl.pallas_call(
        kern,
        grid=(nblk,),
        in_specs=[smem_tile(lambda i: (i, 0, 0)),
                  smem_tile(lambda i: (jnp.minimum(i + 1, nblk - 1), 0, 0)),
                  pl.BlockSpec((tc, D_MODEL), lambda i: (i, 0)),
                  _layer_resident(lw["ln2_g"], lw["layer"]), _layer_resident(lw["ln2_b"], lw["layer"]),
                  pl.BlockSpec(memory_space=pl.ANY)],
        out_specs=[pl.BlockSpec((tc, D_MODEL), lambda i: (jnp.minimum(i, n_first - 1), 0)),
                   pl.BlockSpec((tc, D_MODEL), lambda i: (jnp.maximum(i - n_first, 0), 0))],
        out_shape=[jax.ShapeDtypeStruct((n_prompt, D_MODEL), F32),
                   jax.ShapeDtypeStruct((t - n_prompt, D_MODEL), F32)],
        scratch_shapes=[pltpu.VMEM((2, tc * ROW_CHUNKS, LANES), F32), pltpu.SemaphoreType.DMA((2,))],
        compiler_params=_cparams("arbitrary"),
        name="moe_combine",
    )(slots3, slots3, x, lw["ln2_g"], lw["ln2_b"], os_)


def _moe(x, lw, *, n_prompt):
    t = x.shape[0]
    n_tiles = t // EXPERT_TILE + N_CLASSES + N_EXPERT_GROUPS * (TILES_PER_STEP - 1)
    n_tiles = -(-n_tiles // TILES_PER_STEP) * TILES_PER_STEP
    meta, cnt = _router(x, lw)
    cls = meta[:, 0, :].reshape(t).astype(jnp.int32)
    rank = meta[:, 1, :].reshape(t).astype(jnp.int32)
    counts = cnt[:N_CLASSES, 0].astype(jnp.int32)
    seg_tiles = (counts + EXPERT_TILE - 1) // EXPERT_TILE
    group_tiles = jnp.sum(seg_tiles.reshape(N_EXPERT_GROUPS, PAIRS_PER_GROUP), axis=1)
    group_pad = (-group_tiles) % TILES_PER_STEP
    last_of_group = jnp.arange(N_CLASSES) % PAIRS_PER_GROUP == PAIRS_PER_GROUP - 1
    seg_tiles = seg_tiles + jnp.where(last_of_group, jnp.repeat(group_pad, PAIRS_PER_GROUP), 0)
    tile_ends = jnp.cumsum(seg_tiles)
    seg_start = (tile_ends - seg_tiles) * EXPERT_TILE
    class_ids = jnp.arange(N_CLASSES, dtype=jnp.int32)
    slots = jnp.sum(jnp.where(cls[:, None] == class_ids, seg_start, 0), axis=-1) + rank
    n_valid = tile_ends[-1:]
    tile_ids = jnp.minimum(jnp.arange(n_tiles, dtype=jnp.int32), n_valid[0] - 1)
    tile_class = jnp.sum((tile_ends[None, :] <= tile_ids[:, None]).astype(jnp.int32), axis=1)
    of_class = tile_class[:, None] == class_ids
    class_experts = jnp.asarray(_CLASS_EXPERTS, dtype=jnp.int32)
    tile_ea = jnp.sum(jnp.where(of_class, class_experts[:, 0], 0), axis=-1)
    tile_eb = jnp.sum(jnp.where(of_class, class_experts[:, 1], 0), axis=-1)
    rows_left = jnp.sum(jnp.where(of_class, seg_start + counts, 0), axis=-1) - tile_ids * EXPERT_TILE
    tile_rows = jnp.clip(rows_left, 0, EXPERT_TILE)
    group_ids = jnp.arange(N_EXPERT_GROUPS, dtype=jnp.int32)
    later_used = (group_ids[None, :] > group_ids[:, None]) & ((group_tiles + group_pad) > 0)[None, :]
    next_group = jnp.min(jnp.where(later_used, group_ids[None, :], N_EXPERT_GROUPS), axis=1)
    next_group = jnp.where(next_group == N_EXPERT_GROUPS, -1, next_group)
    tile_group = tile_ea // EXPERTS_PER_GROUP
    tile_next_group = jnp.sum(jnp.where(tile_group[:, None] == group_ids, next_group, 0), axis=-1)
    xs = _dispatch(x, slots, n_tiles * EXPERT_TILE)
    os_ = _experts(xs, tile_ea, tile_eb, tile_rows, tile_next_group, n_valid.astype(jnp.int32), lw)
    return _combine(x, slots, os_, lw, n_prompt=n_prompt)


def _t5_bucket(dist):
    n = jnp.maximum(dist, 0)
    max_exact = REL_BUCKETS // 2
    nf = jnp.maximum(n, 1).astype(F32)
    large = max_exact + (jnp.log(nf / max_exact) / math.log(REL_MAX_DISTANCE / max_exact)
                         * (REL_BUCKETS - max_exact)).astype(jnp.int32)
    large = jnp.minimum(large, REL_BUCKETS - 1)
    return jnp.where(n < max_exact, n, large)


def _head_rows(per_head, rows_of):
    out = []
    for kvh in range(N_KV_HEADS):
        out.append(jnp.stack([jnp.concatenate([jnp.repeat(per_head[..., h, :, :], rep, axis=-2)
                                               for h, rep in rows_of(kvh, par)], axis=-2)
                              for par in range(2)], axis=-3))
    return jnp.stack(out, axis=-4)


def _prompt_rows(kvh, par):
    return [(GROUP * kvh + 2 * p + par, 1) for p in range(2)]


def _head_bias(rel_bias, dist, valid):
    bias = jnp.where(valid[..., None], rel_bias[_t5_bucket(dist)].astype(F32), NEG_INF)
    return jnp.moveaxis(bias, -1, 0)


def _prompt_bias(rel_bias):
    period = 4 * WINDOW
    off = jnp.arange(period)
    off = jnp.where(off < 2 * WINDOW, off, off - period)
    dist = WINDOW - off
    per_off = _head_bias(rel_bias, dist, (dist >= 0) & (dist <= WINDOW))
    tiled = jnp.tile(per_off, (1, WINDOW))[:, :WINDOW * (period - 1)]
    return tiled.reshape(N_HEADS, WINDOW, period - 1)[:, :, :2 * WINDOW]


def _attention_tables(rel_bias, sinks, ntok, g):
    bias_p = _head_rows(_prompt_bias(rel_bias), _prompt_rows)

    decode_rows = lambda kvh, par: [(GROUP * kvh + 2 * p + par, g) for p in range(2)]
    tq = jnp.arange(ntok)[:, None]
    kpos_old = jnp.arange(WINDOW)[None, :] - WINDOW
    kpos_new = jnp.arange(ntok)[None, :]
    tables = []
    for kpos in (kpos_old, kpos_new):
        d = tq - kpos
        per_head = _head_bias(rel_bias, d, (d >= 0) & (d <= WINDOW) & (PAST_LEN + kpos >= 0))
        tables.append(_head_rows(per_head, decode_rows))
    bias_d = tables[0]
    biasn_d = jnp.moveaxis(tables[1], -1, 2)[..., None]

    s = sinks.astype(F32)[:, :, None, None]
    sink_p = _head_rows(jnp.broadcast_to(s, s.shape[:2] + (WINDOW, 1)), _prompt_rows)
    sink_d = _head_rows(jnp.broadcast_to(s, s.shape[:2] + (ntok, 1)), decode_rows)
    return bias_p, bias_d, biasn_d, sink_p, sink_d


def _stacked_weights(w_in, pool_w, pool_scale, proj_a, proj_b, w_out, ln1_g, ln1_b, router_group_w,
                     router_group_b, router_expert_w, router_expert_b, expert_w_gate, expert_w_up,
                     expert_w_down, ln2_g, ln2_b):
    depth = w_in.shape[0]
    same_group = jnp.eye(N_POOL_GROUPS, dtype=bool)[None, :, None, :, None]
    pool_bd = jnp.where(same_group, pool_w[:, :, :, None, :], 0.0).reshape(depth, D_POOL, D_POOL)
    pad = ROUTER_LANES - N_EXPERTS - N_EXPERT_GROUPS
    rw = jnp.concatenate([router_expert_w, router_group_w, jnp.zeros((depth, D_MODEL, pad), F32)], axis=-1)
    rb = jnp.concatenate([router_expert_b, router_group_b, jnp.zeros((depth, pad), F32)], axis=-1)
    r_hi = rw.astype(BF16)
    row = lambda a: a[:, None, :]
    return dict(
        w_in=w_in.astype(BF16), pool_w=pool_bd.astype(BF16), pool_scale=row(pool_scale),
        proj_a=proj_a.astype(BF16), proj_b=proj_b.astype(BF16), w_out=w_out.astype(BF16),
        ln1_g=row(ln1_g), ln1_b=row(ln1_b),
        router_hi=r_hi, router_lo=(rw - r_hi.astype(F32)).astype(BF16), router_b=row(rb),
        w_gate=expert_w_gate, w_up=expert_w_up, w_down=expert_w_down,
        ln2_g=row(ln2_g), ln2_b=row(ln2_b))


def _decode_mixer(xs_tn, lw, cache_k, cache_v, hist_t, tables, buf, new_caches, *, nseq, ntok, g, row_block):
    bias_d, biasn_d, sink_d = tables
    u = _in_proj(xs_tn, lw)
    u3 = u.reshape(ntok, nseq, D_IN)
    b, new_k, new_v = _decode_attn(u3, cache_k, cache_v, bias_d, biasn_d, sink_d, new_caches, g=g,
                                   layer=lw["layer"])
    buf = _decode_tail(xs_tn, u, hist_t, b.reshape(ntok * nseq, Q_DIM), lw, buf, nseq=nseq, ntok=ntok,
                       row_block=row_block)
    return buf, u3, (new_k, new_v)


def kernel(x_prompt, x_sample, cache_k, cache_v, state_pool, rel_bias, w_in, pool_w, pool_scale, proj_a, proj_b, w_out, attn_sinks, ln1_g, ln1_b, router_group_w, router_group_b, router_expert_w, router_expert_b, expert_w_gate, expert_w_up, expert_w_down, ln2_g, ln2_b):
    bsz, seq, _ = x_prompt.shape
    nseq, ntok, _ = x_sample.shape
    g = 16
    depth = w_in.shape[0]
    n_prompt = bsz * seq
    n_decode = ntok * nseq
    assert n_prompt % n_decode == 0
    bias_p, bias_d, biasn_d, sink_p, sink_d = _attention_tables(rel_bias, attn_sinks, ntok, g)
    weights = _stacked_weights(w_in, pool_w, pool_scale, proj_a, proj_b, w_out, ln1_g, ln1_b, router_group_w,
                               router_group_b, router_expert_w, router_expert_b, expert_w_gate, expert_w_up,
                               expert_w_down, ln2_g, ln2_b)

    xp = x_prompt.reshape(n_prompt, D_MODEL)
    xs_tn = jnp.swapaxes(x_sample, 0, 1).reshape(n_decode, D_MODEL)
    hist_t = jnp.swapaxes(state_pool, 1, 2)
    ck_t = jnp.transpose(cache_k, (0, 1, 3, 4, 2))
    cv_t = jnp.transpose(cache_v, (0, 1, 3, 4, 2))
    prompt_state, decode_u, new_caches = [], [], None
    for l in range(depth):
        lw = dict(weights, layer=l)
        x1, kst, vst, pst = _prompt_mixer(xp, lw, bias_p, sink_p, bsz=bsz, n_total=n_prompt + n_decode)
        x1, u3, new_caches = _decode_mixer(xs_tn, lw, ck_t, cv_t, hist_t, (bias_d, biasn_d, sink_d), x1,
                                           new_caches, nseq=nseq, ntok=ntok, g=g,
                                           row_block=n_prompt // n_decode)
        xp, xs_tn = _moe(x1, lw, n_prompt=n_prompt)
        prompt_state.append((kst, vst, pst))
        decode_u.append(u3)

    pk, pv, pp = (jnp.stack(parts) for parts in zip(*prompt_state))
    heads = lambda a: a.reshape(a.shape[:3] + (N_KV_HEADS, HEAD_DIM))
    new_pool = jnp.stack([jnp.swapaxes(u3[:, :, :D_POOL], 0, 1) for u3 in decode_u])
    sample_pool = jnp.concatenate([state_pool[:, :, ntok:], new_pool], axis=2)
    sample_k, sample_v = (jnp.transpose(c, (0, 1, 4, 2, 3)) for c in new_caches)
    y_sample = jnp.swapaxes(xs_tn.reshape(ntok, nseq, D_MODEL), 0, 1)
    return (xp.reshape(bsz, seq, D_MODEL), y_sample, heads(pk), heads(pv), pp[:, :, -POOL_HIST:],
            sample_k, sample_v, sample_pool)
```

```python
import functools
import math

import jax
import jax.numpy as jnp
from jax import lax
from jax.experimental import pallas as pl
from jax.experimental.pallas import tpu as pltpu

D_MODEL = 1024
DEPTH = 2
PAST_LEN = 16384
D_POOL = D_MODEL // 2
POOL_WINDOWS = (2, 4, 8, 16)
N_POOL_GROUPS = len(POOL_WINDOWS)
POOL_GROUP_DIM = D_POOL // N_POOL_GROUPS
POOL_HIST = max(POOL_WINDOWS) - 1
assert all(w & (w - 1) == 0 for w in POOL_WINDOWS)
HEAD_DIM = 64
N_HEADS = (D_MODEL // 2) // HEAD_DIM
N_KV_HEADS = 2
GROUP = N_HEADS // N_KV_HEADS
Q_DIM = N_HEADS * HEAD_DIM
KV_DIM = N_KV_HEADS * HEAD_DIM
WINDOW = 128
ATTN_SCALE = HEAD_DIM ** -0.5
assert math.log2(ATTN_SCALE).is_integer()
REL_BUCKETS = 32
REL_MAX_DISTANCE = 128
D_IN = D_POOL + Q_DIM + 2 * KV_DIM + 2 * D_MODEL
N_EXPERT_GROUPS = 4
EXPERTS_PER_GROUP = 8
N_EXPERTS = N_EXPERT_GROUPS * EXPERTS_PER_GROUP
D_EXPERT = D_MODEL // 4
DEEPNORM_ALPHA = (2 * DEPTH) ** 0.25
LN_EPS = 1e-5

LANES = 128
SUBLANES = 8
VMEM_LIMIT_BYTES = 56 * 1024 * 1024
EXPERT_VMEM_LIMIT_BYTES = 62 * 1024 * 1024

_O_Q = D_POOL
_O_K = _O_Q + Q_DIM
_O_V = _O_K + KV_DIM
_O_GA = _O_V + KV_DIM
_O_GB = _O_GA + D_MODEL

ROW_CHUNKS = D_MODEL // LANES
KEY_PAD = 2 * WINDOW
EXPERT_TILE = 256
OUT_PROJ_CHUNKS = 2
EXPERT_ROW_STEP = 32
TILES_PER_STEP = 2
DMA_QUEUES = 2
ISSUE_UNROLL = 8
PAIRS_PER_GROUP = EXPERTS_PER_GROUP * (EXPERTS_PER_GROUP - 1) // 2
N_CLASSES = N_EXPERT_GROUPS * PAIRS_PER_GROUP
_CLASS_EXPERTS = tuple((g * EXPERTS_PER_GROUP + a, g * EXPERTS_PER_GROUP + b)
                       for g in range(N_EXPERT_GROUPS)
                       for a in range(EXPERTS_PER_GROUP) for b in range(a + 1, EXPERTS_PER_GROUP))
ROUTER_LANES = LANES
_GROUP_LANE0 = N_EXPERTS
assert EXPERTS_PER_GROUP == SUBLANES and N_EXPERT_GROUPS <= SUBLANES

BF16 = jnp.bfloat16
F32 = jnp.float32
NEG_INF = float("-inf")


def _cparams(*sem, flags=None, vmem_limit_bytes=VMEM_LIMIT_BYTES):
    return pltpu.CompilerParams(dimension_semantics=sem, vmem_limit_bytes=vmem_limit_bytes, flags=flags)


def _resident(shape):
    nd = len(shape)
    return pl.BlockSpec(shape, lambda *_: (0,) * nd, pipeline_mode=pl.Buffered(1))


def _layer_resident(stacked, layer):
    nd = stacked.ndim - 1
    return pl.BlockSpec((None,) + stacked.shape[1:], lambda *_: (layer,) + (0,) * nd,
                        pipeline_mode=pl.Buffered(1))


def _layer_norm(y, g, b):
    mu = jnp.mean(y, axis=-1, keepdims=True)
    yc = y - mu
    var = jnp.mean(yc * yc, axis=-1, keepdims=True)
    return yc * lax.rsqrt(var + LN_EPS) * g + b


def _softmax_pv(s, sink, v_bf16):
    m = jnp.maximum(jnp.max(s, axis=-1, keepdims=True), sink)
    p = jnp.exp(s - m)
    den = jnp.sum(p, axis=-1, keepdims=True) + jnp.exp(sink - m)
    o = jnp.dot(p.astype(BF16), v_bf16, preferred_element_type=F32)
    return o / den


def _half_lane_variants(t):
    lane = lax.broadcasted_iota(jnp.int32, t.shape, 1)
    lo = lane < HEAD_DIM
    tr = pltpu.roll(t, HEAD_DIM, 1)
    z = jnp.zeros_like(t)
    return ((jnp.where(lo, t, z), jnp.where(lo, z, tr)),
            (jnp.where(lo, tr, z), jnp.where(lo, z, t)))


def _mix_tail(x, a_in, b_in, ga, gb, pw, ps, pa, pb, wo, g, bt):
    a = jnp.dot(a_in.astype(BF16), pw, preferred_element_type=F32) * ps
    pa_o = jnp.dot(a.astype(BF16), pa, preferred_element_type=F32)
    pb_o = jnp.dot(b_in.astype(BF16), pb, preferred_element_type=F32)
    sigmoid = lambda z: 0.5 + 0.5 * jnp.tanh(0.5 * z)
    merged = (sigmoid(ga) * pa_o + sigmoid(gb) * pb_o).astype(BF16)
    rows = x.shape[0]
    chunk = rows // OUT_PROJ_CHUNKS
    out = []
    for r0 in range(0, rows, chunk):
        mix = jnp.dot(merged[r0:r0 + chunk], wo, preferred_element_type=F32)
        out.append(_layer_norm(DEEPNORM_ALPHA * x[r0:r0 + chunk] + mix, g, bt))
    return jnp.concatenate(out, axis=0)


def _prompt_mixer_kernel(x_ref, win_ref, pw_ref, ps_ref, pa_ref, pb_ref, wo_ref, bias_ref, sink_ref,
                         g_ref, bt_ref, x1_ref, kst_ref, vst_ref, pst_ref, kt_scr, v_scr, ext_scr, *, tm):
    i = pl.program_id(1)
    hist = 2 * SUBLANES

    @pl.when(i == 0)
    def _():
        kt_scr[:, :, :WINDOW] = jnp.zeros((4, LANES, WINDOW), BF16)
        v_scr[:, :WINDOW, :] = jnp.zeros((4, WINDOW, LANES), BF16)
        ext_scr[:hist, :] = jnp.zeros((hist, D_POOL), F32)

    @pl.when(i > 0)
    def _():
        kt_scr[:, :, :WINDOW] = kt_scr[:, :, tm:tm + WINDOW]
        v_scr[:, :WINDOW, :] = v_scr[:, tm:tm + WINDOW, :]
        ext_scr[:hist, :] = ext_scr[tm:tm + hist, :]

    x = x_ref[...]
    u = jnp.dot(x.astype(BF16), win_ref[...], preferred_element_type=F32)
    u_pool = u[:, :D_POOL]
    q = u[:, _O_Q:_O_K]
    k = u[:, _O_K:_O_V]
    v = u[:, _O_V:_O_GA]
    ga = u[:, _O_GA:_O_GB]
    gb = u[:, _O_GB:]

    kst_ref[...] = k[tm - WINDOW:, :]
    vst_ref[...] = v[tm - WINDOW:, :]
    pst_ref[...] = u_pool[tm - hist:, :]

    kt = k.T
    kz = jnp.zeros((HEAD_DIM, tm), F32)
    kt_scr[0, :, WINDOW:] = jnp.concatenate([kt[:HEAD_DIM], kz], axis=0).astype(BF16)
    kt_scr[1, :, WINDOW:] = jnp.concatenate([kz, kt[:HEAD_DIM]], axis=0).astype(BF16)
    kt_scr[2, :, WINDOW:] = jnp.concatenate([kt[HEAD_DIM:], kz], axis=0).astype(BF16)
    kt_scr[3, :, WINDOW:] = jnp.concatenate([kz, kt[HEAD_DIM:]], axis=0).astype(BF16)
    vv = _half_lane_variants(v)
    for kvh in range(N_KV_HEADS):
        for par in range(2):
            v_scr[2 * kvh + par, WINDOW:, :] = vv[kvh][par].astype(BF16)

    col = lax.broadcasted_iota(jnp.int32, (2 * WINDOW, 2 * WINDOW), 1)
    no_prev = col < jnp.where(i == 0, WINDOW, 0)
    qb = (q * ATTN_SCALE).astype(BF16)
    b_rows = []
    for j in range(tm // WINDOW):
        r0 = j * WINDOW
        o_kvh = []
        for kvh in range(N_KV_HEADS):
            c0 = kvh * 2 * LANES
            q2 = jnp.concatenate([qb[r0:r0 + WINDOW, c0:c0 + LANES],
                                  qb[r0:r0 + WINDOW, c0 + LANES:c0 + 2 * LANES]], axis=0)
            acc = None
            for par in range(2):
                s = jnp.dot(q2, kt_scr[2 * kvh + par, :, r0:r0 + 2 * WINDOW], preferred_element_type=F32)
                s = s + bias_ref[kvh, par]
                if j == 0:
                    s = jnp.where(no_prev, NEG_INF, s)
                o = _softmax_pv(s, sink_ref[kvh, par], v_scr[2 * kvh + par, r0:r0 + 2 * WINDOW, :])
                acc = o if acc is None else acc + o
            o_kvh.append(acc)
        b_rows.append(jnp.concatenate([o_kvh[0][:WINDOW], o_kvh[0][WINDOW:],
                                       o_kvh[1][:WINDOW], o_kvh[1][WINDOW:]], axis=1))
    b = jnp.concatenate(b_rows, axis=0)

    ext_scr[hist:, :] = u_pool
    pos = i * tm + lax.broadcasted_iota(jnp.int32, (tm, 1), 0)
    means = []
    for gi, w in enumerate(POOL_WINDOWS):
        c0 = gi * POOL_GROUP_DIM
        acc = ext_scr[:, c0:c0 + POOL_GROUP_DIM]
        d = 1
        while d < w:
            acc = acc + pltpu.roll(acc, d, 0)
            d *= 2
        inv_cnt = 1.0 / jnp.minimum(w, pos + 1).astype(F32)
        means.append(acc[hist:] * inv_cnt)
    pooled = jnp.concatenate(means, axis=1) - u_pool

    x1_ref[...] = _mix_tail(x, pooled, b, ga, gb, pw_ref[...], ps_ref[...], pa_ref[...], pb_ref[...],
                            wo_ref[...], g_ref[...], bt_ref[...])


def _prompt_mixer(x, lw, bias_p, sink_t, *, bsz, n_total, tm=512):
    seq = x.shape[0] // bsz
    nt = seq // tm
    hist = 2 * SUBLANES
    kern = functools.partial(_prompt_mixer_kernel, tm=tm)
    tile = pl.BlockSpec((tm, D_MODEL), lambda b, i: (b * nt + i, 0))
    state = lambda r, w: pl.BlockSpec((None, r, w), lambda b, i: (b, 0, 0))
    params = (lw["w_in"], lw["pool_w"], lw["pool_scale"], lw["proj_a"], lw["proj_b"], lw["w_out"])
    tail = (sink_t, lw["ln1_g"], lw["ln1_b"])
    return pl.pallas_call(
        kern,
        grid=(bsz, nt),
        in_specs=([tile] + [_layer_resident(a, lw["layer"]) for a in params] + [_resident(bias_p.shape)]
                  + [_layer_resident(a, lw["layer"]) for a in tail]),
        out_specs=[tile, state(WINDOW, KV_DIM), state(WINDOW, KV_DIM), state(hist, D_POOL)],
        out_shape=[jax.ShapeDtypeStruct((n_total, D_MODEL), F32),
                   jax.ShapeDtypeStruct((bsz, WINDOW, KV_DIM), F32),
                   jax.ShapeDtypeStruct((bsz, WINDOW, KV_DIM), F32),
                   jax.ShapeDtypeStruct((bsz, hist, D_POOL), F32)],
        scratch_shapes=[pltpu.VMEM((4, LANES, WINDOW + tm), BF16),
                        pltpu.VMEM((4, WINDOW + tm, LANES), BF16),
                        pltpu.VMEM((hist + tm, D_POOL), F32)],
        compiler_params=_cparams("arbitrary", "arbitrary"),
        name="prompt_mixer",
    )(x, lw["w_in"], lw["pool_w"], lw["pool_scale"], lw["proj_a"], lw["proj_b"], lw["w_out"],
      bias_p, sink_t, lw["ln1_g"], lw["ln1_b"])


def _in_proj_kernel(x_ref, win_ref, u_ref):
    u_ref[...] = jnp.dot(x_ref[...].astype(BF16), win_ref[...], preferred_element_type=F32)


def _in_proj(x, lw):
    n = x.shape[0]
    return pl.pallas_call(
        _in_proj_kernel,
        grid=(1,),
        in_specs=[pl.BlockSpec(x.shape, lambda i: (0, 0)), _layer_resident(lw["w_in"], lw["layer"])],
        out_specs=pl.BlockSpec((n, D_IN), lambda i: (0, 0)),
        out_shape=jax.ShapeDtypeStruct((n, D_IN), F32),
        compiler_params=_cparams("arbitrary"),
        name="decode_in_proj",
    )(x, lw["w_in"])


def _decode_attn_kernel(q_ref, kn_ref, vn_ref, kc_ref, vc_ref, bias_ref, biasn_ref, sink_ref, *rest, g, ntok):
    b_ref, sk_ref, sv_ref, new_scr = rest[-4:]
    slabs = 2 * ntok
    rows = slabs * g

    lane = lax.broadcasted_iota(jnp.int32, (KV_DIM, WINDOW), 1)
    fill = jnp.zeros((LANES - g * ntok, KV_DIM), F32)
    for new_ref, old_ref, out_ref in ((kn_ref, kc_ref, sk_ref), (vn_ref, vc_ref, sv_ref)):
        for t in range(ntok):
            new_scr[pl.ds(t, g, stride=ntok), :] = new_ref[t]
        packed = jnp.concatenate([new_scr[...], fill], axis=0).T
        for n in range(g):
            tile = pltpu.roll(old_ref[n].reshape(KV_DIM, WINDOW), WINDOW - ntok, 1)
            cols = pltpu.roll(packed, (WINDOW - ntok - n * ntok) % LANES, 1)
            out_ref[n] = jnp.where(lane >= WINDOW - ntok, cols, tile).reshape(N_KV_HEADS, HEAD_DIM, WINDOW)

    def cache_variants(ref):
        z = jnp.zeros((HEAD_DIM, WINDOW), F32)
        out = []
        for kvh in range(N_KV_HEADS):
            out.append((jnp.concatenate([jnp.concatenate([ref[n, kvh], z], axis=0) for n in range(g)], axis=1),
                        jnp.concatenate([jnp.concatenate([z, ref[n, kvh]], axis=0) for n in range(g)], axis=1)))
        return out

    kk = cache_variants(kc_ref)
    vv = cache_variants(vc_ref)
    kn = [_half_lane_variants(kn_ref[t]) for t in range(ntok)]
    vn = [_half_lane_variants(vn_ref[t]) for t in range(ntok)]
    per_row = lambda a: jnp.concatenate([a] * slabs, axis=0)
    row_seq = lax.rem(lax.broadcasted_iota(jnp.int32, (rows, WINDOW), 0), g)
    diag = (lax.rem(lax.broadcasted_iota(jnp.int32, (rows, g * WINDOW), 0), g)
            == lax.broadcasted_iota(jnp.int32, (rows, g * WINDOW), 1) // WINDOW)
    for kvh in range(N_KV_HEADS):
        c0 = kvh * 2 * LANES
        q2f = jnp.concatenate([q_ref[t, :, c0 + p * LANES:c0 + (p + 1) * LANES]
                               for p in range(2) for t in range(ntok)], axis=0)
        q2 = q2f.astype(BF16)
        acc = None
        for par in range(2):
            full = jnp.dot(q2, kk[kvh][par].astype(BF16), preferred_element_type=F32)
            s = jnp.zeros((rows, WINDOW), F32)
            for n in range(g):
                s = jnp.where(row_seq == n, full[:, n * WINDOW:(n + 1) * WINDOW], s)
            s = s * ATTN_SCALE + bias_ref[kvh, par]
            s_new = [jnp.sum(q2f * per_row(kn[t][kvh][par]), axis=-1, keepdims=True) * ATTN_SCALE
                     + biasn_ref[kvh, par, t] for t in range(ntok)]
            sink = sink_ref[kvh, par]
            m = jnp.maximum(jnp.max(s, axis=-1, keepdims=True), sink)
            for t in range(ntok):
                m = jnp.maximum(m, s_new[t])
            p = jnp.exp(s - m)
            den = jnp.sum(p, axis=-1, keepdims=True) + jnp.exp(sink - m)
            p_bd = jnp.where(diag, jnp.concatenate([p] * g, axis=1), 0.0).astype(BF16)
            o = lax.dot_general(p_bd, vv[kvh][par].astype(BF16), (((1,), (1,)), ((), ())),
                                preferred_element_type=F32)
            for t in range(ntok):
                p_new = jnp.exp(s_new[t] - m)
                den = den + p_new
                o = o + p_new * per_row(vn[t][kvh][par])
            o = o / den
            acc = o if acc is None else acc + o
        for p in range(2):
            for t in range(ntok):
                r0 = (p * ntok + t) * g
                b_ref[t, :, c0 + p * LANES:c0 + (p + 1) * LANES] = acc[r0:r0 + g]


def _decode_attn(u3, cache_k, cache_v, bias_d, biasn_d, sink_d, new_caches, *, g, layer):
    ntok, nseq, _ = u3.shape
    cache = pl.BlockSpec((None, g, N_KV_HEADS, HEAD_DIM, WINDOW), lambda i: (layer, i, 0, 0, 0))
    kern = functools.partial(_decode_attn_kernel, g=g, ntok=ntok)
    new_cols = lambda w, off: pl.BlockSpec((ntok, g, w), lambda i: (0, i, off // w))
    ins = [u3, u3, u3, cache_k, cache_v, bias_d, biasn_d, sink_d]
    in_specs = [new_cols(Q_DIM, _O_Q), new_cols(KV_DIM, _O_K), new_cols(KV_DIM, _O_V), cache, cache,
                _resident(bias_d.shape), _resident(biasn_d.shape), _layer_resident(sink_d, layer)]
    aliases = {}
    if new_caches is not None:
        aliases = {len(ins): 1, len(ins) + 1: 2}
        ins += list(new_caches)
        in_specs += [pl.BlockSpec(memory_space=pl.ANY)] * 2
    return pl.pallas_call(
        kern,
        grid=(nseq // g,),
        in_specs=in_specs,
        out_specs=[pl.BlockSpec((ntok, g, Q_DIM), lambda i: (0, i, 0)), cache, cache],
        out_shape=[jax.ShapeDtypeStruct((ntok, nseq, Q_DIM), F32),
                   jax.ShapeDtypeStruct(cache_k.shape, F32), jax.ShapeDtypeStruct(cache_v.shape, F32)],
        input_output_aliases=aliases,
        scratch_shapes=[pltpu.VMEM((g * ntok, KV_DIM), F32)],
        compiler_params=_cparams("arbitrary"),
        name="decode_attn",
    )(*ins)


def _decode_tail_kernel(x_ref, u_ref, hist_ref, b_ref, pw_ref, ps_ref, pa_ref, pb_ref,
                        wo_ref, g_ref, bt_ref, buf_ref, x1_ref, *, nseq, ntok, start_pos):
    del buf_ref

    def ext_row(r, c0):
        if r < POOL_HIST:
            return hist_ref[r, :, c0:c0 + POOL_GROUP_DIM]
        return u_ref[(r - POOL_HIST) * nseq:(r - POOL_HIST + 1) * nseq, c0:c0 + POOL_GROUP_DIM]

    pooled_rows = []
    for t in range(ntok):
        means = []
        for gi, w in enumerate(POOL_WINDOWS):
            acc = None
            for d in range(w):
                r = POOL_HIST + t - d
                if start_pos - POOL_HIST + r < 0:
                    continue
                term = ext_row(r, gi * POOL_GROUP_DIM)
                acc = term if acc is None else acc + term
            means.append(acc / float(min(w, start_pos + t + 1)))
        pooled_rows.append(jnp.concatenate(means, axis=1))
    pooled = jnp.concatenate(pooled_rows, axis=0) - u_ref[:, :D_POOL]
    x1_ref[...] = _mix_tail(x_ref[...], pooled, b_ref[...], u_ref[:, _O_GA:_O_GB], u_ref[:, _O_GB:], pw_ref[...],
                            ps_ref[...], pa_ref[...], pb_ref[...], wo_ref[...], g_ref[...], bt_ref[...])


def _decode_tail(x, u, hist_t, b, lw, buf, *, nseq, ntok, row_block):
    n = ntok * nseq
    kern = functools.partial(_decode_tail_kernel, nseq=nseq, ntok=ntok, start_pos=PAST_LEN)
    whole = lambda a: pl.BlockSpec(a.shape, lambda i: (0,) * a.ndim)
    layered = lambda a: _layer_resident(a, lw["layer"])
    params = (lw["pool_w"], lw["pool_scale"], lw["proj_a"], lw["proj_b"], lw["w_out"], lw["ln1_g"], lw["ln1_b"])
    ins = (x, u, hist_t, b) + params
    return pl.pallas_call(
        kern,
        grid=(1,),
        in_specs=([whole(x), whole(u), layered(hist_t), whole(b)] + [layered(a) for a in params]
                  + [pl.BlockSpec(memory_space=pl.ANY)]),
        out_specs=pl.BlockSpec((n, D_MODEL), lambda i: (row_block, 0)),
        out_shape=jax.ShapeDtypeStruct(buf.shape, F32),
        input_output_aliases={len(ins): 0},
        compiler_params=_cparams("arbitrary"),
        name="decode_tail",
    )(*ins, buf)


def _router_logits(x, whi, wlo, br):
    xh = x.astype(BF16)
    xl = (x - xh.astype(F32)).astype(BF16)
    return (jnp.dot(xh, whi, preferred_element_type=F32)
            + (jnp.dot(xh, wlo, preferred_element_type=F32)
               + jnp.dot(xl, whi, preferred_element_type=F32))) + br


def _group_logits(logits, lane):
    is_group = jnp.logical_and(lane >= _GROUP_LANE0, lane < _GROUP_LANE0 + N_EXPERT_GROUPS)
    return jnp.where(is_group, logits, NEG_INF)


def _router_kernel(x_ref, whi_ref, wlo_ref, br_ref, meta_ref, cnt_ref, run_scr, *, tr):
    i = pl.program_id(0)

    @pl.when(i == 0)
    def _():
        run_scr[...] = jnp.zeros_like(run_scr)

    lt = _router_logits(x_ref[...], whi_ref[...], wlo_ref[...], br_ref[...]).T
    row = lax.broadcasted_iota(jnp.int32, (EXPERTS_PER_GROUP, tr), 0)
    none = jnp.int32(EXPERTS_PER_GROUP)

    def argmax_rows(vals):
        vmax = jnp.max(vals, axis=0, keepdims=True)
        return jnp.min(jnp.where(vals == vmax, row, none), axis=0, keepdims=True)

    glog = jnp.where(row < N_EXPERT_GROUPS, lt[_GROUP_LANE0:_GROUP_LANE0 + EXPERTS_PER_GROUP], NEG_INF)
    gidx = argmax_rows(glog)
    el = lt[:EXPERTS_PER_GROUP]
    for g in range(1, N_EXPERT_GROUPS):
        el = jnp.where(gidx == g, lt[g * EXPERTS_PER_GROUP:(g + 1) * EXPERTS_PER_GROUP], el)
    e1 = argmax_rows(el)
    e2 = argmax_rows(jnp.where(row == e1, NEG_INF, el))
    lo = jnp.minimum(e1, e2)
    hi = jnp.maximum(e1, e2)
    pair = lax.shift_right_logical(lo * (2 * EXPERTS_PER_GROUP - 1 - lo), 1) + (hi - lo - 1)
    cls = gidx * PAIRS_PER_GROUP + pair

    oh = jnp.where(lax.broadcasted_iota(jnp.int32, (ROUTER_LANES, tr), 0) == cls, 1.0, 0.0)
    rr = lax.broadcasted_iota(jnp.int32, (tr, tr), 0)
    cc = lax.broadcasted_iota(jnp.int32, (tr, tr), 1)
    earlier = jnp.where(rr < cc, 1.0, 0.0).astype(BF16)
    before = jnp.dot(oh.astype(BF16), earlier, preferred_element_type=F32) + run_scr[...]
    rank = jnp.sum(oh * before, axis=0, keepdims=True)
    run_scr[...] += jnp.sum(oh, axis=1, keepdims=True)
    cnt_ref[...] = run_scr[...]

    meta_ref[...] = jnp.where(row == 0, cls.astype(F32), jnp.where(row == 1, rank, 0.0))


def _router(x, lw, *, tr=512):
    t = x.shape[0]
    kern = functools.partial(_router_kernel, tr=tr)
    return pl.pallas_call(
        kern,
        grid=(t // tr,),
        in_specs=[pl.BlockSpec((tr, D_MODEL), lambda i: (i, 0)),
                  _layer_resident(lw["router_hi"], lw["layer"]), _layer_resident(lw["router_lo"], lw["layer"]),
                  _layer_resident(lw["router_b"], lw["layer"])],
        out_specs=[pl.BlockSpec((None, SUBLANES, tr), lambda i: (i, 0, 0)),
                   pl.BlockSpec((ROUTER_LANES, 1), lambda i: (0, 0))],
        out_shape=[jax.ShapeDtypeStruct((t // tr, SUBLANES, tr), F32),
                   jax.ShapeDtypeStruct((ROUTER_LANES, 1), F32)],
        scratch_shapes=[pltpu.VMEM((ROUTER_LANES, 1), F32)],
        compiler_params=_cparams("arbitrary"),
        name="router",
    )(x, lw["router_hi"], lw["router_lo"], lw["router_b"])


def _rows_copy(src, dst, sem):
    return pltpu.make_async_copy(src, dst, sem)


def _dispatch_kernel(slot_ref, x_ref, xs_ref, stage, sems, *, td):
    i = pl.program_id(0)
    n = pl.num_programs(0)
    sb = lax.rem(i, 2)
    x = x_ref[...]
    for j in range(ROW_CHUNKS):
        stage[sb, pl.ds(j, td, stride=ROW_CHUNKS), :] = x[:, j * LANES:(j + 1) * LANES]

    def issue(rb, carry):
        for q in range(ISSUE_UNROLL):
            r = rb * ISSUE_UNROLL + q
            src = stage.at[sb, pl.ds(pl.multiple_of(r * ROW_CHUNKS, ROW_CHUNKS), ROW_CHUNKS)]
            s = slot_ref[0, 0, r]
            dst = xs_ref.at[pl.ds(pl.multiple_of(s * ROW_CHUNKS, ROW_CHUNKS), ROW_CHUNKS)]
            _rows_copy(src, dst, sems.at[sb]).start(priority=q % DMA_QUEUES)
        return carry

    lax.fori_loop(0, td // ISSUE_UNROLL, issue, 0)

    def drain(buf):
        _rows_copy(stage.at[buf], xs_ref.at[pl.ds(0, td * ROW_CHUNKS)], sems.at[buf]).wait()

    @pl.when(i > 0)
    def _():
        drain(1 - sb)

    @pl.when(i == n - 1)
    def _():
        drain(sb)


def _dispatch(x, slots, n_slots, *, td=512):
    t = x.shape[0]
    slots3 = slots.reshape(t // td, 1, td)
    kern = functools.partial(_dispatch_kernel, td=td)
    return pl.pallas_call(
        kern,
        grid=(t // td,),
        in_specs=[pl.BlockSpec((1, 1, td), lambda i: (i, 0, 0), memory_space=pltpu.SMEM),
                  pl.BlockSpec((td, D_MODEL), lambda i: (i, 0))],
        out_specs=pl.BlockSpec(memory_space=pl.ANY),
        out_shape=jax.ShapeDtypeStruct((n_slots * ROW_CHUNKS, LANES), F32),
        scratch_shapes=[pltpu.VMEM((2, td * ROW_CHUNKS, LANES), F32), pltpu.SemaphoreType.DMA((2,))],
        compiler_params=_cparams("arbitrary"),
        name="moe_dispatch",
    )(slots3, x)


def _expert_kernel(ea_ref, eb_ref, rows_ref, next_ref, nv_ref, xs_ref, rhi_ref, rb_ref, wg_hbm, wu_hbm, wd_hbm,
                   os_ref, wg_f32, wu_f32, wd_f32, wg_bf, wu_bf, wd_bf, sems, *, layer):
    i = pl.program_id(0)
    first = i * TILES_PER_STEP
    valid = first < nv_ref[0]
    group = ea_ref[first] // EXPERTS_PER_GROUP
    prev_group = ea_ref[jnp.maximum(first - 1, 0)] // EXPERTS_PER_GROUP
    new_group = jnp.logical_or(i == 0, group != prev_group)

    def weight_copies(grp):
        experts = pl.ds(grp * EXPERTS_PER_GROUP, EXPERTS_PER_GROUP)
        return [pltpu.make_async_copy(src.at[layer, experts], dst, sems.at[k])
                for k, (src, dst) in enumerate(((wg_hbm, wg_f32), (wu_hbm, wu_f32), (wd_hbm, wd_f32)))]

    @pl.when(jnp.logical_and(valid, i == 0))
    def _():
        for c in weight_copies(group):
            c.start()

    @pl.when(jnp.logical_and(valid, new_group))
    def _():
        for c in weight_copies(group):
            c.wait()
        for e in range(EXPERTS_PER_GROUP):
            wg_bf[e] = wg_f32[e].astype(BF16)
            wu_bf[e] = wu_f32[e].astype(BF16)
            wd_bf[e] = wd_f32[e].astype(BF16)
        nxt = next_ref[first]

        @pl.when(nxt >= 0)
        def _():
            for c in weight_copies(nxt):
                c.start()

    def body(row0, ea, eb, m):
        x = jnp.concatenate([xs_ref[pl.ds(row0 + j, m, stride=ROW_CHUNKS), :] for j in range(ROW_CHUNKS)],
                            axis=1)
        xt = x.astype(BF16)
        logits = jnp.dot(xt, rhi_ref[...], preferred_element_type=F32) + rb_ref[...]
        lane = lax.broadcasted_iota(jnp.int32, logits.shape, 1)
        pick = lambda l: jnp.sum(jnp.where(lane == l, logits, 0.0), axis=-1, keepdims=True)
        glog = _group_logits(logits, lane)
        gmax = jnp.max(glog, axis=-1, keepdims=True)
        gw = jnp.exp(pick(_GROUP_LANE0 + group) - gmax) / jnp.sum(jnp.exp(glog - gmax), axis=-1, keepdims=True)
        la = pick(ea)
        lb = pick(eb)
        lmax = jnp.maximum(la, lb)
        pa = jnp.exp(la - lmax)
        pb = jnp.exp(lb - lmax)
        scale = gw / (pa + pb)

        y = None
        for e, w in ((ea, pa * scale), (eb, pb * scale)):
            el = e - group * EXPERTS_PER_GROUP
            gate = jnp.dot(xt, wg_bf[el], preferred_element_type=F32)
            up = jnp.dot(xt, wu_bf[el], preferred_element_type=F32)
            h = (gate * (0.5 + 0.5 * jnp.tanh(0.5 * gate)) * up).astype(BF16)
            term = w * jnp.dot(h, wd_bf[el], preferred_element_type=F32)
            y = term if y is None else y + term
        for j in range(ROW_CHUNKS):
            os_ref[pl.ds(row0 + j, m, stride=ROW_CHUNKS), :] = y[:, j * LANES:(j + 1) * LANES]

    for s in range(TILES_PER_STEP):
        tile = first + s
        row_steps = (rows_ref[tile] + EXPERT_ROW_STEP - 1) // EXPERT_ROW_STEP
        for k in range(1, EXPERT_TILE // EXPERT_ROW_STEP + 1):
            pl.when(jnp.logical_and(valid, row_steps == k))(
                functools.partial(body, s * EXPERT_TILE * ROW_CHUNKS, ea_ref[tile], eb_ref[tile],
                                  k * EXPERT_ROW_STEP))


def _experts(xs, tile_ea, tile_eb, tile_rows, tile_next_group, n_valid, lw):
    layer = lw["layer"]
    n_steps = xs.shape[0] // (TILES_PER_STEP * EXPERT_TILE * ROW_CHUNKS)
    rows = TILES_PER_STEP * EXPERT_TILE * ROW_CHUNKS
    act = pl.BlockSpec((rows, LANES),
                       lambda i, ea, eb, nr, nx, nv: (jnp.minimum(i, nv[0] // TILES_PER_STEP - 1), 0))
    router = lambda a: pl.BlockSpec((None,) + a.shape[1:], lambda i, ea, eb, nr, nx, nv: (layer, 0, 0),
                                    pipeline_mode=pl.Buffered(1))
    in_hbm = pl.BlockSpec(memory_space=pl.ANY)
    gate_shape = (EXPERTS_PER_GROUP, D_MODEL, D_EXPERT)
    down_shape = (EXPERTS_PER_GROUP, D_EXPERT, D_MODEL)
    grid_spec = pltpu.PrefetchScalarGridSpec(
        num_scalar_prefetch=5,
        grid=(n_steps,),
        in_specs=[act, router(lw["router_hi"]), router(lw["router_b"]), in_hbm, in_hbm, in_hbm],
        out_specs=act,
        scratch_shapes=[pltpu.VMEM(gate_shape, F32), pltpu.VMEM(gate_shape, F32), pltpu.VMEM(down_shape, F32),
                        pltpu.VMEM(gate_shape, BF16), pltpu.VMEM(gate_shape, BF16), pltpu.VMEM(down_shape, BF16),
                        pltpu.SemaphoreType.DMA((3,))],
    )
    return pl.pallas_call(
        functools.partial(_expert_kernel, layer=layer),
        grid_spec=grid_spec,
        out_shape=jax.ShapeDtypeStruct(xs.shape, F32),
        compiler_params=_cparams("arbitrary", vmem_limit_bytes=EXPERT_VMEM_LIMIT_BYTES),
        name="moe_experts",
    )(tile_ea, tile_eb, tile_rows, tile_next_group, n_valid, xs, lw["router_hi"], lw["router_b"],
      lw["w_gate"], lw["w_up"], lw["w_down"])


def _gather_rows_issue(slot_ref, os_ref, gbuf, sems, buf, tc):
    def issue(rb, carry):
        for q in range(ISSUE_UNROLL):
            r = rb * ISSUE_UNROLL + q
            s = slot_ref[0, 0, r]
            src = os_ref.at[pl.ds(pl.multiple_of(s * ROW_CHUNKS, ROW_CHUNKS), ROW_CHUNKS)]
            dst = gbuf.at[buf, pl.ds(pl.multiple_of(r * ROW_CHUNKS, ROW_CHUNKS), ROW_CHUNKS)]
            _rows_copy(src, dst, sems.at[buf]).start(priority=q % DMA_QUEUES)
        return carry

    lax.fori_loop(0, tc // ISSUE_UNROLL, issue, 0)


def _combine_kernel(slot_ref, slot_next_ref, x_ref, g_ref, bt_ref, os_ref, xp_ref, xd_ref,
                    gbuf, sems, *, tc, n_first):
    i = pl.program_id(0)
    n = pl.num_programs(0)
    sb = lax.rem(i, 2)

    @pl.when(i == 0)
    def _():
        _gather_rows_issue(slot_ref, os_ref, gbuf, sems, 0, tc)

    @pl.when(i + 1 < n)
    def _():
        _gather_rows_issue(slot_next_ref, os_ref, gbuf, sems, 1 - sb, tc)

    _rows_copy(os_ref.at[pl.ds(0, tc * ROW_CHUNKS)], gbuf.at[sb], sems.at[sb]).wait()
    y = jnp.concatenate([gbuf[sb, pl.ds(j, tc, stride=ROW_CHUNKS), :] for j in range(ROW_CHUNKS)], axis=1)
    x2 = _layer_norm(DEEPNORM_ALPHA * x_ref[...] + y, g_ref[...], bt_ref[...])

    @pl.when(i < n_first)
    def _():
        xp_ref[...] = x2

    @pl.when(i >= n_first)
    def _():
        xd_ref[...] = x2


def _combine(x, slots, os_, lw, *, n_prompt, tc=512):
    t = x.shape[0]
    nblk = t // tc
    n_first = n_prompt // tc
    slots3 = slots.reshape(nblk, 1, tc)
    kern = functools.partial(_combine_kernel, tc=tc, n_first=n_first)
    smem_tile = lambda f: pl.BlockSpec((1, 1, tc), f, memory_space=pltpu.SMEM)
    return pl.pallas_call(
        kern,
        grid=(nblk,),
        in_specs=[smem_tile(lambda i: (i, 0, 0)),
                  smem_tile(lambda i: (jnp.minimum(i + 1, nblk - 1), 0, 0)),
                  pl.BlockSpec((tc, D_MODEL), lambda i: (i, 0)),
                  _layer_resident(lw["ln2_g"], lw["layer"]), _layer_resident(lw["ln2_b"], lw["layer"]),
                  pl.BlockSpec(memory_space=pl.ANY)],
        out_specs=[pl.BlockSpec((tc, D_MODEL), lambda i: (jnp.minimum(i, n_first - 1), 0)),
                   pl.BlockSpec((tc, D_MODEL), lambda i: (jnp.maximum(i - n_first, 0), 0))],
        out_shape=[jax.ShapeDtypeStruct((n_prompt, D_MODEL), F32),
                   jax.ShapeDtypeStruct((t - n_prompt, D_MODEL), F32)],
        scratch_shapes=[pltpu.VMEM((2, tc * ROW_CHUNKS, LANES), F32), pltpu.SemaphoreType.DMA((2,))],
        compiler_params=_cparams("arbitrary"),
        name="moe_combine",
    )(slots3, slots3, x, lw["ln2_g"], lw["ln2_b"], os_)


def _moe(x, lw, *, n_prompt):
    t = x.shape[0]
    n_tiles = t // EXPERT_TILE + N_CLASSES + N_EXPERT_GROUPS * (TILES_PER_STEP - 1)
    n_tiles = -(-n_tiles // TILES_PER_STEP) * TILES_PER_STEP
    meta, cnt = _router(x, lw)
    cls = meta[:, 0, :].reshape(t).astype(jnp.int32)
    rank = meta[:, 1, :].reshape(t).astype(jnp.int32)
    counts = cnt[:N_CLASSES, 0].astype(jnp.int32)
    seg_tiles = (counts + EXPERT_TILE - 1) // EXPERT_TILE
    group_tiles = jnp.sum(seg_tiles.reshape(N_EXPERT_GROUPS, PAIRS_PER_GROUP), axis=1)
    group_pad = (-group_tiles) % TILES_PER_STEP
    last_of_group = jnp.arange(N_CLASSES) % PAIRS_PER_GROUP == PAIRS_PER_GROUP - 1
    seg_tiles = seg_tiles + jnp.where(last_of_group, jnp.repeat(group_pad, PAIRS_PER_GROUP), 0)
    tile_ends = jnp.cumsum(seg_tiles)
    seg_start = (tile_ends - seg_tiles) * EXPERT_TILE
    class_ids = jnp.arange(N_CLASSES, dtype=jnp.int32)
    slots = jnp.sum(jnp.where(cls[:, None] == class_ids, seg_start, 0), axis=-1) + rank
    n_valid = tile_ends[-1:]
    tile_ids = jnp.minimum(jnp.arange(n_tiles, dtype=jnp.int32), n_valid[0] - 1)
    tile_class = jnp.sum((tile_ends[None, :] <= tile_ids[:, None]).astype(jnp.int32), axis=1)
    of_class = tile_class[:, None] == class_ids
    class_experts = jnp.asarray(_CLASS_EXPERTS, dtype=jnp.int32)
    tile_ea = jnp.sum(jnp.where(of_class, class_experts[:, 0], 0), axis=-1)
    tile_eb = jnp.sum(jnp.where(of_class, class_experts[:, 1], 0), axis=-1)
    rows_left = jnp.sum(jnp.where(of_class, seg_start + counts, 0), axis=-1) - tile_ids * EXPERT_TILE
    tile_rows = jnp.clip(rows_left, 0, EXPERT_TILE)
    group_ids = jnp.arange(N_EXPERT_GROUPS, dtype=jnp.int32)
    later_used = (group_ids[None, :] > group_ids[:, None]) & ((group_tiles + group_pad) > 0)[None, :]
    next_group = jnp.min(jnp.where(later_used, group_ids[None, :], N_EXPERT_GROUPS), axis=1)
    next_group = jnp.where(next_group == N_EXPERT_GROUPS, -1, next_group)
    tile_group = tile_ea // EXPERTS_PER_GROUP
    tile_next_group = jnp.sum(jnp.where(tile_group[:, None] == group_ids, next_group, 0), axis=-1)
    xs = _dispatch(x, slots, n_tiles * EXPERT_TILE)
    os_ = _experts(xs, tile_ea, tile_eb, tile_rows, tile_next_group, n_valid.astype(jnp.int32), lw)
    return _combine(x, slots, os_, lw, n_prompt=n_prompt)


def _t5_bucket(dist):
    n = jnp.maximum(dist, 0)
    max_exact = REL_BUCKETS // 2
    nf = jnp.maximum(n, 1).astype(F32)
    large = max_exact + (jnp.log(nf / max_exact) / math.log(REL_MAX_DISTANCE / max_exact)
                         * (REL_BUCKETS - max_exact)).astype(jnp.int32)
    large = jnp.minimum(large, REL_BUCKETS - 1)
    return jnp.where(n < max_exact, n, large)


def _head_rows(per_head, rows_of):
    out = []
    for kvh in range(N_KV_HEADS):
        out.append(jnp.stack([jnp.concatenate([jnp.repeat(per_head[..., h, :, :], rep, axis=-2)
                                               for h, rep in rows_of(kvh, par)], axis=-2)
                              for par in range(2)], axis=-3))
    return jnp.stack(out, axis=-4)


def _prompt_rows(kvh, par):
    return [(GROUP * kvh + 2 * p + par, 1) for p in range(2)]


def _head_bias(rel_bias, dist, valid):
    bias = jnp.where(valid[..., None], rel_bias[_t5_bucket(dist)].astype(F32), NEG_INF)
    return jnp.moveaxis(bias, -1, 0)


def _prompt_bias(rel_bias):
    period = 4 * WINDOW
    off = jnp.arange(period)
    off = jnp.where(off < 2 * WINDOW, off, off - period)
    dist = WINDOW - off
    per_off = _head_bias(rel_bias, dist, (dist >= 0) & (dist <= WINDOW))
    tiled = jnp.tile(per_off, (1, WINDOW))[:, :WINDOW * (period - 1)]
    return tiled.reshape(N_HEADS, WINDOW, period - 1)[:, :, :2 * WINDOW]


def _attention_tables(rel_bias, sinks, ntok, g):
    bias_p = _head_rows(_prompt_bias(rel_bias), _prompt_rows)

    decode_rows = lambda kvh, par: [(GROUP * kvh + 2 * p + par, g) for p in range(2)]
    tq = jnp.arange(ntok)[:, None]
    kpos_old = jnp.arange(WINDOW)[None, :] - WINDOW
    kpos_new = jnp.arange(ntok)[None, :]
    tables = []
    for kpos in (kpos_old, kpos_new):
        d = tq - kpos
        per_head = _head_bias(rel_bias, d, (d >= 0) & (d <= WINDOW) & (PAST_LEN + kpos >= 0))
        tables.append(_head_rows(per_head, decode_rows))
    bias_d = tables[0]
    biasn_d = jnp.moveaxis(tables[1], -1, 2)[..., None]

    s = sinks.astype(F32)[:, :, None, None]
    sink_p = _head_rows(jnp.broadcast_to(s, s.shape[:2] + (WINDOW, 1)), _prompt_rows)
    sink_d = _head_rows(jnp.broadcast_to(s, s.shape[:2] + (ntok, 1)), decode_rows)
    return bias_p, bias_d, biasn_d, sink_p, sink_d


def _stacked_weights(w_in, pool_w, pool_scale, proj_a, proj_b, w_out, ln1_g, ln1_b, router_group_w,
                     router_group_b, router_expert_w, router_expert_b, expert_w_gate, expert_w_up,
                     expert_w_down, ln2_g, ln2_b):
    depth = w_in.shape[0]
    same_group = jnp.eye(N_POOL_GROUPS, dtype=bool)[None, :, None, :, None]
    pool_bd = jnp.where(same_group, pool_w[:, :, :, None, :], 0.0).reshape(depth, D_POOL, D_POOL)
    pad = ROUTER_LANES - N_EXPERTS - N_EXPERT_GROUPS
    rw = jnp.concatenate([router_expert_w, router_group_w, jnp.zeros((depth, D_MODEL, pad), F32)], axis=-1)
    rb = jnp.concatenate([router_expert_b, router_group_b, jnp.zeros((depth, pad), F32)], axis=-1)
    r_hi = rw.astype(BF16)
    row = lambda a: a[:, None, :]
    return dict(
        w_in=w_in.astype(BF16), pool_w=pool_bd.astype(BF16), pool_scale=row(pool_scale),
        proj_a=proj_a.astype(BF16), proj_b=proj_b.astype(BF16), w_out=w_out.astype(BF16),
        ln1_g=row(ln1_g), ln1_b=row(ln1_b),
        router_hi=r_hi, router_lo=(rw - r_hi.astype(F32)).astype(BF16), router_b=row(rb),
        w_gate=expert_w_gate, w_up=expert_w_up, w_down=expert_w_down,
        ln2_g=row(ln2_g), ln2_b=row(ln2_b))


def _decode_mixer(xs_tn, lw, cache_k, cache_v, hist_t, tables, buf, new_caches, *, nseq, ntok, g, row_block):
    bias_d, biasn_d, sink_d = tables
    u = _in_proj(xs_tn, lw)
    u3 = u.reshape(ntok, nseq, D_IN)
    b, new_k, new_v = _decode_attn(u3, cache_k, cache_v, bias_d, biasn_d, sink_d, new_caches, g=g,
                                   layer=lw["layer"])
    buf = _decode_tail(xs_tn, u, hist_t, b.reshape(ntok * nseq, Q_DIM), lw, buf, nseq=nseq, ntok=ntok,
                       row_block=row_block)
    return buf, u3, (new_k, new_v)


def kernel(x_prompt, x_sample, cache_k, cache_v, state_pool, rel_bias, w_in, pool_w, pool_scale, proj_a, proj_b, w_out, attn_sinks, ln1_g, ln1_b, router_group_w, router_group_b, router_expert_w, router_expert_b, expert_w_gate, expert_w_up, expert_w_down, ln2_g, ln2_b):
    bsz, seq, _ = x_prompt.shape
    nseq, ntok, _ = x_sample.shape
    g = 16
    depth = w_in.shape[0]
    n_prompt = bsz * seq
    n_decode = ntok * nseq
    assert n_prompt % n_decode == 0
    bias_p, bias_d, biasn_d, sink_p, sink_d = _attention_tables(rel_bias, attn_sinks, ntok, g)
    weights = _stacked_weights(w_in, pool_w, pool_scale, proj_a, proj_b, w_out, ln1_g, ln1_b, router_group_w,
                               router_group_b, router_expert_w, router_expert_b, expert_w_gate, expert_w_up,
                               expert_w_down, ln2_g, ln2_b)

    xp = x_prompt.reshape(n_prompt, D_MODEL)
    xs_tn = jnp.swapaxes(x_sample, 0, 1).reshape(n_decode, D_MODEL)
    hist_t = jnp.swapaxes(state_pool, 1, 2)
    ck_t = jnp.transpose(cache_k, (0, 1, 3, 4, 2))
    cv_t = jnp.transpose(cache_v, (0, 1, 3, 4, 2))
    prompt_state, decode_u, new_caches = [], [], None
    for l in range(depth):
        lw = dict(weights, layer=l)
        x1, kst, vst, pst = _prompt_mixer(xp, lw, bias_p, sink_p, bsz=bsz, n_total=n_prompt + n_decode)
        x1, u3, new_caches = _decode_mixer(xs_tn, lw, ck_t, cv_t, hist_t, (bias_d, biasn_d, sink_d), x1,
                                           new_caches, nseq=nseq, ntok=ntok, g=g,
                                           row_block=n_prompt // n_decode)
        xp, xs_tn = _moe(x1, lw, n_prompt=n_prompt)
        prompt_state.append((kst, vst, pst))
        decode_u.append(u3)

    pk, pv, pp = (jnp.stack(parts) for parts in zip(*prompt_state))
    heads = lambda a: a.reshape(a.shape[:3] + (N_KV_HEADS, HEAD_DIM))
    new_pool = jnp.stack([jnp.swapaxes(u3[:, :, :D_POOL], 0, 1) for u3 in decode_u])
    sample_pool = jnp.concatenate([state_pool[:, :, ntok:], new_pool], axis=2)
    sample_k, sample_v = (jnp.transpose(c, (0, 1, 4, 2, 3)) for c in new_caches)
    y_sample = jnp.swapaxes(xs_tn.reshape(ntok, nseq, D_MODEL), 0, 1)
    return (xp.reshape(bsz, seq, D_MODEL), y_sample, heads(pk), heads(pv), pp[:, :, -POOL_HIST:],
            sample_k, sample_v, sample_pool)
```

```python
import functools
import math

import jax
import jax.numpy as jnp
from jax import lax
from jax.experimental import pallas as pl
from jax.experimental.pallas import tpu as pltpu

D_MODEL = 1024
DEPTH = 2
PAST_LEN = 16384
D_POOL = D_MODEL // 2
POOL_WINDOWS = (2, 4, 8, 16)
N_POOL_GROUPS = len(POOL_WINDOWS)
POOL_GROUP_DIM = D_POOL // N_POOL_GROUPS
POOL_HIST = max(POOL_WINDOWS) - 1
assert all(w & (w - 1) == 0 for w in POOL_WINDOWS)
HEAD_DIM = 64
N_HEADS = (D_MODEL // 2) // HEAD_DIM
N_KV_HEADS = 2
GROUP = N_HEADS // N_KV_HEADS
Q_DIM = N_HEADS * HEAD_DIM
KV_DIM = N_KV_HEADS * HEAD_DIM
WINDOW = 128
ATTN_SCALE = HEAD_DIM ** -0.5
assert math.log2(ATTN_SCALE).is_integer()
REL_BUCKETS = 32
REL_MAX_DISTANCE = 128
D_IN = D_POOL + Q_DIM + 2 * KV_DIM + 2 * D_MODEL
N_EXPERT_GROUPS = 4
EXPERTS_PER_GROUP = 8
N_EXPERTS = N_EXPERT_GROUPS * EXPERTS_PER_GROUP
D_EXPERT = D_MODEL // 4
DEEPNORM_ALPHA = (2 * DEPTH) ** 0.25
LN_EPS = 1e-5

LANES = 128
SUBLANES = 8
VMEM_LIMIT_BYTES = 56 * 1024 * 1024
EXPERT_VMEM_LIMIT_BYTES = 62 * 1024 * 1024

_O_Q = D_POOL
_O_K = _O_Q + Q_DIM
_O_V = _O_K + KV_DIM
_O_GA = _O_V + KV_DIM
_O_GB = _O_GA + D_MODEL

ROW_CHUNKS = D_MODEL // LANES
KEY_PAD = 2 * WINDOW
EXPERT_TILE = 192
OUT_PROJ_CHUNKS = 2
EXPERT_ROW_STEP = 32
TILES_PER_STEP = 2
DMA_QUEUES = 2
ISSUE_UNROLL = 8
PAIRS_PER_GROUP = EXPERTS_PER_GROUP * (EXPERTS_PER_GROUP - 1) // 2
N_CLASSES = N_EXPERT_GROUPS * PAIRS_PER_GROUP
_CLASS_EXPERTS = tuple((g * EXPERTS_PER_GROUP + a, g * EXPERTS_PER_GROUP + b)
                       for g in range(N_EXPERT_GROUPS)
                       for a in range(EXPERTS_PER_GROUP) for b in range(a + 1, EXPERTS_PER_GROUP))
ROUTER_LANES = LANES
_GROUP_LANE0 = N_EXPERTS
assert EXPERTS_PER_GROUP == SUBLANES and N_EXPERT_GROUPS <= SUBLANES

BF16 = jnp.bfloat16
F32 = jnp.float32
NEG_INF = float("-inf")


def _cparams(*sem, flags=None, vmem_limit_bytes=VMEM_LIMIT_BYTES):
    return pltpu.CompilerParams(dimension_semantics=sem, vmem_limit_bytes=vmem_limit_bytes, flags=flags)


def _resident(shape):
    nd = len(shape)
    return pl.BlockSpec(shape, lambda *_: (0,) * nd, pipeline_mode=pl.Buffered(1))


def _layer_resident(stacked, layer):
    nd = stacked.ndim - 1
    return pl.BlockSpec((None,) + stacked.shape[1:], lambda *_: (layer,) + (0,) * nd,
                        pipeline_mode=pl.Buffered(1))


def _layer_norm(y, g, b):
    mu = jnp.mean(y, axis=-1, keepdims=True)
    yc = y - mu
    var = jnp.mean(yc * yc, axis=-1, keepdims=True)
    return yc * lax.rsqrt(var + LN_EPS) * g + b


def _softmax_pv(s, sink, v_bf16):
    m = jnp.maximum(jnp.max(s, axis=-1, keepdims=True), sink)
    p = jnp.exp(s - m)
    den = jnp.sum(p, axis=-1, keepdims=True) + jnp.exp(sink - m)
    o = jnp.dot(p.astype(BF16), v_bf16, preferred_element_type=F32)
    return o / den


def _half_lane_variants(t):
    lane = lax.broadcasted_iota(jnp.int32, t.shape, 1)
    lo = lane < HEAD_DIM
    tr = pltpu.roll(t, HEAD_DIM, 1)
    z = jnp.zeros_like(t)
    return ((jnp.where(lo, t, z), jnp.where(lo, z, tr)),
            (jnp.where(lo, tr, z), jnp.where(lo, z, t)))


def _mix_tail(x, a_in, b_in, ga, gb, pw, ps, pa, pb, wo, g, bt):
    a = jnp.dot(a_in.astype(BF16), pw, preferred_element_type=F32) * ps
    pa_o = jnp.dot(a.astype(BF16), pa, preferred_element_type=F32)
    pb_o = jnp.dot(b_in.astype(BF16), pb, preferred_element_type=F32)
    sigmoid = lambda z: 0.5 + 0.5 * jnp.tanh(0.5 * z)
    merged = (sigmoid(ga) * pa_o + sigmoid(gb) * pb_o).astype(BF16)
    rows = x.shape[0]
    chunk = rows // OUT_PROJ_CHUNKS
    out = []
    for r0 in range(0, rows, chunk):
        mix = jnp.dot(merged[r0:r0 + chunk], wo, preferred_element_type=F32)
        out.append(_layer_norm(DEEPNORM_ALPHA * x[r0:r0 + chunk] + mix, g, bt))
    return jnp.concatenate(out, axis=0)


def _prompt_mixer_kernel(x_ref, win_ref, pw_ref, ps_ref, pa_ref, pb_ref, wo_ref, bias_ref, sink_ref,
                         g_ref, bt_ref, x1_ref, kst_ref, vst_ref, pst_ref, kt_scr, v_scr, ext_scr, *, tm):
    i = pl.program_id(1)
    hist = 2 * SUBLANES

    @pl.when(i == 0)
    def _():
        kt_scr[:, :, :WINDOW] = jnp.zeros((4, LANES, WINDOW), BF16)
        v_scr[:, :WINDOW, :] = jnp.zeros((4, WINDOW, LANES), BF16)
        ext_scr[:hist, :] = jnp.zeros((hist, D_POOL), F32)

    @pl.when(i > 0)
    def _():
        kt_scr[:, :, :WINDOW] = kt_scr[:, :, tm:tm + WINDOW]
        v_scr[:, :WINDOW, :] = v_scr[:, tm:tm + WINDOW, :]
        ext_scr[:hist, :] = ext_scr[tm:tm + hist, :]

    x = x_ref[...]
    u = jnp.dot(x.astype(BF16), win_ref[...], preferred_element_type=F32)
    u_pool = u[:, :D_POOL]
    q = u[:, _O_Q:_O_K]
    k = u[:, _O_K:_O_V]
    v = u[:, _O_V:_O_GA]
    ga = u[:, _O_GA:_O_GB]
    gb = u[:, _O_GB:]

    kst_ref[...] = k[tm - WINDOW:, :]
    vst_ref[...] = v[tm - WINDOW:, :]
    pst_ref[...] = u_pool[tm - hist:, :]

    kt = k.T
    kz = jnp.zeros((HEAD_DIM, tm), F32)
    kt_scr[0, :, WINDOW:] = jnp.concatenate([kt[:HEAD_DIM], kz], axis=0).astype(BF16)
    kt_scr[1, :, WINDOW:] = jnp.concatenate([kz, kt[:HEAD_DIM]], axis=0).astype(BF16)
    kt_scr[2, :, WINDOW:] = jnp.concatenate([kt[HEAD_DIM:], kz], axis=0).astype(BF16)
    kt_scr[3, :, WINDOW:] = jnp.concatenate([kz, kt[HEAD_DIM:]], axis=0).astype(BF16)
    vv = _half_lane_variants(v)
    for kvh in range(N_KV_HEADS):
        for par in range(2):
            v_scr[2 * kvh + par, WINDOW:, :] = vv[kvh][par].astype(BF16)

    col = lax.broadcasted_iota(jnp.int32, (2 * WINDOW, 2 * WINDOW), 1)
    no_prev = col < jnp.where(i == 0, WINDOW, 0)
    qb = (q * ATTN_SCALE).astype(BF16)
    b_rows = []
    for j in range(tm // WINDOW):
        r0 = j * WINDOW
        o_kvh = []
        for kvh in range(N_KV_HEADS):
            c0 = kvh * 2 * LANES
            q2 = jnp.concatenate([qb[r0:r0 + WINDOW, c0:c0 + LANES],
                                  qb[r0:r0 + WINDOW, c0 + LANES:c0 + 2 * LANES]], axis=0)
            acc = None
            for par in range(2):
                s = jnp.dot(q2, kt_scr[2 * kvh + par, :, r0:r0 + 2 * WINDOW], preferred_element_type=F32)
                s = s + bias_ref[kvh, par]
                if j == 0:
                    s = jnp.where(no_prev, NEG_INF, s)
                o = _softmax_pv(s, sink_ref[kvh, par], v_scr[2 * kvh + par, r0:r0 + 2 * WINDOW, :])
                acc = o if acc is None else acc + o
            o_kvh.append(acc)
        b_rows.append(jnp.concatenate([o_kvh[0][:WINDOW], o_kvh[0][WINDOW:],
                                       o_kvh[1][:WINDOW], o_kvh[1][WINDOW:]], axis=1))
    b = jnp.concatenate(b_rows, axis=0)

    ext_scr[hist:, :] = u_pool
    pos = i * tm + lax.broadcasted_iota(jnp.int32, (tm, 1), 0)
    means = []
    for gi, w in enumerate(POOL_WINDOWS):
        c0 = gi * POOL_GROUP_DIM
        acc = ext_scr[:, c0:c0 + POOL_GROUP_DIM]
        d = 1
        while d < w:
            acc = acc + pltpu.roll(acc, d, 0)
            d *= 2
        inv_cnt = 1.0 / jnp.minimum(w, pos + 1).astype(F32)
        means.append(acc[hist:] * inv_cnt)
    pooled = jnp.concatenate(means, axis=1) - u_pool

    x1_ref[...] = _mix_tail(x, pooled, b, ga, gb, pw_ref[...], ps_ref[...], pa_ref[...], pb_ref[...],
                            wo_ref[...], g_ref[...], bt_ref[...])


def _prompt_mixer(x, lw, bias_p, sink_t, *, bsz, n_total, tm=512):
    seq = x.shape[0] // bsz
    nt = seq // tm
    hist = 2 * SUBLANES
    kern = functools.partial(_prompt_mixer_kernel, tm=tm)
    tile = pl.BlockSpec((tm, D_MODEL), lambda b, i: (b * nt + i, 0))
    state = lambda r, w: pl.BlockSpec((None, r, w), lambda b, i: (b, 0, 0))
    params = (lw["w_in"], lw["pool_w"], lw["pool_scale"], lw["proj_a"], lw["proj_b"], lw["w_out"])
    tail = (sink_t, lw["ln1_g"], lw["ln1_b"])
    return pl.pallas_call(
        kern,
        grid=(bsz, nt),
        in_specs=([tile] + [_layer_resident(a, lw["layer"]) for a in params] + [_resident(bias_p.shape)]
                  + [_layer_resident(a, lw["layer"]) for a in tail]),
        out_specs=[tile, state(WINDOW, KV_DIM), state(WINDOW, KV_DIM), state(hist, D_POOL)],
        out_shape=[jax.ShapeDtypeStruct((n_total, D_MODEL), F32),
                   jax.ShapeDtypeStruct((bsz, WINDOW, KV_DIM), F32),
                   jax.ShapeDtypeStruct((bsz, WINDOW, KV_DIM), F32),
                   jax.ShapeDtypeStruct((bsz, hist, D_POOL), F32)],
        scratch_shapes=[pltpu.VMEM((4, LANES, WINDOW + tm), BF16),
                        pltpu.VMEM((4, WINDOW + tm, LANES), BF16),
                        pltpu.VMEM((hist + tm, D_POOL), F32)],
        compiler_params=_cparams("arbitrary", "arbitrary"),
        name="prompt_mixer",
    )(x, lw["w_in"], lw["pool_w"], lw["pool_scale"], lw["proj_a"], lw["proj_b"], lw["w_out"],
      bias_p, sink_t, lw["ln1_g"], lw["ln1_b"])


def _in_proj_kernel(x_ref, win_ref, u_ref):
    u_ref[...] = jnp.dot(x_ref[...].astype(BF16), win_ref[...], preferred_element_type=F32)


def _in_proj(x, lw):
    n = x.shape[0]
    return pl.pallas_call(
        _in_proj_kernel,
        grid=(1,),
        in_specs=[pl.BlockSpec(x.shape, lambda i: (0, 0)), _layer_resident(lw["w_in"], lw["layer"])],
        out_specs=pl.BlockSpec((n, D_IN), lambda i: (0, 0)),
        out_shape=jax.ShapeDtypeStruct((n, D_IN), F32),
        compiler_params=_cparams("arbitrary"),
        name="decode_in_proj",
    )(x, lw["w_in"])


def _decode_attn_kernel(q_ref, kn_ref, vn_ref, kc_ref, vc_ref, bias_ref, biasn_ref, sink_ref, *rest, g, ntok):
    b_ref, sk_ref, sv_ref, new_scr = rest[-4:]
    slabs = 2 * ntok
    rows = slabs * g

    lane = lax.broadcasted_iota(jnp.int32, (KV_DIM, WINDOW), 1)
    fill = jnp.zeros((LANES - g * ntok, KV_DIM), F32)
    for new_ref, old_ref, out_ref in ((kn_ref, kc_ref, sk_ref), (vn_ref, vc_ref, sv_ref)):
        for t in range(ntok):
            new_scr[pl.ds(t, g, stride=ntok), :] = new_ref[t]
        packed = jnp.concatenate([new_scr[...], fill], axis=0).T
        for n in range(g):
            tile = pltpu.roll(old_ref[n].reshape(KV_DIM, WINDOW), WINDOW - ntok, 1)
            cols = pltpu.roll(packed, (WINDOW - ntok - n * ntok) % LANES, 1)
            out_ref[n] = jnp.where(lane >= WINDOW - ntok, cols, tile).reshape(N_KV_HEADS, HEAD_DIM, WINDOW)

    def cache_variants(ref):
        z = jnp.zeros((HEAD_DIM, WINDOW), F32)
        out = []
        for kvh in range(N_KV_HEADS):
            out.append((jnp.concatenate([jnp.concatenate([ref[n, kvh], z], axis=0) for n in range(g)], axis=1),
                        jnp.concatenate([jnp.concatenate([z, ref[n, kvh]], axis=0) for n in range(g)], axis=1)))
        return out

    kk = cache_variants(kc_ref)
    vv = cache_variants(vc_ref)
    kn = [_half_lane_variants(kn_ref[t]) for t in range(ntok)]
    vn = [_half_lane_variants(vn_ref[t]) for t in range(ntok)]
    per_row = lambda a: jnp.concatenate([a] * slabs, axis=0)
    row_seq = lax.rem(lax.broadcasted_iota(jnp.int32, (rows, WINDOW), 0), g)
    diag = (lax.rem(lax.broadcasted_iota(jnp.int32, (rows, g * WINDOW), 0), g)
            == lax.broadcasted_iota(jnp.int32, (rows, g * WINDOW), 1) // WINDOW)
    for kvh in range(N_KV_HEADS):
        c0 = kvh * 2 * LANES
        q2f = jnp.concatenate([q_ref[t, :, c0 + p * LANES:c0 + (p + 1) * LANES]
                               for p in range(2) for t in range(ntok)], axis=0)
        q2 = q2f.astype(BF16)
        acc = None
        for par in range(2):
            full = jnp.dot(q2, kk[kvh][par].astype(BF16), preferred_element_type=F32)
            s = jnp.zeros((rows, WINDOW), F32)
            for n in range(g):
                s = jnp.where(row_seq == n, full[:, n * WINDOW:(n + 1) * WINDOW], s)
            s = s * ATTN_SCALE + bias_ref[kvh, par]
            s_new = [jnp.sum(q2f * per_row(kn[t][kvh][par]), axis=-1, keepdims=True) * ATTN_SCALE
                     + biasn_ref[kvh, par, t] for t in range(ntok)]
            sink = sink_ref[kvh, par]
            m = jnp.maximum(jnp.max(s, axis=-1, keepdims=True), sink)
            for t in range(ntok):
                m = jnp.maximum(m, s_new[t])
            p = jnp.exp(s - m)
            den = jnp.sum(p, axis=-1, keepdims=True) + jnp.exp(sink - m)
            p_bd = jnp.where(diag, jnp.concatenate([p] * g, axis=1), 0.0).astype(BF16)
            o = lax.dot_general(p_bd, vv[kvh][par].astype(BF16), (((1,), (1,)), ((), ())),
                                preferred_element_type=F32)
            for t in range(ntok):
                p_new = jnp.exp(s_new[t] - m)
                den = den + p_new
                o = o + p_new * per_row(vn[t][kvh][par])
            o = o / den
            acc = o if acc is None else acc + o
        for p in range(2):
            for t in range(ntok):
                r0 = (p * ntok + t) * g
                b_ref[t, :, c0 + p * LANES:c0 + (p + 1) * LANES] = acc[r0:r0 + g]


def _decode_attn(u3, cache_k, cache_v, bias_d, biasn_d, sink_d, new_caches, *, g, layer):
    ntok, nseq, _ = u3.shape
    cache = pl.BlockSpec((None, g, N_KV_HEADS, HEAD_DIM, WINDOW), lambda i: (layer, i, 0, 0, 0))
    kern = functools.partial(_decode_attn_kernel, g=g, ntok=ntok)
    new_cols = lambda w, off: pl.BlockSpec((ntok, g, w), lambda i: (0, i, off // w))
    ins = [u3, u3, u3, cache_k, cache_v, bias_d, biasn_d, sink_d]
    in_specs = [new_cols(Q_DIM, _O_Q), new_cols(KV_DIM, _O_K), new_cols(KV_DIM, _O_V), cache, cache,
                _resident(bias_d.shape), _resident(biasn_d.shape), _layer_resident(sink_d, layer)]
    aliases = {}
    if new_caches is not None:
        aliases = {len(ins): 1, len(ins) + 1: 2}
        ins += list(new_caches)
        in_specs += [pl.BlockSpec(memory_space=pl.ANY)] * 2
    return pl.pallas_call(
        kern,
        grid=(nseq // g,),
        in_specs=in_specs,
        out_specs=[pl.BlockSpec((ntok, g, Q_DIM), lambda i: (0, i, 0)), cache, cache],
        out_shape=[jax.ShapeDtypeStruct((ntok, nseq, Q_DIM), F32),
                   jax.ShapeDtypeStruct(cache_k.shape, F32), jax.ShapeDtypeStruct(cache_v.shape, F32)],
        input_output_aliases=aliases,
        scratch_shapes=[pltpu.VMEM((g * ntok, KV_DIM), F32)],
        compiler_params=_cparams("arbitrary"),
        name="decode_attn",
    )(*ins)


def _decode_tail_kernel(x_ref, u_ref, hist_ref, b_ref, pw_ref, ps_ref, pa_ref, pb_ref,
                        wo_ref, g_ref, bt_ref, buf_ref, x1_ref, *, nseq, ntok, start_pos):
    del buf_ref

    def ext_row(r, c0):
        if r < POOL_HIST:
            return hist_ref[r, :, c0:c0 + POOL_GROUP_DIM]
        return u_ref[(r - POOL_HIST) * nseq:(r - POOL_HIST + 1) * nseq, c0:c0 + POOL_GROUP_DIM]

    pooled_rows = []
    for t in range(ntok):
        means = []
        for gi, w in enumerate(POOL_WINDOWS):
            acc = None
            for d in range(w):
                r = POOL_HIST + t - d
                if start_pos - POOL_HIST + r < 0:
                    continue
                term = ext_row(r, gi * POOL_GROUP_DIM)
                acc = term if acc is None else acc + term
            means.append(acc / float(min(w, start_pos + t + 1)))
        pooled_rows.append(jnp.concatenate(means, axis=1))
    pooled = jnp.concatenate(pooled_rows, axis=0) - u_ref[:, :D_POOL]
    x1_ref[...] = _mix_tail(x_ref[...], pooled, b_ref[...], u_ref[:, _O_GA:_O_GB], u_ref[:, _O_GB:], pw_ref[...],
                            ps_ref[...], pa_ref[...], pb_ref[...], wo_ref[...], g_ref[...], bt_ref[...])


def _decode_tail(x, u, hist_t, b, lw, buf, *, nseq, ntok, row_block):
    n = ntok * nseq
    kern = functools.partial(_decode_tail_kernel, nseq=nseq, ntok=ntok, start_pos=PAST_LEN)
    whole = lambda a: pl.BlockSpec(a.shape, lambda i: (0,) * a.ndim)
    layered = lambda a: _layer_resident(a, lw["layer"])
    params = (lw["pool_w"], lw["pool_scale"], lw["proj_a"], lw["proj_b"], lw["w_out"], lw["ln1_g"], lw["ln1_b"])
    ins = (x, u, hist_t, b) + params
    return pl.pallas_call(
        kern,
        grid=(1,),
        in_specs=([whole(x), whole(u), layered(hist_t), whole(b)] + [layered(a) for a in params]
                  + [pl.BlockSpec(memory_space=pl.ANY)]),
        out_specs=pl.BlockSpec((n, D_MODEL), lambda i: (row_block, 0)),
        out_shape=jax.ShapeDtypeStruct(buf.shape, F32),
        input_output_aliases={len(ins): 0},
        compiler_params=_cparams("arbitrary"),
        name="decode_tail",
    )(*ins, buf)


def _router_logits(x, whi, wlo, br):
    xh = x.astype(BF16)
    xl = (x - xh.astype(F32)).astype(BF16)
    return (jnp.dot(xh, whi, preferred_element_type=F32)
            + (jnp.dot(xh, wlo, preferred_element_type=F32)
               + jnp.dot(xl, whi, preferred_element_type=F32))) + br


def _group_logits(logits, lane):
    is_group = jnp.logical_and(lane >= _GROUP_LANE0, lane < _GROUP_LANE0 + N_EXPERT_GROUPS)
    return jnp.where(is_group, logits, NEG_INF)


def _router_kernel(x_ref, whi_ref, wlo_ref, br_ref, meta_ref, cnt_ref, run_scr, *, tr):
    i = pl.program_id(0)

    @pl.when(i == 0)
    def _():
        run_scr[...] = jnp.zeros_like(run_scr)

    lt = _router_logits(x_ref[...], whi_ref[...], wlo_ref[...], br_ref[...]).T
    row = lax.broadcasted_iota(jnp.int32, (EXPERTS_PER_GROUP, tr), 0)
    none = jnp.int32(EXPERTS_PER_GROUP)

    def argmax_rows(vals):
        vmax = jnp.max(vals, axis=0, keepdims=True)
        return jnp.min(jnp.where(vals == vmax, row, none), axis=0, keepdims=True)

    glog = jnp.where(row < N_EXPERT_GROUPS, lt[_GROUP_LANE0:_GROUP_LANE0 + EXPERTS_PER_GROUP], NEG_INF)
    gidx = argmax_rows(glog)
    el = lt[:EXPERTS_PER_GROUP]
    for g in range(1, N_EXPERT_GROUPS):
        el = jnp.where(gidx == g, lt[g * EXPERTS_PER_GROUP:(g + 1) * EXPERTS_PER_GROUP], el)
    e1 = argmax_rows(el)
    e2 = argmax_rows(jnp.where(row == e1, NEG_INF, el))
    lo = jnp.minimum(e1, e2)
    hi = jnp.maximum(e1, e2)
    pair = lax.shift_right_logical(lo * (2 * EXPERTS_PER_GROUP - 1 - lo), 1) + (hi - lo - 1)
    cls = gidx * PAIRS_PER_GROUP + pair

    oh = jnp.where(lax.broadcasted_iota(jnp.int32, (ROUTER_LANES, tr), 0) == cls, 1.0, 0.0)
    rr = lax.broadcasted_iota(jnp.int32, (tr, tr), 0)
    cc = lax.broadcasted_iota(jnp.int32, (tr, tr), 1)
    earlier = jnp.where(rr < cc, 1.0, 0.0).astype(BF16)
    before = jnp.dot(oh.astype(BF16), earlier, preferred_element_type=F32) + run_scr[...]
    rank = jnp.sum(oh * before, axis=0, keepdims=True)
    run_scr[...] += jnp.sum(oh, axis=1, keepdims=True)
    cnt_ref[...] = run_scr[...]

    meta_ref[...] = jnp.where(row == 0, cls.astype(F32), jnp.where(row == 1, rank, 0.0))


def _router(x, lw, *, tr=512):
    t = x.shape[0]
    kern = functools.partial(_router_kernel, tr=tr)
    return pl.pallas_call(
        kern,
        grid=(t // tr,),
        in_specs=[pl.BlockSpec((tr, D_MODEL), lambda i: (i, 0)),
                  _layer_resident(lw["router_hi"], lw["layer"]), _layer_resident(lw["router_lo"], lw["layer"]),
                  _layer_resident(lw["router_b"], lw["layer"])],
        out_specs=[pl.BlockSpec((None, SUBLANES, tr), lambda i: (i, 0, 0)),
                   pl.BlockSpec((ROUTER_LANES, 1), lambda i: (0, 0))],
        out_shape=[jax.ShapeDtypeStruct((t // tr, SUBLANES, tr), F32),
                   jax.ShapeDtypeStruct((ROUTER_LANES, 1), F32)],
        scratch_shapes=[pltpu.VMEM((ROUTER_LANES, 1), F32)],
        compiler_params=_cparams("arbitrary"),
        name="router",
    )(x, lw["router_hi"], lw["router_lo"], lw["router_b"])


def _rows_copy(src, dst, sem):
    return pltpu.make_async_copy(src, dst, sem)


def _dispatch_kernel(slot_ref, x_ref, xs_ref, stage, sems, *, td):
    i = pl.program_id(0)
    n = pl.num_programs(0)
    sb = lax.rem(i, 2)
    x = x_ref[...]
    for j in range(ROW_CHUNKS):
        stage[sb, pl.ds(j, td, stride=ROW_CHUNKS), :] = x[:, j * LANES:(j + 1) * LANES]

    def issue(rb, carry):
        for q in range(ISSUE_UNROLL):
            r = rb * ISSUE_UNROLL + q
            src = stage.at[sb, pl.ds(pl.multiple_of(r * ROW_CHUNKS, ROW_CHUNKS), ROW_CHUNKS)]
            s = slot_ref[0, 0, r]
            dst = xs_ref.at[pl.ds(pl.multiple_of(s * ROW_CHUNKS, ROW_CHUNKS), ROW_CHUNKS)]
            _rows_copy(src, dst, sems.at[sb]).start(priority=q % DMA_QUEUES)
        return carry

    lax.fori_loop(0, td // ISSUE_UNROLL, issue, 0)

    def drain(buf):
        _rows_copy(stage.at[buf], xs_ref.at[pl.ds(0, td * ROW_CHUNKS)], sems.at[buf]).wait()

    @pl.when(i > 0)
    def _():
        drain(1 - sb)

    @pl.when(i == n - 1)
    def _():
        drain(sb)


def _dispatch(x, slots, n_slots, *, td=512):
    t = x.shape[0]
    slots3 = slots.reshape(t // td, 1, td)
    kern = functools.partial(_dispatch_kernel, td=td)
    return pl.pallas_call(
        kern,
        grid=(t // td,),
        in_specs=[pl.BlockSpec((1, 1, td), lambda i: (i, 0, 0), memory_space=pltpu.SMEM),
                  pl.BlockSpec((td, D_MODEL), lambda i: (i, 0))],
        out_specs=pl.BlockSpec(memory_space=pl.ANY),
        out_shape=jax.ShapeDtypeStruct((n_slots * ROW_CHUNKS, LANES), F32),
        scratch_shapes=[pltpu.VMEM((2, td * ROW_CHUNKS, LANES), F32), pltpu.SemaphoreType.DMA((2,))],
        compiler_params=_cparams("arbitrary"),
        name="moe_dispatch",
    )(slots3, x)


def _expert_kernel(ea_ref, eb_ref, rows_ref, next_ref, nv_ref, xs_ref, rhi_ref, rb_ref, wg_hbm, wu_hbm, wd_hbm,
                   os_ref, wg_f32, wu_f32, wd_f32, wg_bf, wu_bf, wd_bf, sems, *, layer):
    i = pl.program_id(0)
    first = i * TILES_PER_STEP
    valid = first < nv_ref[0]
    group = ea_ref[first] // EXPERTS_PER_GROUP
    prev_group = ea_ref[jnp.maximum(first - 1, 0)] // EXPERTS_PER_GROUP
    new_group = jnp.logical_or(i == 0, group != prev_group)

    def weight_copies(grp):
        experts = pl.ds(grp * EXPERTS_PER_GROUP, EXPERTS_PER_GROUP)
        return [pltpu.make_async_copy(src.at[layer, experts], dst, sems.at[k])
                for k, (src, dst) in enumerate(((wg_hbm, wg_f32), (wu_hbm, wu_f32), (wd_hbm, wd_f32)))]

    @pl.when(jnp.logical_and(valid, i == 0))
    def _():
        for c in weight_copies(group):
            c.start()

    @pl.when(jnp.logical_and(valid, new_group))
    def _():
        for c in weight_copies(group):
            c.wait()
        for e in range(EXPERTS_PER_GROUP):
            wg_bf[e] = wg_f32[e].astype(BF16)
            wu_bf[e] = wu_f32[e].astype(BF16)
            wd_bf[e] = wd_f32[e].astype(BF16)
        nxt = next_ref[first]

        @pl.when(nxt >= 0)
        def _():
            for c in weight_copies(nxt):
                c.start()

    def body(row0, ea, eb, m):
        x = jnp.concatenate([xs_ref[pl.ds(row0 + j, m, stride=ROW_CHUNKS), :] for j in range(ROW_CHUNKS)],
                            axis=1)
        xt = x.astype(BF16)
        logits = jnp.dot(xt, rhi_ref[...], preferred_element_type=F32) + rb_ref[...]
        lane = lax.broadcasted_iota(jnp.int32, logits.shape, 1)
        pick = lambda l: jnp.sum(jnp.where(lane == l, logits, 0.0), axis=-1, keepdims=True)
        glog = _group_logits(logits, lane)
        gmax = jnp.max(glog, axis=-1, keepdims=True)
        gw = jnp.exp(pick(_GROUP_LANE0 + group) - gmax) / jnp.sum(jnp.exp(glog - gmax), axis=-1, keepdims=True)
        la = pick(ea)
        lb = pick(eb)
        lmax = jnp.maximum(la, lb)
        pa = jnp.exp(la - lmax)
        pb = jnp.exp(lb - lmax)
        scale = gw / (pa + pb)

        y = None
        for e, w in ((ea, pa * scale), (eb, pb * scale)):
            el = e - group * EXPERTS_PER_GROUP
            gate = jnp.dot(xt, wg_bf[el], preferred_element_type=F32)
            up = jnp.dot(xt, wu_bf[el], preferred_element_type=F32)
            h = (gate * (0.5 + 0.5 * jnp.tanh(0.5 * gate)) * up).astype(BF16)
            term = w * jnp.dot(h, wd_bf[el], preferred_element_type=F32)
            y = term if y is None else y + term
        for j in range(ROW_CHUNKS):
            os_ref[pl.ds(row0 + j, m, stride=ROW_CHUNKS), :] = y[:, j * LANES:(j + 1) * LANES]

    for s in range(TILES_PER_STEP):
        tile = first + s
        row_steps = (rows_ref[tile] + EXPERT_ROW_STEP - 1) // EXPERT_ROW_STEP
        for k in range(1, EXPERT_TILE // EXPERT_ROW_STEP + 1):
            pl.when(jnp.logical_and(valid, row_steps == k))(
                functools.partial(body, s * EXPERT_TILE * ROW_CHUNKS, ea_ref[tile], eb_ref[tile],
                                  k * EXPERT_ROW_STEP))


def _experts(xs, tile_ea, tile_eb, tile_rows, tile_next_group, n_valid, lw):
    layer = lw["layer"]
    n_steps = xs.shape[0] // (TILES_PER_STEP * EXPERT_TILE * ROW_CHUNKS)
    rows = TILES_PER_STEP * EXPERT_TILE * ROW_CHUNKS
    act = pl.BlockSpec((rows, LANES),
                       lambda i, ea, eb, nr, nx, nv: (jnp.minimum(i, nv[0] // TILES_PER_STEP - 1), 0))
    router = lambda a: pl.BlockSpec((None,) + a.shape[1:], lambda i, ea, eb, nr, nx, nv: (layer, 0, 0),
                                    pipeline_mode=pl.Buffered(1))
    in_hbm = pl.BlockSpec(memory_space=pl.ANY)
    gate_shape = (EXPERTS_PER_GROUP, D_MODEL, D_EXPERT)
    down_shape = (EXPERTS_PER_GROUP, D_EXPERT, D_MODEL)
    grid_spec = pltpu.PrefetchScalarGridSpec(
        num_scalar_prefetch=5,
        grid=(n_steps,),
        in_specs=[act, router(lw["router_hi"]), router(lw["router_b"]), in_hbm, in_hbm, in_hbm],
        out_specs=act,
        scratch_shapes=[pltpu.VMEM(gate_shape, F32), pltpu.VMEM(gate_shape, F32), pltpu.VMEM(down_shape, F32),
                        pltpu.VMEM(gate_shape, BF16), pltpu.VMEM(gate_shape, BF16), pltpu.VMEM(down_shape, BF16),
                        pltpu.SemaphoreType.DMA((3,))],
    )
    return pl.pallas_call(
        functools.partial(_expert_kernel, layer=layer),
        grid_spec=grid_spec,
        out_shape=jax.ShapeDtypeStruct(xs.shape, F32),
        compiler_params=_cparams("arbitrary", vmem_limit_bytes=EXPERT_VMEM_LIMIT_BYTES),
        name="moe_experts",
    )(tile_ea, tile_eb, tile_rows, tile_next_group, n_valid, xs, lw["router_hi"], lw["router_b"],
      lw["w_gate"], lw["w_up"], lw["w_down"])


def _gather_rows_issue(slot_ref, os_ref, gbuf, sems, buf, tc):
    def issue(rb, carry):
        for q in range(ISSUE_UNROLL):
            r = rb * ISSUE_UNROLL + q
            s = slot_ref[0, 0, r]
            src = os_ref.at[pl.ds(pl.multiple_of(s * ROW_CHUNKS, ROW_CHUNKS), ROW_CHUNKS)]
            dst = gbuf.at[buf, pl.ds(pl.multiple_of(r * ROW_CHUNKS, ROW_CHUNKS), ROW_CHUNKS)]
            _rows_copy(src, dst, sems.at[buf]).start(priority=q % DMA_QUEUES)
        return carry

    lax.fori_loop(0, tc // ISSUE_UNROLL, issue, 0)


def _combine_kernel(slot_ref, slot_next_ref, x_ref, g_ref, bt_ref, os_ref, xp_ref, xd_ref,
                    gbuf, sems, *, tc, n_first):
    i = pl.program_id(0)
    n = pl.num_programs(0)
    sb = lax.rem(i, 2)

    @pl.when(i == 0)
    def _():
        _gather_rows_issue(slot_ref, os_ref, gbuf, sems, 0, tc)

    @pl.when(i + 1 < n)
    def _():
        _gather_rows_issue(slot_next_ref, os_ref, gbuf, sems, 1 - sb, tc)

    _rows_copy(os_ref.at[pl.ds(0, tc * ROW_CHUNKS)], gbuf.at[sb], sems.at[sb]).wait()
    y = jnp.concatenate([gbuf[sb, pl.ds(j, tc, stride=ROW_CHUNKS), :] for j in range(ROW_CHUNKS)], axis=1)
    x2 = _layer_norm(DEEPNORM_ALPHA * x_ref[...] + y, g_ref[...], bt_ref[...])

    @pl.when(i < n_first)
    def _():
        xp_ref[...] = x2

    @pl.when(i >= n_first)
    def _():
        xd_ref[...] = x2


def _combine(x, slots, os_, lw, *, n_prompt, tc=512):
    t = x.shape[0]
    nblk = t // tc
    n_first = n_prompt // tc
    slots3 = slots.reshape(nblk, 1, tc)
    kern = functools.partial(_combine_kernel, tc=tc, n_first=n_first)
    smem_tile = lambda f: pl.BlockSpec((1, 1, tc), f, memory_space=pltpu.SMEM)
    return pl.pallas_call(
        kern,
        grid=(nblk,),
        in_specs=[smem_tile(lambda i: (i, 0, 0)),
                  smem_tile(lambda i: (jnp.minimum(i + 1, nblk - 1), 0, 0)),
                  pl.BlockSpec((tc, D_MODEL), lambda i: (i, 0)),
                  _layer_resident(lw["ln2_g"], lw["layer"]), _layer_resident(lw["ln2_b"], lw["layer"]),
                  pl.BlockSpec(memory_space=pl.ANY)],
        out_specs=[pl.BlockSpec((tc, D_MODEL), lambda i: (jnp.minimum(i, n_first - 1), 0)),
                   pl.BlockSpec((tc, D_MODEL), lambda i: (jnp.maximum(i - n_first, 0), 0))],
        out_shape=[jax.ShapeDtypeStruct((n_prompt, D_MODEL), F32),
                   jax.ShapeDtypeStruct((t - n_prompt, D_MODEL), F32)],
        scratch_shapes=[pltpu.VMEM((2, tc * ROW_CHUNKS, LANES), F32), pltpu.SemaphoreType.DMA((2,))],
        compiler_params=_cparams("arbitrary"),
        name="moe_combine",
    )(slots3, slots3, x, lw["ln2_g"], lw["ln2_b"], os_)


def _moe(x, lw, *, n_prompt):
    t = x.shape[0]
    n_tiles = t // EXPERT_TILE + N_CLASSES + N_EXPERT_GROUPS * (TILES_PER_STEP - 1)
    n_tiles = -(-n_tiles // TILES_PER_STEP) * TILES_PER_STEP
    meta, cnt = _router(x, lw)
    cls = meta[:, 0, :].reshape(t).astype(jnp.int32)
    rank = meta[:, 1, :].reshape(t).astype(jnp.int32)
    counts = cnt[:N_CLASSES, 0].astype(jnp.int32)
    seg_tiles = (counts + EXPERT_TILE - 1) // EXPERT_TILE
    group_tiles = jnp.sum(seg_tiles.reshape(N_EXPERT_GROUPS, PAIRS_PER_GROUP), axis=1)
    group_pad = (-group_tiles) % TILES_PER_STEP
    last_of_group = jnp.arange(N_CLASSES) % PAIRS_PER_GROUP == PAIRS_PER_GROUP - 1
    seg_tiles = seg_tiles + jnp.where(last_of_group, jnp.repeat(group_pad, PAIRS_PER_GROUP), 0)
    tile_ends = jnp.cumsum(seg_tiles)
    seg_start = (tile_ends - seg_tiles) * EXPERT_TILE
    class_ids = jnp.arange(N_CLASSES, dtype=jnp.int32)
    slots = jnp.sum(jnp.where(cls[:, None] == class_ids, seg_start, 0), axis=-1) + rank
    n_valid = tile_ends[-1:]
    tile_ids = jnp.minimum(jnp.arange(n_tiles, dtype=jnp.int32), n_valid[0] - 1)
    tile_class = jnp.sum((tile_ends[None, :] <= tile_ids[:, None]).astype(jnp.int32), axis=1)
    of_class = tile_class[:, None] == class_ids
    class_experts = jnp.asarray(_CLASS_EXPERTS, dtype=jnp.int32)
    tile_ea = jnp.sum(jnp.where(of_class, class_experts[:, 0], 0), axis=-1)
    tile_eb = jnp.sum(jnp.where(of_class, class_experts[:, 1], 0), axis=-1)
    rows_left = jnp.sum(jnp.where(of_class, seg_start + counts, 0), axis=-1) - tile_ids * EXPERT_TILE
    tile_rows = jnp.clip(rows_left, 0, EXPERT_TILE)
    group_ids = jnp.arange(N_EXPERT_GROUPS, dtype=jnp.int32)
    later_used = (group_ids[None, :] > group_ids[:, None]) & ((group_tiles + group_pad) > 0)[None, :]
    next_group = jnp.min(jnp.where(later_used, group_ids[None, :], N_EXPERT_GROUPS), axis=1)
    next_group = jnp.where(next_group == N_EXPERT_GROUPS, -1, next_group)
    tile_group = tile_ea // EXPERTS_PER_GROUP
    tile_next_group = jnp.sum(jnp.where(tile_group[:, None] == group_ids, next_group, 0), axis=-1)
    xs = _dispatch(x, slots, n_tiles * EXPERT_TILE)
    os_ = _experts(xs, tile_ea, tile_eb, tile_rows, tile_next_group, n_valid.astype(jnp.int32), lw)
    return _combine(x, slots, os_, lw, n_prompt=n_prompt)


def _t5_bucket(dist):
    n = jnp.maximum(dist, 0)
    max_exact = REL_BUCKETS // 2
    nf = jnp.maximum(n, 1).astype(F32)
    large = max_exact + (jnp.log(nf / max_exact) / math.log(REL_MAX_DISTANCE / max_exact)
                         * (REL_BUCKETS - max_exact)).astype(jnp.int32)
    large = jnp.minimum(large, REL_BUCKETS - 1)
    return jnp.where(n < max_exact, n, large)


def _head_rows(per_head, rows_of):
    out = []
    for kvh in range(N_KV_HEADS):
        out.append(jnp.stack([jnp.concatenate([jnp.repeat(per_head[..., h, :, :], rep, axis=-2)
                                               for h, rep in rows_of(kvh, par)], axis=-2)
                              for par in range(2)], axis=-3))
    return jnp.stack(out, axis=-4)


def _prompt_rows(kvh, par):
    return [(GROUP * kvh + 2 * p + par, 1) for p in range(2)]


def _head_bias(rel_bias, dist, valid):
    bias = jnp.where(valid[..., None], rel_bias[_t5_bucket(dist)].astype(F32), NEG_INF)
    return jnp.moveaxis(bias, -1, 0)


def _prompt_bias(rel_bias):
    period = 4 * WINDOW
    off = jnp.arange(period)
    off = jnp.where(off < 2 * WINDOW, off, off - period)
    dist = WINDOW - off
    per_off = _head_bias(rel_bias, dist, (dist >= 0) & (dist <= WINDOW))
    tiled = jnp.tile(per_off, (1, WINDOW))[:, :WINDOW * (period - 1)]
    return tiled.reshape(N_HEADS, WINDOW, period - 1)[:, :, :2 * WINDOW]


def _attention_tables(rel_bias, sinks, ntok, g):
    bias_p = _head_rows(_prompt_bias(rel_bias), _prompt_rows)

    decode_rows = lambda kvh, par: [(GROUP * kvh + 2 * p + par, g) for p in range(2)]
    tq = jnp.arange(ntok)[:, None]
    kpos_old = jnp.arange(WINDOW)[None, :] - WINDOW
    kpos_new = jnp.arange(ntok)[None, :]
    tables = []
    for kpos in (kpos_old, kpos_new):
        d = tq - kpos
        per_head = _head_bias(rel_bias, d, (d >= 0) & (d <= WINDOW) & (PAST_LEN + kpos >= 0))
        tables.append(_head_rows(per_head, decode_rows))
    bias_d = tables[0]
    biasn_d = jnp.moveaxis(tables[1], -1, 2)[..., None]

    s = sinks.astype(F32)[:, :, None, None]
    sink_p = _head_rows(jnp.broadcast_to(s, s.shape[:2] + (WINDOW, 1)), _prompt_rows)
    sink_d = _head_rows(jnp.broadcast_to(s, s.shape[:2] + (ntok, 1)), decode_rows)
    return bias_p, bias_d, biasn_d, sink_p, sink_d


def _stacked_weights(w_in, pool_w, pool_scale, proj_a, proj_b, w_out, ln1_g, ln1_b, router_group_w,
                     router_group_b, router_expert_w, router_expert_b, expert_w_gate, expert_w_up,
                     expert_w_down, ln2_g, ln2_b):
    depth = w_in.shape[0]
    same_group = jnp.eye(N_POOL_GROUPS, dtype=bool)[None, :, None, :, None]
    pool_bd = jnp.where(same_group, pool_w[:, :, :, None, :], 0.0).reshape(depth, D_POOL, D_POOL)
    pad = ROUTER_LANES - N_EXPERTS - N_EXPERT_GROUPS
    rw = jnp.concatenate([router_expert_w, router_group_w, jnp.zeros((depth, D_MODEL, pad), F32)], axis=-1)
    rb = jnp.concatenate([router_expert_b, router_group_b, jnp.zeros((depth, pad), F32)], axis=-1)
    r_hi = rw.astype(BF16)
    row = lambda a: a[:, None, :]
    return dict(
        w_in=w_in.astype(BF16), pool_w=pool_bd.astype(BF16), pool_scale=row(pool_scale),
        proj_a=proj_a.astype(BF16), proj_b=proj_b.astype(BF16), w_out=w_out.astype(BF16),
        ln1_g=row(ln1_g), ln1_b=row(ln1_b),
        router_hi=r_hi, router_lo=(rw - r_hi.astype(F32)).astype(BF16), router_b=row(rb),
        w_gate=expert_w_gate, w_up=expert_w_up, w_down=expert_w_down,
        ln2_g=row(ln2_g), ln2_b=row(ln2_b))


def _decode_mixer(xs_tn, lw, cache_k, cache_v, hist_t, tables, buf, new_caches, *, nseq, ntok, g, row_block):
    bias_d, biasn_d, sink_d = tables
    u = _in_proj(xs_tn, lw)
    u3 = u.reshape(ntok, nseq, D_IN)
    b, new_k, new_v = _decode_attn(u3, cache_k, cache_v, bias_d, biasn_d, sink_d, new_caches, g=g,
                                   layer=lw["layer"])
    buf = _decode_tail(xs_tn, u, hist_t, b.reshape(ntok * nseq, Q_DIM), lw, buf, nseq=nseq, ntok=ntok,
                       row_block=row_block)
    return buf, u3, (new_k, new_v)


def kernel(x_prompt, x_sample, cache_k, cache_v, state_pool, rel_bias, w_in, pool_w, pool_scale, proj_a, proj_b, w_out, attn_sinks, ln1_g, ln1_b, router_group_w, router_group_b, router_expert_w, router_expert_b, expert_w_gate, expert_w_up, expert_w_down, ln2_g, ln2_b):
    bsz, seq, _ = x_prompt.shape
    nseq, ntok, _ = x_sample.shape
    g = 16
    depth = w_in.shape[0]
    n_prompt = bsz * seq
    n_decode = ntok * nseq
    assert n_prompt % n_decode == 0
    bias_p, bias_d, biasn_d, sink_p, sink_d = _attention_tables(rel_bias, attn_sinks, ntok, g)
    weights = _stacked_weights(w_in, pool_w, pool_scale, proj_a, proj_b, w_out, ln1_g, ln1_b, router_group_w,
                               router_group_b, router_expert_w, router_expert_b, expert_w_gate, expert_w_up,
                               expert_w_down, ln2_g, ln2_b)

    xp = x_prompt.reshape(n_prompt, D_MODEL)
    xs_tn = jnp.swapaxes(x_sample, 0, 1).reshape(n_decode, D_MODEL)
    hist_t = jnp.swapaxes(state_pool, 1, 2)
    ck_t = jnp.transpose(cache_k, (0, 1, 3, 4, 2))
    cv_t = jnp.transpose(cache_v, (0, 1, 3, 4, 2))
    prompt_state, decode_u, new_caches = [], [], None
    for l in range(depth):
        lw = dict(weights, layer=l)
        x1, kst, vst, pst = _prompt_mixer(xp, lw, bias_p, sink_p, bsz=bsz, n_total=n_prompt + n_decode)
        x1, u3, new_caches = _decode_mixer(xs_tn, lw, ck_t, cv_t, hist_t, (bias_d, biasn_d, sink_d), x1,
                                           new_caches, nseq=nseq, ntok=ntok, g=g,
                                           row_block=n_prompt // n_decode)
        xp, xs_tn = _moe(x1, lw, n_prompt=n_prompt)
        prompt_state.append((kst, vst, pst))
        decode_u.append(u3)

    pk, pv, pp = (jnp.stack(parts) for parts in zip(*prompt_state))
    heads = lambda a: a.reshape(a.shape[:3] + (N_KV_HEADS, HEAD_DIM))
    new_pool = jnp.stack([jnp.swapaxes(u3[:, :, :D_POOL], 0, 1) for u3 in decode_u])
    sample_pool = jnp.concatenate([state_pool[:, :, ntok:], new_pool], axis=2)
    sample_k, sample_v = (jnp.transpose(c, (0, 1, 4, 2, 3)) for c in new_caches)
    y_sample = jnp.swapaxes(xs_tn.reshape(ntok, nseq, D_MODEL), 0, 1)
    return (xp.reshape(bsz, seq, D_MODEL), y_sample, heads(pk), heads(pv), pp[:, :, -POOL_HIST:],
            sample_k, sample_v, sample_pool)
```

```python
import functools
import math

import jax
import jax.numpy as jnp
from jax import lax
from jax.experimental import pallas as pl
from jax.experimental.pallas import tpu as pltpu

D_MODEL = 1024
DEPTH = 2
PAST_LEN = 16384
D_POOL = D_MODEL // 2
POOL_WINDOWS = (2, 4, 8, 16)
N_POOL_GROUPS = len(POOL_WINDOWS)
POOL_GROUP_DIM = D_POOL // N_POOL_GROUPS
POOL_HIST = max(POOL_WINDOWS) - 1
assert all(w & (w - 1) == 0 for w in POOL_WINDOWS)
HEAD_DIM = 64
N_HEADS = (D_MODEL // 2) // HEAD_DIM
N_KV_HEADS = 2
GROUP = N_HEADS // N_KV_HEADS
Q_DIM = N_HEADS * HEAD_DIM
KV_DIM = N_KV_HEADS * HEAD_DIM
WINDOW = 128
ATTN_SCALE = HEAD_DIM ** -0.5
assert math.log2(ATTN_SCALE).is_integer()
REL_BUCKETS = 32
REL_MAX_DISTANCE = 128
D_IN = D_POOL + Q_DIM + 2 * KV_DIM + 2 * D_MODEL
N_EXPERT_GROUPS = 4
EXPERTS_PER_GROUP = 8
N_EXPERTS = N_EXPERT_GROUPS * EXPERTS_PER_GROUP
D_EXPERT = D_MODEL // 4
DEEPNORM_ALPHA = (2 * DEPTH) ** 0.25
LN_EPS = 1e-5

LANES = 128
SUBLANES = 8
VMEM_LIMIT_BYTES = 56 * 1024 * 1024
EXPERT_VMEM_LIMIT_BYTES = 62 * 1024 * 1024

_O_Q = D_POOL
_O_K = _O_Q + Q_DIM
_O_V = _O_K + KV_DIM
_O_GA = _O_V + KV_DIM
_O_GB = _O_GA + D_MODEL

KEY_PAD = 2 * WINDOW
EXPERT_TILE = 192
OUT_PROJ_CHUNKS = 2
EXPERT_ROW_STEP = 32
TILES_PER_STEP = 2
DMA_QUEUES = 2
ISSUE_UNROLL = 8
PAIRS_PER_GROUP = EXPERTS_PER_GROUP * (EXPERTS_PER_GROUP - 1) // 2
N_CLASSES = N_EXPERT_GROUPS * PAIRS_PER_GROUP
_CLASS_EXPERTS = tuple((g * EXPERTS_PER_GROUP + a, g * EXPERTS_PER_GROUP + b)
                       for g in range(N_EXPERT_GROUPS)
                       for a in range(EXPERTS_PER_GROUP) for b in range(a + 1, EXPERTS_PER_GROUP))
ROUTER_LANES = LANES
_GROUP_LANE0 = N_EXPERTS
assert EXPERTS_PER_GROUP == SUBLANES and N_EXPERT_GROUPS <= SUBLANES

BF16 = jnp.bfloat16
F32 = jnp.float32
NEG_INF = float("-inf")


def _cparams(*sem, flags=None, vmem_limit_bytes=VMEM_LIMIT_BYTES):
    return pltpu.CompilerParams(dimension_semantics=sem, vmem_limit_bytes=vmem_limit_bytes, flags=flags)


def _resident(shape):
    nd = len(shape)
    return pl.BlockSpec(shape, lambda *_: (0,) * nd, pipeline_mode=pl.Buffered(1))


def _layer_resident(stacked, layer):
    nd = stacked.ndim - 1
    return pl.BlockSpec((None,) + stacked.shape[1:], lambda *_: (layer,) + (0,) * nd,
                        pipeline_mode=pl.Buffered(1))


def _layer_norm(y, g, b):
    mu = jnp.mean(y, axis=-1, keepdims=True)
    yc = y - mu
    var = jnp.mean(yc * yc, axis=-1, keepdims=True)
    return yc * lax.rsqrt(var + LN_EPS) * g + b


def _softmax_pv(s, sink, v_bf16):
    m = jnp.maximum(jnp.max(s, axis=-1, keepdims=True), sink)
    p = jnp.exp(s - m)
    den = jnp.sum(p, axis=-1, keepdims=True) + jnp.exp(sink - m)
    o = jnp.dot(p.astype(BF16), v_bf16, preferred_element_type=F32)
    return o * (1.0 / den)


def _half_lane_variants(t):
    lane = lax.broadcasted_iota(jnp.int32, t.shape, 1)
    lo = lane < HEAD_DIM
    tr = pltpu.roll(t, HEAD_DIM, 1)
    z = jnp.zeros_like(t)
    return ((jnp.where(lo, t, z), jnp.where(lo, z, tr)),
            (jnp.where(lo, tr, z), jnp.where(lo, z, t)))


def _mix_tail(x, a_in, b_in, ga, gb, pw, ps, pa, pb, wo, g, bt):
    a = jnp.dot(a_in.astype(BF16), pw, preferred_element_type=F32) * ps
    pa_o = jnp.dot(a.astype(BF16), pa, preferred_element_type=F32)
    pb_o = jnp.dot(b_in.astype(BF16), pb, preferred_element_type=F32)
    sigmoid = lambda z: 0.5 + 0.5 * jnp.tanh(0.5 * z)
    merged = (sigmoid(ga) * pa_o + sigmoid(gb) * pb_o).astype(BF16)
    rows = x.shape[0]
    chunk = rows // OUT_PROJ_CHUNKS
    out = []
    for r0 in range(0, rows, chunk):
        mix = jnp.dot(merged[r0:r0 + chunk], wo, preferred_element_type=F32)
        out.append(_layer_norm(DEEPNORM_ALPHA * x[r0:r0 + chunk] + mix, g, bt))
    return jnp.concatenate(out, axis=0)


def _prompt_mixer_kernel(x_ref, win_ref, pw_ref, ps_ref, pa_ref, pb_ref, wo_ref, bias_ref, sink_ref,
                         g_ref, bt_ref, x1_ref, kst_ref, vst_ref, pst_ref, kt_scr, v_scr, ext_scr, *, tm):
    i = pl.program_id(1)
    hist = 2 * SUBLANES

    @pl.when(i == 0)
    def _():
        kt_scr[:, :, :WINDOW] = jnp.zeros((4, LANES, WINDOW), BF16)
        v_scr[:, :WINDOW, :] = jnp.zeros((4, WINDOW, LANES), BF16)
        ext_scr[:hist, :] = jnp.zeros((hist, D_POOL), F32)

    @pl.when(i > 0)
    def _():
        kt_scr[:, :, :WINDOW] = kt_scr[:, :, tm:tm + WINDOW]
        v_scr[:, :WINDOW, :] = v_scr[:, tm:tm + WINDOW, :]
        ext_scr[:hist, :] = ext_scr[tm:tm + hist, :]

    x = x_ref[...]
    u = jnp.dot(x.astype(BF16), win_ref[...], preferred_element_type=F32)
    u_pool = u[:, :D_POOL]
    q = u[:, _O_Q:_O_K]
    k = u[:, _O_K:_O_V]
    v = u[:, _O_V:_O_GA]
    ga = u[:, _O_GA:_O_GB]
    gb = u[:, _O_GB:]

    kst_ref[...] = k[tm - WINDOW:, :]
    vst_ref[...] = v[tm - WINDOW:, :]
    pst_ref[...] = u_pool[tm - hist:, :]

    kt = k.T
    kz = jnp.zeros((HEAD_DIM, tm), F32)
    kt_scr[0, :, WINDOW:] = jnp.concatenate([kt[:HEAD_DIM], kz], axis=0).astype(BF16)
    kt_scr[1, :, WINDOW:] = jnp.concatenate([kz, kt[:HEAD_DIM]], axis=0).astype(BF16)
    kt_scr[2, :, WINDOW:] = jnp.concatenate([kt[HEAD_DIM:], kz], axis=0).astype(BF16)
    kt_scr[3, :, WINDOW:] = jnp.concatenate([kz, kt[HEAD_DIM:]], axis=0).astype(BF16)
    vv = _half_lane_variants(v)
    for kvh in range(N_KV_HEADS):
        for par in range(2):
            v_scr[2 * kvh + par, WINDOW:, :] = vv[kvh][par].astype(BF16)

    col = lax.broadcasted_iota(jnp.int32, (2 * WINDOW, 2 * WINDOW), 1)
    no_prev = col < jnp.where(i == 0, WINDOW, 0)
    qb = (q * ATTN_SCALE).astype(BF16)
    b_rows = []
    for j in range(tm // WINDOW):
        r0 = j * WINDOW
        o_kvh = []
        for kvh in range(N_KV_HEADS):
            c0 = kvh * 2 * LANES
            q2 = jnp.concatenate([qb[r0:r0 + WINDOW, c0:c0 + LANES],
                                  qb[r0:r0 + WINDOW, c0 + LANES:c0 + 2 * LANES]], axis=0)
            acc = None
            for par in range(2):
                s = jnp.dot(q2, kt_scr[2 * kvh + par, :, r0:r0 + 2 * WINDOW], preferred_element_type=F32)
                s = s + bias_ref[kvh, par]
                if j == 0:
                    s = jnp.where(no_prev, NEG_INF, s)
                o = _softmax_pv(s, sink_ref[kvh, par], v_scr[2 * kvh + par, r0:r0 + 2 * WINDOW, :])
                acc = o if acc is None else acc + o
            o_kvh.append(acc)
        b_rows.append(jnp.concatenate([o_kvh[0][:WINDOW], o_kvh[0][WINDOW:],
                                       o_kvh[1][:WINDOW], o_kvh[1][WINDOW:]], axis=1))
    b = jnp.concatenate(b_rows, axis=0)

    ext_scr[hist:, :] = u_pool
    pos = i * tm + lax.broadcasted_iota(jnp.int32, (tm, 1), 0)
    means = []
    for gi, w in enumerate(POOL_WINDOWS):
        c0 = gi * POOL_GROUP_DIM
        acc = ext_scr[:, c0:c0 + POOL_GROUP_DIM]
        d = 1
        while d < w:
            acc = acc + pltpu.roll(acc, d, 0)
            d *= 2
        inv_cnt = 1.0 / jnp.minimum(w, pos + 1).astype(F32)
        means.append(acc[hist:] * inv_cnt)
    pooled = jnp.concatenate(means, axis=1) - u_pool

    x1_ref[...] = _mix_tail(x, pooled, b, ga, gb, pw_ref[...], ps_ref[...], pa_ref[...], pb_ref[...],
                            wo_ref[...], g_ref[...], bt_ref[...])


def _prompt_mixer(x, lw, bias_p, sink_t, *, bsz, n_total, tm=512):
    seq = x.shape[0] // bsz
    nt = seq // tm
    hist = 2 * SUBLANES
    kern = functools.partial(_prompt_mixer_kernel, tm=tm)
    tile = pl.BlockSpec((tm, D_MODEL), lambda b, i: (b * nt + i, 0))
    state = lambda r, w: pl.BlockSpec((None, r, w), lambda b, i: (b, 0, 0))
    params = (lw["w_in"], lw["pool_w"], lw["pool_scale"], lw["proj_a"], lw["proj_b"], lw["w_out"])
    tail = (sink_t, lw["ln1_g"], lw["ln1_b"])
    return pl.pallas_call(
        kern,
        grid=(bsz, nt),
        in_specs=([tile] + [_layer_resident(a, lw["layer"]) for a in params] + [_resident(bias_p.shape)]
                  + [_layer_resident(a, lw["layer"]) for a in tail]),
        out_specs=[tile, state(WINDOW, KV_DIM), state(WINDOW, KV_DIM), state(hist, D_POOL)],
        out_shape=[jax.ShapeDtypeStruct((n_total, D_MODEL), F32),
                   jax.ShapeDtypeStruct((bsz, WINDOW, KV_DIM), F32),
                   jax.ShapeDtypeStruct((bsz, WINDOW, KV_DIM), F32),
                   jax.ShapeDtypeStruct((bsz, hist, D_POOL), F32)],
        scratch_shapes=[pltpu.VMEM((4, LANES, WINDOW + tm), BF16),
                        pltpu.VMEM((4, WINDOW + tm, LANES), BF16),
                        pltpu.VMEM((hist + tm, D_POOL), F32)],
        compiler_params=_cparams("arbitrary", "arbitrary"),
        name="prompt_mixer",
    )(x, lw["w_in"], lw["pool_w"], lw["pool_scale"], lw["proj_a"], lw["proj_b"], lw["w_out"],
      bias_p, sink_t, lw["ln1_g"], lw["ln1_b"])


def _in_proj_kernel(x_ref, win_ref, u_ref):
    u_ref[...] = jnp.dot(x_ref[...].astype(BF16), win_ref[...], preferred_element_type=F32)


def _in_proj(x, lw):
    n = x.shape[0]
    return pl.pallas_call(
        _in_proj_kernel,
        grid=(1,),
        in_specs=[pl.BlockSpec(x.shape, lambda i: (0, 0)), _layer_resident(lw["w_in"], lw["layer"])],
        out_specs=pl.BlockSpec((n, D_IN), lambda i: (0, 0)),
        out_shape=jax.ShapeDtypeStruct((n, D_IN), F32),
        compiler_params=_cparams("arbitrary"),
        name="decode_in_proj",
    )(x, lw["w_in"])


def _decode_attn_kernel(q_ref, kn_ref, vn_ref, kc_ref, vc_ref, bias_ref, biasn_ref, sink_ref, *rest, g, ntok):
    b_ref, sk_ref, sv_ref, new_scr = rest[-4:]
    slabs = 2 * ntok
    rows = slabs * g

    lane = lax.broadcasted_iota(jnp.int32, (KV_DIM, WINDOW), 1)
    fill = jnp.zeros((LANES - g * ntok, KV_DIM), F32)
    for new_ref, old_ref, out_ref in ((kn_ref, kc_ref, sk_ref), (vn_ref, vc_ref, sv_ref)):
        for t in range(ntok):
            new_scr[pl.ds(t, g, stride=ntok), :] = new_ref[t]
        packed = jnp.concatenate([new_scr[...], fill], axis=0).T
        for n in range(g):
            tile = pltpu.roll(old_ref[n].reshape(KV_DIM, WINDOW), WINDOW - ntok, 1)
            cols = pltpu.roll(packed, (WINDOW - ntok - n * ntok) % LANES, 1)
            out_ref[n] = jnp.where(lane >= WINDOW - ntok, cols, tile).reshape(N_KV_HEADS, HEAD_DIM, WINDOW)

    def cache_variants(ref):
        z = jnp.zeros((HEAD_DIM, WINDOW), F32)
        out = []
        for kvh in range(N_KV_HEADS):
            out.append((jnp.concatenate([jnp.concatenate([ref[n, kvh], z], axis=0) for n in range(g)], axis=1),
                        jnp.concatenate([jnp.concatenate([z, ref[n, kvh]], axis=0) for n in range(g)], axis=1)))
        return out

    kk = cache_variants(kc_ref)
    vv = cache_variants(vc_ref)
    kn = [_half_lane_variants(kn_ref[t]) for t in range(ntok)]
    vn = [_half_lane_variants(vn_ref[t]) for t in range(ntok)]
    per_row = lambda a: jnp.concatenate([a] * slabs, axis=0)
    row_seq = lax.rem(lax.broadcasted_iota(jnp.int32, (rows, WINDOW), 0), g)
    diag = (lax.rem(lax.broadcasted_iota(jnp.int32, (rows, g * WINDOW), 0), g)
            == lax.broadcasted_iota(jnp.int32, (rows, g * WINDOW), 1) // WINDOW)
    for kvh in range(N_KV_HEADS):
        c0 = kvh * 2 * LANES
        q2f = jnp.concatenate([q_ref[t, :, c0 + p * LANES:c0 + (p + 1) * LANES]
                               for p in range(2) for t in range(ntok)], axis=0)
        q2 = q2f.astype(BF16)
        acc = None
        for par in range(2):
            full = jnp.dot(q2, kk[kvh][par].astype(BF16), preferred_element_type=F32)
            s = jnp.zeros((rows, WINDOW), F32)
            for n in range(g):
                s = jnp.where(row_seq == n, full[:, n * WINDOW:(n + 1) * WINDOW], s)
            s = s * ATTN_SCALE + bias_ref[kvh, par]
            s_new = [jnp.sum(q2f * per_row(kn[t][kvh][par]), axis=-1, keepdims=True) * ATTN_SCALE
                     + biasn_ref[kvh, par, t] for t in range(ntok)]
            sink = sink_ref[kvh, par]
            m = jnp.maximum(jnp.max(s, axis=-1, keepdims=True), sink)
            for t in range(ntok):
                m = jnp.maximum(m, s_new[t])
            p = jnp.exp(s - m)
            den = jnp.sum(p, axis=-1, keepdims=True) + jnp.exp(sink - m)
            p_bd = jnp.where(diag, jnp.concatenate([p] * g, axis=1), 0.0).astype(BF16)
            o = lax.dot_general(p_bd, vv[kvh][par].astype(BF16), (((1,), (1,)), ((), ())),
                                preferred_element_type=F32)
            for t in range(ntok):
                p_new = jnp.exp(s_new[t] - m)
                den = den + p_new
                o = o + p_new * per_row(vn[t][kvh][par])
            o = o / den
            acc = o if acc is None else acc + o
        for p in range(2):
            for t in range(ntok):
                r0 = (p * ntok + t) * g
                b_ref[t, :, c0 + p * LANES:c0 + (p + 1) * LANES] = acc[r0:r0 + g]


def _decode_attn(u3, cache_k, cache_v, bias_d, biasn_d, sink_d, new_caches, *, g, layer):
    ntok, nseq, _ = u3.shape
    cache = pl.BlockSpec((None, g, N_KV_HEADS, HEAD_DIM, WINDOW), lambda i: (layer, i, 0, 0, 0))
    kern = functools.partial(_decode_attn_kernel, g=g, ntok=ntok)
    new_cols = lambda w, off: pl.BlockSpec((ntok, g, w), lambda i: (0, i, off // w))
    ins = [u3, u3, u3, cache_k, cache_v, bias_d, biasn_d, sink_d]
    in_specs = [new_cols(Q_DIM, _O_Q), new_cols(KV_DIM, _O_K), new_cols(KV_DIM, _O_V), cache, cache,
                _resident(bias_d.shape), _resident(biasn_d.shape), _layer_resident(sink_d, layer)]
    aliases = {}
    if new_caches is not None:
        aliases = {len(ins): 1, len(ins) + 1: 2}
        ins += list(new_caches)
        in_specs += [pl.BlockSpec(memory_space=pl.ANY)] * 2
    return pl.pallas_call(
        kern,
        grid=(nseq // g,),
        in_specs=in_specs,
        out_specs=[pl.BlockSpec((ntok, g, Q_DIM), lambda i: (0, i, 0)), cache, cache],
        out_shape=[jax.ShapeDtypeStruct((ntok, nseq, Q_DIM), F32),
                   jax.ShapeDtypeStruct(cache_k.shape, F32), jax.ShapeDtypeStruct(cache_v.shape, F32)],
        input_output_aliases=aliases,
        scratch_shapes=[pltpu.VMEM((g * ntok, KV_DIM), F32)],
        compiler_params=_cparams("arbitrary"),
        name="decode_attn",
    )(*ins)


def _decode_tail_kernel(x_ref, u_ref, hist_ref, b_ref, pw_ref, ps_ref, pa_ref, pb_ref,
                        wo_ref, g_ref, bt_ref, buf_ref, x1_ref, *, nseq, ntok, start_pos):
    del buf_ref

    def ext_row(r, c0):
        if r < POOL_HIST:
            return hist_ref[r, :, c0:c0 + POOL_GROUP_DIM]
        return u_ref[(r - POOL_HIST) * nseq:(r - POOL_HIST + 1) * nseq, c0:c0 + POOL_GROUP_DIM]

    pooled_rows = []
    for t in range(ntok):
        means = []
        for gi, w in enumerate(POOL_WINDOWS):
            acc = None
            for d in range(w):
                r = POOL_HIST + t - d
                if start_pos - POOL_HIST + r < 0:
                    continue
                term = ext_row(r, gi * POOL_GROUP_DIM)
                acc = term if acc is None else acc + term
            means.append(acc / float(min(w, start_pos + t + 1)))
        pooled_rows.append(jnp.concatenate(means, axis=1))
    pooled = jnp.concatenate(pooled_rows, axis=0) - u_ref[:, :D_POOL]
    x1_ref[...] = _mix_tail(x_ref[...], pooled, b_ref[...], u_ref[:, _O_GA:_O_GB], u_ref[:, _O_GB:], pw_ref[...],
                            ps_ref[...], pa_ref[...], pb_ref[...], wo_ref[...], g_ref[...], bt_ref[...])


def _decode_tail(x, u, hist_t, b, lw, buf, *, nseq, ntok, row_block):
    n = ntok * nseq
    kern = functools.partial(_decode_tail_kernel, nseq=nseq, ntok=ntok, start_pos=PAST_LEN)
    whole = lambda a: pl.BlockSpec(a.shape, lambda i: (0,) * a.ndim)
    layered = lambda a: _layer_resident(a, lw["layer"])
    params = (lw["pool_w"], lw["pool_scale"], lw["proj_a"], lw["proj_b"], lw["w_out"], lw["ln1_g"], lw["ln1_b"])
    ins = (x, u, hist_t, b) + params
    return pl.pallas_call(
        kern,
        grid=(1,),
        in_specs=([whole(x), whole(u), layered(hist_t), whole(b)] + [layered(a) for a in params]
                  + [pl.BlockSpec(memory_space=pl.ANY)]),
        out_specs=pl.BlockSpec((n, D_MODEL), lambda i: (row_block, 0)),
        out_shape=jax.ShapeDtypeStruct(buf.shape, F32),
        input_output_aliases={len(ins): 0},
        compiler_params=_cparams("arbitrary"),
        name="decode_tail",
    )(*ins, buf)


def _router_logits(x, whi, wlo, br):
    xh = x.astype(BF16)
    xl = (x - xh.astype(F32)).astype(BF16)
    return (jnp.dot(xh, whi, preferred_element_type=F32)
            + (jnp.dot(xh, wlo, preferred_element_type=F32)
               + jnp.dot(xl, whi, preferred_element_type=F32))) + br


def _group_logits(logits, lane):
    is_group = jnp.logical_and(lane >= _GROUP_LANE0, lane < _GROUP_LANE0 + N_EXPERT_GROUPS)
    return jnp.where(is_group, logits, NEG_INF)


def _router_kernel(x_ref, whi_ref, wlo_ref, br_ref, meta_ref, cnt_ref, run_scr, *, tr):
    i = pl.program_id(0)

    @pl.when(i == 0)
    def _():
        run_scr[...] = jnp.zeros_like(run_scr)

    lt = _router_logits(x_ref[...], whi_ref[...], wlo_ref[...], br_ref[...]).T
    row = lax.broadcasted_iota(jnp.int32, (EXPERTS_PER_GROUP, tr), 0)
    none = jnp.int32(EXPERTS_PER_GROUP)

    def argmax_rows(vals):
        vmax = jnp.max(vals, axis=0, keepdims=True)
        return jnp.min(jnp.where(vals == vmax, row, none), axis=0, keepdims=True)

    glog = jnp.where(row < N_EXPERT_GROUPS, lt[_GROUP_LANE0:_GROUP_LANE0 + EXPERTS_PER_GROUP], NEG_INF)
    gidx = argmax_rows(glog)
    el = lt[:EXPERTS_PER_GROUP]
    for g in range(1, N_EXPERT_GROUPS):
        el = jnp.where(gidx == g, lt[g * EXPERTS_PER_GROUP:(g + 1) * EXPERTS_PER_GROUP], el)
    e1 = argmax_rows(el)
    e2 = argmax_rows(jnp.where(row == e1, NEG_INF, el))
    lo = jnp.minimum(e1, e2)
    hi = jnp.maximum(e1, e2)
    pair = lax.shift_right_logical(lo * (2 * EXPERTS_PER_GROUP - 1 - lo), 1) + (hi - lo - 1)
    cls = gidx * PAIRS_PER_GROUP + pair

    oh = jnp.where(lax.broadcasted_iota(jnp.int32, (ROUTER_LANES, tr), 0) == cls, 1.0, 0.0)
    rr = lax.broadcasted_iota(jnp.int32, (tr, tr), 0)
    cc = lax.broadcasted_iota(jnp.int32, (tr, tr), 1)
    earlier = jnp.where(rr < cc, 1.0, 0.0).astype(BF16)
    before = jnp.dot(oh.astype(BF16), earlier, preferred_element_type=F32) + run_scr[...]
    rank = jnp.sum(oh * before, axis=0, keepdims=True)
    run_scr[...] += jnp.sum(oh, axis=1, keepdims=True)
    cnt_ref[...] = run_scr[...]

    meta_ref[...] = jnp.where(row == 0, cls.astype(F32), jnp.where(row == 1, rank, 0.0))


def _router(x, lw, *, tr=512):
    t = x.shape[0]
    kern = functools.partial(_router_kernel, tr=tr)
    return pl.pallas_call(
        kern,
        grid=(t // tr,),
        in_specs=[pl.BlockSpec((tr, D_MODEL), lambda i: (i, 0)),
                  _layer_resident(lw["router_hi"], lw["layer"]), _layer_resident(lw["router_lo"], lw["layer"]),
                  _layer_resident(lw["router_b"], lw["layer"])],
        out_specs=[pl.BlockSpec((None, SUBLANES, tr), lambda i: (i, 0, 0)),
                   pl.BlockSpec((ROUTER_LANES, 1), lambda i: (0, 0))],
        out_shape=[jax.ShapeDtypeStruct((t // tr, SUBLANES, tr), F32),
                   jax.ShapeDtypeStruct((ROUTER_LANES, 1), F32)],
        scratch_shapes=[pltpu.VMEM((ROUTER_LANES, 1), F32)],
        compiler_params=_cparams("arbitrary"),
        name="router",
    )(x, lw["router_hi"], lw["router_lo"], lw["router_b"])


def _rows_copy(src, dst, sem):
    return pltpu.make_async_copy(src, dst, sem)


def _dispatch_kernel(slot_ref, x_ref, xs_ref, stage, sems, *, td):
    i = pl.program_id(0)
    n = pl.num_programs(0)
    sb = lax.rem(i, 2)
    stage[sb] = x_ref[...]

    def issue(rb, carry):
        for q in range(ISSUE_UNROLL):
            r = rb * ISSUE_UNROLL + q
            s = slot_ref[0, 0, r]
            _rows_copy(stage.at[sb, pl.ds(r, 1)], xs_ref.at[pl.ds(s, 1)], sems.at[sb]).start(
                priority=q % DMA_QUEUES)
        return carry

    lax.fori_loop(0, td // ISSUE_UNROLL, issue, 0)

    def drain(buf):
        _rows_copy(stage.at[buf], xs_ref.at[pl.ds(0, td)], sems.at[buf]).wait()

    @pl.when(i > 0)
    def _():
        drain(1 - sb)

    @pl.when(i == n - 1)
    def _():
        drain(sb)


def _dispatch(x, slots, n_slots, *, td=512):
    t = x.shape[0]
    slots3 = slots.reshape(t // td, 1, td)
    kern = functools.partial(_dispatch_kernel, td=td)
    return pl.pallas_call(
        kern,
        grid=(t // td,),
        in_specs=[pl.BlockSpec((1, 1, td), lambda i: (i, 0, 0), memory_space=pltpu.SMEM),
                  pl.BlockSpec((td, D_MODEL), lambda i: (i, 0))],
        out_specs=pl.BlockSpec(memory_space=pl.ANY),
        out_shape=jax.ShapeDtypeStruct((n_slots, D_MODEL), F32),
        scratch_shapes=[pltpu.VMEM((2, td, D_MODEL), F32), pltpu.SemaphoreType.DMA((2,))],
        compiler_params=_cparams("arbitrary"),
        name="moe_dispatch",
    )(slots3, x)


def _expert_kernel(ea_ref, eb_ref, rows_ref, next_ref, nv_ref, xs_ref, rhi_ref, rb_ref, wg_hbm, wu_hbm, wd_hbm,
                   os_ref, wg_f32, wu_f32, wd_f32, wg_bf, wu_bf, wd_bf, sems, *, layer):
    i = pl.program_id(0)
    first = i * TILES_PER_STEP
    valid = first < nv_ref[0]
    group = ea_ref[first] // EXPERTS_PER_GROUP
    prev_group = ea_ref[jnp.maximum(first - 1, 0)] // EXPERTS_PER_GROUP
    new_group = jnp.logical_or(i == 0, group != prev_group)

    def weight_copies(grp):
        experts = pl.ds(grp * EXPERTS_PER_GROUP, EXPERTS_PER_GROUP)
        return [pltpu.make_async_copy(src.at[layer, experts], dst, sems.at[k])
                for k, (src, dst) in enumerate(((wg_hbm, wg_f32), (wu_hbm, wu_f32), (wd_hbm, wd_f32)))]

    @pl.when(jnp.logical_and(valid, i == 0))
    def _():
        for c in weight_copies(group):
            c.start()

    @pl.when(jnp.logical_and(valid, new_group))
    def _():
        for c in weight_copies(group):
            c.wait()
        for e in range(EXPERTS_PER_GROUP):
            wg_bf[e] = wg_f32[e].astype(BF16)
            wu_bf[e] = wu_f32[e].astype(BF16)
            wd_bf[e] = wd_f32[e].astype(BF16)
        nxt = next_ref[first]

        @pl.when(nxt >= 0)
        def _():
            for c in weight_copies(nxt):
                c.start()

    def body(row0, ea, eb, m):
        xt = xs_ref[pl.ds(row0, m), :].astype(BF16)
        logits = jnp.dot(xt, rhi_ref[...], preferred_element_type=F32) + rb_ref[...]
        lane = lax.broadcasted_iota(jnp.int32, logits.shape, 1)
        pick = lambda l: jnp.sum(jnp.where(lane == l, logits, 0.0), axis=-1, keepdims=True)
        glog = _group_logits(logits, lane)
        gmax = jnp.max(glog, axis=-1, keepdims=True)
        gw = jnp.exp(pick(_GROUP_LANE0 + group) - gmax) / jnp.sum(jnp.exp(glog - gmax), axis=-1, keepdims=True)
        la = pick(ea)
        lb = pick(eb)
        lmax = jnp.maximum(la, lb)
        pa = jnp.exp(la - lmax)
        pb = jnp.exp(lb - lmax)
        scale = gw / (pa + pb)

        y = None
        for e, w in ((ea, pa * scale), (eb, pb * scale)):
            el = e - group * EXPERTS_PER_GROUP
            gate = jnp.dot(xt, wg_bf[el], preferred_element_type=F32)
            up = jnp.dot(xt, wu_bf[el], preferred_element_type=F32)
            h = (gate * (0.5 + 0.5 * jnp.tanh(0.5 * gate)) * up).astype(BF16)
            term = w * jnp.dot(h, wd_bf[el], preferred_element_type=F32)
            y = term if y is None else y + term
        os_ref[pl.ds(row0, m), :] = y

    for s in range(TILES_PER_STEP):
        tile = first + s
        row_steps = (rows_ref[tile] + EXPERT_ROW_STEP - 1) // EXPERT_ROW_STEP
        for k in range(1, EXPERT_TILE // EXPERT_ROW_STEP + 1):
            pl.when(jnp.logical_and(valid, row_steps == k))(
                functools.partial(body, s * EXPERT_TILE, ea_ref[tile], eb_ref[tile],
                                  k * EXPERT_ROW_STEP))


def _experts(xs, tile_ea, tile_eb, tile_rows, tile_next_group, n_valid, lw):
    layer = lw["layer"]
    n_steps = xs.shape[0] // (TILES_PER_STEP * EXPERT_TILE)
    rows = TILES_PER_STEP * EXPERT_TILE
    act = pl.BlockSpec((rows, D_MODEL),
                       lambda i, ea, eb, nr, nx, nv: (jnp.minimum(i, nv[0] // TILES_PER_STEP - 1), 0))
    router = lambda a: pl.BlockSpec((None,) + a.shape[1:], lambda i, ea, eb, nr, nx, nv: (layer, 0, 0),
                                    pipeline_mode=pl.Buffered(1))
    in_hbm = pl.BlockSpec(memory_space=pl.ANY)
    gate_shape = (EXPERTS_PER_GROUP, D_MODEL, D_EXPERT)
    down_shape = (EXPERTS_PER_GROUP, D_EXPERT, D_MODEL)
    grid_spec = pltpu.PrefetchScalarGridSpec(
        num_scalar_prefetch=5,
        grid=(n_steps,),
        in_specs=[act, router(lw["router_hi"]), router(lw["router_b"]), in_hbm, in_hbm, in_hbm],
        out_specs=act,
        scratch_shapes=[pltpu.VMEM(gate_shape, F32), pltpu.VMEM(gate_shape, F32), pltpu.VMEM(down_shape, F32),
                        pltpu.VMEM(gate_shape, BF16), pltpu.VMEM(gate_shape, BF16), pltpu.VMEM(down_shape, BF16),
                        pltpu.SemaphoreType.DMA((3,))],
    )
    return pl.pallas_call(
        functools.partial(_expert_kernel, layer=layer),
        grid_spec=grid_spec,
        out_shape=jax.ShapeDtypeStruct(xs.shape, F32),
        compiler_params=_cparams("arbitrary", vmem_limit_bytes=EXPERT_VMEM_LIMIT_BYTES),
        name="moe_experts",
    )(tile_ea, tile_eb, tile_rows, tile_next_group, n_valid, xs, lw["router_hi"], lw["router_b"],
      lw["w_gate"], lw["w_up"], lw["w_down"])


def _gather_rows_issue(slot_ref, os_ref, gbuf, sems, buf, tc):
    def issue(rb, carry):
        for q in range(ISSUE_UNROLL):
            r = rb * ISSUE_UNROLL + q
            s = slot_ref[0, 0, r]
            _rows_copy(os_ref.at[pl.ds(s, 1)], gbuf.at[buf, pl.ds(r, 1)], sems.at[buf]).start(
                priority=q % DMA_QUEUES)
        return carry

    lax.fori_loop(0, tc // ISSUE_UNROLL, issue, 0)


def _combine_kernel(slot_ref, slot_next_ref, x_ref, g_ref, bt_ref, os_ref, xp_ref, xd_ref,
                    gbuf, sems, *, tc, n_first):
    i = pl.program_id(0)
    n = pl.num_programs(0)
    sb = lax.rem(i, 2)

    @pl.when(i == 0)
    def _():
        _gather_rows_issue(slot_ref, os_ref, gbuf, sems, 0, tc)

    @pl.when(i + 1 < n)
    def _():
        _gather_rows_issue(slot_next_ref, os_ref, gbuf, sems, 1 - sb, tc)

    _rows_copy(os_ref.at[pl.ds(0, tc)], gbuf.at[sb], sems.at[sb]).wait()
    x2 = _layer_norm(DEEPNORM_ALPHA * x_ref[...] + gbuf[sb], g_ref[...], bt_ref[...])

    @pl.when(i < n_first)
    def _():
        xp_ref[...] = x2

    @pl.when(i >= n_first)
    def _():
        xd_ref[...] = x2


def _combine(x, slots, os_, lw, *, n_prompt, tc=512):
    t = x.shape[0]
    nblk = t // tc
    n_first = n_prompt // tc
    slots3 = slots.reshape(nblk, 1, tc)
    kern = functools.partial(_combine_kernel, tc=tc, n_first=n_first)
    smem_tile = lambda f: pl.BlockSpec((1, 1, tc), f, memory_space=pltpu.SMEM)
    return pl.pallas_call(
        kern,
        grid=(nblk,),
        in_specs=[smem_tile(lambda i: (i, 0, 0)),
                  smem_tile(lambda i: (jnp.minimum(i + 1, nblk - 1), 0, 0)),
                  pl.BlockSpec((tc, D_MODEL), lambda i: (i, 0)),
                  _layer_resident(lw["ln2_g"], lw["layer"]), _layer_resident(lw["ln2_b"], lw["layer"]),
                  pl.BlockSpec(memory_space=pl.ANY)],
        out_specs=[pl.BlockSpec((tc, D_MODEL), lambda i: (jnp.minimum(i, n_first - 1), 0)),
                   pl.BlockSpec((tc, D_MODEL), lambda i: (jnp.maximum(i - n_first, 0), 0))],
        out_shape=[jax.ShapeDtypeStruct((n_prompt, D_MODEL), F32),
                   jax.ShapeDtypeStruct((t - n_prompt, D_MODEL), F32)],
        scratch_shapes=[pltpu.VMEM((2, tc, D_MODEL), F32), pltpu.SemaphoreType.DMA((2,))],
        compiler_params=_cparams("arbitrary"),
        name="moe_combine",
    )(slots3, slots3, x, lw["ln2_g"], lw["ln2_b"], os_)


def _moe(x, lw, *, n_prompt):
    t = x.shape[0]
    n_tiles = t // EXPERT_TILE + N_CLASSES + N_EXPERT_GROUPS * (TILES_PER_STEP - 1)
    n_tiles = -(-n_tiles // TILES_PER_STEP) * TILES_PER_STEP
    meta, cnt = _router(x, lw)
    cls = meta[:, 0, :].reshape(t).astype(jnp.int32)
    rank = meta[:, 1, :].reshape(t).astype(jnp.int32)
    counts = cnt[:N_CLASSES, 0].astype(jnp.int32)
    seg_tiles = (counts + EXPERT_TILE - 1) // EXPERT_TILE
    group_tiles = jnp.sum(seg_tiles.reshape(N_EXPERT_GROUPS, PAIRS_PER_GROUP), axis=1)
    group_pad = (-group_tiles) % TILES_PER_STEP
    last_of_group = jnp.arange(N_CLASSES) % PAIRS_PER_GROUP == PAIRS_PER_GROUP - 1
    seg_tiles = seg_tiles + jnp.where(last_of_group, jnp.repeat(group_pad, PAIRS_PER_GROUP), 0)
    tile_ends = jnp.cumsum(seg_tiles)
    seg_start = (tile_ends - seg_tiles) * EXPERT_TILE
    class_ids = jnp.arange(N_CLASSES, dtype=jnp.int32)
    slots = jnp.sum(jnp.where(cls[:, None] == class_ids, seg_start, 0), axis=-1) + rank
    n_valid = tile_ends[-1:]
    tile_ids = jnp.minimum(jnp.arange(n_tiles, dtype=jnp.int32), n_valid[0] - 1)
    tile_class = jnp.sum((tile_ends[None, :] <= tile_ids[:, None]).astype(jnp.int32), axis=1)
    of_class = tile_class[:, None] == class_ids
    class_experts = jnp.asarray(_CLASS_EXPERTS, dtype=jnp.int32)
    tile_ea = jnp.sum(jnp.where(of_class, class_experts[:, 0], 0), axis=-1)
    tile_eb = jnp.sum(jnp.where(of_class, class_experts[:, 1], 0), axis=-1)
    rows_left = jnp.sum(jnp.where(of_class, seg_start + counts, 0), axis=-1) - tile_ids * EXPERT_TILE
    tile_rows = jnp.clip(rows_left, 0, EXPERT_TILE)
    group_ids = jnp.arange(N_EXPERT_GROUPS, dtype=jnp.int32)
    later_used = (group_ids[None, :] > group_ids[:, None]) & ((group_tiles + group_pad) > 0)[None, :]
    next_group = jnp.min(jnp.where(later_used, group_ids[None, :], N_EXPERT_GROUPS), axis=1)
    next_group = jnp.where(next_group == N_EXPERT_GROUPS, -1, next_group)
    tile_group = tile_ea // EXPERTS_PER_GROUP
    tile_next_group = jnp.sum(jnp.where(tile_group[:, None] == group_ids, next_group, 0), axis=-1)
    xs = _dispatch(x, slots, n_tiles * EXPERT_TILE)
    os_ = _experts(xs, tile_ea, tile_eb, tile_rows, tile_next_group, n_valid.astype(jnp.int32), lw)
    return _combine(x, slots, os_, lw, n_prompt=n_prompt)


def _t5_bucket(dist):
    n = jnp.maximum(dist, 0)
    max_exact = REL_BUCKETS // 2
    nf = jnp.maximum(n, 1).astype(F32)
    large = max_exact + (jnp.log(nf / max_exact) / math.log(REL_MAX_DISTANCE / max_exact)
                         * (REL_BUCKETS - max_exact)).astype(jnp.int32)
    large = jnp.minimum(large, REL_BUCKETS - 1)
    return jnp.where(n < max_exact, n, large)


def _head_rows(per_head, rows_of):
    out = []
    for kvh in range(N_KV_HEADS):
        out.append(jnp.stack([jnp.concatenate([jnp.repeat(per_head[..., h, :, :], rep, axis=-2)
                                               for h, rep in rows_of(kvh, par)], axis=-2)
                              for par in range(2)], axis=-3))
    return jnp.stack(out, axis=-4)


def _prompt_rows(kvh, par):
    return [(GROUP * kvh + 2 * p + par, 1) for p in range(2)]


def _head_bias(rel_bias, dist, valid):
    bias = jnp.where(valid[..., None], rel_bias[_t5_bucket(dist)].astype(F32), NEG_INF)
    return jnp.moveaxis(bias, -1, 0)


def _prompt_bias(rel_bias):
    period = 4 * WINDOW
    off = jnp.arange(period)
    off = jnp.where(off < 2 * WINDOW, off, off - period)
    dist = WINDOW - off
    per_off = _head_bias(rel_bias, dist, (dist >= 0) & (dist <= WINDOW))
    tiled = jnp.tile(per_off, (1, WINDOW))[:, :WINDOW * (period - 1)]
    return tiled.reshape(N_HEADS, WINDOW, period - 1)[:, :, :2 * WINDOW]


def _attention_tables(rel_bias, sinks, ntok, g):
    bias_p = _head_rows(_prompt_bias(rel_bias), _prompt_rows)

    decode_rows = lambda kvh, par: [(GROUP * kvh + 2 * p + par, g) for p in range(2)]
    tq = jnp.arange(ntok)[:, None]
    kpos_old = jnp.arange(WINDOW)[None, :] - WINDOW
    kpos_new = jnp.arange(ntok)[None, :]
    tables = []
    for kpos in (kpos_old, kpos_new):
        d = tq - kpos
        per_head = _head_bias(rel_bias, d, (d >= 0) & (d <= WINDOW) & (PAST_LEN + kpos >= 0))
        tables.append(_head_rows(per_head, decode_rows))
    bias_d = tables[0]
    biasn_d = jnp.moveaxis(tables[1], -1, 2)[..., None]

    s = sinks.astype(F32)[:, :, None, None]
    sink_p = _head_rows(jnp.broadcast_to(s, s.shape[:2] + (WINDOW, 1)), _prompt_rows)
    sink_d = _head_rows(jnp.broadcast_to(s, s.shape[:2] + (ntok, 1)), decode_rows)
    return bias_p, bias_d, biasn_d, sink_p, sink_d


def _stacked_weights(w_in, pool_w, pool_scale, proj_a, proj_b, w_out, ln1_g, ln1_b, router_group_w,
                     router_group_b, router_expert_w, router_expert_b, expert_w_gate, expert_w_up,
                     expert_w_down, ln2_g, ln2_b):
    depth = w_in.shape[0]
    same_group = jnp.eye(N_POOL_GROUPS, dtype=bool)[None, :, None, :, None]
    pool_bd = jnp.where(same_group, pool_w[:, :, :, None, :], 0.0).reshape(depth, D_POOL, D_POOL)
    pad = ROUTER_LANES - N_EXPERTS - N_EXPERT_GROUPS
    rw = jnp.concatenate([router_expert_w, router_group_w, jnp.zeros((depth, D_MODEL, pad), F32)], axis=-1)
    rb = jnp.concatenate([router_expert_b, router_group_b, jnp.zeros((depth, pad), F32)], axis=-1)
    r_hi = rw.astype(BF16)
    row = lambda a: a[:, None, :]
    return dict(
        w_in=w_in.astype(BF16), pool_w=pool_bd.astype(BF16), pool_scale=row(pool_scale),
        proj_a=proj_a.astype(BF16), proj_b=proj_b.astype(BF16), w_out=w_out.astype(BF16),
        ln1_g=row(ln1_g), ln1_b=row(ln1_b),
        router_hi=r_hi, router_lo=(rw - r_hi.astype(F32)).astype(BF16), router_b=row(rb),
        w_gate=expert_w_gate, w_up=expert_w_up, w_down=expert_w_down,
        ln2_g=row(ln2_g), ln2_b=row(ln2_b))


def _decode_mixer(xs_tn, lw, cache_k, cache_v, hist_t, tables, buf, new_caches, *, nseq, ntok, g, row_block):
    bias_d, biasn_d, sink_d = tables
    u = _in_proj(xs_tn, lw)
    u3 = u.reshape(ntok, nseq, D_IN)
    b, new_k, new_v = _decode_attn(u3, cache_k, cache_v, bias_d, biasn_d, sink_d, new_caches, g=g,
                                   layer=lw["layer"])
    buf = _decode_tail(xs_tn, u, hist_t, b.reshape(ntok * nseq, Q_DIM), lw, buf, nseq=nseq, ntok=ntok,
                       row_block=row_block)
    return buf, u3, (new_k, new_v)


def kernel(x_prompt, x_sample, cache_k, cache_v, state_pool, rel_bias, w_in, pool_w, pool_scale, proj_a, proj_b, w_out, attn_sinks, ln1_g, ln1_b, router_group_w, router_group_b, router_expert_w, router_expert_b, expert_w_gate, expert_w_up, expert_w_down, ln2_g, ln2_b):
    bsz, seq, _ = x_prompt.shape
    nseq, ntok, _ = x_sample.shape
    g = 16
    depth = w_in.shape[0]
    n_prompt = bsz * seq
    n_decode = ntok * nseq
    assert n_prompt % n_decode == 0
    bias_p, bias_d, biasn_d, sink_p, sink_d = _attention_tables(rel_bias, attn_sinks, ntok, g)
    weights = _stacked_weights(w_in, pool_w, pool_scale, proj_a, proj_b, w_out, ln1_g, ln1_b, router_group_w,
                               router_group_b, router_expert_w, router_expert_b, expert_w_gate, expert_w_up,
                               expert_w_down, ln2_g, ln2_b)

    xp = x_prompt.reshape(n_prompt, D_MODEL)
    xs_tn = jnp.swapaxes(x_sample, 0, 1).reshape(n_decode, D_MODEL)
    hist_t = jnp.swapaxes(state_pool, 1, 2)
    ck_t = jnp.transpose(cache_k, (0, 1, 3, 4, 2))
    cv_t = jnp.transpose(cache_v, (0, 1, 3, 4, 2))
    prompt_state, decode_u, new_caches = [], [], None
    for l in range(depth):
        lw = dict(weights, layer=l)
        x1, kst, vst, pst = _prompt_mixer(xp, lw, bias_p, sink_p, bsz=bsz, n_total=n_prompt + n_decode)
        x1, u3, new_caches = _decode_mixer(xs_tn, lw, ck_t, cv_t, hist_t, (bias_d, biasn_d, sink_d), x1,
                                           new_caches, nseq=nseq, ntok=ntok, g=g,
                                           row_block=n_prompt // n_decode)
        xp, xs_tn = _moe(x1, lw, n_prompt=n_prompt)
        prompt_state.append((kst, vst, pst))
        decode_u.append(u3)

    pk, pv, pp = (jnp.stack(parts) for parts in zip(*prompt_state))
    heads = lambda a: a.reshape(a.shape[:3] + (N_KV_HEADS, HEAD_DIM))
    new_pool = jnp.stack([jnp.swapaxes(u3[:, :, :D_POOL], 0, 1) for u3 in decode_u])
    sample_pool = jnp.concatenate([state_pool[:, :, ntok:], new_pool], axis=2)
    sample_k, sample_v = (jnp.transpose(c, (0, 1, 4, 2, 3)) for c in new_caches)
    y_sample = jnp.swapaxes(xs_tn.reshape(ntok, nseq, D_MODEL), 0, 1)
    return (xp.reshape(bsz, seq, D_MODEL), y_sample, heads(pk), heads(pv), pp[:, :, -POOL_HIST:],
            sample_k, sample_v, sample_pool)
```

```python
import functools
import math

import jax
import jax.numpy as jnp
from jax import lax
from jax.experimental import pallas as pl
from jax.experimental.pallas import tpu as pltpu

D_MODEL = 1024
DEPTH = 2
PAST_LEN = 16384
D_POOL = D_MODEL // 2
POOL_WINDOWS = (2, 4, 8, 16)
N_POOL_GROUPS = len(POOL_WINDOWS)
POOL_GROUP_DIM = D_POOL // N_POOL_GROUPS
POOL_HIST = max(POOL_WINDOWS) - 1
assert all(w & (w - 1) == 0 for w in POOL_WINDOWS)
HEAD_DIM = 64
N_HEADS = (D_MODEL // 2) // HEAD_DIM
N_KV_HEADS = 2
GROUP = N_HEADS // N_KV_HEADS
Q_DIM = N_HEADS * HEAD_DIM
KV_DIM = N_KV_HEADS * HEAD_DIM
WINDOW = 128
ATTN_SCALE = HEAD_DIM ** -0.5
assert math.log2(ATTN_SCALE).is_integer()
REL_BUCKETS = 32
REL_MAX_DISTANCE = 128
D_IN = D_POOL + Q_DIM + 2 * KV_DIM + 2 * D_MODEL
N_EXPERT_GROUPS = 4
EXPERTS_PER_GROUP = 8
N_EXPERTS = N_EXPERT_GROUPS * EXPERTS_PER_GROUP
D_EXPERT = D_MODEL // 4
DEEPNORM_ALPHA = (2 * DEPTH) ** 0.25
LN_EPS = 1e-5

LANES = 128
SUBLANES = 8
VMEM_LIMIT_BYTES = 56 * 1024 * 1024
EXPERT_VMEM_LIMIT_BYTES = 62 * 1024 * 1024

_O_Q = D_POOL
_O_K = _O_Q + Q_DIM
_O_V = _O_K + KV_DIM
_O_GA = _O_V + KV_DIM
_O_GB = _O_GA + D_MODEL

ROW_CHUNKS = D_MODEL // LANES
EXPERT_TILE = 192
OUT_PROJ_CHUNKS = 2
EXPERT_ROW_STEP = 32
TILES_PER_STEP = 4
DMA_QUEUES = 2
ISSUE_UNROLL = 16
PAIRS_PER_GROUP = EXPERTS_PER_GROUP * (EXPERTS_PER_GROUP - 1) // 2
N_CLASSES = N_EXPERT_GROUPS * PAIRS_PER_GROUP
_CLASS_EXPERTS = tuple((g * EXPERTS_PER_GROUP + a, g * EXPERTS_PER_GROUP + b)
                       for g in range(N_EXPERT_GROUPS)
                       for a in range(EXPERTS_PER_GROUP) for b in range(a + 1, EXPERTS_PER_GROUP))
ROUTER_LANES = LANES
_GROUP_LANE0 = N_EXPERTS
assert EXPERTS_PER_GROUP == SUBLANES and N_EXPERT_GROUPS <= SUBLANES

BF16 = jnp.bfloat16
F32 = jnp.float32
NEG_INF = float("-inf")


def _cparams(*sem, flags=None, vmem_limit_bytes=VMEM_LIMIT_BYTES):
    return pltpu.CompilerParams(dimension_semantics=sem, vmem_limit_bytes=vmem_limit_bytes, flags=flags)


def _resident(shape):
    nd = len(shape)
    return pl.BlockSpec(shape, lambda *_: (0,) * nd, pipeline_mode=pl.Buffered(1))


def _layer_resident(stacked, layer):
    nd = stacked.ndim - 1
    return pl.BlockSpec((None,) + stacked.shape[1:], lambda *_: (layer,) + (0,) * nd,
                        pipeline_mode=pl.Buffered(1))


def _layer_norm(y, g, b):
    mu = jnp.mean(y, axis=-1, keepdims=True)
    yc = y - mu
    var = jnp.mean(yc * yc, axis=-1, keepdims=True)
    return yc * lax.rsqrt(var + LN_EPS) * g + b


def _softmax_pv(s, sink, v_bf16):
    m = jnp.maximum(jnp.max(s, axis=-1, keepdims=True), sink)
    p = jnp.exp(s - m)
    den = jnp.sum(p, axis=-1, keepdims=True) + jnp.exp(sink - m)
    o = jnp.dot(p.astype(BF16), v_bf16, preferred_element_type=F32)
    return o / den


def _half_lane_variants(t):
    lane = lax.broadcasted_iota(jnp.int32, t.shape, 1)
    lo = lane < HEAD_DIM
    tr = pltpu.roll(t, HEAD_DIM, 1)
    z = jnp.zeros_like(t)
    return ((jnp.where(lo, t, z), jnp.where(lo, z, tr)),
            (jnp.where(lo, tr, z), jnp.where(lo, z, t)))


def _mix_tail(x, a_in, b_in, ga, gb, pw, ps, pa, pb, wo, g, bt):
    a = jnp.dot(a_in.astype(BF16), pw, preferred_element_type=F32) * ps
    pa_o = jnp.dot(a.astype(BF16), pa, preferred_element_type=F32)
    pb_o = jnp.dot(b_in.astype(BF16), pb, preferred_element_type=F32)
    sigmoid = lambda z: 0.5 + 0.5 * jnp.tanh(0.5 * z)
    merged = (sigmoid(ga) * pa_o + sigmoid(gb) * pb_o).astype(BF16)
    rows = x.shape[0]
    chunk = rows // OUT_PROJ_CHUNKS
    out = []
    for r0 in range(0, rows, chunk):
        mix = jnp.dot(merged[r0:r0 + chunk], wo, preferred_element_type=F32)
        out.append(_layer_norm(DEEPNORM_ALPHA * x[r0:r0 + chunk] + mix, g, bt))
    return jnp.concatenate(out, axis=0)


def _prompt_mixer_kernel(x_ref, win_ref, pw_ref, ps_ref, pa_ref, pb_ref, wo_ref, bias_ref, sink_ref,
                         g_ref, bt_ref, x1_ref, kst_ref, vst_ref, pst_ref, kt_scr, v_scr, ext_scr, *, tm):
    i = pl.program_id(1)
    hist = 2 * SUBLANES

    @pl.when(i == 0)
    def _():
        kt_scr[:, :, :WINDOW] = jnp.zeros((4, LANES, WINDOW), BF16)
        v_scr[:, :WINDOW, :] = jnp.zeros((4, WINDOW, LANES), BF16)
        ext_scr[:hist, :] = jnp.zeros((hist, D_POOL), F32)

    @pl.when(i > 0)
    def _():
        kt_scr[:, :, :WINDOW] = kt_scr[:, :, tm:tm + WINDOW]
        v_scr[:, :WINDOW, :] = v_scr[:, tm:tm + WINDOW, :]
        ext_scr[:hist, :] = ext_scr[tm:tm + hist, :]

    x = x_ref[...]
    u = jnp.dot(x.astype(BF16), win_ref[...], preferred_element_type=F32)
    u_pool = u[:, :D_POOL]
    q = u[:, _O_Q:_O_K]
    k = u[:, _O_K:_O_V]
    v = u[:, _O_V:_O_GA]
    ga = u[:, _O_GA:_O_GB]
    gb = u[:, _O_GB:]

    kst_ref[...] = k[tm - WINDOW:, :]
    vst_ref[...] = v[tm - WINDOW:, :]
    pst_ref[...] = u_pool[tm - hist:, :]

    kt = k.T
    kz = jnp.zeros((HEAD_DIM, tm), F32)
    kt_scr[0, :, WINDOW:] = jnp.concatenate([kt[:HEAD_DIM], kz], axis=0).astype(BF16)
    kt_scr[1, :, WINDOW:] = jnp.concatenate([kz, kt[:HEAD_DIM]], axis=0).astype(BF16)
    kt_scr[2, :, WINDOW:] = jnp.concatenate([kt[HEAD_DIM:], kz], axis=0).astype(BF16)
    kt_scr[3, :, WINDOW:] = jnp.concatenate([kz, kt[HEAD_DIM:]], axis=0).astype(BF16)
    vv = _half_lane_variants(v)
    for kvh in range(N_KV_HEADS):
        for par in range(2):
            v_scr[2 * kvh + par, WINDOW:, :] = vv[kvh][par].astype(BF16)

    col = lax.broadcasted_iota(jnp.int32, (2 * WINDOW, 2 * WINDOW), 1)
    no_prev = col < jnp.where(i == 0, WINDOW, 0)
    qb = (q * ATTN_SCALE).astype(BF16)
    b_rows = []
    for j in range(tm // WINDOW):
        r0 = j * WINDOW
        o_kvh = []
        for kvh in range(N_KV_HEADS):
            c0 = kvh * 2 * LANES
            q2 = jnp.concatenate([qb[r0:r0 + WINDOW, c0:c0 + LANES],
                                  qb[r0:r0 + WINDOW, c0 + LANES:c0 + 2 * LANES]], axis=0)
            acc = None
            for par in range(2):
                s = jnp.dot(q2, kt_scr[2 * kvh + par, :, r0:r0 + 2 * WINDOW], preferred_element_type=F32)
                s = s + bias_ref[kvh, par]
                if j == 0:
                    s = jnp.where(no_prev, NEG_INF, s)
                o = _softmax_pv(s, sink_ref[kvh, par], v_scr[2 * kvh + par, r0:r0 + 2 * WINDOW, :])
                acc = o if acc is None else acc + o
            o_kvh.append(acc)
        b_rows.append(jnp.concatenate([o_kvh[0][:WINDOW], o_kvh[0][WINDOW:],
                                       o_kvh[1][:WINDOW], o_kvh[1][WINDOW:]], axis=1))
    b = jnp.concatenate(b_rows, axis=0)

    ext_scr[hist:, :] = u_pool
    pos = i * tm + lax.broadcasted_iota(jnp.int32, (tm, 1), 0)
    means = []
    for gi, w in enumerate(POOL_WINDOWS):
        c0 = gi * POOL_GROUP_DIM
        acc = ext_scr[:, c0:c0 + POOL_GROUP_DIM]
        d = 1
        while d < w:
            acc = acc + pltpu.roll(acc, d, 0)
            d *= 2
        inv_cnt = 1.0 / jnp.minimum(w, pos + 1).astype(F32)
        means.append(acc[hist:] * inv_cnt)
    pooled = jnp.concatenate(means, axis=1) - u_pool

    x1_ref[...] = _mix_tail(x, pooled, b, ga, gb, pw_ref[...], ps_ref[...], pa_ref[...], pb_ref[...],
                            wo_ref[...], g_ref[...], bt_ref[...])


def _prompt_mixer(x, lw, bias_p, sink_t, *, bsz, n_total, tm=512):
    seq = x.shape[0] // bsz
    nt = seq // tm
    hist = 2 * SUBLANES
    kern = functools.partial(_prompt_mixer_kernel, tm=tm)
    tile = pl.BlockSpec((tm, D_MODEL), lambda b, i: (b * nt + i, 0))
    state = lambda r, w: pl.BlockSpec((None, r, w), lambda b, i: (b, 0, 0))
    params = (lw["w_in"], lw["pool_w"], lw["pool_scale"], lw["proj_a"], lw["proj_b"], lw["w_out"])
    tail = (sink_t, lw["ln1_g"], lw["ln1_b"])
    return pl.pallas_call(
        kern,
        grid=(bsz, nt),
        in_specs=([tile] + [_layer_resident(a, lw["layer"]) for a in params] + [_resident(bias_p.shape)]
                  + [_layer_resident(a, lw["layer"]) for a in tail]),
        out_specs=[tile, state(WINDOW, KV_DIM), state(WINDOW, KV_DIM), state(hist, D_POOL)],
        out_shape=[jax.ShapeDtypeStruct((n_total, D_MODEL), F32),
                   jax.ShapeDtypeStruct((bsz, WINDOW, KV_DIM), F32),
                   jax.ShapeDtypeStruct((bsz, WINDOW, KV_DIM), F32),
                   jax.ShapeDtypeStruct((bsz, hist, D_POOL), F32)],
        scratch_shapes=[pltpu.VMEM((4, LANES, WINDOW + tm), BF16),
                        pltpu.VMEM((4, WINDOW + tm, LANES), BF16),
                        pltpu.VMEM((hist + tm, D_POOL), F32)],
        compiler_params=_cparams("arbitrary", "arbitrary"),
        name="prompt_mixer",
    )(x, lw["w_in"], lw["pool_w"], lw["pool_scale"], lw["proj_a"], lw["proj_b"], lw["w_out"],
      bias_p, sink_t, lw["ln1_g"], lw["ln1_b"])


def _in_proj_kernel(x_ref, win_ref, u_ref):
    u_ref[...] = jnp.dot(x_ref[...].astype(BF16), win_ref[...], preferred_element_type=F32)


def _in_proj(x, lw):
    n = x.shape[0]
    return pl.pallas_call(
        _in_proj_kernel,
        grid=(1,),
        in_specs=[pl.BlockSpec(x.shape, lambda i: (0, 0)), _layer_resident(lw["w_in"], lw["layer"])],
        out_specs=pl.BlockSpec((n, D_IN), lambda i: (0, 0)),
        out_shape=jax.ShapeDtypeStruct((n, D_IN), F32),
        compiler_params=_cparams("arbitrary"),
        name="decode_in_proj",
    )(x, lw["w_in"])


def _decode_attn_kernel(q_ref, kn_ref, vn_ref, kc_ref, vc_ref, bias_ref, biasn_ref, sink_ref, *rest, g, ntok):
    b_ref, sk_ref, sv_ref, new_scr = rest[-4:]
    slabs = 2 * ntok
    rows = slabs * g

    lane = lax.broadcasted_iota(jnp.int32, (KV_DIM, WINDOW), 1)
    fill = jnp.zeros((LANES - g * ntok, KV_DIM), F32)
    for new_ref, old_ref, out_ref in ((kn_ref, kc_ref, sk_ref), (vn_ref, vc_ref, sv_ref)):
        for t in range(ntok):
            new_scr[pl.ds(t, g, stride=ntok), :] = new_ref[t]
        packed = jnp.concatenate([new_scr[...], fill], axis=0).T
        for n in range(g):
            tile = pltpu.roll(old_ref[n].reshape(KV_DIM, WINDOW), WINDOW - ntok, 1)
            cols = pltpu.roll(packed, (WINDOW - ntok - n * ntok) % LANES, 1)
            out_ref[n] = jnp.where(lane >= WINDOW - ntok, cols, tile).reshape(N_KV_HEADS, HEAD_DIM, WINDOW)

    def cache_variants(ref):
        z = jnp.zeros((HEAD_DIM, WINDOW), F32)
        out = []
        for kvh in range(N_KV_HEADS):
            out.append((jnp.concatenate([jnp.concatenate([ref[n, kvh], z], axis=0) for n in range(g)], axis=1),
                        jnp.concatenate([jnp.concatenate([z, ref[n, kvh]], axis=0) for n in range(g)], axis=1)))
        return out

    kk = cache_variants(kc_ref)
    vv = cache_variants(vc_ref)
    kn = [_half_lane_variants(kn_ref[t]) for t in range(ntok)]
    vn = [_half_lane_variants(vn_ref[t]) for t in range(ntok)]
    per_row = lambda a: jnp.concatenate([a] * slabs, axis=0)
    row_seq = lax.rem(lax.broadcasted_iota(jnp.int32, (rows, WINDOW), 0), g)
    diag = (lax.rem(lax.broadcasted_iota(jnp.int32, (rows, g * WINDOW), 0), g)
            == lax.broadcasted_iota(jnp.int32, (rows, g * WINDOW), 1) // WINDOW)
    for kvh in range(N_KV_HEADS):
        c0 = kvh * 2 * LANES
        q2f = jnp.concatenate([q_ref[t, :, c0 + p * LANES:c0 + (p + 1) * LANES]
                               for p in range(2) for t in range(ntok)], axis=0)
        q2 = q2f.astype(BF16)
        acc = None
        for par in range(2):
            full = jnp.dot(q2, kk[kvh][par].astype(BF16), preferred_element_type=F32)
            s = jnp.zeros((rows, WINDOW), F32)
            for n in range(g):
                s = jnp.where(row_seq == n, full[:, n * WINDOW:(n + 1) * WINDOW], s)
            s = s * ATTN_SCALE + bias_ref[kvh, par]
            s_new = [jnp.sum(q2f * per_row(kn[t][kvh][par]), axis=-1, keepdims=True) * ATTN_SCALE
                     + biasn_ref[kvh, par, t] for t in range(ntok)]
            sink = sink_ref[kvh, par]
            m = jnp.maximum(jnp.max(s, axis=-1, keepdims=True), sink)
            for t in range(ntok):
                m = jnp.maximum(m, s_new[t])
            p = jnp.exp(s - m)
            den = jnp.sum(p, axis=-1, keepdims=True) + jnp.exp(sink - m)
            p_bd = jnp.where(diag, jnp.concatenate([p] * g, axis=1), 0.0).astype(BF16)
            o = lax.dot_general(p_bd, vv[kvh][par].astype(BF16), (((1,), (1,)), ((), ())),
                                preferred_element_type=F32)
            for t in range(ntok):
                p_new = jnp.exp(s_new[t] - m)
                den = den + p_new
                o = o + p_new * per_row(vn[t][kvh][par])
            o = o / den
            acc = o if acc is None else acc + o
        for p in range(2):
            for t in range(ntok):
                r0 = (p * ntok + t) * g
                b_ref[t, :, c0 + p * LANES:c0 + (p + 1) * LANES] = acc[r0:r0 + g]


def _decode_attn(u3, cache_k, cache_v, bias_d, biasn_d, sink_d, new_caches, *, g, layer):
    ntok, nseq, _ = u3.shape
    cache = pl.BlockSpec((None, g, N_KV_HEADS, HEAD_DIM, WINDOW), lambda i: (layer, i, 0, 0, 0))
    kern = functools.partial(_decode_attn_kernel, g=g, ntok=ntok)
    new_cols = lambda w, off: pl.BlockSpec((ntok, g, w), lambda i: (0, i, off // w))
    ins = [u3, u3, u3, cache_k, cache_v, bias_d, biasn_d, sink_d]
    in_specs = [new_cols(Q_DIM, _O_Q), new_cols(KV_DIM, _O_K), new_cols(KV_DIM, _O_V), cache, cache,
                _resident(bias_d.shape), _resident(biasn_d.shape), _layer_resident(sink_d, layer)]
    aliases = {}
    if new_caches is not None:
        aliases = {len(ins): 1, len(ins) + 1: 2}
        ins += list(new_caches)
        in_specs += [pl.BlockSpec(memory_space=pl.ANY)] * 2
    return pl.pallas_call(
        kern,
        grid=(nseq // g,),
        in_specs=in_specs,
        out_specs=[pl.BlockSpec((ntok, g, Q_DIM), lambda i: (0, i, 0)), cache, cache],
        out_shape=[jax.ShapeDtypeStruct((ntok, nseq, Q_DIM), F32),
                   jax.ShapeDtypeStruct(cache_k.shape, F32), jax.ShapeDtypeStruct(cache_v.shape, F32)],
        input_output_aliases=aliases,
        scratch_shapes=[pltpu.VMEM((g * ntok, KV_DIM), F32)],
        compiler_params=_cparams("arbitrary"),
        name="decode_attn",
    )(*ins)


def _decode_tail_kernel(x_ref, u_ref, hist_ref, b_ref, pw_ref, ps_ref, pa_ref, pb_ref,
                        wo_ref, g_ref, bt_ref, buf_ref, x1_ref, *, nseq, ntok, start_pos):
    del buf_ref

    def ext_row(r, c0):
        if r < POOL_HIST:
            return hist_ref[r, :, c0:c0 + POOL_GROUP_DIM]
        return u_ref[(r - POOL_HIST) * nseq:(r - POOL_HIST + 1) * nseq, c0:c0 + POOL_GROUP_DIM]

    pooled_rows = []
    for t in range(ntok):
        means = []
        for gi, w in enumerate(POOL_WINDOWS):
            acc = None
            for d in range(w):
                r = POOL_HIST + t - d
                if start_pos - POOL_HIST + r < 0:
                    continue
                term = ext_row(r, gi * POOL_GROUP_DIM)
                acc = term if acc is None else acc + term
            means.append(acc / float(min(w, start_pos + t + 1)))
        pooled_rows.append(jnp.concatenate(means, axis=1))
    pooled = jnp.concatenate(pooled_rows, axis=0) - u_ref[:, :D_POOL]
    x1_ref[...] = _mix_tail(x_ref[...], pooled, b_ref[...], u_ref[:, _O_GA:_O_GB], u_ref[:, _O_GB:], pw_ref[...],
                            ps_ref[...], pa_ref[...], pb_ref[...], wo_ref[...], g_ref[...], bt_ref[...])


def _decode_tail(x, u, hist_t, b, lw, buf, *, nseq, ntok, row_block):
    n = ntok * nseq
    kern = functools.partial(_decode_tail_kernel, nseq=nseq, ntok=ntok, start_pos=PAST_LEN)
    whole = lambda a: pl.BlockSpec(a.shape, lambda i: (0,) * a.ndim)
    layered = lambda a: _layer_resident(a, lw["layer"])
    params = (lw["pool_w"], lw["pool_scale"], lw["proj_a"], lw["proj_b"], lw["w_out"], lw["ln1_g"], lw["ln1_b"])
    ins = (x, u, hist_t, b) + params
    return pl.pallas_call(
        kern,
        grid=(1,),
        in_specs=([whole(x), whole(u), layered(hist_t), whole(b)] + [layered(a) for a in params]
                  + [pl.BlockSpec(memory_space=pl.ANY)]),
        out_specs=pl.BlockSpec((n, D_MODEL), lambda i: (row_block, 0)),
        out_shape=jax.ShapeDtypeStruct(buf.shape, F32),
        input_output_aliases={len(ins): 0},
        compiler_params=_cparams("arbitrary"),
        name="decode_tail",
    )(*ins, buf)


def _router_logits(x, whi, wlo, br):
    xh = x.astype(BF16)
    xl = (x - xh.astype(F32)).astype(BF16)
    return (jnp.dot(xh, whi, preferred_element_type=F32)
            + (jnp.dot(xh, wlo, preferred_element_type=F32)
               + jnp.dot(xl, whi, preferred_element_type=F32))) + br


def _group_logits(logits, lane):
    is_group = jnp.logical_and(lane >= _GROUP_LANE0, lane < _GROUP_LANE0 + N_EXPERT_GROUPS)
    return jnp.where(is_group, logits, NEG_INF)


def _router_kernel(x_ref, whi_ref, wlo_ref, br_ref, meta_ref, cnt_ref, run_scr, *, tr):
    i = pl.program_id(0)

    @pl.when(i == 0)
    def _():
        run_scr[...] = jnp.zeros_like(run_scr)

    lt = _router_logits(x_ref[...], whi_ref[...], wlo_ref[...], br_ref[...]).T
    row = lax.broadcasted_iota(jnp.int32, (EXPERTS_PER_GROUP, tr), 0)
    none = jnp.int32(EXPERTS_PER_GROUP)

    def argmax_rows(vals):
        vmax = jnp.max(vals, axis=0, keepdims=True)
        return jnp.min(jnp.where(vals == vmax, row, none), axis=0, keepdims=True)

    glog = jnp.where(row < N_EXPERT_GROUPS, lt[_GROUP_LANE0:_GROUP_LANE0 + EXPERTS_PER_GROUP], NEG_INF)
    gidx = argmax_rows(glog)
    el = lt[:EXPERTS_PER_GROUP]
    for g in range(1, N_EXPERT_GROUPS):
        el = jnp.where(gidx == g, lt[g * EXPERTS_PER_GROUP:(g + 1) * EXPERTS_PER_GROUP], el)
    e1 = argmax_rows(el)
    e2 = argmax_rows(jnp.where(row == e1, NEG_INF, el))
    lo = jnp.minimum(e1, e2)
    hi = jnp.maximum(e1, e2)
    pair = lax.shift_right_logical(lo * (2 * EXPERTS_PER_GROUP - 1 - lo), 1) + (hi - lo - 1)
    cls = gidx * PAIRS_PER_GROUP + pair

    oh = jnp.where(lax.broadcasted_iota(jnp.int32, (ROUTER_LANES, tr), 0) == cls, 1.0, 0.0)
    rr = lax.broadcasted_iota(jnp.int32, (tr, tr), 0)
    cc = lax.broadcasted_iota(jnp.int32, (tr, tr), 1)
    earlier = jnp.where(rr < cc, 1.0, 0.0).astype(BF16)
    before = jnp.dot(oh.astype(BF16), earlier, preferred_element_type=F32) + run_scr[...]
    rank = jnp.sum(oh * before, axis=0, keepdims=True)
    run_scr[...] += jnp.sum(oh, axis=1, keepdims=True)
    cnt_ref[...] = run_scr[...]

    meta_ref[...] = jnp.where(row == 0, cls.astype(F32), jnp.where(row == 1, rank, 0.0))


def _router(x, lw, *, tr=512):
    t = x.shape[0]
    kern = functools.partial(_router_kernel, tr=tr)
    return pl.pallas_call(
        kern,
        grid=(t // tr,),
        in_specs=[pl.BlockSpec((tr, D_MODEL), lambda i: (i, 0)),
                  _layer_resident(lw["router_hi"], lw["layer"]), _layer_resident(lw["router_lo"], lw["layer"]),
                  _layer_resident(lw["router_b"], lw["layer"])],
        out_specs=[pl.BlockSpec((None, SUBLANES, tr), lambda i: (i, 0, 0)),
                   pl.BlockSpec((ROUTER_LANES, 1), lambda i: (0, 0))],
        out_shape=[jax.ShapeDtypeStruct((t // tr, SUBLANES, tr), F32),
                   jax.ShapeDtypeStruct((ROUTER_LANES, 1), F32)],
        scratch_shapes=[pltpu.VMEM((ROUTER_LANES, 1), F32)],
        compiler_params=_cparams("arbitrary"),
        name="router",
    )(x, lw["router_hi"], lw["router_lo"], lw["router_b"])


def _rows_copy(src, dst, sem):
    return pltpu.make_async_copy(src, dst, sem)


def _dispatch_kernel(slot_ref, x_ref, xs_ref, stage, sems, *, td):
    i = pl.program_id(0)
    n = pl.num_programs(0)
    sb = lax.rem(i, 2)
    x = x_ref[...]
    for j in range(ROW_CHUNKS):
        stage[sb, pl.ds(j, td, stride=ROW_CHUNKS), :] = x[:, j * LANES:(j + 1) * LANES]

    def issue(rb, carry):
        for q in range(ISSUE_UNROLL):
            r = rb * ISSUE_UNROLL + q
            src = stage.at[sb, pl.ds(pl.multiple_of(r * ROW_CHUNKS, ROW_CHUNKS), ROW_CHUNKS)]
            s = slot_ref[0, 0, r]
            dst = xs_ref.at[pl.ds(pl.multiple_of(s * ROW_CHUNKS, ROW_CHUNKS), ROW_CHUNKS)]
            _rows_copy(src, dst, sems.at[sb]).start(priority=q % DMA_QUEUES)
        return carry

    lax.fori_loop(0, td // ISSUE_UNROLL, issue, 0)

    def drain(buf):
        _rows_copy(stage.at[buf], xs_ref.at[pl.ds(0, td * ROW_CHUNKS)], sems.at[buf]).wait()

    @pl.when(i > 0)
    def _():
        drain(1 - sb)

    @pl.when(i == n - 1)
    def _():
        drain(sb)


def _dispatch(x, slots, n_slots, *, td=512):
    t = x.shape[0]
    slots3 = slots.reshape(t // td, 1, td)
    kern = functools.partial(_dispatch_kernel, td=td)
    return pl.pallas_call(
        kern,
        grid=(t // td,),
        in_specs=[pl.BlockSpec((1, 1, td), lambda i: (i, 0, 0), memory_space=pltpu.SMEM),
                  pl.BlockSpec((td, D_MODEL), lambda i: (i, 0))],
        out_specs=pl.BlockSpec(memory_space=pl.ANY),
        out_shape=jax.ShapeDtypeStruct((n_slots * ROW_CHUNKS, LANES), F32),
        scratch_shapes=[pltpu.VMEM((2, td * ROW_CHUNKS, LANES), F32), pltpu.SemaphoreType.DMA((2,))],
        compiler_params=_cparams("arbitrary"),
        name="moe_dispatch",
    )(slots3, x)


def _expert_kernel(ea_ref, eb_ref, rows_ref, next_ref, nv_ref, xs_ref, rhi_ref, rb_ref, wg_hbm, wu_hbm, wd_hbm,
                   os_ref, wg_f32, wu_f32, wd_f32, wg_bf, wu_bf, wd_bf, sems, *, layer):
    i = pl.program_id(0)
    first = i * TILES_PER_STEP
    valid = first < nv_ref[0]
    group = ea_ref[first] // EXPERTS_PER_GROUP
    prev_group = ea_ref[jnp.maximum(first - 1, 0)] // EXPERTS_PER_GROUP
    new_group = jnp.logical_or(i == 0, group != prev_group)

    def weight_copies(grp):
        experts = pl.ds(grp * EXPERTS_PER_GROUP, EXPERTS_PER_GROUP)
        return [pltpu.make_async_copy(src.at[layer, experts], dst, sems.at[k])
                for k, (src, dst) in enumerate(((wg_hbm, wg_f32), (wu_hbm, wu_f32), (wd_hbm, wd_f32)))]

    @pl.when(jnp.logical_and(valid, i == 0))
    def _():
        for c in weight_copies(group):
            c.start()

    @pl.when(jnp.logical_and(valid, new_group))
    def _():
        for c in weight_copies(group):
            c.wait()
        for e in range(EXPERTS_PER_GROUP):
            wg_bf[e] = wg_f32[e].astype(BF16)
            wu_bf[e] = wu_f32[e].astype(BF16)
            wd_bf[e] = wd_f32[e].astype(BF16)
        nxt = next_ref[first]

        @pl.when(nxt >= 0)
        def _():
            for c in weight_copies(nxt):
                c.start()

    def body(row0, ea, eb, m):
        x = jnp.concatenate([xs_ref[pl.ds(row0 + j, m, stride=ROW_CHUNKS), :] for j in range(ROW_CHUNKS)],
                            axis=1)
        xt = x.astype(BF16)
        logits = jnp.dot(xt, rhi_ref[...], preferred_element_type=F32) + rb_ref[...]
        lane = lax.broadcasted_iota(jnp.int32, logits.shape, 1)
        pick = lambda l: jnp.sum(jnp.where(lane == l, logits, 0.0), axis=-1, keepdims=True)
        glog = _group_logits(logits, lane)
        gmax = jnp.max(glog, axis=-1, keepdims=True)
        gw = jnp.exp(pick(_GROUP_LANE0 + group) - gmax) / jnp.sum(jnp.exp(glog - gmax), axis=-1, keepdims=True)
        la = pick(ea)
        lb = pick(eb)
        lmax = jnp.maximum(la, lb)
        pa = jnp.exp(la - lmax)
        pb = jnp.exp(lb - lmax)
        scale = gw / (pa + pb)

        y = None
        for e, w in ((ea, pa * scale), (eb, pb * scale)):
            el = e - group * EXPERTS_PER_GROUP
            gate = jnp.dot(xt, wg_bf[el], preferred_element_type=F32)
            up = jnp.dot(xt, wu_bf[el], preferred_element_type=F32)
            h = (gate * (0.5 + 0.5 * jnp.tanh(0.5 * gate)) * up).astype(BF16)
            term = w * jnp.dot(h, wd_bf[el], preferred_element_type=F32)
            y = term if y is None else y + term
        for j in range(ROW_CHUNKS):
            os_ref[pl.ds(row0 + j, m, stride=ROW_CHUNKS), :] = y[:, j * LANES:(j + 1) * LANES]

    for s in range(TILES_PER_STEP):
        tile = first + s
        row_steps = (rows_ref[tile] + EXPERT_ROW_STEP - 1) // EXPERT_ROW_STEP
        for k in range(1, EXPERT_TILE // EXPERT_ROW_STEP + 1):
            pl.when(jnp.logical_and(valid, row_steps == k))(
                functools.partial(body, s * EXPERT_TILE * ROW_CHUNKS, ea_ref[tile], eb_ref[tile],
                                  k * EXPERT_ROW_STEP))


def _experts(xs, tile_ea, tile_eb, tile_rows, tile_next_group, n_valid, lw):
    layer = lw["layer"]
    n_steps = xs.shape[0] // (TILES_PER_STEP * EXPERT_TILE * ROW_CHUNKS)
    rows = TILES_PER_STEP * EXPERT_TILE * ROW_CHUNKS
    act = pl.BlockSpec((rows, LANES),
                       lambda i, ea, eb, nr, nx, nv: (jnp.minimum(i, nv[0] // TILES_PER_STEP - 1), 0))
    router = lambda a: pl.BlockSpec((None,) + a.shape[1:], lambda i, ea, eb, nr, nx, nv: (layer, 0, 0),
                                    pipeline_mode=pl.Buffered(1))
    in_hbm = pl.BlockSpec(memory_space=pl.ANY)
    gate_shape = (EXPERTS_PER_GROUP, D_MODEL, D_EXPERT)
    down_shape = (EXPERTS_PER_GROUP, D_EXPERT, D_MODEL)
    grid_spec = pltpu.PrefetchScalarGridSpec(
        num_scalar_prefetch=5,
        grid=(n_steps,),
        in_specs=[act, router(lw["router_hi"]), router(lw["router_b"]), in_hbm, in_hbm, in_hbm],
        out_specs=act,
        scratch_shapes=[pltpu.VMEM(gate_shape, F32), pltpu.VMEM(gate_shape, F32), pltpu.VMEM(down_shape, F32),
                        pltpu.VMEM(gate_shape, BF16), pltpu.VMEM(gate_shape, BF16), pltpu.VMEM(down_shape, BF16),
                        pltpu.SemaphoreType.DMA((3,))],
    )
    return pl.pallas_call(
        functools.partial(_expert_kernel, layer=layer),
        grid_spec=grid_spec,
        out_shape=jax.ShapeDtypeStruct(xs.shape, F32),
        compiler_params=_cparams("arbitrary", vmem_limit_bytes=EXPERT_VMEM_LIMIT_BYTES),
        name="moe_experts",
    )(tile_ea, tile_eb, tile_rows, tile_next_group, n_valid, xs, lw["router_hi"], lw["router_b"],
      lw["w_gate"], lw["w_up"], lw["w_down"])


def _gather_rows_issue(slot_ref, os_ref, gbuf, sems, buf, tc):
    def issue(rb, carry):
        for q in range(ISSUE_UNROLL):
            r = rb * ISSUE_UNROLL + q
            s = slot_ref[0, 0, r]
            src = os_ref.at[pl.ds(pl.multiple_of(s * ROW_CHUNKS, ROW_CHUNKS), ROW_CHUNKS)]
            dst = gbuf.at[buf, pl.ds(pl.multiple_of(r * ROW_CHUNKS, ROW_CHUNKS), ROW_CHUNKS)]
            _rows_copy(src, dst, sems.at[buf]).start(priority=q % DMA_QUEUES)
        return carry

    lax.fori_loop(0, tc // ISSUE_UNROLL, issue, 0)


def _combine_kernel(slot_ref, slot_next_ref, x_ref, g_ref, bt_ref, os_ref, xp_ref, xd_ref,
                    gbuf, sems, *, tc, n_first):
    i = pl.program_id(0)
    n = pl.num_programs(0)
    sb = lax.rem(i, 2)

    @pl.when(i == 0)
    def _():
        _gather_rows_issue(slot_ref, os_ref, gbuf, sems, 0, tc)

    @pl.when(i + 1 < n)
    def _():
        _gather_rows_issue(slot_next_ref, os_ref, gbuf, sems, 1 - sb, tc)

    _rows_copy(os_ref.at[pl.ds(0, tc * ROW_CHUNKS)], gbuf.at[sb], sems.at[sb]).wait()
    y = jnp.concatenate([gbuf[sb, pl.ds(j, tc, stride=ROW_CHUNKS), :] for j in range(ROW_CHUNKS)], axis=1)
    x2 = _layer_norm(DEEPNORM_ALPHA * x_ref[...] + y, g_ref[...], bt_ref[...])

    @pl.when(i < n_first)
    def _():
        xp_ref[...] = x2

    @pl.when(i >= n_first)
    def _():
        xd_ref[...] = x2


def _combine(x, slots, os_, lw, *, n_prompt, tc=512):
    t = x.shape[0]
    nblk = t // tc
    n_first = n_prompt // tc
    slots3 = slots.reshape(nblk, 1, tc)
    kern = functools.partial(_combine_kernel, tc=tc, n_first=n_first)
    smem_tile = lambda f: pl.BlockSpec((1, 1, tc), f, memory_space=pltpu.SMEM)
    return pl.pallas_call(
        kern,
        grid=(nblk,),
        in_specs=[smem_tile(lambda i: (i, 0, 0)),
                  smem_tile(lambda i: (jnp.minimum(i + 1, nblk - 1), 0, 0)),
                  pl.BlockSpec((tc, D_MODEL), lambda i: (i, 0)),
                  _layer_resident(lw["ln2_g"], lw["layer"]), _layer_resident(lw["ln2_b"], lw["layer"]),
                  pl.BlockSpec(memory_space=pl.ANY)],
        out_specs=[pl.BlockSpec((tc, D_MODEL), lambda i: (jnp.minimum(i, n_first - 1), 0)),
                   pl.BlockSpec((tc, D_MODEL), lambda i: (jnp.maximum(i - n_first, 0), 0))],
        out_shape=[jax.ShapeDtypeStruct((n_prompt, D_MODEL), F32),
                   jax.ShapeDtypeStruct((t - n_prompt, D_MODEL), F32)],
        scratch_shapes=[pltpu.VMEM((2, tc * ROW_CHUNKS, LANES), F32), pltpu.SemaphoreType.DMA((2,))],
        compiler_params=_cparams("arbitrary"),
        name="moe_combine",
    )(slots3, slots3, x, lw["ln2_g"], lw["ln2_b"], os_)


def _moe(x, lw, *, n_prompt):
    t = x.shape[0]
    n_tiles = t // EXPERT_TILE + N_CLASSES + N_EXPERT_GROUPS * (TILES_PER_STEP - 1)
    n_tiles = -(-n_tiles // TILES_PER_STEP) * TILES_PER_STEP
    meta, cnt = _router(x, lw)
    cls = meta[:, 0, :].reshape(t).astype(jnp.int32)
    rank = meta[:, 1, :].reshape(t).astype(jnp.int32)
    counts = cnt[:N_CLASSES, 0].astype(jnp.int32)
    seg_tiles = (counts + EXPERT_TILE - 1) // EXPERT_TILE
    group_tiles = jnp.sum(seg_tiles.reshape(N_EXPERT_GROUPS, PAIRS_PER_GROUP), axis=1)
    group_pad = (-group_tiles) % TILES_PER_STEP
    last_of_group = jnp.arange(N_CLASSES) % PAIRS_PER_GROUP == PAIRS_PER_GROUP - 1
    seg_tiles = seg_tiles + jnp.where(last_of_group, jnp.repeat(group_pad, PAIRS_PER_GROUP), 0)
    tile_ends = jnp.cumsum(seg_tiles)
    seg_start = (tile_ends - seg_tiles) * EXPERT_TILE
    class_ids = jnp.arange(N_CLASSES, dtype=jnp.int32)
    slots = jnp.sum(jnp.where(cls[:, None] == class_ids, seg_start, 0), axis=-1) + rank
    n_valid = tile_ends[-1:]
    tile_ids = jnp.minimum(jnp.arange(n_tiles, dtype=jnp.int32), n_valid[0] - 1)
    tile_class = jnp.sum((tile_ends[None, :] <= tile_ids[:, None]).astype(jnp.int32), axis=1)
    of_class = tile_class[:, None] == class_ids
    class_experts = jnp.asarray(_CLASS_EXPERTS, dtype=jnp.int32)
    tile_ea = jnp.sum(jnp.where(of_class, class_experts[:, 0], 0), axis=-1)
    tile_eb = jnp.sum(jnp.where(of_class, class_experts[:, 1], 0), axis=-1)
    rows_left = jnp.sum(jnp.where(of_class, seg_start + counts, 0), axis=-1) - tile_ids * EXPERT_TILE
    tile_rows = jnp.clip(rows_left, 0, EXPERT_TILE)
    group_ids = jnp.arange(N_EXPERT_GROUPS, dtype=jnp.int32)
    later_used = (group_ids[None, :] > group_ids[:, None]) & ((group_tiles + group_pad) > 0)[None, :]
    next_group = jnp.min(jnp.where(later_used, group_ids[None, :], N_EXPERT_GROUPS), axis=1)
    next_group = jnp.where(next_group == N_EXPERT_GROUPS, -1, next_group)
    tile_group = tile_ea // EXPERTS_PER_GROUP
    tile_next_group = jnp.sum(jnp.where(tile_group[:, None] == group_ids, next_group, 0), axis=-1)
    xs = _dispatch(x, slots, n_tiles * EXPERT_TILE)
    os_ = _experts(xs, tile_ea, tile_eb, tile_rows, tile_next_group, n_valid.astype(jnp.int32), lw)
    return _combine(x, slots, os_, lw, n_prompt=n_prompt)


def _t5_bucket(dist):
    n = jnp.maximum(dist, 0)
    max_exact = REL_BUCKETS // 2
    nf = jnp.maximum(n, 1).astype(F32)
    large = max_exact + (jnp.log(nf / max_exact) / math.log(REL_MAX_DISTANCE / max_exact)
                         * (REL_BUCKETS - max_exact)).astype(jnp.int32)
    large = jnp.minimum(large, REL_BUCKETS - 1)
    return jnp.where(n < max_exact, n, large)


def _head_rows(per_head, rows_of):
    out = []
    for kvh in range(N_KV_HEADS):
        out.append(jnp.stack([jnp.concatenate([jnp.repeat(per_head[..., h, :, :], rep, axis=-2)
                                               for h, rep in rows_of(kvh, par)], axis=-2)
                              for par in range(2)], axis=-3))
    return jnp.stack(out, axis=-4)


def _prompt_rows(kvh, par):
    return [(GROUP * kvh + 2 * p + par, 1) for p in range(2)]


def _head_bias(rel_bias, dist, valid):
    bias = jnp.where(valid[..., None], rel_bias[_t5_bucket(dist)].astype(F32), NEG_INF)
    return jnp.moveaxis(bias, -1, 0)


def _prompt_bias(rel_bias):
    period = 4 * WINDOW
    off = jnp.arange(period)
    off = jnp.where(off < 2 * WINDOW, off, off - period)
    dist = WINDOW - off
    per_off = _head_bias(rel_bias, dist, (dist >= 0) & (dist <= WINDOW))
    tiled = jnp.tile(per_off, (1, WINDOW))[:, :WINDOW * (period - 1)]
    return tiled.reshape(N_HEADS, WINDOW, period - 1)[:, :, :2 * WINDOW]


def _attention_tables(rel_bias, sinks, ntok, g):
    bias_p = _head_rows(_prompt_bias(rel_bias), _prompt_rows)

    decode_rows = lambda kvh, par: [(GROUP * kvh + 2 * p + par, g) for p in range(2)]
    tq = jnp.arange(ntok)[:, None]
    kpos_old = jnp.arange(WINDOW)[None, :] - WINDOW
    kpos_new = jnp.arange(ntok)[None, :]
    tables = []
    for kpos in (kpos_old, kpos_new):
        d = tq - kpos
        per_head = _head_bias(rel_bias, d, (d >= 0) & (d <= WINDOW) & (PAST_LEN + kpos >= 0))
        tables.append(_head_rows(per_head, decode_rows))
    bias_d = tables[0]
    biasn_d = jnp.moveaxis(tables[1], -1, 2)[..., None]

    s = sinks.astype(F32)[:, :, None, None]
    sink_p = _head_rows(jnp.broadcast_to(s, s.shape[:2] + (WINDOW, 1)), _prompt_rows)
    sink_d = _head_rows(jnp.broadcast_to(s, s.shape[:2] + (ntok, 1)), decode_rows)
    return bias_p, bias_d, biasn_d, sink_p, sink_d


def _stacked_weights(w_in, pool_w, pool_scale, proj_a, proj_b, w_out, ln1_g, ln1_b, router_group_w,
                     router_group_b, router_expert_w, router_expert_b, expert_w_gate, expert_w_up,
                     expert_w_down, ln2_g, ln2_b):
    depth = w_in.shape[0]
    same_group = jnp.eye(N_POOL_GROUPS, dtype=bool)[None, :, None, :, None]
    pool_bd = jnp.where(same_group, pool_w[:, :, :, None, :], 0.0).reshape(depth, D_POOL, D_POOL)
    pad = ROUTER_LANES - N_EXPERTS - N_EXPERT_GROUPS
    rw = jnp.concatenate([router_expert_w, router_group_w, jnp.zeros((depth, D_MODEL, pad), F32)], axis=-1)
    rb = jnp.concatenate([router_expert_b, router_group_b, jnp.zeros((depth, pad), F32)], axis=-1)
    r_hi = rw.astype(BF16)
    row = lambda a: a[:, None, :]
    return dict(
        w_in=w_in.astype(BF16), pool_w=pool_bd.astype(BF16), pool_scale=row(pool_scale),
        proj_a=proj_a.astype(BF16), proj_b=proj_b.astype(BF16), w_out=w_out.astype(BF16),
        ln1_g=row(ln1_g), ln1_b=row(ln1_b),
        router_hi=r_hi, router_lo=(rw - r_hi.astype(F32)).astype(BF16), router_b=row(rb),
        w_gate=expert_w_gate, w_up=expert_w_up, w_down=expert_w_down,
        ln2_g=row(ln2_g), ln2_b=row(ln2_b))


def _decode_mixer(xs_tn, lw, cache_k, cache_v, hist_t, tables, buf, new_caches, *, nseq, ntok, g, row_block):
    bias_d, biasn_d, sink_d = tables
    u = _in_proj(xs_tn, lw)
    u3 = u.reshape(ntok, nseq, D_IN)
    b, new_k, new_v = _decode_attn(u3, cache_k, cache_v, bias_d, biasn_d, sink_d, new_caches, g=g,
                                   layer=lw["layer"])
    buf = _decode_tail(xs_tn, u, hist_t, b.reshape(ntok * nseq, Q_DIM), lw, buf, nseq=nseq, ntok=ntok,
                       row_block=row_block)
    return buf, u3, (new_k, new_v)


def kernel(x_prompt, x_sample, cache_k, cache_v, state_pool, rel_bias, w_in, pool_w, pool_scale, proj_a, proj_b, w_out, attn_sinks, ln1_g, ln1_b, router_group_w, router_group_b, router_expert_w, router_expert_b, expert_w_gate, expert_w_up, expert_w_down, ln2_g, ln2_b):
    bsz, seq, _ = x_prompt.shape
    nseq, ntok, _ = x_sample.shape
    g = 16
    depth = w_in.shape[0]
    n_prompt = bsz * seq
    n_decode = ntok * nseq
    assert n_prompt % n_decode == 0
    bias_p, bias_d, biasn_d, sink_p, sink_d = _attention_tables(rel_bias, attn_sinks, ntok, g)
    weights = _stacked_weights(w_in, pool_w, pool_scale, proj_a, proj_b, w_out, ln1_g, ln1_b, router_group_w,
                               router_group_b, router_expert_w, router_expert_b, expert_w_gate, expert_w_up,
                               expert_w_down, ln2_g, ln2_b)

    xp = x_prompt.reshape(n_prompt, D_MODEL)
    xs_tn = jnp.swapaxes(x_sample, 0, 1).reshape(n_decode, D_MODEL)
    hist_t = jnp.swapaxes(state_pool, 1, 2)
    ck_t = jnp.transpose(cache_k, (0, 1, 3, 4, 2))
    cv_t = jnp.transpose(cache_v, (0, 1, 3, 4, 2))
    prompt_state, decode_u, new_caches = [], [], None
    for l in range(depth):
        lw = dict(weights, layer=l)
        x1, kst, vst, pst = _prompt_mixer(xp, lw, bias_p, sink_p, bsz=bsz, n_total=n_prompt + n_decode)
        x1, u3, new_caches = _decode_mixer(xs_tn, lw, ck_t, cv_t, hist_t, (bias_d, biasn_d, sink_d), x1,
                                           new_caches, nseq=nseq, ntok=ntok, g=g,
                                           row_block=n_prompt // n_decode)
        xp, xs_tn = _moe(x1, lw, n_prompt=n_prompt)
        prompt_state.append((kst, vst, pst))
        decode_u.append(u3)

    pk, pv, pp = (jnp.stack(parts) for parts in zip(*prompt_state))
    heads = lambda a: a.reshape(a.shape[:3] + (N_KV_HEADS, HEAD_DIM))
    new_pool = jnp.stack([jnp.swapaxes(u3[:, :, :D_POOL], 0, 1) for u3 in decode_u])
    sample_pool = jnp.concatenate([state_pool[:, :, ntok:], new_pool], axis=2)
    sample_k, sample_v = (jnp.transpose(c, (0, 1, 4, 2, 3)) for c in new_caches)
    y_sample = jnp.swapaxes(xs_tn.reshape(ntok, nseq, D_MODEL), 0, 1)
    return (xp.reshape(bsz, seq, D_MODEL), y_sample, heads(pk), heads(pv), pp[:, :, -POOL_HIST:],
            sample_k, sample_v, sample_pool)
```

```python
import functools
import math

import jax
import jax.numpy as jnp
from jax import lax
from jax.experimental import pallas as pl
from jax.experimental.pallas import tpu as pltpu

D_MODEL = 1024
DEPTH = 2
PAST_LEN = 16384
D_POOL = D_MODEL // 2
POOL_WINDOWS = (2, 4, 8, 16)
N_POOL_GROUPS = len(POOL_WINDOWS)
POOL_GROUP_DIM = D_POOL // N_POOL_GROUPS
POOL_HIST = max(POOL_WINDOWS) - 1
assert all(w & (w - 1) == 0 for w in POOL_WINDOWS)
HEAD_DIM = 64
N_HEADS = (D_MODEL // 2) // HEAD_DIM
N_KV_HEADS = 2
GROUP = N_HEADS // N_KV_HEADS
Q_DIM = N_HEADS * HEAD_DIM
KV_DIM = N_KV_HEADS * HEAD_DIM
WINDOW = 128
ATTN_SCALE = HEAD_DIM ** -0.5
assert math.log2(ATTN_SCALE).is_integer()
REL_BUCKETS = 32
REL_MAX_DISTANCE = 128
D_IN = D_POOL + Q_DIM + 2 * KV_DIM + 2 * D_MODEL
N_EXPERT_GROUPS = 4
EXPERTS_PER_GROUP = 8
N_EXPERTS = N_EXPERT_GROUPS * EXPERTS_PER_GROUP
D_EXPERT = D_MODEL // 4
DEEPNORM_ALPHA = (2 * DEPTH) ** 0.25
LN_EPS = 1e-5

LANES = 128
SUBLANES = 8
VMEM_LIMIT_BYTES = 56 * 1024 * 1024
EXPERT_VMEM_LIMIT_BYTES = 62 * 1024 * 1024

_O_Q = D_POOL
_O_K = _O_Q + Q_DIM
_O_V = _O_K + KV_DIM
_O_GA = _O_V + KV_DIM
_O_GB = _O_GA + D_MODEL

ROW_CHUNKS = D_MODEL // LANES
EXPERT_TILE = 192
OUT_PROJ_CHUNKS = 2
EXPERT_ROW_STEP = 32
TILES_PER_STEP = 4
DMA_QUEUES = 2
ISSUE_UNROLL = 16
PAIRS_PER_GROUP = EXPERTS_PER_GROUP * (EXPERTS_PER_GROUP - 1) // 2
N_CLASSES = N_EXPERT_GROUPS * PAIRS_PER_GROUP
_CLASS_EXPERTS = tuple((g * EXPERTS_PER_GROUP + a, g * EXPERTS_PER_GROUP + b)
                       for g in range(N_EXPERT_GROUPS)
                       for a in range(EXPERTS_PER_GROUP) for b in range(a + 1, EXPERTS_PER_GROUP))
ROUTER_LANES = LANES
_GROUP_LANE0 = N_EXPERTS
assert EXPERTS_PER_GROUP == SUBLANES and N_EXPERT_GROUPS <= SUBLANES

BF16 = jnp.bfloat16
F32 = jnp.float32
NEG_INF = float("-inf")


def _cparams(*sem, flags=None, vmem_limit_bytes=VMEM_LIMIT_BYTES):
    return pltpu.CompilerParams(dimension_semantics=sem, vmem_limit_bytes=vmem_limit_bytes, flags=flags)


def _resident(shape):
    nd = len(shape)
    return pl.BlockSpec(shape, lambda *_: (0,) * nd, pipeline_mode=pl.Buffered(1))


def _layer_resident(stacked, layer):
    nd = stacked.ndim - 1
    return pl.BlockSpec((None,) + stacked.shape[1:], lambda *_: (layer,) + (0,) * nd,
                        pipeline_mode=pl.Buffered(1))


def _layer_norm(y, g, b):
    mu = jnp.mean(y, axis=-1, keepdims=True)
    yc = y - mu
    var = jnp.mean(yc * yc, axis=-1, keepdims=True)
    return yc * lax.rsqrt(var + LN_EPS) * g + b


def _softmax_pv(s, sink, v_bf16):
    m = jnp.maximum(jnp.max(s, axis=-1, keepdims=True), sink)
    p = jnp.exp(s - m)
    den = jnp.sum(p, axis=-1, keepdims=True) + jnp.exp(sink - m)
    o = jnp.dot(p.astype(BF16), v_bf16, preferred_element_type=F32)
    return o / den


def _half_lane_variants(t):
    lane = lax.broadcasted_iota(jnp.int32, t.shape, 1)
    lo = lane < HEAD_DIM
    tr = pltpu.roll(t, HEAD_DIM, 1)
    z = jnp.zeros_like(t)
    return ((jnp.where(lo, t, z), jnp.where(lo, z, tr)),
            (jnp.where(lo, tr, z), jnp.where(lo, z, t)))


def _mix_tail(x_ref, out_ref, a_in, b_in, ga, gb, pw, ps, pa, pb, wo, g, bt):
    a = jnp.dot(a_in.astype(BF16), pw, preferred_element_type=F32) * ps
    pa_o = jnp.dot(a.astype(BF16), pa, preferred_element_type=F32)
    pb_o = jnp.dot(b_in.astype(BF16), pb, preferred_element_type=F32)
    sigmoid = lambda z: 0.5 + 0.5 * jnp.tanh(0.5 * z)
    merged = (sigmoid(ga) * pa_o + sigmoid(gb) * pb_o).astype(BF16)
    rows = x_ref.shape[0]
    chunk = rows // OUT_PROJ_CHUNKS
    for r0 in range(0, rows, chunk):
        mix = jnp.dot(merged[r0:r0 + chunk], wo, preferred_element_type=F32)
        out_ref[r0:r0 + chunk, :] = _layer_norm(DEEPNORM_ALPHA * x_ref[r0:r0 + chunk, :] + mix, g, bt)


def _prompt_mixer_kernel(x_ref, win_ref, pw_ref, ps_ref, pa_ref, pb_ref, wo_ref, bias_ref, sink_ref,
                         g_ref, bt_ref, x1_ref, kst_ref, vst_ref, pst_ref, kt_scr, v_scr, ext_scr, *, tm):
    i = pl.program_id(1)
    hist = 2 * SUBLANES

    @pl.when(i == 0)
    def _():
        kt_scr[:, :, :WINDOW] = jnp.zeros((4, LANES, WINDOW), BF16)
        v_scr[:, :WINDOW, :] = jnp.zeros((4, WINDOW, LANES), BF16)
        ext_scr[:hist, :] = jnp.zeros((hist, D_POOL), F32)

    @pl.when(i > 0)
    def _():
        kt_scr[:, :, :WINDOW] = kt_scr[:, :, tm:tm + WINDOW]
        v_scr[:, :WINDOW, :] = v_scr[:, tm:tm + WINDOW, :]
        ext_scr[:hist, :] = ext_scr[tm:tm + hist, :]

    u = jnp.dot(x_ref[...].astype(BF16), win_ref[...], preferred_element_type=F32)
    u_pool = u[:, :D_POOL]
    q = u[:, _O_Q:_O_K]
    k = u[:, _O_K:_O_V]
    v = u[:, _O_V:_O_GA]
    ga = u[:, _O_GA:_O_GB]
    gb = u[:, _O_GB:]

    kst_ref[...] = k[tm - WINDOW:, :]
    vst_ref[...] = v[tm - WINDOW:, :]
    pst_ref[...] = u_pool[tm - hist:, :]

    kt = k.T
    kz = jnp.zeros((HEAD_DIM, tm), F32)
    kt_scr[0, :, WINDOW:] = jnp.concatenate([kt[:HEAD_DIM], kz], axis=0).astype(BF16)
    kt_scr[1, :, WINDOW:] = jnp.concatenate([kz, kt[:HEAD_DIM]], axis=0).astype(BF16)
    kt_scr[2, :, WINDOW:] = jnp.concatenate([kt[HEAD_DIM:], kz], axis=0).astype(BF16)
    kt_scr[3, :, WINDOW:] = jnp.concatenate([kz, kt[HEAD_DIM:]], axis=0).astype(BF16)
    vv = _half_lane_variants(v)
    for kvh in range(N_KV_HEADS):
        for par in range(2):
            v_scr[2 * kvh + par, WINDOW:, :] = vv[kvh][par].astype(BF16)

    col = lax.broadcasted_iota(jnp.int32, (2 * WINDOW, 2 * WINDOW), 1)
    no_prev = col < jnp.where(i == 0, WINDOW, 0)
    qb = (q * ATTN_SCALE).astype(BF16)
    b_rows = []
    for j in range(tm // WINDOW):
        r0 = j * WINDOW
        o_kvh = []
        for kvh in range(N_KV_HEADS):
            c0 = kvh * 2 * LANES
            q2 = jnp.concatenate([qb[r0:r0 + WINDOW, c0:c0 + LANES],
                                  qb[r0:r0 + WINDOW, c0 + LANES:c0 + 2 * LANES]], axis=0)
            acc = None
            for par in range(2):
                s = jnp.dot(q2, kt_scr[2 * kvh + par, :, r0:r0 + 2 * WINDOW], preferred_element_type=F32)
                s = s + bias_ref[kvh, par]
                if j == 0:
                    s = jnp.where(no_prev, NEG_INF, s)
                o = _softmax_pv(s, sink_ref[kvh, par], v_scr[2 * kvh + par, r0:r0 + 2 * WINDOW, :])
                acc = o if acc is None else acc + o
            o_kvh.append(acc)
        b_rows.append(jnp.concatenate([o_kvh[0][:WINDOW], o_kvh[0][WINDOW:],
                                       o_kvh[1][:WINDOW], o_kvh[1][WINDOW:]], axis=1))
    b = jnp.concatenate(b_rows, axis=0)

    ext_scr[hist:, :] = u_pool
    pos = i * tm + lax.broadcasted_iota(jnp.int32, (tm, 1), 0)
    means = []
    for gi, w in enumerate(POOL_WINDOWS):
        c0 = gi * POOL_GROUP_DIM
        acc = ext_scr[:, c0:c0 + POOL_GROUP_DIM]
        d = 1
        while d < w:
            acc = acc + pltpu.roll(acc, d, 0)
            d *= 2
        inv_cnt = 1.0 / jnp.minimum(w, pos + 1).astype(F32)
        means.append(acc[hist:] * inv_cnt)
    pooled = jnp.concatenate(means, axis=1) - u_pool

    _mix_tail(x_ref, x1_ref, pooled, b, ga, gb, pw_ref[...], ps_ref[...], pa_ref[...], pb_ref[...],
              wo_ref[...], g_ref[...], bt_ref[...])


def _prompt_mixer(x, lw, bias_p, sink_t, *, bsz, n_total, tm=512):
    seq = x.shape[0] // bsz
    nt = seq // tm
    hist = 2 * SUBLANES
    kern = functools.partial(_prompt_mixer_kernel, tm=tm)
    tile = pl.BlockSpec((tm, D_MODEL), lambda b, i: (b * nt + i, 0))
    state = lambda r, w: pl.BlockSpec((None, r, w), lambda b, i: (b, 0, 0))
    params = (lw["w_in"], lw["pool_w"], lw["pool_scale"], lw["proj_a"], lw["proj_b"], lw["w_out"])
    tail = (sink_t, lw["ln1_g"], lw["ln1_b"])
    return pl.pallas_call(
        kern,
        grid=(bsz, nt),
        in_specs=([tile] + [_layer_resident(a, lw["layer"]) for a in params] + [_resident(bias_p.shape)]
                  + [_layer_resident(a, lw["layer"]) for a in tail]),
        out_specs=[tile, state(WINDOW, KV_DIM), state(WINDOW, KV_DIM), state(hist, D_POOL)],
        out_shape=[jax.ShapeDtypeStruct((n_total, D_MODEL), F32),
                   jax.ShapeDtypeStruct((bsz, WINDOW, KV_DIM), F32),
                   jax.ShapeDtypeStruct((bsz, WINDOW, KV_DIM), F32),
                   jax.ShapeDtypeStruct((bsz, hist, D_POOL), F32)],
        scratch_shapes=[pltpu.VMEM((4, LANES, WINDOW + tm), BF16),
                        pltpu.VMEM((4, WINDOW + tm, LANES), BF16),
                        pltpu.VMEM((hist + tm, D_POOL), F32)],
        compiler_params=_cparams("arbitrary", "arbitrary"),
        name="prompt_mixer",
    )(x, lw["w_in"], lw["pool_w"], lw["pool_scale"], lw["proj_a"], lw["proj_b"], lw["w_out"],
      bias_p, sink_t, lw["ln1_g"], lw["ln1_b"])


def _in_proj_kernel(x_ref, win_ref, u_ref):
    u_ref[...] = jnp.dot(x_ref[...].astype(BF16), win_ref[...], preferred_element_type=F32)


def _in_proj(x, lw):
    n = x.shape[0]
    return pl.pallas_call(
        _in_proj_kernel,
        grid=(1,),
        in_specs=[pl.BlockSpec(x.shape, lambda i: (0, 0)), _layer_resident(lw["w_in"], lw["layer"])],
        out_specs=pl.BlockSpec((n, D_IN), lambda i: (0, 0)),
        out_shape=jax.ShapeDtypeStruct((n, D_IN), F32),
        compiler_params=_cparams("arbitrary"),
        name="decode_in_proj",
    )(x, lw["w_in"])


def _decode_attn_kernel(q_ref, kn_ref, vn_ref, kc_ref, vc_ref, bias_ref, biasn_ref, sink_ref, *rest, g, ntok):
    b_ref, sk_ref, sv_ref, new_scr = rest[-4:]
    slabs = 2 * ntok
    rows = slabs * g

    lane = lax.broadcasted_iota(jnp.int32, (KV_DIM, WINDOW), 1)
    fill = jnp.zeros((LANES - g * ntok, KV_DIM), F32)
    for new_ref, old_ref, out_ref in ((kn_ref, kc_ref, sk_ref), (vn_ref, vc_ref, sv_ref)):
        for t in range(ntok):
            new_scr[pl.ds(t, g, stride=ntok), :] = new_ref[t]
        packed = jnp.concatenate([new_scr[...], fill], axis=0).T
        for n in range(g):
            tile = pltpu.roll(old_ref[n].reshape(KV_DIM, WINDOW), WINDOW - ntok, 1)
            cols = pltpu.roll(packed, (WINDOW - ntok - n * ntok) % LANES, 1)
            out_ref[n] = jnp.where(lane >= WINDOW - ntok, cols, tile).reshape(N_KV_HEADS, HEAD_DIM, WINDOW)

    def cache_variants(ref):
        z = jnp.zeros((HEAD_DIM, WINDOW), F32)
        out = []
        for kvh in range(N_KV_HEADS):
            out.append((jnp.concatenate([jnp.concatenate([ref[n, kvh], z], axis=0) for n in range(g)], axis=1),
                        jnp.concatenate([jnp.concatenate([z, ref[n, kvh]], axis=0) for n in range(g)], axis=1)))
        return out

    kk = cache_variants(kc_ref)
    vv = cache_variants(vc_ref)
    kn = [_half_lane_variants(kn_ref[t]) for t in range(ntok)]
    vn = [_half_lane_variants(vn_ref[t]) for t in range(ntok)]
    per_row = lambda a: jnp.concatenate([a] * slabs, axis=0)
    row_seq = lax.rem(lax.broadcasted_iota(jnp.int32, (rows, WINDOW), 0), g)
    diag = (lax.rem(lax.broadcasted_iota(jnp.int32, (rows, g * WINDOW), 0), g)
            == lax.broadcasted_iota(jnp.int32, (rows, g * WINDOW), 1) // WINDOW)
    for kvh in range(N_KV_HEADS):
        c0 = kvh * 2 * LANES
        q2f = jnp.concatenate([q_ref[t, :, c0 + p * LANES:c0 + (p + 1) * LANES]
                               for p in range(2) for t in range(ntok)], axis=0)
        q2 = q2f.astype(BF16)
        acc = None
        for par in range(2):
            full = jnp.dot(q2, kk[kvh][par].astype(BF16), preferred_element_type=F32)
            s = jnp.zeros((rows, WINDOW), F32)
            for n in range(g):
                s = jnp.where(row_seq == n, full[:, n * WINDOW:(n + 1) * WINDOW], s)
            s = s * ATTN_SCALE + bias_ref[kvh, par]
            s_new = [jnp.sum(q2f * per_row(kn[t][kvh][par]), axis=-1, keepdims=True) * ATTN_SCALE
                     + biasn_ref[kvh, par, t] for t in range(ntok)]
            sink = sink_ref[kvh, par]
            m = jnp.maximum(jnp.max(s, axis=-1, keepdims=True), sink)
            for t in range(ntok):
                m = jnp.maximum(m, s_new[t])
            p = jnp.exp(s - m)
            den = jnp.sum(p, axis=-1, keepdims=True) + jnp.exp(sink - m)
            p_bd = jnp.where(diag, jnp.concatenate([p] * g, axis=1), 0.0).astype(BF16)
            o = lax.dot_general(p_bd, vv[kvh][par].astype(BF16), (((1,), (1,)), ((), ())),
                                preferred_element_type=F32)
            for t in range(ntok):
                p_new = jnp.exp(s_new[t] - m)
                den = den + p_new
                o = o + p_new * per_row(vn[t][kvh][par])
            o = o / den
            acc = o if acc is None else acc + o
        for p in range(2):
            for t in range(ntok):
                r0 = (p * ntok + t) * g
                b_ref[t, :, c0 + p * LANES:c0 + (p + 1) * LANES] = acc[r0:r0 + g]


def _decode_attn(u3, cache_k, cache_v, bias_d, biasn_d, sink_d, new_caches, *, g, layer):
    ntok, nseq, _ = u3.shape
    cache = pl.BlockSpec((None, g, N_KV_HEADS, HEAD_DIM, WINDOW), lambda i: (layer, i, 0, 0, 0))
    kern = functools.partial(_decode_attn_kernel, g=g, ntok=ntok)
    new_cols = lambda w, off: pl.BlockSpec((ntok, g, w), lambda i: (0, i, off // w))
    ins = [u3, u3, u3, cache_k, cache_v, bias_d, biasn_d, sink_d]
    in_specs = [new_cols(Q_DIM, _O_Q), new_cols(KV_DIM, _O_K), new_cols(KV_DIM, _O_V), cache, cache,
                _resident(bias_d.shape), _resident(biasn_d.shape), _layer_resident(sink_d, layer)]
    aliases = {}
    if new_caches is not None:
        aliases = {len(ins): 1, len(ins) + 1: 2}
        ins += list(new_caches)
        in_specs += [pl.BlockSpec(memory_space=pl.ANY)] * 2
    return pl.pallas_call(
        kern,
        grid=(nseq // g,),
        in_specs=in_specs,
        out_specs=[pl.BlockSpec((ntok, g, Q_DIM), lambda i: (0, i, 0)), cache, cache],
        out_shape=[jax.ShapeDtypeStruct((ntok, nseq, Q_DIM), F32),
                   jax.ShapeDtypeStruct(cache_k.shape, F32), jax.ShapeDtypeStruct(cache_v.shape, F32)],
        input_output_aliases=aliases,
        scratch_shapes=[pltpu.VMEM((g * ntok, KV_DIM), F32)],
        compiler_params=_cparams("arbitrary"),
        name="decode_attn",
    )(*ins)


def _decode_tail_kernel(x_ref, u_ref, hist_ref, b_ref, pw_ref, ps_ref, pa_ref, pb_ref,
                        wo_ref, g_ref, bt_ref, buf_ref, x1_ref, *, nseq, ntok, start_pos):
    del buf_ref

    def ext_row(r, c0):
        if r < POOL_HIST:
            return hist_ref[r, :, c0:c0 + POOL_GROUP_DIM]
        return u_ref[(r - POOL_HIST) * nseq:(r - POOL_HIST + 1) * nseq, c0:c0 + POOL_GROUP_DIM]

    pooled_rows = []
    for t in range(ntok):
        means = []
        for gi, w in enumerate(POOL_WINDOWS):
            acc = None
            for d in range(w):
                r = POOL_HIST + t - d
                if start_pos - POOL_HIST + r < 0:
                    continue
                term = ext_row(r, gi * POOL_GROUP_DIM)
                acc = term if acc is None else acc + term
            means.append(acc / float(min(w, start_pos + t + 1)))
        pooled_rows.append(jnp.concatenate(means, axis=1))
    pooled = jnp.concatenate(pooled_rows, axis=0) - u_ref[:, :D_POOL]
    _mix_tail(x_ref, x1_ref, pooled, b_ref[...], u_ref[:, _O_GA:_O_GB], u_ref[:, _O_GB:], pw_ref[...],
              ps_ref[...], pa_ref[...], pb_ref[...], wo_ref[...], g_ref[...], bt_ref[...])


def _decode_tail(x, u, hist_t, b, lw, buf, *, nseq, ntok, row_block):
    n = ntok * nseq
    kern = functools.partial(_decode_tail_kernel, nseq=nseq, ntok=ntok, start_pos=PAST_LEN)
    whole = lambda a: pl.BlockSpec(a.shape, lambda i: (0,) * a.ndim)
    layered = lambda a: _layer_resident(a, lw["layer"])
    params = (lw["pool_w"], lw["pool_scale"], lw["proj_a"], lw["proj_b"], lw["w_out"], lw["ln1_g"], lw["ln1_b"])
    ins = (x, u, hist_t, b) + params
    return pl.pallas_call(
        kern,
        grid=(1,),
        in_specs=([whole(x), whole(u), layered(hist_t), whole(b)] + [layered(a) for a in params]
                  + [pl.BlockSpec(memory_space=pl.ANY)]),
        out_specs=pl.BlockSpec((n, D_MODEL), lambda i: (row_block, 0)),
        out_shape=jax.ShapeDtypeStruct(buf.shape, F32),
        input_output_aliases={len(ins): 0},
        compiler_params=_cparams("arbitrary"),
        name="decode_tail",
    )(*ins, buf)


def _router_logits(x, whi, wlo, br):
    xh = x.astype(BF16)
    xl = (x - xh.astype(F32)).astype(BF16)
    return (jnp.dot(xh, whi, preferred_element_type=F32)
            + (jnp.dot(xh, wlo, preferred_element_type=F32)
               + jnp.dot(xl, whi, preferred_element_type=F32))) + br


def _group_logits(logits, lane):
    is_group = jnp.logical_and(lane >= _GROUP_LANE0, lane < _GROUP_LANE0 + N_EXPERT_GROUPS)
    return jnp.where(is_group, logits, NEG_INF)


def _router_kernel(x_ref, whi_ref, wlo_ref, br_ref, meta_ref, cnt_ref, run_scr, *, tr):
    i = pl.program_id(0)

    @pl.when(i == 0)
    def _():
        run_scr[...] = jnp.zeros_like(run_scr)

    lt = _router_logits(x_ref[...], whi_ref[...], wlo_ref[...], br_ref[...]).T
    row = lax.broadcasted_iota(jnp.int32, (EXPERTS_PER_GROUP, tr), 0)
    none = jnp.int32(EXPERTS_PER_GROUP)

    def argmax_rows(vals):
        vmax = jnp.max(vals, axis=0, keepdims=True)
        return jnp.min(jnp.where(vals == vmax, row, none), axis=0, keepdims=True)

    glog = jnp.where(row < N_EXPERT_GROUPS, lt[_GROUP_LANE0:_GROUP_LANE0 + EXPERTS_PER_GROUP], NEG_INF)
    gidx = argmax_rows(glog)
    el = lt[:EXPERTS_PER_GROUP]
    for g in range(1, N_EXPERT_GROUPS):
        el = jnp.where(gidx == g, lt[g * EXPERTS_PER_GROUP:(g + 1) * EXPERTS_PER_GROUP], el)
    e1 = argmax_rows(el)
    e2 = argmax_rows(jnp.where(row == e1, NEG_INF, el))
    lo = jnp.minimum(e1, e2)
    hi = jnp.maximum(e1, e2)
    pair = lax.shift_right_logical(lo * (2 * EXPERTS_PER_GROUP - 1 - lo), 1) + (hi - lo - 1)
    cls = gidx * PAIRS_PER_GROUP + pair

    oh = jnp.where(lax.broadcasted_iota(jnp.int32, (ROUTER_LANES, tr), 0) == cls, 1.0, 0.0)
    rr = lax.broadcasted_iota(jnp.int32, (tr, tr), 0)
    cc = lax.broadcasted_iota(jnp.int32, (tr, tr), 1)
    earlier = jnp.where(rr < cc, 1.0, 0.0).astype(BF16)
    before = jnp.dot(oh.astype(BF16), earlier, preferred_element_type=F32) + run_scr[...]
    rank = jnp.sum(oh * before, axis=0, keepdims=True)
    run_scr[...] += jnp.sum(oh, axis=1, keepdims=True)
    cnt_ref[...] = run_scr[...]

    meta_ref[...] = jnp.where(row == 0, cls.astype(F32), jnp.where(row == 1, rank, 0.0))


def _router(x, lw, *, tr=512):
    t = x.shape[0]
    kern = functools.partial(_router_kernel, tr=tr)
    return pl.pallas_call(
        kern,
        grid=(t // tr,),
        in_specs=[pl.BlockSpec((tr, D_MODEL), lambda i: (i, 0)),
                  _layer_resident(lw["router_hi"], lw["layer"]), _layer_resident(lw["router_lo"], lw["layer"]),
                  _layer_resident(lw["router_b"], lw["layer"])],
        out_specs=[pl.BlockSpec((None, SUBLANES, tr), lambda i: (i, 0, 0)),
                   pl.BlockSpec((ROUTER_LANES, 1), lambda i: (0, 0))],
        out_shape=[jax.ShapeDtypeStruct((t // tr, SUBLANES, tr), F32),
                   jax.ShapeDtypeStruct((ROUTER_LANES, 1), F32)],
        scratch_shapes=[pltpu.VMEM((ROUTER_LANES, 1), F32)],
        compiler_params=_cparams("arbitrary"),
        name="router",
    )(x, lw["router_hi"], lw["router_lo"], lw["router_b"])


def _rows_copy(src, dst, sem):
    return pltpu.make_async_copy(src, dst, sem)


def _dispatch_kernel(slot_ref, x_ref, xs_ref, stage, sems, *, td):
    i = pl.program_id(0)
    n = pl.num_programs(0)
    sb = lax.rem(i, 2)
    x = x_ref[...]
    for j in range(ROW_CHUNKS):
        stage[sb, pl.ds(j, td, stride=ROW_CHUNKS), :] = x[:, j * LANES:(j + 1) * LANES]

    def issue(rb, carry):
        for q in range(ISSUE_UNROLL):
            r = rb * ISSUE_UNROLL + q
            src = stage.at[sb, pl.ds(pl.multiple_of(r * ROW_CHUNKS, ROW_CHUNKS), ROW_CHUNKS)]
            s = slot_ref[0, 0, r]
            dst = xs_ref.at[pl.ds(pl.multiple_of(s * ROW_CHUNKS, ROW_CHUNKS), ROW_CHUNKS)]
            _rows_copy(src, dst, sems.at[sb]).start(priority=q % DMA_QUEUES)
        return carry

    lax.fori_loop(0, td // ISSUE_UNROLL, issue, 0)

    def drain(buf):
        _rows_copy(stage.at[buf], xs_ref.at[pl.ds(0, td * ROW_CHUNKS)], sems.at[buf]).wait()

    @pl.when(i > 0)
    def _():
        drain(1 - sb)

    @pl.when(i == n - 1)
    def _():
        drain(sb)


def _dispatch(x, slots, n_slots, *, td=512):
    t = x.shape[0]
    slots3 = slots.reshape(t // td, 1, td)
    kern = functools.partial(_dispatch_kernel, td=td)
    return pl.pallas_call(
        kern,
        grid=(t // td,),
        in_specs=[pl.BlockSpec((1, 1, td), lambda i: (i, 0, 0), memory_space=pltpu.SMEM),
                  pl.BlockSpec((td, D_MODEL), lambda i: (i, 0))],
        out_specs=pl.BlockSpec(memory_space=pl.ANY),
        out_shape=jax.ShapeDtypeStruct((n_slots * ROW_CHUNKS, LANES), F32),
        scratch_shapes=[pltpu.VMEM((2, td * ROW_CHUNKS, LANES), F32), pltpu.SemaphoreType.DMA((2,))],
        compiler_params=_cparams("arbitrary"),
        name="moe_dispatch",
    )(slots3, x)


def _expert_kernel(ea_ref, eb_ref, rows_ref, next_ref, nv_ref, xs_ref, rhi_ref, rb_ref, wg_hbm, wu_hbm, wd_hbm,
                   os_ref, wg_f32, wu_f32, wd_f32, wg_bf, wu_bf, wd_bf, sems, *, layer):
    i = pl.program_id(0)
    first = i * TILES_PER_STEP
    valid = first < nv_ref[0]
    group = ea_ref[first] // EXPERTS_PER_GROUP
    prev_group = ea_ref[jnp.maximum(first - 1, 0)] // EXPERTS_PER_GROUP
    new_group = jnp.logical_or(i == 0, group != prev_group)

    def weight_copies(grp):
        experts = pl.ds(grp * EXPERTS_PER_GROUP, EXPERTS_PER_GROUP)
        return [pltpu.make_async_copy(src.at[layer, experts], dst, sems.at[k])
                for k, (src, dst) in enumerate(((wg_hbm, wg_f32), (wu_hbm, wu_f32), (wd_hbm, wd_f32)))]

    @pl.when(jnp.logical_and(valid, i == 0))
    def _():
        for c in weight_copies(group):
            c.start()

    @pl.when(jnp.logical_and(valid, new_group))
    def _():
        for c in weight_copies(group):
            c.wait()
        for e in range(EXPERTS_PER_GROUP):
            wg_bf[e] = wg_f32[e].astype(BF16)
            wu_bf[e] = wu_f32[e].astype(BF16)
            wd_bf[e] = wd_f32[e].astype(BF16)
        nxt = next_ref[first]

        @pl.when(nxt >= 0)
        def _():
            for c in weight_copies(nxt):
                c.start()

    def body(row0, ea, eb, m):
        x = jnp.concatenate([xs_ref[pl.ds(row0 + j, m, stride=ROW_CHUNKS), :] for j in range(ROW_CHUNKS)],
                            axis=1)
        xt = x.astype(BF16)
        logits = jnp.dot(xt, rhi_ref[...], preferred_element_type=F32) + rb_ref[...]
        lane = lax.broadcasted_iota(jnp.int32, logits.shape, 1)
        pick = lambda l: jnp.sum(jnp.where(lane == l, logits, 0.0), axis=-1, keepdims=True)
        glog = _group_logits(logits, lane)
        gmax = jnp.max(glog, axis=-1, keepdims=True)
        gw = jnp.exp(pick(_GROUP_LANE0 + group) - gmax) / jnp.sum(jnp.exp(glog - gmax), axis=-1, keepdims=True)
        la = pick(ea)
        lb = pick(eb)
        lmax = jnp.maximum(la, lb)
        pa = jnp.exp(la - lmax)
        pb = jnp.exp(lb - lmax)
        scale = gw / (pa + pb)

        y = None
        for e, w in ((ea, pa * scale), (eb, pb * scale)):
            el = e - group * EXPERTS_PER_GROUP
            gate = jnp.dot(xt, wg_bf[el], preferred_element_type=F32)
            up = jnp.dot(xt, wu_bf[el], preferred_element_type=F32)
            h = (gate * (0.5 + 0.5 * jnp.tanh(0.5 * gate)) * up).astype(BF16)
            term = w * jnp.dot(h, wd_bf[el], preferred_element_type=F32)
            y = term if y is None else y + term
        for j in range(ROW_CHUNKS):
            os_ref[pl.ds(row0 + j, m, stride=ROW_CHUNKS), :] = y[:, j * LANES:(j + 1) * LANES]

    for s in range(TILES_PER_STEP):
        tile = first + s
        row_steps = (rows_ref[tile] + EXPERT_ROW_STEP - 1) // EXPERT_ROW_STEP
        for k in range(1, EXPERT_TILE // EXPERT_ROW_STEP + 1):
            pl.when(jnp.logical_and(valid, row_steps == k))(
                functools.partial(body, s * EXPERT_TILE * ROW_CHUNKS, ea_ref[tile], eb_ref[tile],
                                  k * EXPERT_ROW_STEP))


def _experts(xs, tile_ea, tile_eb, tile_rows, tile_next_group, n_valid, lw):
    layer = lw["layer"]
    n_steps = xs.shape[0] // (TILES_PER_STEP * EXPERT_TILE * ROW_CHUNKS)
    rows = TILES_PER_STEP * EXPERT_TILE * ROW_CHUNKS
    act = pl.BlockSpec((rows, LANES),
                       lambda i, ea, eb, nr, nx, nv: (jnp.minimum(i, nv[0] // TILES_PER_STEP - 1), 0))
    router = lambda a: pl.BlockSpec((None,) + a.shape[1:], lambda i, ea, eb, nr, nx, nv: (layer, 0, 0),
                                    pipeline_mode=pl.Buffered(1))
    in_hbm = pl.BlockSpec(memory_space=pl.ANY)
    gate_shape = (EXPERTS_PER_GROUP, D_MODEL, D_EXPERT)
    down_shape = (EXPERTS_PER_GROUP, D_EXPERT, D_MODEL)
    grid_spec = pltpu.PrefetchScalarGridSpec(
        num_scalar_prefetch=5,
        grid=(n_steps,),
        in_specs=[act, router(lw["router_hi"]), router(lw["router_b"]), in_hbm, in_hbm, in_hbm],
        out_specs=act,
        scratch_shapes=[pltpu.VMEM(gate_shape, F32), pltpu.VMEM(gate_shape, F32), pltpu.VMEM(down_shape, F32),
                        pltpu.VMEM(gate_shape, BF16), pltpu.VMEM(gate_shape, BF16), pltpu.VMEM(down_shape, BF16),
                        pltpu.SemaphoreType.DMA((3,))],
    )
    return pl.pallas_call(
        functools.partial(_expert_kernel, layer=layer),
        grid_spec=grid_spec,
        out_shape=jax.ShapeDtypeStruct(xs.shape, F32),
        compiler_params=_cparams("arbitrary", vmem_limit_bytes=EXPERT_VMEM_LIMIT_BYTES),
        name="moe_experts",
    )(tile_ea, tile_eb, tile_rows, tile_next_group, n_valid, xs, lw["router_hi"], lw["router_b"],
      lw["w_gate"], lw["w_up"], lw["w_down"])


def _gather_rows_issue(slot_ref, os_ref, gbuf, sems, buf, tc):
    def issue(rb, carry):
        for q in range(ISSUE_UNROLL):
            r = rb * ISSUE_UNROLL + q
            s = slot_ref[0, 0, r]
            src = os_ref.at[pl.ds(pl.multiple_of(s * ROW_CHUNKS, ROW_CHUNKS), ROW_CHUNKS)]
            dst = gbuf.at[buf, pl.ds(pl.multiple_of(r * ROW_CHUNKS, ROW_CHUNKS), ROW_CHUNKS)]
            _rows_copy(src, dst, sems.at[buf]).start(priority=q % DMA_QUEUES)
        return carry

    lax.fori_loop(0, tc // ISSUE_UNROLL, issue, 0)


def _combine_kernel(slot_ref, slot_next_ref, x_ref, g_ref, bt_ref, os_ref, xp_ref, xd_ref,
                    gbuf, sems, *, tc, n_first):
    i = pl.program_id(0)
    n = pl.num_programs(0)
    sb = lax.rem(i, 2)

    @pl.when(i == 0)
    def _():
        _gather_rows_issue(slot_ref, os_ref, gbuf, sems, 0, tc)

    @pl.when(i + 1 < n)
    def _():
        _gather_rows_issue(slot_next_ref, os_ref, gbuf, sems, 1 - sb, tc)

    _rows_copy(os_ref.at[pl.ds(0, tc * ROW_CHUNKS)], gbuf.at[sb], sems.at[sb]).wait()
    y = jnp.concatenate([gbuf[sb, pl.ds(j, tc, stride=ROW_CHUNKS), :] for j in range(ROW_CHUNKS)], axis=1)
    x2 = _layer_norm(DEEPNORM_ALPHA * x_ref[...] + y, g_ref[...], bt_ref[...])

    @pl.when(i < n_first)
    def _():
        xp_ref[...] = x2

    @pl.when(i >= n_first)
    def _():
        xd_ref[...] = x2


def _combine(x, slots, os_, lw, *, n_prompt, tc=512):
    t = x.shape[0]
    nblk = t // tc
    n_first = n_prompt // tc
    slots3 = slots.reshape(nblk, 1, tc)
    kern = functools.partial(_combine_kernel, tc=tc, n_first=n_first)
    smem_tile = lambda f: pl.BlockSpec((1, 1, tc), f, memory_space=pltpu.SMEM)
    return pl.pallas_call(
        kern,
        grid=(nblk,),
        in_specs=[smem_tile(lambda i: (i, 0, 0)),
                  smem_tile(lambda i: (jnp.minimum(i + 1, nblk - 1), 0, 0)),
                  pl.BlockSpec((tc, D_MODEL), lambda i: (i, 0)),
                  _layer_resident(lw["ln2_g"], lw["layer"]), _layer_resident(lw["ln2_b"], lw["layer"]),
                  pl.BlockSpec(memory_space=pl.ANY)],
        out_specs=[pl.BlockSpec((tc, D_MODEL), lambda i: (jnp.minimum(i, n_first - 1), 0)),
                   pl.BlockSpec((tc, D_MODEL), lambda i: (jnp.maximum(i - n_first, 0), 0))],
        out_shape=[jax.ShapeDtypeStruct((n_prompt, D_MODEL), F32),
                   jax.ShapeDtypeStruct((t - n_prompt, D_MODEL), F32)],
        scratch_shapes=[pltpu.VMEM((2, tc * ROW_CHUNKS, LANES), F32), pltpu.SemaphoreType.DMA((2,))],
        compiler_params=_cparams("arbitrary"),
        name="moe_combine",
    )(slots3, slots3, x, lw["ln2_g"], lw["ln2_b"], os_)


def _moe(x, lw, *, n_prompt):
    t = x.shape[0]
    n_tiles = t // EXPERT_TILE + N_CLASSES + N_EXPERT_GROUPS * (TILES_PER_STEP - 1)
    n_tiles = -(-n_tiles // TILES_PER_STEP) * TILES_PER_STEP
    meta, cnt = _router(x, lw)
    cls = meta[:, 0, :].reshape(t).astype(jnp.int32)
    rank = meta[:, 1, :].reshape(t).astype(jnp.int32)
    counts = cnt[:N_CLASSES, 0].astype(jnp.int32)
    seg_tiles = (counts + EXPERT_TILE - 1) // EXPERT_TILE
    group_tiles = jnp.sum(seg_tiles.reshape(N_EXPERT_GROUPS, PAIRS_PER_GROUP), axis=1)
    group_pad = (-group_tiles) % TILES_PER_STEP
    last_of_group = jnp.arange(N_CLASSES) % PAIRS_PER_GROUP == PAIRS_PER_GROUP - 1
    seg_tiles = seg_tiles + jnp.where(last_of_group, jnp.repeat(group_pad, PAIRS_PER_GROUP), 0)
    tile_ends = jnp.cumsum(seg_tiles)
    seg_start = (tile_ends - seg_tiles) * EXPERT_TILE
    class_ids = jnp.arange(N_CLASSES, dtype=jnp.int32)
    slots = jnp.sum(jnp.where(cls[:, None] == class_ids, seg_start, 0), axis=-1) + rank
    n_valid = tile_ends[-1:]
    tile_ids = jnp.minimum(jnp.arange(n_tiles, dtype=jnp.int32), n_valid[0] - 1)
    tile_class = jnp.sum((tile_ends[None, :] <= tile_ids[:, None]).astype(jnp.int32), axis=1)
    of_class = tile_class[:, None] == class_ids
    class_experts = jnp.asarray(_CLASS_EXPERTS, dtype=jnp.int32)
    tile_ea = jnp.sum(jnp.where(of_class, class_experts[:, 0], 0), axis=-1)
    tile_eb = jnp.sum(jnp.where(of_class, class_experts[:, 1], 0), axis=-1)
    rows_left = jnp.sum(jnp.where(of_class, seg_start + counts, 0), axis=-1) - tile_ids * EXPERT_TILE
    tile_rows = jnp.clip(rows_left, 0, EXPERT_TILE)
    group_ids = jnp.arange(N_EXPERT_GROUPS, dtype=jnp.int32)
    later_used = (group_ids[None, :] > group_ids[:, None]) & ((group_tiles + group_pad) > 0)[None, :]
    next_group = jnp.min(jnp.where(later_used, group_ids[None, :], N_EXPERT_GROUPS), axis=1)
    next_group = jnp.where(next_group == N_EXPERT_GROUPS, -1, next_group)
    tile_group = tile_ea // EXPERTS_PER_GROUP
    tile_next_group = jnp.sum(jnp.where(tile_group[:, None] == group_ids, next_group, 0), axis=-1)
    xs = _dispatch(x, slots, n_tiles * EXPERT_TILE)
    os_ = _experts(xs, tile_ea, tile_eb, tile_rows, tile_next_group, n_valid.astype(jnp.int32), lw)
    return _combine(x, slots, os_, lw, n_prompt=n_prompt)


def _t5_bucket(dist):
    n = jnp.maximum(dist, 0)
    max_exact = REL_BUCKETS // 2
    nf = jnp.maximum(n, 1).astype(F32)
    large = max_exact + (jnp.log(nf / max_exact) / math.log(REL_MAX_DISTANCE / max_exact)
                         * (REL_BUCKETS - max_exact)).astype(jnp.int32)
    large = jnp.minimum(large, REL_BUCKETS - 1)
    return jnp.where(n < max_exact, n, large)


def _head_rows(per_head, rows_of):
    out = []
    for kvh in range(N_KV_HEADS):
        out.append(jnp.stack([jnp.concatenate([jnp.repeat(per_head[..., h, :, :], rep, axis=-2)
                                               for h, rep in rows_of(kvh, par)], axis=-2)
                              for par in range(2)], axis=-3))
    return jnp.stack(out, axis=-4)


def _prompt_rows(kvh, par):
    return [(GROUP * kvh + 2 * p + par, 1) for p in range(2)]


def _head_bias(rel_bias, dist, valid):
    bias = jnp.where(valid[..., None], rel_bias[_t5_bucket(dist)].astype(F32), NEG_INF)
    return jnp.moveaxis(bias, -1, 0)


def _prompt_bias(rel_bias):
    period = 4 * WINDOW
    off = jnp.arange(period)
    off = jnp.where(off < 2 * WINDOW, off, off - period)
    dist = WINDOW - off
    per_off = _head_bias(rel_bias, dist, (dist >= 0) & (dist <= WINDOW))
    tiled = jnp.tile(per_off, (1, WINDOW))[:, :WINDOW * (period - 1)]
    return tiled.reshape(N_HEADS, WINDOW, period - 1)[:, :, :2 * WINDOW]


def _attention_tables(rel_bias, sinks, ntok, g):
    bias_p = _head_rows(_prompt_bias(rel_bias), _prompt_rows)

    decode_rows = lambda kvh, par: [(GROUP * kvh + 2 * p + par, g) for p in range(2)]
    tq = jnp.arange(ntok)[:, None]
    kpos_old = jnp.arange(WINDOW)[None, :] - WINDOW
    kpos_new = jnp.arange(ntok)[None, :]
    tables = []
    for kpos in (kpos_old, kpos_new):
        d = tq - kpos
        per_head = _head_bias(rel_bias, d, (d >= 0) & (d <= WINDOW) & (PAST_LEN + kpos >= 0))
        tables.append(_head_rows(per_head, decode_rows))
    bias_d = tables[0]
    biasn_d = jnp.moveaxis(tables[1], -1, 2)[..., None]

    s = sinks.astype(F32)[:, :, None, None]
    sink_p = _head_rows(jnp.broadcast_to(s, s.shape[:2] + (WINDOW, 1)), _prompt_rows)
    sink_d = _head_rows(jnp.broadcast_to(s, s.shape[:2] + (ntok, 1)), decode_rows)
    return bias_p, bias_d, biasn_d, sink_p, sink_d


def _stacked_weights(w_in, pool_w, pool_scale, proj_a, proj_b, w_out, ln1_g, ln1_b, router_group_w,
                     router_group_b, router_expert_w, router_expert_b, expert_w_gate, expert_w_up,
                     expert_w_down, ln2_g, ln2_b):
    depth = w_in.shape[0]
    same_group = jnp.eye(N_POOL_GROUPS, dtype=bool)[None, :, None, :, None]
    pool_bd = jnp.where(same_group, pool_w[:, :, :, None, :], 0.0).reshape(depth, D_POOL, D_POOL)
    pad = ROUTER_LANES - N_EXPERTS - N_EXPERT_GROUPS
    rw = jnp.concatenate([router_expert_w, router_group_w, jnp.zeros((depth, D_MODEL, pad), F32)], axis=-1)
    rb = jnp.concatenate([router_expert_b, router_group_b, jnp.zeros((depth, pad), F32)], axis=-1)
    r_hi = rw.astype(BF16)
    row = lambda a: a[:, None, :]
    return dict(
        w_in=w_in.astype(BF16), pool_w=pool_bd.astype(BF16), pool_scale=row(pool_scale),
        proj_a=proj_a.astype(BF16), proj_b=proj_b.astype(BF16), w_out=w_out.astype(BF16),
        ln1_g=row(ln1_g), ln1_b=row(ln1_b),
        router_hi=r_hi, router_lo=(rw - r_hi.astype(F32)).astype(BF16), router_b=row(rb),
        w_gate=expert_w_gate, w_up=expert_w_up, w_down=expert_w_down,
        ln2_g=row(ln2_g), ln2_b=row(ln2_b))


def _decode_mixer(xs_tn, lw, cache_k, cache_v, hist_t, tables, buf, new_caches, *, nseq, ntok, g, row_block):
    bias_d, biasn_d, sink_d = tables
    u = _in_proj(xs_tn, lw)
    u3 = u.reshape(ntok, nseq, D_IN)
    b, new_k, new_v = _decode_attn(u3, cache_k, cache_v, bias_d, biasn_d, sink_d, new_caches, g=g,
                                   layer=lw["layer"])
    buf = _decode_tail(xs_tn, u, hist_t, b.reshape(ntok * nseq, Q_DIM), lw, buf, nseq=nseq, ntok=ntok,
                       row_block=row_block)
    return buf, u3, (new_k, new_v)


def kernel(x_prompt, x_sample, cache_k, cache_v, state_pool, rel_bias, w_in, pool_w, pool_scale, proj_a, proj_b, w_out, attn_sinks, ln1_g, ln1_b, router_group_w, router_group_b, router_expert_w, router_expert_b, expert_w_gate, expert_w_up, expert_w_down, ln2_g, ln2_b):
    bsz, seq, _ = x_prompt.shape
    nseq, ntok, _ = x_sample.shape
    g = 16
    depth = w_in.shape[0]
    n_prompt = bsz * seq
    n_decode = ntok * nseq
    assert n_prompt % n_decode == 0
    bias_p, bias_d, biasn_d, sink_p, sink_d = _attention_tables(rel_bias, attn_sinks, ntok, g)
    weights = _stacked_weights(w_in, pool_w, pool_scale, proj_a, proj_b, w_out, ln1_g, ln1_b, router_group_w,
                               router_group_b, router_expert_w, router_expert_b, expert_w_gate, expert_w_up,
                               expert_w_down, ln2_g, ln2_b)

    xp = x_prompt.reshape(n_prompt, D_MODEL)
    xs_tn = jnp.swapaxes(x_sample, 0, 1).reshape(n_decode, D_MODEL)
    hist_t = jnp.swapaxes(state_pool, 1, 2)
    ck_t = jnp.transpose(cache_k, (0, 1, 3, 4, 2))
    cv_t = jnp.transpose(cache_v, (0, 1, 3, 4, 2))
    prompt_state, decode_u, new_caches = [], [], None
    for l in range(depth):
        lw = dict(weights, layer=l)
        x1, kst, vst, pst = _prompt_mixer(xp, lw, bias_p, sink_p, bsz=bsz, n_total=n_prompt + n_decode)
        x1, u3, new_caches = _decode_mixer(xs_tn, lw, ck_t, cv_t, hist_t, (bias_d, biasn_d, sink_d), x1,
                                           new_caches, nseq=nseq, ntok=ntok, g=g,
                                           row_block=n_prompt // n_decode)
        xp, xs_tn = _moe(x1, lw, n_prompt=n_prompt)
        prompt_state.append((kst, vst, pst))
        decode_u.append(u3)

    pk, pv, pp = (jnp.stack(parts) for parts in zip(*prompt_state))
    heads = lambda a: a.reshape(a.shape[:3] + (N_KV_HEADS, HEAD_DIM))
    new_pool = jnp.stack([jnp.swapaxes(u3[:, :, :D_POOL], 0, 1) for u3 in decode_u])
    sample_pool = jnp.concatenate([state_pool[:, :, ntok:], new_pool], axis=2)
    sample_k, sample_v = (jnp.transpose(c, (0, 1, 4, 2, 3)) for c in new_caches)
    y_sample = jnp.swapaxes(xs_tn.reshape(ntok, nseq, D_MODEL), 0, 1)
    return (xp.reshape(bsz, seq, D_MODEL), y_sample, heads(pk), heads(pv), pp[:, :, -POOL_HIST:],
            sample_k, sample_v, sample_pool)
```
